```python
import jax, jax.numpy as jnp
from jax import lax
import numpy as np

D_MODEL = 2048
BATCH = 4
SEQ = 2048
DEPTH = 2

CHUNK = 64
N_HEADS = 16
HEAD_DIM = 128
N_KV_HEADS = 4
HEADS_PER_KV = N_HEADS // N_KV_HEADS
IDX_HEADS = 16
IDX_DIM = 64
TOPK_MAX = 256
ROPE_THETA = 10000.0
POOL_WINDOWS = (2, 4, 8, 16)
POOL_W = D_MODEL
POOL_GROUP = POOL_W // len(POOL_WINDOWS)
LRU_W = D_MODEL
LRU_BLOCKS = 16
LRU_BLOCK = LRU_W // LRU_BLOCKS
CONV_WIDTH = 4
LRU_C = 8.0
ATTN_W = N_HEADS * HEAD_DIM
KV_W = N_KV_HEADS * HEAD_DIM
MIX_W = ATTN_W + POOL_W + LRU_W
IN_SPLITS = (ATTN_W, KV_W, KV_W, IDX_HEADS * IDX_DIM, IDX_DIM, IDX_HEADS,
             POOL_W, LRU_W, LRU_W, MIX_W)
IN_COLS = sum(IN_SPLITS)
IN_OFFSETS = tuple(int(v) for v in np.cumsum(IN_SPLITS)[:-1])
D_FF = 5504
N_EXPERTS = 8
TOP_K = 2
D_EXPERT = 7168
MOE_BLOCK = 128
N_DENSE = (DEPTH + 1) // 2
N_MOE = DEPTH // 2
ALPHA = (2 * DEPTH) ** 0.25
BETA = (8 * DEPTH) ** -0.25
LN_EPS = 1e-5

kernel_name = "chunk_causal_hybrid_dsa_pool_rglru_moe"


def layer_norm(x, g, b):
    xf = x.astype(jnp.float32)
    mu = jnp.mean(xf, axis=-1, keepdims=True)
    var = jnp.mean(jnp.square(xf - mu), axis=-1, keepdims=True)
    y = (xf - mu) * lax.rsqrt(var + LN_EPS) * g.astype(jnp.float32) + b.astype(jnp.float32)
    return y.astype(x.dtype)


def rope_tables(positions, dim):
    inv = ROPE_THETA ** (-jnp.arange(0, dim, 2, dtype=jnp.float32) / dim)
    ang = positions.astype(jnp.float32)[..., None] * inv
    return jnp.cos(ang)[:, :, None, :], jnp.sin(ang)[:, :, None, :]


def apply_rope(x, cos, sin):
    half = x.shape[-1] // 2
    x1 = x[..., :half].astype(jnp.float32)
    x2 = x[..., half:].astype(jnp.float32)
    return jnp.concatenate([x1 * cos - x2 * sin, x2 * cos + x1 * sin], axis=-1).astype(x.dtype)


def dsa_attention(q, k, v, qi, ki, wi):
    B, S = q.shape[0], q.shape[1]
    n_blk = S // CHUNK
    topk = min(TOPK_MAX, S // 4)
    key_pos = jnp.arange(S)
    wi = wi * (IDX_HEADS ** -0.5)

    def to_blocks(t):
        return jnp.swapaxes(t.reshape((B, n_blk, CHUNK) + t.shape[2:]), 0, 1)

    def one_chunk(args):
        q_b, qi_b, wi_b, blk = args
        limit = (blk + 1) * CHUNK
        s_idx = jnp.einsum('bqhd,bsd->bqhs', qi_b, ki) * (IDX_DIM ** -0.5)
        score = jnp.einsum('bqh,bqhs->bqs', wi_b.astype(jnp.float32),
                           jax.nn.relu(s_idx).astype(jnp.float32))
        score = jnp.where(key_pos < limit, score, -jnp.inf)
        _, sel = lax.top_k(score, topk)
        valid = sel < limit
        k_sel = jax.vmap(lambda kb, ib: kb[ib])(k, sel)
        v_sel = jax.vmap(lambda vb, ib: vb[ib])(v, sel)
        qg = q_b.reshape(B, CHUNK, N_KV_HEADS, HEADS_PER_KV, HEAD_DIM)
        logits = jnp.einsum('bqgrd,bqkgd->bqgrk', qg, k_sel).astype(jnp.float32) * (HEAD_DIM ** -0.5)
        logits = jnp.where(valid[:, :, None, None, :], logits, -jnp.inf)
        p = jax.nn.softmax(logits, axis=-1).astype(v.dtype)
        o = jnp.einsum('bqgrk,bqkgd->bqgrd', p, v_sel)
        return o.reshape(B, CHUNK, ATTN_W)

    out = lax.map(one_chunk, (to_blocks(q), to_blocks(qi), to_blocks(wi), jnp.arange(n_blk)))
    return jnp.swapaxes(out, 0, 1).reshape(B, S, ATTN_W)


def pool_mixer(p, pool_w, pool_scale):
    B, S, _ = p.shape
    pg = p.reshape(B, S, len(POOL_WINDOWS), POOL_GROUP)
    pf = pg.astype(jnp.float32)
    csum = jnp.cumsum(pf, axis=1)
    t = jnp.arange(S)
    means = []
    for g, win in enumerate(POOL_WINDOWS):
        c = csum[:, :, g]
        prev = jnp.pad(c, ((0, 0), (win, 0), (0, 0)))[:, :S]
        cnt = jnp.minimum(t + 1, win).astype(jnp.float32)[None, :, None]
        means.append((c - prev) / cnt)
    diff = (jnp.stack(means, axis=2) - pf).astype(p.dtype)
    y = jnp.einsum('bsgc,gcd->bsgd', diff, pool_w).reshape(B, S, POOL_W)
    return y * pool_scale


def rglru_mixer(xr, gr, conv_w, conv_b, wa, ba, wx, bx, lam):
    B, S, _ = xr.shape
    xp = jnp.pad(xr, ((0, 0), (CONV_WIDTH - 1, 0), (0, 0)))
    xc = conv_b
    for tap in range(CONV_WIDTH):
        xc = xc + xp[:, tap:tap + S] * conv_w[tap]
    xb = xc.reshape(B, S, LRU_BLOCKS, LRU_BLOCK)
    r = jax.nn.sigmoid(jnp.einsum('bsni,nij->bsnj', xb, wa).reshape(B, S, LRU_W) + ba)
    i = jax.nn.sigmoid(jnp.einsum('bsni,nij->bsnj', xb, wx).reshape(B, S, LRU_W) + bx)
    log_a = -LRU_C * r.astype(jnp.float32) * jax.nn.softplus(-lam.astype(jnp.float32))
    a = jnp.exp(log_a)
    b_in = jnp.sqrt(-jnp.expm1(2.0 * log_a)) * (i * xc).astype(jnp.float32)

    def combine(left, right):
        a1, b1 = left
        a2, b2 = right
        return a1 * a2, a2 * b1 + b2

    _, h = lax.associative_scan(combine, (a, b_in), axis=1)
    return h.astype(xr.dtype) * jax.nn.gelu(gr)


def hybrid_mixer(x, cos_a, sin_a, cos_i, sin_i, w_in, w_out, pool_w, pool_scale,
                 conv_w, conv_b, wa, ba, wx, bx, lam):
    B, S, _ = x.shape
    proj = jnp.einsum('bsd,dc->bsc', x, w_in)
    q, k, v, qi, ki, wi, p, xr, gr, gates = jnp.split(proj, IN_OFFSETS, axis=-1)
    q = apply_rope(q.reshape(B, S, N_HEADS, HEAD_DIM), cos_a, sin_a)
    k = apply_rope(k.reshape(B, S, N_KV_HEADS, HEAD_DIM), cos_a, sin_a)
    v = v.reshape(B, S, N_KV_HEADS, HEAD_DIM)
    qi = apply_rope(qi.reshape(B, S, IDX_HEADS, IDX_DIM), cos_i, sin_i)
    ki = apply_rope(ki[:, :, None, :], cos_i, sin_i)[:, :, 0]
    y_a = dsa_attention(q, k, v, qi, ki, wi)
    y_b = pool_mixer(p, pool_w, pool_scale)
    y_c = rglru_mixer(xr, gr, conv_w, conv_b, wa, ba, wx, bx, lam)
    merged = jax.nn.sigmoid(gates) * jnp.concatenate([y_a, y_b, y_c], axis=-1)
    return jnp.einsum('bsc,cd->bsd', merged, w_out)


def swiglu(x, w_gate, w_up, w_down):
    h = jax.nn.silu(jnp.einsum('bsd,df->bsf', x, w_gate)) * jnp.einsum('bsd,df->bsf', x, w_up)
    return jnp.einsum('bsf,fd->bsd', h, w_down)


def moe_swiglu(x, w_router, w_gate, w_up, w_down):
    B, S, D = x.shape
    xt = x.reshape(-1, D)
    n = xt.shape[0]
    logits = jnp.einsum('nd,de->ne', xt.astype(jnp.float32), w_router.astype(jnp.float32))
    top_logit, top_e = lax.top_k(logits, TOP_K)
    top_w = jax.nn.softmax(top_logit, axis=-1)
    n_assign = n * TOP_K
    flat_e = top_e.reshape(-1)
    flat_w = top_w.reshape(-1)
    flat_tok = jnp.repeat(jnp.arange(n, dtype=jnp.int32), TOP_K)
    order = jnp.argsort(flat_e)
    se, st, sw = flat_e[order], flat_tok[order], flat_w[order]
    counts = jnp.bincount(flat_e, length=N_EXPERTS)
    padded = (counts + MOE_BLOCK - 1) // MOE_BLOCK * MOE_BLOCK
    start_sorted = jnp.cumsum(counts) - counts
    end_padded = jnp.cumsum(padded)
    start_padded = end_padded - padded
    dest = start_padded[se] + jnp.arange(n_assign) - start_sorted[se]
    n_rows = n_assign + N_EXPERTS * MOE_BLOCK
    n_blocks = n_rows // MOE_BLOCK
    row_tok = jnp.zeros((n_rows,), jnp.int32).at[dest].set(st)
    row_w = jnp.zeros((n_rows,), jnp.float32).at[dest].set(sw)
    blk_e = jnp.minimum(jnp.searchsorted(end_padded, jnp.arange(n_blocks) * MOE_BLOCK, side='right'),
                        N_EXPERTS - 1)
    xb = xt[row_tok].reshape(n_blocks, MOE_BLOCK, D)

    def expert_rows(args):
        xblk, e = args
        h = jax.nn.silu(xblk @ w_gate[e]) * (xblk @ w_up[e])
        return h @ w_down[e]

    yb = lax.map(expert_rows, (xb, blk_e)).reshape(n_rows, D)
    y = jnp.zeros((n, D), jnp.float32).at[row_tok].add(yb.astype(jnp.float32) * row_w[:, None])
    return y.astype(x.dtype).reshape(B, S, D)


def setup_inputs(seed: int = 0) -> dict:
    key = jax.random.key(seed)
    ks = jax.random.split(key, 26)
    f32 = jnp.float32

    def nrm(k, shape, scale):
        return jax.random.normal(k, shape, f32) * scale

    x = jax.random.normal(ks[0], (BATCH, SEQ, D_MODEL), f32)
    offsets = jax.random.randint(ks[1], (BATCH, 1), 0, 64) * CHUNK
    positions = (offsets + jnp.arange(SEQ, dtype=jnp.int32)[None, :]).astype(jnp.int32)
    u = jax.random.uniform(ks[12], (DEPTH, LRU_W), f32, minval=0.9, maxval=0.999)
    s_base = u ** (1.0 / LRU_C)
    lru_lam = jnp.log(s_base) - jnp.log1p(-s_base)
    return {
        "x": x,
        "positions": positions,
        "mix_w_in": nrm(ks[2], (DEPTH, D_MODEL, IN_COLS), D_MODEL ** -0.5),
        "mix_w_out": nrm(ks[3], (DEPTH, MIX_W, D_MODEL), BETA * MIX_W ** -0.5),
        "pool_w": nrm(ks[4], (DEPTH, len(POOL_WINDOWS), POOL_GROUP, POOL_GROUP), POOL_GROUP ** -0.5),
        "pool_scale": 1.0 + nrm(ks[5], (DEPTH, POOL_W), 0.02),
        "conv_w": nrm(ks[6], (DEPTH, CONV_WIDTH, LRU_W), CONV_WIDTH ** -0.5),
        "conv_b": nrm(ks[7], (DEPTH, LRU_W), 0.02),
        "lru_wa": nrm(ks[8], (DEPTH, LRU_BLOCKS, LRU_BLOCK, LRU_BLOCK), LRU_BLOCK ** -0.5),
        "lru_ba": nrm(ks[9], (DEPTH, LRU_W), 0.02),
        "lru_wx": nrm(ks[10], (DEPTH, LRU_BLOCKS, LRU_BLOCK, LRU_BLOCK), LRU_BLOCK ** -0.5),
        "lru_bx": nrm(ks[11], (DEPTH, LRU_W), 0.02),
        "lru_lam": lru_lam,
        "ln_mix_g": 1.0 + nrm(ks[13], (DEPTH, D_MODEL), 0.02),
        "ln_mix_b": nrm(ks[14], (DEPTH, D_MODEL), 0.02),
        "ln_ffn_g": 1.0 + nrm(ks[15], (DEPTH, D_MODEL), 0.02),
        "ln_ffn_b": nrm(ks[16], (DEPTH, D_MODEL), 0.02),
        "dense_w_gate": nrm(ks[17], (N_DENSE, D_MODEL, D_FF), D_MODEL ** -0.5),
        "dense_w_up": nrm(ks[18], (N_DENSE, D_MODEL, D_FF), D_MODEL ** -0.5),
        "dense_w_down": nrm(ks[19], (N_DENSE, D_FF, D_MODEL), BETA * D_FF ** -0.5),
        "moe_router": nrm(ks[20], (N_MOE, D_MODEL, N_EXPERTS), D_MODEL ** -0.5),
        "moe_w_gate": nrm(ks[21], (N_MOE, N_EXPERTS, D_MODEL, D_EXPERT), D_MODEL ** -0.5),
        "moe_w_up": nrm(ks[22], (N_MOE, N_EXPERTS, D_MODEL, D_EXPERT), D_MODEL ** -0.5),
        "moe_w_down": nrm(ks[23], (N_MOE, N_EXPERTS, D_EXPERT, D_MODEL), BETA * D_EXPERT ** -0.5),
    }


def reference(x, positions, mix_w_in, mix_w_out, pool_w, pool_scale, conv_w, conv_b,
              lru_wa, lru_ba, lru_wx, lru_bx, lru_lam, ln_mix_g, ln_mix_b, ln_ffn_g, ln_ffn_b,
              dense_w_gate, dense_w_up, dense_w_down, moe_router, moe_w_gate, moe_w_up, moe_w_down):
    cos_a, sin_a = rope_tables(positions, HEAD_DIM)
    cos_i, sin_i = rope_tables(positions, IDX_DIM)
    for layer in range(DEPTH):
        m = hybrid_mixer(x, cos_a, sin_a, cos_i, sin_i, mix_w_in[layer], mix_w_out[layer],
                         pool_w[layer], pool_scale[layer], conv_w[layer], conv_b[layer],
                         lru_wa[layer], lru_ba[layer], lru_wx[layer], lru_bx[layer], lru_lam[layer])
        x = layer_norm(ALPHA * x + m, ln_mix_g[layer], ln_mix_b[layer])
        if layer % 2 == 0:
            j = layer // 2
            f = swiglu(x, dense_w_gate[j], dense_w_up[j], dense_w_down[j])
        else:
            j = layer // 2
            f = moe_swiglu(x, moe_router[j], moe_w_gate[j], moe_w_up[j], moe_w_down[j])
        x = layer_norm(ALPHA * x + f, ln_ffn_g[layer], ln_ffn_b[layer])
    return x
```

```python
import functools

import jax
import jax.numpy as jnp
import numpy as np
from jax import lax
from jax.experimental import pallas as pl
from jax.experimental.pallas import tpu as pltpu

F32 = jnp.float32
BF16 = jnp.bfloat16
I32 = jnp.int32

LANES = 128
SUBLANES = 8
VMEM_LIMIT = 56 * 1024 * 1024

CHUNK = 64
N_HEADS = 16
HEAD_DIM = 128
N_KV = 4
HEADS_PER_KV = N_HEADS // N_KV
IDX_HEADS = 16
IDX_DIM = 64
TOPK_MAX = 256
ROPE_THETA = 10000.0
POOL_WINDOWS = (2, 4, 8, 16)
LRU_BLOCK = 128
CONV_WIDTH = 4
LRU_C = 8.0
N_EXPERTS = 8
TOP_K = 2
LN_EPS = 1e-5
INT_MIN = -2 ** 31
NEG_BIG = -1e30


def _cparams(*sem):
    return pltpu.CompilerParams(dimension_semantics=sem, vmem_limit_bytes=VMEM_LIMIT)


def _sigmoid(x):
    return 1.0 / (1.0 + jnp.exp(-x))


def _gelu_tanh(x):
    c = np.float32(np.sqrt(2.0 / np.pi))
    return 0.5 * x * (1.0 + jnp.tanh(c * (x + np.float32(0.044715) * (x * x * x))))


def _layer_norm(y, g, b):
    mu = jnp.mean(y, axis=-1, keepdims=True)
    d = y - mu
    var = jnp.mean(d * d, axis=-1, keepdims=True)
    return d * lax.rsqrt(var + LN_EPS) * g + b


def _rope_tab_kernel(pos_ref, inv_a_ref, inv_i_ref, sgn_a_ref, m_cos_ref, add_cos_ref,
                     m_sa_ref, m_sb_ref, cos_a_ref, sin_a_ref, cos_i_ref, sin_ia_ref, sin_ib_ref):
    pos = pos_ref[...]
    ang_a = pos * inv_a_ref[...]
    cos_a_ref[...] = jnp.cos(ang_a)
    sin_a_ref[...] = jnp.sin(ang_a) * sgn_a_ref[...]
    ang_i = pos * inv_i_ref[...]
    s_i = jnp.sin(ang_i)
    cos_i_ref[...] = jnp.cos(ang_i) * m_cos_ref[...] + add_cos_ref[...]
    sin_ia_ref[...] = s_i * m_sa_ref[...]
    sin_ib_ref[...] = s_i * m_sb_ref[...]


def _rope_tables(positions):
    n = positions.size
    pos = jnp.broadcast_to(positions.reshape(n, 1).astype(F32), (n, LANES))
    lane = np.arange(LANES)
    inv_a = (ROPE_THETA ** (-jnp.arange(0, HEAD_DIM, 2, dtype=F32) / HEAD_DIM))
    inv_i = (ROPE_THETA ** (-jnp.arange(0, IDX_DIM, 2, dtype=F32) / IDX_DIM))
    inv_a_row = jnp.concatenate([inv_a, inv_a])[None, :]
    inv_i_row = jnp.concatenate([inv_i, inv_i, jnp.zeros((LANES - IDX_DIM,), F32)])[None, :]
    sgn_a = jnp.asarray(np.where(lane < HEAD_DIM // 2, -1.0, 1.0), F32)[None, :]
    m_cos = jnp.asarray((lane < IDX_DIM).astype(np.float32))[None, :]
    wi_scale = (IDX_HEADS ** -0.5) * (IDX_DIM ** -0.5)
    add_cos = jnp.asarray(np.where((lane >= IDX_DIM) & (lane < IDX_DIM + IDX_HEADS), wi_scale, 0.0), F32)[None, :]
    m_sa = jnp.asarray(np.where(lane < IDX_DIM // 2, -1.0, 0.0), F32)[None, :]
    m_sb = jnp.asarray(np.where((lane >= IDX_DIM // 2) & (lane < IDX_DIM), 1.0, 0.0), F32)[None, :]
    tm = min(n, 1024)
    row = pl.BlockSpec((1, LANES), lambda i: (0, 0))
    tok = pl.BlockSpec((tm, LANES), lambda i: (i, 0))
    out = jax.ShapeDtypeStruct((n, LANES), F32)
    return pl.pallas_call(
        _rope_tab_kernel,
        out_shape=(out,) * 5,
        grid=(n // tm,),
        in_specs=[tok] + [row] * 7,
        out_specs=(tok,) * 5,
        compiler_params=_cparams("parallel"),
        name="rope_tables",
    )(pos, inv_a_row, inv_i_row, sgn_a, m_cos, add_cos, m_sa, m_sb)


def _proj_heads_kernel(x_ref, w_ref, *rest, shifts, heads):
    tabs, o_ref = rest[:-1], rest[-1]
    acc = jnp.dot(x_ref[...].astype(BF16), w_ref[...], preferred_element_type=F32)
    for h in range(heads):
        xh = acc[:, h * LANES:(h + 1) * LANES]
        if tabs:
            y = xh * tabs[0][...]
            for s, t in zip(shifts, tabs[1:]):
                y = y + pltpu.roll(xh, s, 1) * t[...]
        else:
            y = xh
        o_ref[h] = y.astype(o_ref.dtype)


def _proj_heads(x, w, tabs, shifts, out_dtype, *, tm=512, heads_per_tile=4):
    n, d = x.shape
    h_total = w.shape[1] // LANES
    hp = min(heads_per_tile, h_total)
    tm = min(tm, n)
    tab_spec = pl.BlockSpec((tm, LANES), lambda i, j: (i, 0))
    return pl.pallas_call(
        functools.partial(_proj_heads_kernel, shifts=shifts, heads=hp),
        out_shape=jax.ShapeDtypeStruct((h_total, n, LANES), out_dtype),
        grid=(n // tm, h_total // hp),
        in_specs=[pl.BlockSpec((tm, d), lambda i, j: (i, 0)),
                  pl.BlockSpec((d, hp * LANES), lambda i, j: (0, j))] + [tab_spec] * len(tabs),
        out_specs=pl.BlockSpec((hp, tm, LANES), lambda i, j: (j, i, 0)),
        compiler_params=_cparams("parallel", "arbitrary"),
        name="proj_heads",
    )(x, w, *tabs)


def _proj_plain_kernel(x_ref, w_ref, o_ref, *, act):
    acc = jnp.dot(x_ref[...].astype(BF16), w_ref[...], preferred_element_type=F32)
    if act == "gelu":
        acc = _gelu_tanh(acc)
    elif act == "sigmoid":
        acc = _sigmoid(acc)
    o_ref[...] = acc.astype(o_ref.dtype)


def _proj_plain(x, w, act, out_dtype, *, tm=512, tn=512):
    n, d = x.shape
    c = w.shape[1]
    tm = min(tm, n)
    return pl.pallas_call(
        functools.partial(_proj_plain_kernel, act=act),
        out_shape=jax.ShapeDtypeStruct((n, c), out_dtype),
        grid=(n // tm, c // tn),
        in_specs=[pl.BlockSpec((tm, d), lambda i, j: (i, 0)),
                  pl.BlockSpec((d, tn), lambda i, j: (0, j))],
        out_specs=pl.BlockSpec((tm, tn), lambda i, j: (i, j)),
        compiler_params=_cparams("parallel", "arbitrary"),
        name="proj_plain",
    )(x, w)


def _attn_kernel(q_ref, k_ref, v_ref, qi_ref, ki_ref, wi_ref, gate_ref, o_ref,
                 wb_s, key_s, m_s, l_s, acc_s, x_s, *, tq, topk, seq):
    kb = tq
    n_sub = kb // LANES
    qt = pl.program_id(1)
    nkb = qt + 1
    n_hi = IDX_HEADS
    rows_g = HEADS_PER_KV * tq

    wi = wi_ref[...]
    for h in range(n_hi):
        wb_s[h] = jnp.broadcast_to(wi[:, IDX_DIM + h:IDX_DIM + h + 1], (tq, LANES))
    qi = qi_ref[...].reshape(n_hi * tq, LANES)
    q_row = qt * tq + lax.broadcasted_iota(I32, (tq, LANES), 0)
    limit = (q_row // CHUNK + 1) * CHUNK
    lane_pos = lax.broadcasted_iota(I32, (tq, LANES), 1)

    def score_body(j, carry):
        start = pl.multiple_of(j * kb, kb)
        ki_blk = ki_ref[pl.ds(start, kb), :].astype(BF16)
        s = lax.dot_general(qi, ki_blk, (((1,), (1,)), ((), ())), preferred_element_type=F32)
        for c in range(n_sub):
            sc = jnp.zeros((tq, LANES), F32)
            for h in range(n_hi):
                sh = s[h * tq:(h + 1) * tq, c * LANES:(c + 1) * LANES]
                sc = sc + wb_s[h] * jnp.maximum(sh, 0.0)
            sc = jnp.where(sc == 0.0, 0.0, sc)
            bits = pltpu.bitcast(sc, I32)
            key = bits ^ ((bits >> 31) & 0x7FFFFFFF)
            kpos = start + c * LANES + lane_pos
            key = jnp.where(kpos < limit, key, INT_MIN)
            key_s[j, :, c * LANES:(c + 1) * LANES] = key
        return carry

    lax.fori_loop(0, nkb, score_body, 0)

    def count_ge(thr):
        def body(j, cnt):
            kblk = key_s[j]
            for c in range(n_sub):
                cnt = cnt + jnp.where(kblk[:, c * LANES:(c + 1) * LANES] >= thr, 1.0, 0.0)
            return cnt
        cnt = lax.fori_loop(0, nkb, body, jnp.zeros((tq, LANES), F32))
        return jnp.sum(cnt, axis=1, keepdims=True)

    kf = np.float32(topk)
    c0 = count_ge(jnp.zeros((tq, LANES), I32))
    thr0 = jnp.where(jnp.broadcast_to(c0, (tq, LANES)) >= kf, 0, INT_MIN).astype(I32)

    def thr_body(it, thr):
        bit = jnp.left_shift(jnp.int32(1), 30 - it)
        cnt = count_ge(thr | bit)
        ok = jnp.broadcast_to(cnt, (tq, LANES)) >= kf
        return jnp.where(ok, thr | bit, thr)

    thr = lax.fori_loop(0, 31, thr_body, thr0)
    n_ge = jnp.broadcast_to(count_ge(thr), (tq, LANES))
    n_gt = jnp.broadcast_to(count_ge(thr + 1), (tq, LANES))
    need = kf - n_gt
    searched = limit > topk
    thr = jnp.where(searched, thr, INT_MIN)
    x_s[...] = jnp.where(searched, seq, -1).astype(I32)
    has_tie = jnp.max(jnp.where(searched & (n_ge > kf), 1.0, 0.0), axis=(0, 1), keepdims=True)

    @pl.when(has_tie[0, 0] > 0.0)
    def _():
        def count_eq_below(xcut):
            def body(j, cnt):
                kblk = key_s[j]
                for c in range(n_sub):
                    kpos = j * kb + c * LANES + lane_pos
                    hit = (kblk[:, c * LANES:(c + 1) * LANES] == thr) & (kpos < xcut)
                    cnt = cnt + jnp.where(hit, 1.0, 0.0)
                return cnt
            cnt = lax.fori_loop(0, nkb, body, jnp.zeros((tq, LANES), F32))
            return jnp.broadcast_to(jnp.sum(cnt, axis=1, keepdims=True), (tq, LANES))

        nbits = max(int(seq - 1).bit_length(), 1)

        def x_body(it, xcut):
            bit = jnp.left_shift(jnp.int32(1), nbits - 1 - it)
            cand = xcut | bit
            return jnp.where(count_eq_below(cand) < need, cand, xcut)

        xcut = lax.fori_loop(0, nbits, x_body, jnp.zeros((tq, LANES), I32))
        x_s[...] = jnp.where(searched, xcut, -1)

    xcut = x_s[...]

    m_s[...] = jnp.full(m_s.shape, NEG_BIG, F32)
    l_s[...] = jnp.zeros(l_s.shape, F32)
    acc_s[...] = jnp.zeros(acc_s.shape, F32)
    scale = np.float32(HEAD_DIM ** -0.5)

    def attn_body(j, carry):
        start = pl.multiple_of(j * kb, kb)
        kblk = key_s[j]
        sel_parts = []
        for c in range(n_sub):
            kc = kblk[:, c * LANES:(c + 1) * LANES]
            kpos = start + c * LANES + lane_pos
            sel = (kc > thr) | ((kc == thr) & (kpos <= xcut))
            sel_parts.append(jnp.where(sel, 1.0, 0.0))
        self = jnp.concatenate(sel_parts, axis=1)
        sel4 = jnp.concatenate([self] * HEADS_PER_KV, axis=0)
        for g in range(N_KV):
            qg = q_ref[g * HEADS_PER_KV:(g + 1) * HEADS_PER_KV].reshape(rows_g, LANES)
            kg = k_ref[g, pl.ds(start, kb), :]
            vg = v_ref[g, pl.ds(start, kb), :]
            lg = lax.dot_general(qg, kg, (((1,), (1,)), ((), ())), preferred_element_type=F32) * scale
            lg = jnp.where(sel4 > 0.0, lg, NEG_BIG)
            m_prev = m_s[g]
            m_new = jnp.maximum(m_prev, jnp.max(lg, axis=1, keepdims=True))
            alpha = jnp.exp(m_prev - m_new)
            p = jnp.exp(lg - m_new[:, :1]) * sel4
            l_s[g] = alpha * l_s[g] + jnp.sum(p, axis=1, keepdims=True)
            acc_s[g] = alpha * acc_s[g] + jnp.dot(p.astype(BF16), vg, preferred_element_type=F32)
            m_s[g] = m_new
        return carry

    lax.fori_loop(0, nkb, attn_body, 0)

    for g in range(N_KV):
        og = acc_s[g] / l_s[g]
        for r in range(HEADS_PER_KV):
            col = (g * HEADS_PER_KV + r) * LANES
            y = og[r * tq:(r + 1) * tq] * gate_ref[:, col:col + LANES].astype(F32)
            o_ref[:, col:col + LANES] = y.astype(o_ref.dtype)


def _attention(qh, kh, vh, qih, kiwi, sg, batch, seq, *, tq):
    n = batch * seq
    topk = min(TOPK_MAX, seq // 4)
    tq = min(tq, seq)
    nqt = seq // tq
    attn_w = N_HEADS * HEAD_DIM
    rows_g = HEADS_PER_KV * tq
    return pl.pallas_call(
        functools.partial(_attn_kernel, tq=tq, topk=topk, seq=seq),
        out_shape=jax.ShapeDtypeStruct((n, attn_w), BF16),
        grid=(batch, nqt),
        in_specs=[
            pl.BlockSpec((N_HEADS, tq, LANES), lambda b, t: (0, b * nqt + t, 0)),
            pl.BlockSpec((N_KV, seq, LANES), lambda b, t: (0, b, 0)),
            pl.BlockSpec((N_KV, seq, LANES), lambda b, t: (0, b, 0)),
            pl.BlockSpec((IDX_HEADS, tq, LANES), lambda b, t: (0, b * nqt + t, 0)),
            pl.BlockSpec((None, seq, LANES), lambda b, t: (0, b, 0)),
            pl.BlockSpec((None, tq, LANES), lambda b, t: (0, b * nqt + t, 0)),
            pl.BlockSpec((tq, attn_w), lambda b, t: (b * nqt + t, 0)),
        ],
        out_specs=pl.BlockSpec((tq, attn_w), lambda b, t: (b * nqt + t, 0)),
        scratch_shapes=[
            pltpu.VMEM((IDX_HEADS, tq, LANES), F32),
            pltpu.VMEM((nqt, tq, tq), I32),
            pltpu.VMEM((N_KV, rows_g, LANES), F32),
            pltpu.VMEM((N_KV, rows_g, LANES), F32),
            pltpu.VMEM((N_KV, rows_g, LANES), F32),
            pltpu.VMEM((tq, LANES), I32),
        ],
        compiler_params=_cparams("parallel", "arbitrary"),
        name="sparse_attention",
    )(qh, kh, vh, qih, kiwi, kiwi, sg)


def _pool_kernel(p_ref, gate_ref, w_ref, scale_ref, o_ref, buf_a, buf_b, *, seq):
    pad = 16
    g = pl.program_id(1)
    p = p_ref[...]
    zeros = jnp.zeros((pad, p.shape[1]), F32)
    buf_a[0:pad, :] = zeros
    buf_b[0:pad, :] = zeros
    buf_a[pad:pad + seq, :] = p
    s2 = p + buf_a[pad - 1:pad - 1 + seq, :]
    buf_b[pad:pad + seq, :] = s2
    s4 = s2 + buf_b[pad - 2:pad - 2 + seq, :]
    buf_a[pad:pad + seq, :] = s4
    s8 = s4 + buf_a[pad - 4:pad - 4 + seq, :]
    buf_b[pad:pad + seq, :] = s8
    s16 = s8 + buf_b[pad - 8:pad - 8 + seq, :]
    t1 = (lax.broadcasted_iota(I32, p.shape, 0) + 1).astype(F32)
    win = jnp.where(g == 0, 2.0, jnp.where(g == 1, 4.0, jnp.where(g == 2, 8.0, 16.0))).astype(F32)
    total = jnp.where(g == 0, s2, jnp.where(g == 1, s4, jnp.where(g == 2, s8, s16)))
    mean = total / jnp.minimum(t1, win)
    diff = (mean - p).astype(BF16)
    y = jnp.dot(diff, w_ref[...], preferred_element_type=F32)
    o_ref[...] = (y * scale_ref[...] * gate_ref[...].astype(F32)).astype(o_ref.dtype)


def _pool_mixer(px, sg, pool_w, pool_scale, batch, seq):
    n = batch * seq
    width = pool_scale.shape[-1]
    ng = len(POOL_WINDOWS)
    cg = width // ng
    assert POOL_WINDOWS == (2, 4, 8, 16)
    return pl.pallas_call(
        functools.partial(_pool_kernel, seq=seq),
        out_shape=jax.ShapeDtypeStruct((n, width), BF16),
        grid=(batch, ng),
        in_specs=[
            pl.BlockSpec((seq, cg), lambda b, g: (b, g)),
            pl.BlockSpec((seq, cg), lambda b, g: (b, ng + g)),
            pl.BlockSpec((None, cg, cg), lambda b, g: (g, 0, 0)),
            pl.BlockSpec((1, cg), lambda b, g: (0, g)),
        ],
        out_specs=pl.BlockSpec((seq, cg), lambda b, g: (b, g)),
        scratch_shapes=[pltpu.VMEM((seq + 16, cg), F32), pltpu.VMEM((seq + 16, cg), F32)],
        compiler_params=_cparams("parallel", "arbitrary"),
        name="pool_mixer",
    )(px, sg, pool_w, pool_scale.reshape(1, width))


def _lru_kernel(x_ref, gr_ref, gate_ref, cw_ref, cb_ref, wa_ref, ba_ref, wx_ref, bx_ref, lam_ref,
                o_ref, xs, a_s, b_s, *, seq, ct):
    pad = SUBLANES
    x = x_ref[...]
    xs[0:pad, :] = jnp.zeros((pad, ct), F32)
    xs[pad:pad + seq, :] = x
    xc = jnp.broadcast_to(cb_ref[...], (seq, ct))
    for tap in range(CONV_WIDTH):
        d = CONV_WIDTH - 1 - tap
        xc = xc + xs[pad - d:pad - d + seq, :] * cw_ref[tap:tap + 1, :]
    xcb = xc.astype(BF16)
    nb = ct // LRU_BLOCK
    r_parts, i_parts = [], []
    for blk in range(nb):
        xb = xcb[:, blk * LRU_BLOCK:(blk + 1) * LRU_BLOCK]
        r_parts.append(jnp.dot(xb, wa_ref[blk], preferred_element_type=F32))
        i_parts.append(jnp.dot(xb, wx_ref[blk], preferred_element_type=F32))
    r = _sigmoid(jnp.concatenate(r_parts, axis=1) + ba_ref[...])
    gi = _sigmoid(jnp.concatenate(i_parts, axis=1) + bx_ref[...])
    lam = lam_ref[...]
    softplus_neg_lam = jnp.log(1.0 + jnp.exp(-lam))
    log_a = -LRU_C * r * softplus_neg_lam
    a = jnp.exp(log_a)
    b = jnp.sqrt(1.0 - a * a) * (gi * xc)

    sub = lax.broadcasted_iota(I32, (seq, ct), 0) % SUBLANES
    for d in (1, 2, 4):
        keep = sub >= d
        a_sh = jnp.where(keep, pltpu.roll(a, d, 0), 1.0)
        b_sh = jnp.where(keep, pltpu.roll(b, d, 0), 0.0)
        b = a * b_sh + b
        a = a * a_sh
    a_s[...] = a
    b_s[...] = b

    def body(t, carry):
        r0 = pl.multiple_of(t * SUBLANES, SUBLANES)
        h = a_s[pl.ds(r0, SUBLANES), :] * carry + b_s[pl.ds(r0, SUBLANES), :]
        b_s[pl.ds(r0, SUBLANES), :] = h
        return jnp.broadcast_to(h[SUBLANES - 1:SUBLANES, :], (SUBLANES, ct))

    lax.fori_loop(0, seq // SUBLANES, body, jnp.zeros((SUBLANES, ct), F32), unroll=8)
    h = b_s[...]
    o_ref[...] = (h * gr_ref[...].astype(F32) * gate_ref[...].astype(F32)).astype(o_ref.dtype)


def _lru_mixer(px, gg, sg, conv_w, conv_b, wa, ba, wx, bx, lam, batch, seq, *, ct=256):
    n = batch * seq
    width = conv_b.shape[-1]
    nct = width // ct
    nb = ct // LRU_BLOCK
    row = lambda a: a.reshape(1, width)
    rspec = pl.BlockSpec((1, ct), lambda b, j: (0, j))
    return pl.pallas_call(
        functools.partial(_lru_kernel, seq=seq, ct=ct),
        out_shape=jax.ShapeDtypeStruct((n, width), BF16),
        grid=(batch, nct),
        in_specs=[
            pl.BlockSpec((seq, ct), lambda b, j: (b, nct + j)),
            pl.BlockSpec((seq, ct), lambda b, j: (b, j)),
            pl.BlockSpec((seq, ct), lambda b, j: (b, 2 * nct + j)),
            pl.BlockSpec((CONV_WIDTH, ct), lambda b, j: (0, j)),
            rspec,
            pl.BlockSpec((nb, LRU_BLOCK, LRU_BLOCK), lambda b, j: (j, 0, 0)),
            rspec,
            pl.BlockSpec((nb, LRU_BLOCK, LRU_BLOCK), lambda b, j: (j, 0, 0)),
            rspec,
            rspec,
        ],
        out_specs=pl.BlockSpec((seq, ct), lambda b, j: (b, j)),
        scratch_shapes=[pltpu.VMEM((seq + SUBLANES, ct), F32), pltpu.VMEM((seq, ct), F32),
                        pltpu.VMEM((seq, ct), F32)],
        compiler_params=_cparams("parallel", "arbitrary"),
        name="rglru_mixer",
    )(px, gg, sg, conv_w, row(conv_b), wa, row(ba), wx, row(bx), row(lam))


def _out_proj_kernel(ya_ref, yb_ref, yc_ref, w_ref, o_ref):
    acc = jnp.dot(ya_ref[...], w_ref[0], preferred_element_type=F32)
    acc = acc + jnp.dot(yb_ref[...], w_ref[1], preferred_element_type=F32)
    acc = acc + jnp.dot(yc_ref[...], w_ref[2], preferred_element_type=F32)
    o_ref[...] = acc


def _out_proj(ya, yb, yc, w3, *, tm=512, tn=512):
    n, width = ya.shape
    d = w3.shape[-1]
    tm = min(tm, n)
    aspec = pl.BlockSpec((tm, width), lambda i, j: (i, 0))
    return pl.pallas_call(
        _out_proj_kernel,
        out_shape=jax.ShapeDtypeStruct((n, d), F32),
        grid=(n // tm, d // tn),
        in_specs=[aspec, aspec, aspec, pl.BlockSpec((3, width, tn), lambda i, j: (0, 0, j))],
        out_specs=pl.BlockSpec((tm, tn), lambda i, j: (i, j)),
        compiler_params=_cparams("parallel", "arbitrary"),
        name="out_proj",
    )(ya, yb, yc, w3)


def _add_ln_kernel(x_ref, m_ref, g_ref, b_ref, o_ref, ob_ref, *, alpha):
    y = alpha * x_ref[...] + m_ref[...]
    out = _layer_norm(y, g_ref[...], b_ref[...])
    o_ref[...] = out
    ob_ref[...] = out.astype(BF16)


def _add_ln(x, m, g, b, alpha, *, tm=256):
    n, d = x.shape
    tm = min(tm, n)
    tok = pl.BlockSpec((tm, d), lambda i: (i, 0))
    row = pl.BlockSpec((1, d), lambda i: (0, 0))
    return pl.pallas_call(
        functools.partial(_add_ln_kernel, alpha=alpha),
        out_shape=(jax.ShapeDtypeStruct((n, d), F32), jax.ShapeDtypeStruct((n, d), BF16)),
        grid=(n // tm,),
        in_specs=[tok, tok, row, row],
        out_specs=(tok, tok),
        compiler_params=_cparams("parallel"),
        name="add_layer_norm",
    )(x, m, g.reshape(1, d), b.reshape(1, d))


def _ffn_dense_kernel(xb_ref, x_ref, wg_ref, wu_ref, wd_ref, g_ref, b_ref, o_ref, ob_ref, acc_s, *, alpha):
    f = pl.program_id(1)

    @pl.when(f == 0)
    def _():
        acc_s[...] = jnp.zeros(acc_s.shape, F32)

    xb = xb_ref[...]
    hg = jnp.dot(xb, wg_ref[...], preferred_element_type=F32)
    hu = jnp.dot(xb, wu_ref[...], preferred_element_type=F32)
    h = (hg * _sigmoid(hg) * hu).astype(BF16)
    acc_s[...] += jnp.dot(h, wd_ref[...], preferred_element_type=F32)

    @pl.when(f == pl.num_programs(1) - 1)
    def _():
        out = _layer_norm(alpha * x_ref[...] + acc_s[...], g_ref[...], b_ref[...])
        o_ref[...] = out
        ob_ref[...] = out.astype(BF16)


def _ffn_dense(xb, x, wg, wu, wd, g, b, alpha, *, tm=512, tf=512):
    n, d = x.shape
    ff = wg.shape[1]
    tm = min(tm, n)
    tok = pl.BlockSpec((tm, d), lambda i, f: (i, 0))
    row = pl.BlockSpec((1, d), lambda i, f: (0, 0))
    return pl.pallas_call(
        functools.partial(_ffn_dense_kernel, alpha=alpha),
        out_shape=(jax.ShapeDtypeStruct((n, d), F32), jax.ShapeDtypeStruct((n, d), BF16)),
        grid=(n // tm, ff // tf),
        in_specs=[tok, tok,
                  pl.BlockSpec((d, tf), lambda i, f: (0, f)),
                  pl.BlockSpec((d, tf), lambda i, f: (0, f)),
                  pl.BlockSpec((tf, d), lambda i, f: (f, 0)),
                  row, row],
        out_specs=(tok, tok),
        scratch_shapes=[pltpu.VMEM((tm, d), F32)],
        compiler_params=_cparams("parallel", "arbitrary"),
        name="ffn_dense",
    )(xb, x, wg, wu, wd, g.reshape(1, d), b.reshape(1, d))


def _router_kernel(x_ref, w_ref, ids_ref, wts_ref):
    logits = jnp.dot(x_ref[...], w_ref[...], preferred_element_type=F32, precision=lax.Precision.HIGHEST)
    lane_i = lax.broadcasted_iota(I32, logits.shape, 1)
    lane = lane_i.astype(F32)
    logits = jnp.where(lane_i < N_EXPERTS, logits, -jnp.inf)
    m1 = jnp.max(logits, axis=1, keepdims=True)
    i1 = jnp.min(jnp.where(logits == m1, lane, float(LANES)), axis=1, keepdims=True)
    rest = jnp.where(lane == i1, -jnp.inf, logits)
    m2 = jnp.max(rest, axis=1, keepdims=True)
    i2 = jnp.min(jnp.where(rest == m2, lane, float(LANES)), axis=1, keepdims=True)
    e2 = jnp.exp(m2 - m1)
    w1 = 1.0 / (1.0 + e2)
    w2 = e2 / (1.0 + e2)
    ids_ref[...] = jnp.where(lane_i == 0, i1, jnp.where(lane_i == 1, i2, 0.0)).astype(I32)
    wts_ref[...] = jnp.where(lane_i == 0, w1, jnp.where(lane_i == 1, w2, 0.0))


def _router(x, w_router, *, tm=512):
    n, d = x.shape
    wpad = jnp.zeros((d, LANES), F32).at[:, :N_EXPERTS].set(w_router.astype(F32))
    tm = min(tm, n)
    tok = pl.BlockSpec((tm, LANES), lambda i: (i, 0))
    return pl.pallas_call(
        _router_kernel,
        out_shape=(jax.ShapeDtypeStruct((n, LANES), I32), jax.ShapeDtypeStruct((n, LANES), F32)),
        grid=(n // tm,),
        in_specs=[pl.BlockSpec((tm, d), lambda i: (i, 0)), pl.BlockSpec((d, LANES), lambda i: (0, 0))],
        out_specs=(tok, tok),
        compiler_params=_cparams("parallel"),
        name="moe_router",
    )(x, wpad)


def _gather_rows_kernel(tok_ref, x_hbm, o_ref, buf, sem, *, tm):
    base = pl.program_id(0) * tm

    def issue(r, carry):
        t = tok_ref[base + r]
        pltpu.make_async_copy(x_hbm.at[pl.ds(t, 1), :], buf.at[pl.ds(r, 1), :], sem).start()
        return carry

    lax.fori_loop(0, tm, issue, 0)

    def drain(r, carry):
        pltpu.make_async_copy(x_hbm.at[pl.ds(0, 1), :], buf.at[pl.ds(r, 1), :], sem).wait()
        return carry

    lax.fori_loop(0, tm, drain, 0)
    o_ref[...] = buf[...].astype(o_ref.dtype)


def _gather_rows(x, row_tok, *, tm=256):
    n, d = x.shape
    rows = row_tok.shape[0]
    return pl.pallas_call(
        functools.partial(_gather_rows_kernel, tm=tm),
        out_shape=jax.ShapeDtypeStruct((rows, d), BF16),
        grid_spec=pltpu.PrefetchScalarGridSpec(
            num_scalar_prefetch=1,
            grid=(rows // tm,),
            in_specs=[pl.BlockSpec(memory_space=pl.ANY)],
            out_specs=pl.BlockSpec((tm, d), lambda i, tok: (i, 0)),
            scratch_shapes=[pltpu.VMEM((tm, d), F32), pltpu.SemaphoreType.DMA(())],
        ),
        compiler_params=_cparams("arbitrary"),
        name="moe_gather_rows",
    )(row_tok, x)


def _moe_up_kernel(te_ref, nv_ref, x_ref, wg_ref, wu_ref, o_ref):
    i = pl.program_id(1)

    @pl.when(i < nv_ref[0])
    def _():
        xb = x_ref[...]
        hg = jnp.dot(xb, wg_ref[...], preferred_element_type=F32)
        hu = jnp.dot(xb, wu_ref[...], preferred_element_type=F32)
        o_ref[...] = (hg * _sigmoid(hg) * hu).astype(o_ref.dtype)

    @pl.when(i >= nv_ref[0])
    def _():
        o_ref[...] = jnp.zeros(o_ref.shape, o_ref.dtype)


def _moe_up(xs, wg, wu, tile_e, n_valid, *, tm, tf=512):
    rows, d = xs.shape
    ff = wg.shape[-1]
    return pl.pallas_call(
        _moe_up_kernel,
        out_shape=jax.ShapeDtypeStruct((rows, ff), BF16),
        grid_spec=pltpu.PrefetchScalarGridSpec(
            num_scalar_prefetch=2,
            grid=(ff // tf, rows // tm),
            in_specs=[pl.BlockSpec((tm, d), lambda j, i, te, nv: (i, 0)),
                      pl.BlockSpec((None, d, tf), lambda j, i, te, nv: (te[i], 0, j)),
                      pl.BlockSpec((None, d, tf), lambda j, i, te, nv: (te[i], 0, j))],
            out_specs=pl.BlockSpec((tm, tf), lambda j, i, te, nv: (i, j)),
        ),
        compiler_params=_cparams("arbitrary", "arbitrary"),
        name="moe_up",
    )(tile_e, n_valid, xs, wg, wu)


def _moe_down_kernel(te_ref, nv_ref, h_ref, wd_ref, o_ref):
    i = pl.program_id(1)

    @pl.when(i < nv_ref[0])
    def _():
        o_ref[...] = jnp.dot(h_ref[...], wd_ref[...], preferred_element_type=F32)

    @pl.when(i >= nv_ref[0])
    def _():
        o_ref[...] = jnp.zeros(o_ref.shape, o_ref.dtype)


def _moe_down(h, wd, tile_e, n_valid, *, tm, tn=512):
    rows, ff = h.shape
    d = wd.shape[-1]
    return pl.pallas_call(
        _moe_down_kernel,
        out_shape=jax.ShapeDtypeStruct((rows, d), F32),
        grid_spec=pltpu.PrefetchScalarGridSpec(
            num_scalar_prefetch=2,
            grid=(d // tn, rows // tm),
            in_specs=[pl.BlockSpec((tm, ff), lambda j, i, te, nv: (i, 0)),
                      pl.BlockSpec((None, ff, tn), lambda j, i, te, nv: (te[i], 0, j))],
            out_specs=pl.BlockSpec((tm, tn), lambda j, i, te, nv: (i, j)),
        ),
        compiler_params=_cparams("arbitrary", "arbitrary"),
        name="moe_down",
    )(tile_e, n_valid, h, wd)


def _combine_ln_kernel(pos_ref, y_hbm, x_ref, wts_ref, g_ref, b_ref, o_ref, buf, sem, *, tm, alpha):
    base = pl.program_id(0) * tm

    def issue(r, carry):
        for k in range(TOP_K):
            row = pos_ref[(base + r) * TOP_K + k]
            pltpu.make_async_copy(y_hbm.at[pl.ds(row, 1), :], buf.at[k, pl.ds(r, 1), :], sem).start()
        return carry

    lax.fori_loop(0, tm, issue, 0)

    def drain(r, carry):
        for k in range(TOP_K):
            pltpu.make_async_copy(y_hbm.at[pl.ds(0, 1), :], buf.at[k, pl.ds(r, 1), :], sem).wait()
        return carry

    lax.fori_loop(0, tm, drain, 0)
    wts = wts_ref[...]
    y = buf[0] * wts[:, 0:1] + buf[1] * wts[:, 1:2]
    o_ref[...] = _layer_norm(alpha * x_ref[...] + y, g_ref[...], b_ref[...])


def _combine_ln(yrows, pos, x, wts, g, b, alpha, *, tm=128):
    n, d = x.shape
    tm = min(tm, n)
    return pl.pallas_call(
        functools.partial(_combine_ln_kernel, tm=tm, alpha=alpha),
        out_shape=jax.ShapeDtypeStruct((n, d), F32),
        grid_spec=pltpu.PrefetchScalarGridSpec(
            num_scalar_prefetch=1,
            grid=(n // tm,),
            in_specs=[pl.BlockSpec(memory_space=pl.ANY),
                      pl.BlockSpec((tm, d), lambda i, pos: (i, 0)),
                      pl.BlockSpec((tm, LANES), lambda i, pos: (i, 0)),
                      pl.BlockSpec((1, d), lambda i, pos: (0, 0)),
                      pl.BlockSpec((1, d), lambda i, pos: (0, 0))],
            out_specs=pl.BlockSpec((tm, d), lambda i, pos: (i, 0)),
            scratch_shapes=[pltpu.VMEM((TOP_K, tm, d), F32), pltpu.SemaphoreType.DMA(())],
        ),
        compiler_params=_cparams("arbitrary"),
        name="moe_combine_ln",
    )(pos, yrows, x, wts, g.reshape(1, d), b.reshape(1, d))


def _moe_block(x, w_router, wg, wu, wd, g, b, alpha, *, tm=256):
    n, d = x.shape
    ids, wts = _router(x, w_router)
    e_flat = ids[:, :TOP_K].reshape(-1)
    n_assign = n * TOP_K
    onehot = (e_flat[:, None] == jnp.arange(N_EXPERTS, dtype=I32)[None, :]).astype(I32)
    rank = jnp.sum((jnp.cumsum(onehot, axis=0) - onehot) * onehot, axis=1)
    counts = jnp.sum(onehot, axis=0)
    padded = (counts + tm - 1) // tm * tm
    end_padded = jnp.cumsum(padded)
    start_padded = end_padded - padded
    dest = (start_padded[e_flat] + rank).astype(I32)
    rows = n_assign + N_EXPERTS * tm
    n_tiles = rows // tm
    flat_tok = jnp.arange(n_assign, dtype=I32) // TOP_K
    row_tok = jnp.zeros((rows,), I32).at[dest].set(flat_tok)
    tile_e = jnp.minimum(jnp.searchsorted(end_padded, jnp.arange(n_tiles, dtype=I32) * tm, side="right"),
                         N_EXPERTS - 1).astype(I32)
    n_valid = (end_padded[-1:] // tm).astype(I32)
    xs = _gather_rows(x, row_tok, tm=tm)
    h = _moe_up(xs, wg, wu, tile_e, n_valid, tm=tm)
    yrows = _moe_down(h, wd, tile_e, n_valid, tm=tm)
    return _combine_ln(yrows, dest, x, wts, g, b, alpha)


def _split_w_in(w_in, d_model):
    attn_w = N_HEADS * HEAD_DIM
    kv_w = N_KV * HEAD_DIM
    idx_w = IDX_HEADS * IDX_DIM
    o = np.cumsum([0, attn_w, kv_w, kv_w, idx_w, IDX_DIM, IDX_HEADS, d_model, d_model, d_model])
    wq = w_in[:, o[0]:o[1]].astype(BF16)
    wk = w_in[:, o[1]:o[2]].astype(BF16)
    wv = w_in[:, o[2]:o[3]].astype(BF16)
    wqi = w_in[:, o[3]:o[4]].reshape(-1, IDX_HEADS, IDX_DIM)
    wqi = jnp.pad(wqi, ((0, 0), (0, 0), (0, LANES - IDX_DIM))).reshape(-1, IDX_HEADS * LANES).astype(BF16)
    wkiwi = jnp.pad(w_in[:, o[4]:o[6]], ((0, 0), (0, LANES - IDX_DIM - IDX_HEADS))).astype(BF16)
    wpx = w_in[:, o[6]:o[8]].astype(BF16)
    wgr = w_in[:, o[8]:o[9]].astype(BF16)
    wgates = w_in[:, o[9]:].astype(BF16)
    return wq, wk, wv, wqi, wkiwi, wpx, wgr, wgates


def _mixer(x_mm, tabs, w_in, w_out, pool_w, pool_scale, conv_w, conv_b, wa, ba, wx, bx, lam,
           batch, seq, d_model, tq):
    cos_a, sin_a, cos_i, sin_ia, sin_ib = tabs
    wq, wk, wv, wqi, wkiwi, wpx, wgr, wgates = _split_w_in(w_in, d_model)
    rope_a = ((cos_a, sin_a), (HEAD_DIM // 2,))
    rope_i = ((cos_i, sin_ia, sin_ib), (LANES - IDX_DIM // 2, IDX_DIM // 2))
    qh = _proj_heads(x_mm, wq, *rope_a, BF16)
    kh = _proj_heads(x_mm, wk, *rope_a, BF16)
    vh = _proj_heads(x_mm, wv, (), (), BF16)
    qih = _proj_heads(x_mm, wqi, *rope_i, BF16)
    kiwi = _proj_heads(x_mm, wkiwi, *rope_i, F32)
    px = _proj_plain(x_mm, wpx, None, F32)
    gg = _proj_plain(x_mm, wgr, "gelu", BF16)
    sg = _proj_plain(x_mm, wgates, "sigmoid", BF16)
    ya = _attention(qh, kh, vh, qih, kiwi, sg, batch, seq, tq=tq)
    yb = _pool_mixer(px, sg, pool_w.astype(BF16), pool_scale, batch, seq)
    yc = _lru_mixer(px, gg, sg, conv_w, conv_b, wa.astype(BF16), ba, wx.astype(BF16), bx, lam, batch, seq)
    mix_w = w_out.shape[0]
    return _out_proj(ya, yb, yc, w_out.astype(BF16).reshape(3, mix_w // 3, d_model))


def _pad_ff(w, axis, mult):
    ff = w.shape[axis]
    padn = (-ff) % mult
    if padn == 0:
        return w
    widths = [(0, 0)] * w.ndim
    widths[axis] = (0, padn)
    return jnp.pad(w, widths)


def kernel(x, positions, mix_w_in, mix_w_out, pool_w, pool_scale, conv_w, conv_b, lru_wa, lru_ba, lru_wx, lru_bx, lru_lam, ln_mix_g, ln_mix_b, ln_ffn_g, ln_ffn_b, dense_w_gate, dense_w_up, dense_w_down, moe_router, moe_w_gate, moe_w_up, moe_w_down):
    batch, seq, d_model = x.shape
    depth = mix_w_in.shape[0]
    alpha = np.float32((2 * depth) ** 0.25)
    n = batch * seq
    tabs = _rope_tables(positions)
    xf = x.reshape(n, d_model)
    x_mm = xf
    for layer in range(depth):
        m = _mixer(x_mm, tabs, mix_w_in[layer], mix_w_out[layer], pool_w[layer], pool_scale[layer],
                   conv_w[layer], conv_b[layer], lru_wa[layer], lru_ba[layer], lru_wx[layer],
                   lru_bx[layer], lru_lam[layer], batch, seq, d_model, tq=256)
        xf, xb = _add_ln(xf, m, ln_mix_g[layer], ln_mix_b[layer], alpha)
        j = layer // 2
        if layer % 2 == 0:
            wg = _pad_ff(dense_w_gate[j], 1, 512).astype(BF16)
            wu = _pad_ff(dense_w_up[j], 1, 512).astype(BF16)
            wd = _pad_ff(dense_w_down[j], 0, 512).astype(BF16)
            xf, x_mm = _ffn_dense(xb, xf, wg, wu, wd, ln_ffn_g[layer], ln_ffn_b[layer], alpha)
        else:
            xf = _moe_block(xf, moe_router[j], moe_w_gate[j].astype(BF16), moe_w_up[j].astype(BF16),
                            moe_w_down[j].astype(BF16), ln_ffn_g[layer], ln_ffn_b[layer], alpha)
            x_mm = xf
    return xf.reshape(batch, seq, d_model)
```

```python
import functools

import jax
import jax.numpy as jnp
import numpy as np
from jax import lax
from jax.experimental import pallas as pl
from jax.experimental.pallas import tpu as pltpu

F32 = jnp.float32
BF16 = jnp.bfloat16
I32 = jnp.int32

LANES = 128
SUBLANES = 8
VMEM_LIMIT = 56 * 1024 * 1024

CHUNK = 64
N_HEADS = 16
HEAD_DIM = 128
N_KV = 4
HEADS_PER_KV = N_HEADS // N_KV
IDX_HEADS = 16
IDX_DIM = 64
TOPK_MAX = 256
ROPE_THETA = 10000.0
POOL_WINDOWS = (2, 4, 8, 16)
LRU_BLOCK = 128
CONV_WIDTH = 4
LRU_C = 8.0
N_EXPERTS = 8
TOP_K = 2
LN_EPS = 1e-5
LN_ROWS = 128
INT_MIN = -2 ** 31
NEG_BIG = -1e30


def _cparams(*sem):
    return pltpu.CompilerParams(dimension_semantics=sem, vmem_limit_bytes=VMEM_LIMIT)


def _sigmoid(x):
    return 1.0 / (1.0 + jnp.exp(-x))


def _gelu_tanh(x):
    c = np.float32(np.sqrt(2.0 / np.pi))
    return 0.5 * x * (1.0 + jnp.tanh(c * (x + np.float32(0.044715) * (x * x * x))))


def _layer_norm(y, g, b):
    mu = jnp.mean(y, axis=-1, keepdims=True)
    d = y - mu
    var = jnp.mean(d * d, axis=-1, keepdims=True)
    return d * lax.rsqrt(var + LN_EPS) * g + b


def _rope_tab_kernel(pos_ref, inv_a_ref, inv_i_ref, sgn_a_ref, m_cos_ref, add_cos_ref,
                     m_sa_ref, m_sb_ref, cos_a_ref, sin_a_ref, cos_i_ref, sin_ia_ref, sin_ib_ref,
                     cos_q_ref, sin_q_ref):
    pos = pos_ref[...]
    ang_a = pos * inv_a_ref[...]
    cos_a = jnp.cos(ang_a)
    sin_a = jnp.sin(ang_a) * sgn_a_ref[...]
    cos_a_ref[...] = cos_a
    sin_a_ref[...] = sin_a
    q_scale = np.float32(HEAD_DIM ** -0.5 * np.log2(np.e))
    cos_q_ref[...] = cos_a * q_scale
    sin_q_ref[...] = sin_a * q_scale
    ang_i = pos * inv_i_ref[...]
    s_i = jnp.sin(ang_i)
    cos_i_ref[...] = jnp.cos(ang_i) * m_cos_ref[...] + add_cos_ref[...]
    sin_ia_ref[...] = s_i * m_sa_ref[...]
    sin_ib_ref[...] = s_i * m_sb_ref[...]


def _rope_tables(positions):
    n = positions.size
    pos = jnp.broadcast_to(positions.reshape(n, 1).astype(F32), (n, LANES))
    lane = np.arange(LANES)
    inv_a = (ROPE_THETA ** (-jnp.arange(0, HEAD_DIM, 2, dtype=F32) / HEAD_DIM))
    inv_i = (ROPE_THETA ** (-jnp.arange(0, IDX_DIM, 2, dtype=F32) / IDX_DIM))
    inv_a_row = jnp.concatenate([inv_a, inv_a])[None, :]
    inv_i_row = jnp.concatenate([inv_i, inv_i, jnp.zeros((LANES - IDX_DIM,), F32)])[None, :]
    sgn_a = jnp.asarray(np.where(lane < HEAD_DIM // 2, -1.0, 1.0), F32)[None, :]
    m_cos = jnp.asarray((lane < IDX_DIM).astype(np.float32))[None, :]
    wi_scale = (IDX_HEADS ** -0.5) * (IDX_DIM ** -0.5)
    add_cos = jnp.asarray(np.where((lane >= IDX_DIM) & (lane < IDX_DIM + IDX_HEADS), wi_scale, 0.0), F32)[None, :]
    m_sa = jnp.asarray(np.where(lane < IDX_DIM // 2, -1.0, 0.0), F32)[None, :]
    m_sb = jnp.asarray(np.where((lane >= IDX_DIM // 2) & (lane < IDX_DIM), 1.0, 0.0), F32)[None, :]
    tm = min(n, 1024)
    row = pl.BlockSpec((1, LANES), lambda i: (0, 0))
    tok = pl.BlockSpec((tm, LANES), lambda i: (i, 0))
    out = jax.ShapeDtypeStruct((n, LANES), F32)
    return pl.pallas_call(
        _rope_tab_kernel,
        out_shape=(out,) * 7,
        grid=(n // tm,),
        in_specs=[tok] + [row] * 7,
        out_specs=(tok,) * 7,
        compiler_params=_cparams("parallel"),
        name="rope_tables",
    )(pos, inv_a_row, inv_i_row, sgn_a, m_cos, add_cos, m_sa, m_sb)


def _proj_heads_kernel(x_ref, w_ref, *rest, shifts, heads):
    tabs, o_ref = rest[:-1], rest[-1]
    acc = jnp.dot(x_ref[...].astype(BF16), w_ref[...], preferred_element_type=F32)
    for h in range(heads):
        xh = acc[:, h * LANES:(h + 1) * LANES]
        if tabs:
            y = xh * tabs[0][...]
            for s, t in zip(shifts, tabs[1:]):
                y = y + pltpu.roll(xh, s, 1) * t[...]
        else:
            y = xh
        o_ref[h] = y.astype(o_ref.dtype)


def _proj_heads(x, w, tabs, shifts, out_dtype, *, tm=1024, heads_per_tile=8):
    n, d = x.shape
    h_total = w.shape[1] // LANES
    hp = min(heads_per_tile, h_total)
    tm = min(tm, n)
    tab_spec = pl.BlockSpec((tm, LANES), lambda i, j: (i, 0))
    return pl.pallas_call(
        functools.partial(_proj_heads_kernel, shifts=shifts, heads=hp),
        out_shape=jax.ShapeDtypeStruct((h_total, n, LANES), out_dtype),
        grid=(n // tm, h_total // hp),
        in_specs=[pl.BlockSpec((tm, d), lambda i, j: (i, 0)),
                  pl.BlockSpec((d, hp * LANES), lambda i, j: (0, j))] + [tab_spec] * len(tabs),
        out_specs=pl.BlockSpec((hp, tm, LANES), lambda i, j: (j, i, 0)),
        compiler_params=_cparams("parallel", "arbitrary"),
        name="proj_heads",
    )(x, w, *tabs)


def _proj_plain_kernel(x_ref, w_ref, o_ref, *, act):
    acc = jnp.dot(x_ref[...].astype(BF16), w_ref[...], preferred_element_type=F32)
    if act == "gelu":
        acc = _gelu_tanh(acc)
    elif act == "sigmoid":
        acc = _sigmoid(acc)
    o_ref[...] = acc.astype(o_ref.dtype)


def _proj_plain(x, w, act, out_dtype, *, tm=1024, tn=1024):
    n, d = x.shape
    c = w.shape[1]
    tm = min(tm, n)
    return pl.pallas_call(
        functools.partial(_proj_plain_kernel, act=act),
        out_shape=jax.ShapeDtypeStruct((n, c), out_dtype),
        grid=(n // tm, c // tn),
        in_specs=[pl.BlockSpec((tm, d), lambda i, j: (i, 0)),
                  pl.BlockSpec((d, tn), lambda i, j: (0, j))],
        out_specs=pl.BlockSpec((tm, tn), lambda i, j: (i, j)),
        compiler_params=_cparams("parallel", "arbitrary"),
        name="proj_plain",
    )(x, w)


def _key_to_float(key):
    return pltpu.bitcast(key ^ ((key >> 31) & 0x7FFFFFFF), F32)


def _attn_kernel(q_ref, k_ref, v_ref, qi_ref, ki_ref, wi_ref, gate_ref, o_ref,
                 wb_s, key_s, keyt_s, x_s, thr_s, xrow_s, *state, tq, topk, seq):
    m_s, acc_s = state[:N_KV], state[N_KV:]
    kb = tq
    n_sub = kb // LANES
    qt = pl.program_id(1)
    nkb = qt + 1
    n_hi = IDX_HEADS
    rows_g = HEADS_PER_KV * tq

    wi = wi_ref[...]
    for h in range(n_hi):
        wb_s[h] = jnp.broadcast_to(wi[:, IDX_DIM + h:IDX_DIM + h + 1], (tq, LANES))
    qi = qi_ref[...].reshape(n_hi * tq, LANES)
    q_row = qt * tq + lax.broadcasted_iota(I32, (tq, LANES), 0)
    limit = (q_row // CHUNK + 1) * CHUNK
    lane_pos = lax.broadcasted_iota(I32, (tq, LANES), 1)
    limit_t = ((qt * tq + lax.broadcasted_iota(I32, (1, tq), 1)) // CHUNK + 1) * CHUNK

    def score_body(j, carry):
        start = pl.multiple_of(j * kb, kb)
        ki_blk = ki_ref[pl.ds(start, kb), :].astype(BF16)
        s = lax.dot_general(qi, ki_blk, (((1,), (1,)), ((), ())), preferred_element_type=F32)
        parts = []
        for c in range(n_sub):
            sc = jnp.zeros((tq, LANES), F32)
            for h in range(n_hi):
                sh = s[h * tq:(h + 1) * tq, c * LANES:(c + 1) * LANES]
                sc = sc + wb_s[h] * jnp.maximum(sh, 0.0)
            sc = jnp.where(sc == 0.0, 0.0, sc)
            parts.append(sc)
            kpos = start + c * LANES + lane_pos
            key_s[j, :, c * LANES:(c + 1) * LANES] = jnp.where(kpos < limit, sc, -jnp.inf)
        sc_t = jnp.concatenate(parts, axis=1).T
        kpos_t = start + lax.broadcasted_iota(I32, (kb, tq), 0)
        keyt_s[j] = jnp.where(kpos_t < limit_t, sc_t, -jnp.inf)
        return carry

    lax.fori_loop(0, nkb, score_body, 0)

    kf = np.float32(topk)
    searched = limit_t > topk

    def count_keys(pred):
        def body(j, cnt):
            hit = jnp.where(pred(keyt_s[j], j), 1.0, 0.0)
            return cnt + jnp.sum(hit.reshape(kb // SUBLANES, SUBLANES, tq), axis=0)
        cnt = lax.fori_loop(0, nkb, body, jnp.zeros((SUBLANES, tq), F32))
        return jnp.sum(cnt, axis=0, keepdims=True)

    thr0 = jnp.where(count_keys(lambda s, j: s >= 0.0) >= kf, 0, INT_MIN).astype(I32)

    def thr_body(it, thr):
        cand = thr | jnp.left_shift(jnp.int32(1), 30 - it)
        cand_f = _key_to_float(cand)
        return jnp.where(count_keys(lambda s, j: s >= cand_f) >= kf, cand, thr)

    thr = _key_to_float(lax.fori_loop(0, 31, thr_body, thr0))
    thr = jnp.where(searched, thr, -jnp.inf)
    n_ge = count_keys(lambda s, j: s >= thr)
    need = kf - count_keys(lambda s, j: s > thr)
    xrow_s[...] = jnp.where(searched, seq, -1).astype(I32)
    tie_flag = jnp.max(jnp.where(searched & (n_ge > kf), 1.0, 0.0), axis=(0, 1), keepdims=True)

    @pl.when(tie_flag[0, 0] > 0.0)
    def _():
        nbits = max(int(seq - 1).bit_length(), 1)
        row_pos = lax.broadcasted_iota(I32, (kb, tq), 0)

        def x_body(it, xcut):
            cand = xcut | jnp.left_shift(jnp.int32(1), nbits - 1 - it)
            cnt = count_keys(lambda s, j: (s == thr) & (j * kb + row_pos < cand))
            return jnp.where(cnt < need, cand, xcut)

        xcut = lax.fori_loop(0, nbits, x_body, jnp.zeros((1, tq), I32))
        xrow_s[...] = jnp.where(searched, xcut, -1)

    def to_col(row_f32):
        return jnp.broadcast_to(row_f32, (LANES, tq)).T

    thr_s[...] = to_col(thr)
    x_s[...] = to_col(xrow_s[...].astype(F32)).astype(I32)
    rb = min(tq, 128)
    n_rc = tq // rb
    lane_rb = lax.broadcasted_iota(I32, (rb, LANES), 1)

    for g in range(N_KV):
        m_s[g][...] = jnp.full((rows_g, LANES), NEG_BIG, F32)
        acc_s[g][...] = jnp.zeros((rows_g, 2 * LANES), F32)
    ones_v = jnp.ones((kb, LANES), BF16)

    def attn_body(j, carry):
        start = pl.multiple_of(j * kb, kb)
        bias = []
        for rc in range(n_rc):
            thr_c = thr_s[rc * rb:(rc + 1) * rb, :]
            xcut_c = x_s[rc * rb:(rc + 1) * rb, :]
            parts = []
            for c in range(n_sub):
                kc = key_s[j, rc * rb:(rc + 1) * rb, c * LANES:(c + 1) * LANES]
                kpos = start + c * LANES + lane_rb
                sel = (kc > thr_c) | ((kc == thr_c) & (kpos <= xcut_c))
                parts.append(jnp.where(sel, 0.0, NEG_BIG))
            bias.append(jnp.concatenate(parts, axis=1))
        for g in range(N_KV):
            qg = q_ref[g * HEADS_PER_KV:(g + 1) * HEADS_PER_KV].reshape(rows_g, LANES)
            kg = k_ref[g, pl.ds(start, kb), :]
            vg = jnp.concatenate([v_ref[g, pl.ds(start, kb), :], ones_v], axis=1)
            lg_all = lax.dot_general(qg, kg, (((1,), (1,)), ((), ())), preferred_element_type=F32)
            m_prev_all = m_s[g][...]
            p_parts, a_parts, m_parts = [], [], []
            for r in range(HEADS_PER_KV):
                for rc in range(n_rc):
                    r0 = r * tq + rc * rb
                    lg = lg_all[r0:r0 + rb] + bias[rc]
                    m_prev = m_prev_all[r0:r0 + rb]
                    m_new = jnp.maximum(m_prev, jnp.max(lg, axis=1, keepdims=True))
                    p = jnp.exp2(lg - jnp.concatenate([m_new] * n_sub, axis=1))
                    p_parts.append(p.astype(BF16))
                    a_parts.append(jnp.exp2(m_prev - m_new))
                    m_parts.append(m_new)
            p_all = jnp.concatenate(p_parts, axis=0)
            alpha_all = jnp.concatenate(a_parts, axis=0)
            pv = jnp.dot(p_all, vg, preferred_element_type=F32)
            acc_s[g][...] = jnp.concatenate([alpha_all, alpha_all], axis=1) * acc_s[g][...] + pv
            m_s[g][...] = jnp.concatenate(m_parts, axis=0)
        return carry

    lax.fori_loop(0, nkb, attn_body, 0)

    for g in range(N_KV):
        acc = acc_s[g][...]
        og = acc[:, :LANES] / acc[:, LANES:]
        for r in range(HEADS_PER_KV):
            col = (g * HEADS_PER_KV + r) * LANES
            y = og[r * tq:(r + 1) * tq] * gate_ref[:, col:col + LANES].astype(F32)
            o_ref[:, col:col + LANES] = y.astype(o_ref.dtype)


def _attention(qh, kh, vh, qih, kiwi, sg, batch, seq, *, tq):
    n = batch * seq
    topk = min(TOPK_MAX, seq // 4)
    tq = min(tq, seq)
    nqt = seq // tq
    attn_w = N_HEADS * HEAD_DIM
    rows_g = HEADS_PER_KV * tq
    return pl.pallas_call(
        functools.partial(_attn_kernel, tq=tq, topk=topk, seq=seq),
        out_shape=jax.ShapeDtypeStruct((n, attn_w), BF16),
        grid=(batch, nqt),
        in_specs=[
            pl.BlockSpec((N_HEADS, tq, LANES), lambda b, t: (0, b * nqt + t, 0)),
            pl.BlockSpec((N_KV, seq, LANES), lambda b, t: (0, b, 0)),
            pl.BlockSpec((N_KV, seq, LANES), lambda b, t: (0, b, 0)),
            pl.BlockSpec((IDX_HEADS, tq, LANES), lambda b, t: (0, b * nqt + t, 0)),
            pl.BlockSpec((None, seq, LANES), lambda b, t: (0, b, 0)),
            pl.BlockSpec((None, tq, LANES), lambda b, t: (0, b * nqt + t, 0)),
            pl.BlockSpec((tq, attn_w), lambda b, t: (b * nqt + t, 0)),
        ],
        out_specs=pl.BlockSpec((tq, attn_w), lambda b, t: (b * nqt + t, 0)),
        scratch_shapes=[
            pltpu.VMEM((IDX_HEADS, tq, LANES), F32),
            pltpu.VMEM((nqt, tq, tq), F32),
            pltpu.VMEM((nqt, tq, tq), F32),
            pltpu.VMEM((tq, LANES), I32),
            pltpu.VMEM((tq, LANES), F32),
            pltpu.VMEM((1, tq), I32),
        ] + [pltpu.VMEM((rows_g, LANES), F32)] * N_KV + [pltpu.VMEM((rows_g, 2 * LANES), F32)] * N_KV,
        compiler_params=_cparams("parallel", "arbitrary"),
        name="sparse_attention",
    )(qh, kh, vh, qih, kiwi, kiwi, sg)


def _pool_kernel(p_ref, gate_ref, w_ref, scale_ref, o_ref, buf_a, buf_b, *, seq):
    pad = 16
    g = pl.program_id(1)
    p = p_ref[...]
    zeros = jnp.zeros((pad, p.shape[1]), F32)
    buf_a[0:pad, :] = zeros
    buf_b[0:pad, :] = zeros
    buf_a[pad:pad + seq, :] = p
    s2 = p + buf_a[pad - 1:pad - 1 + seq, :]
    buf_b[pad:pad + seq, :] = s2
    s4 = s2 + buf_b[pad - 2:pad - 2 + seq, :]
    buf_a[pad:pad + seq, :] = s4
    s8 = s4 + buf_a[pad - 4:pad - 4 + seq, :]
    buf_b[pad:pad + seq, :] = s8
    s16 = s8 + buf_b[pad - 8:pad - 8 + seq, :]
    t1 = (lax.broadcasted_iota(I32, p.shape, 0) + 1).astype(F32)
    win = jnp.where(g == 0, 2.0, jnp.where(g == 1, 4.0, jnp.where(g == 2, 8.0, 16.0))).astype(F32)
    total = jnp.where(g == 0, s2, jnp.where(g == 1, s4, jnp.where(g == 2, s8, s16)))
    mean = total / jnp.minimum(t1, win)
    diff = (mean - p).astype(BF16)
    y = jnp.dot(diff, w_ref[...], preferred_element_type=F32)
    o_ref[...] = (y * scale_ref[...] * gate_ref[...].astype(F32)).astype(o_ref.dtype)


def _pool_mixer(px, sg, pool_w, pool_scale, batch, seq):
    n = batch * seq
    width = pool_scale.shape[-1]
    ng = len(POOL_WINDOWS)
    cg = width // ng
    assert POOL_WINDOWS == (2, 4, 8, 16)
    return pl.pallas_call(
        functools.partial(_pool_kernel, seq=seq),
        out_shape=jax.ShapeDtypeStruct((n, width), BF16),
        grid=(batch, ng),
        in_specs=[
            pl.BlockSpec((seq, cg), lambda b, g: (b, g)),
            pl.BlockSpec((seq, cg), lambda b, g: (b, ng + g)),
            pl.BlockSpec((None, cg, cg), lambda b, g: (g, 0, 0)),
            pl.BlockSpec((1, cg), lambda b, g: (0, g)),
        ],
        out_specs=pl.BlockSpec((seq, cg), lambda b, g: (b, g)),
        scratch_shapes=[pltpu.VMEM((seq + 16, cg), F32), pltpu.VMEM((seq + 16, cg), F32)],
        compiler_params=_cparams("parallel", "arbitrary"),
        name="pool_mixer",
    )(px, sg, pool_w, pool_scale.reshape(1, width))


def _lru_kernel(x_ref, gr_ref, gate_ref, cw_ref, cb_ref, wa_ref, ba_ref, wx_ref, bx_ref, lam_ref,
                o_ref, xs, a_s, b_s, *, seq, ct):
    pad = SUBLANES
    x = x_ref[...]
    xs[0:pad, :] = jnp.zeros((pad, ct), F32)
    xs[pad:pad + seq, :] = x
    xc = jnp.broadcast_to(cb_ref[...], (seq, ct))
    for tap in range(CONV_WIDTH):
        d = CONV_WIDTH - 1 - tap
        xc = xc + xs[pad - d:pad - d + seq, :] * cw_ref[tap:tap + 1, :]
    xcb = xc.astype(BF16)
    nb = ct // LRU_BLOCK
    r_parts, i_parts = [], []
    for blk in range(nb):
        xb = xcb[:, blk * LRU_BLOCK:(blk + 1) * LRU_BLOCK]
        r_parts.append(jnp.dot(xb, wa_ref[blk], preferred_element_type=F32))
        i_parts.append(jnp.dot(xb, wx_ref[blk], preferred_element_type=F32))
    r = _sigmoid(jnp.concatenate(r_parts, axis=1) + ba_ref[...])
    gi = _sigmoid(jnp.concatenate(i_parts, axis=1) + bx_ref[...])
    lam = lam_ref[...]
    softplus_neg_lam = jnp.log(1.0 + jnp.exp(-lam))
    log_a = -LRU_C * r * softplus_neg_lam
    a = jnp.exp(log_a)
    b = jnp.sqrt(1.0 - a * a) * (gi * xc)

    sub = lax.broadcasted_iota(I32, (seq, ct), 0) % SUBLANES
    for d in (1, 2, 4):
        keep = sub >= d
        a_sh = jnp.where(keep, pltpu.roll(a, d, 0), 1.0)
        b_sh = jnp.where(keep, pltpu.roll(b, d, 0), 0.0)
        b = a * b_sh + b
        a = a * a_sh
    a_s[...] = a
    b_s[...] = b

    def body(t, carry):
        r0 = pl.multiple_of(t * SUBLANES, SUBLANES)
        h = a_s[pl.ds(r0, SUBLANES), :] * carry + b_s[pl.ds(r0, SUBLANES), :]
        b_s[pl.ds(r0, SUBLANES), :] = h
        return jnp.broadcast_to(h[SUBLANES - 1:SUBLANES, :], (SUBLANES, ct))

    lax.fori_loop(0, seq // SUBLANES, body, jnp.zeros((SUBLANES, ct), F32), unroll=8)
    h = b_s[...]
    o_ref[...] = (h * gr_ref[...].astype(F32) * gate_ref[...].astype(F32)).astype(o_ref.dtype)


def _lru_mixer(px, gg, sg, conv_w, conv_b, wa, ba, wx, bx, lam, batch, seq, *, ct=256):
    n = batch * seq
    width = conv_b.shape[-1]
    nct = width // ct
    nb = ct // LRU_BLOCK
    row = lambda a: a.reshape(1, width)
    rspec = pl.BlockSpec((1, ct), lambda b, j: (0, j))
    return pl.pallas_call(
        functools.partial(_lru_kernel, seq=seq, ct=ct),
        out_shape=jax.ShapeDtypeStruct((n, width), BF16),
        grid=(batch, nct),
        in_specs=[
            pl.BlockSpec((seq, ct), lambda b, j: (b, nct + j)),
            pl.BlockSpec((seq, ct), lambda b, j: (b, j)),
            pl.BlockSpec((seq, ct), lambda b, j: (b, 2 * nct + j)),
            pl.BlockSpec((CONV_WIDTH, ct), lambda b, j: (0, j)),
            rspec,
            pl.BlockSpec((nb, LRU_BLOCK, LRU_BLOCK), lambda b, j: (j, 0, 0)),
            rspec,
            pl.BlockSpec((nb, LRU_BLOCK, LRU_BLOCK), lambda b, j: (j, 0, 0)),
            rspec,
            rspec,
        ],
        out_specs=pl.BlockSpec((seq, ct), lambda b, j: (b, j)),
        scratch_shapes=[pltpu.VMEM((seq + SUBLANES, ct), F32), pltpu.VMEM((seq, ct), F32),
                        pltpu.VMEM((seq, ct), F32)],
        compiler_params=_cparams("parallel", "arbitrary"),
        name="rglru_mixer",
    )(px, gg, sg, conv_w, row(conv_b), wa, row(ba), wx, row(bx), row(lam))


def _out_proj_kernel(ya_ref, yb_ref, yc_ref, w_ref, o_ref):
    acc = jnp.dot(ya_ref[...], w_ref[0], preferred_element_type=F32)
    acc = acc + jnp.dot(yb_ref[...], w_ref[1], preferred_element_type=F32)
    acc = acc + jnp.dot(yc_ref[...], w_ref[2], preferred_element_type=F32)
    o_ref[...] = acc


def _out_proj(ya, yb, yc, w3, *, tm=1024, tn=512):
    n, width = ya.shape
    d = w3.shape[-1]
    tm = min(tm, n)
    aspec = pl.BlockSpec((tm, width), lambda i, j: (i, 0))
    return pl.pallas_call(
        _out_proj_kernel,
        out_shape=jax.ShapeDtypeStruct((n, d), F32),
        grid=(n // tm, d // tn),
        in_specs=[aspec, aspec, aspec, pl.BlockSpec((3, width, tn), lambda i, j: (0, 0, j))],
        out_specs=pl.BlockSpec((tm, tn), lambda i, j: (i, j)),
        compiler_params=_cparams("parallel", "arbitrary"),
        name="out_proj",
    )(ya, yb, yc, w3)


def _add_ln_kernel(x_ref, m_ref, g_ref, b_ref, o_ref, ob_ref, *, alpha):
    y = alpha * x_ref[...] + m_ref[...]
    out = _layer_norm(y, g_ref[...], b_ref[...])
    o_ref[...] = out
    ob_ref[...] = out.astype(BF16)


def _add_ln(x, m, g, b, alpha, *, tm=256):
    n, d = x.shape
    tm = min(tm, n)
    tok = pl.BlockSpec((tm, d), lambda i: (i, 0))
    row = pl.BlockSpec((1, d), lambda i: (0, 0))
    return pl.pallas_call(
        functools.partial(_add_ln_kernel, alpha=alpha),
        out_shape=(jax.ShapeDtypeStruct((n, d), F32), jax.ShapeDtypeStruct((n, d), BF16)),
        grid=(n // tm,),
        in_specs=[tok, tok, row, row],
        out_specs=(tok, tok),
        compiler_params=_cparams("parallel"),
        name="add_layer_norm",
    )(x, m, g.reshape(1, d), b.reshape(1, d))


def _ffn_dense_kernel(xb_ref, x_ref, wg_ref, wu_ref, wd_ref, g_ref, b_ref, o_ref, ob_ref, *, alpha):
    f = pl.program_id(1)
    xb = xb_ref[...]
    hg = jnp.dot(xb, wg_ref[...], preferred_element_type=F32)
    hu = jnp.dot(xb, wu_ref[...], preferred_element_type=F32)
    h = (hg * _sigmoid(hg) * hu).astype(BF16)
    y = jnp.dot(h, wd_ref[...], preferred_element_type=F32)

    @pl.when(f == 0)
    def _():
        o_ref[...] = y

    @pl.when(f > 0)
    def _():
        o_ref[...] += y

    @pl.when(f == pl.num_programs(1) - 1)
    def _():
        rc = min(LN_ROWS, o_ref.shape[0])

        def body(c, carry):
            rows = pl.ds(pl.multiple_of(c * rc, rc), rc)
            out = _layer_norm(alpha * x_ref[rows, :] + o_ref[rows, :], g_ref[...], b_ref[...])
            o_ref[rows, :] = out
            ob_ref[rows, :] = out.astype(BF16)
            return carry

        lax.fori_loop(0, o_ref.shape[0] // rc, body, 0)


def _ffn_dense(xb, x, wg, wu, wd, g, b, alpha, *, tm=1024, tf=256):
    n, d = x.shape
    ff = wg.shape[1]
    tm = min(tm, n)
    tok = pl.BlockSpec((tm, d), lambda i, f: (i, 0))
    tok_once = pl.BlockSpec((tm, d), lambda i, f: (i, 0), pipeline_mode=pl.Buffered(1))
    row = pl.BlockSpec((1, d), lambda i, f: (0, 0))
    return pl.pallas_call(
        functools.partial(_ffn_dense_kernel, alpha=alpha),
        out_shape=(jax.ShapeDtypeStruct((n, d), F32), jax.ShapeDtypeStruct((n, d), BF16)),
        grid=(n // tm, ff // tf),
        in_specs=[tok_once, tok_once,
                  pl.BlockSpec((d, tf), lambda i, f: (0, f)),
                  pl.BlockSpec((d, tf), lambda i, f: (0, f)),
                  pl.BlockSpec((tf, d), lambda i, f: (f, 0)),
                  row, row],
        out_specs=(tok, tok),
        compiler_params=_cparams("parallel", "arbitrary"),
        name="ffn_dense",
    )(xb, x, wg, wu, wd, g.reshape(1, d), b.reshape(1, d))


def _router_kernel(x_ref, w_ref, ids_ref, wts_ref):
    logits = jnp.dot(x_ref[...], w_ref[...], preferred_element_type=F32, precision=lax.Precision.HIGHEST)
    lane_i = lax.broadcasted_iota(I32, logits.shape, 1)
    lane = lane_i.astype(F32)
    logits = jnp.where(lane_i < N_EXPERTS, logits, -jnp.inf)
    m1 = jnp.max(logits, axis=1, keepdims=True)
    i1 = jnp.min(jnp.where(logits == m1, lane, float(LANES)), axis=1, keepdims=True)
    rest = jnp.where(lane == i1, -jnp.inf, logits)
    m2 = jnp.max(rest, axis=1, keepdims=True)
    i2 = jnp.min(jnp.where(rest == m2, lane, float(LANES)), axis=1, keepdims=True)
    e2 = jnp.exp(m2 - m1)
    w1 = 1.0 / (1.0 + e2)
    w2 = e2 / (1.0 + e2)
    ids_ref[...] = jnp.where(lane_i == 0, i1, jnp.where(lane_i == 1, i2, 0.0)).astype(I32)
    wts_ref[...] = jnp.where(lane_i == 0, w1, jnp.where(lane_i == 1, w2, 0.0))


def _router(x, w_router, *, tm=512):
    n, d = x.shape
    wpad = jnp.zeros((d, LANES), F32).at[:, :N_EXPERTS].set(w_router.astype(F32))
    tm = min(tm, n)
    tok = pl.BlockSpec((tm, LANES), lambda i: (i, 0))
    return pl.pallas_call(
        _router_kernel,
        out_shape=(jax.ShapeDtypeStruct((n, LANES), I32), jax.ShapeDtypeStruct((n, LANES), F32)),
        grid=(n // tm,),
        in_specs=[pl.BlockSpec((tm, d), lambda i: (i, 0)), pl.BlockSpec((d, LANES), lambda i: (0, 0))],
        out_specs=(tok, tok),
        compiler_params=_cparams("parallel"),
        name="moe_router",
    )(x, wpad)


def _gather_rows_kernel(tok_ref, x_hbm, o_ref, buf, sem, *, tm):
    base = pl.program_id(0) * tm

    def issue(r, carry):
        t = tok_ref[base + r]
        pltpu.make_async_copy(x_hbm.at[pl.ds(t, 1), :], buf.at[pl.ds(r, 1), :], sem).start()
        return carry

    lax.fori_loop(0, tm, issue, 0)

    def drain(r, carry):
        pltpu.make_async_copy(x_hbm.at[pl.ds(0, 1), :], buf.at[pl.ds(r, 1), :], sem).wait()
        return carry

    lax.fori_loop(0, tm, drain, 0)
    o_ref[...] = buf[...].astype(o_ref.dtype)


def _gather_rows(x, row_tok, *, tm=256):
    n, d = x.shape
    rows = row_tok.shape[0]
    return pl.pallas_call(
        functools.partial(_gather_rows_kernel, tm=tm),
        out_shape=jax.ShapeDtypeStruct((rows, d), BF16),
        grid_spec=pltpu.PrefetchScalarGridSpec(
            num_scalar_prefetch=1,
            grid=(rows // tm,),
            in_specs=[pl.BlockSpec(memory_space=pl.ANY)],
            out_specs=pl.BlockSpec((tm, d), lambda i, tok: (i, 0)),
            scratch_shapes=[pltpu.VMEM((tm, d), F32), pltpu.SemaphoreType.DMA(())],
        ),
        compiler_params=_cparams("arbitrary"),
        name="moe_gather_rows",
    )(row_tok, x)


def _expert_changed(te_ref, i):
    return (i == 0) | (te_ref[i] != te_ref[jnp.maximum(i - 1, 0)])


def _moe_up_kernel(te_ref, nv_ref, x_ref, wg_ref, wu_ref, o_ref, wg_s, wu_s):
    i = pl.program_id(1)

    @pl.when(_expert_changed(te_ref, i))
    def _():
        wg_s[...] = wg_ref[...].astype(BF16)
        wu_s[...] = wu_ref[...].astype(BF16)

    @pl.when(i < nv_ref[0])
    def _():
        xb = x_ref[...]
        hg = jnp.dot(xb, wg_s[...], preferred_element_type=F32)
        hu = jnp.dot(xb, wu_s[...], preferred_element_type=F32)
        o_ref[...] = (hg * _sigmoid(hg) * hu).astype(o_ref.dtype)

    @pl.when(i >= nv_ref[0])
    def _():
        o_ref[...] = jnp.zeros(o_ref.shape, o_ref.dtype)


def _moe_up(xs, wg, wu, tile_e, n_valid, *, tm, tf=1024):
    rows, d = xs.shape
    ff = wg.shape[-1]
    return pl.pallas_call(
        _moe_up_kernel,
        out_shape=jax.ShapeDtypeStruct((rows, ff), BF16),
        grid_spec=pltpu.PrefetchScalarGridSpec(
            num_scalar_prefetch=2,
            grid=(ff // tf, rows // tm),
            in_specs=[pl.BlockSpec((tm, d), lambda j, i, te, nv: (i, 0)),
                      pl.BlockSpec((None, d, tf), lambda j, i, te, nv: (te[i], 0, j)),
                      pl.BlockSpec((None, d, tf), lambda j, i, te, nv: (te[i], 0, j))],
            out_specs=pl.BlockSpec((tm, tf), lambda j, i, te, nv: (i, j)),
            scratch_shapes=[pltpu.VMEM((d, tf), BF16), pltpu.VMEM((d, tf), BF16)],
        ),
        compiler_params=_cparams("arbitrary", "arbitrary"),
        name="moe_up",
    )(tile_e, n_valid, xs, wg, wu)


def _moe_down_kernel(te_ref, nv_ref, h_ref, wd_ref, o_ref, wd_s):
    i = pl.program_id(1)

    @pl.when(_expert_changed(te_ref, i))
    def _():
        wd_s[...] = wd_ref[...].astype(BF16)

    @pl.when(i < nv_ref[0])
    def _():
        o_ref[...] = jnp.dot(h_ref[...], wd_s[...], preferred_element_type=F32)

    @pl.when(i >= nv_ref[0])
    def _():
        o_ref[...] = jnp.zeros(o_ref.shape, o_ref.dtype)


def _moe_down(h, wd, tile_e, n_valid, *, tm, tn=512):
    rows, ff = h.shape
    d = wd.shape[-1]
    return pl.pallas_call(
        _moe_down_kernel,
        out_shape=jax.ShapeDtypeStruct((rows, d), F32),
        grid_spec=pltpu.PrefetchScalarGridSpec(
            num_scalar_prefetch=2,
            grid=(d // tn, rows // tm),
            in_specs=[pl.BlockSpec((tm, ff), lambda j, i, te, nv: (i, 0)),
                      pl.BlockSpec((None, ff, tn), lambda j, i, te, nv: (te[i], 0, j))],
            out_specs=pl.BlockSpec((tm, tn), lambda j, i, te, nv: (i, j)),
            scratch_shapes=[pltpu.VMEM((ff, tn), BF16)],
        ),
        compiler_params=_cparams("arbitrary", "arbitrary"),
        name="moe_down",
    )(tile_e, n_valid, h, wd)


def _combine_ln_kernel(pos_ref, y_hbm, x_ref, wts_ref, g_ref, b_ref, o_ref, buf, sem, *, tm, alpha):
    base = pl.program_id(0) * tm

    def issue(r, carry):
        for k in range(TOP_K):
            row = pos_ref[(base + r) * TOP_K + k]
            pltpu.make_async_copy(y_hbm.at[pl.ds(row, 1), :], buf.at[k, pl.ds(r, 1), :], sem).start()
        return carry

    lax.fori_loop(0, tm, issue, 0)

    def drain(r, carry):
        for k in range(TOP_K):
            pltpu.make_async_copy(y_hbm.at[pl.ds(0, 1), :], buf.at[k, pl.ds(r, 1), :], sem).wait()
        return carry

    lax.fori_loop(0, tm, drain, 0)
    wts = wts_ref[...]
    y = buf[0] * wts[:, 0:1] + buf[1] * wts[:, 1:2]
    o_ref[...] = _layer_norm(alpha * x_ref[...] + y, g_ref[...], b_ref[...])


def _combine_ln(yrows, pos, x, wts, g, b, alpha, *, tm=128):
    n, d = x.shape
    tm = min(tm, n)
    return pl.pallas_call(
        functools.partial(_combine_ln_kernel, tm=tm, alpha=alpha),
        out_shape=jax.ShapeDtypeStruct((n, d), F32),
        grid_spec=pltpu.PrefetchScalarGridSpec(
            num_scalar_prefetch=1,
            grid=(n // tm,),
            in_specs=[pl.BlockSpec(memory_space=pl.ANY),
                      pl.BlockSpec((tm, d), lambda i, pos: (i, 0)),
                      pl.BlockSpec((tm, LANES), lambda i, pos: (i, 0)),
                      pl.BlockSpec((1, d), lambda i, pos: (0, 0)),
                      pl.BlockSpec((1, d), lambda i, pos: (0, 0))],
            out_specs=pl.BlockSpec((tm, d), lambda i, pos: (i, 0)),
            scratch_shapes=[pltpu.VMEM((TOP_K, tm, d), F32), pltpu.SemaphoreType.DMA(())],
        ),
        compiler_params=_cparams("arbitrary"),
        name="moe_combine_ln",
    )(pos, yrows, x, wts, g.reshape(1, d), b.reshape(1, d))


def _moe_block(x, w_router, wg, wu, wd, g, b, alpha, *, tm=256):
    n, d = x.shape
    ids, wts = _router(x, w_router)
    e_flat = ids[:, :TOP_K].reshape(-1)
    n_assign = n * TOP_K
    onehot = (e_flat[:, None] == jnp.arange(N_EXPERTS, dtype=I32)[None, :]).astype(I32)
    rank = jnp.sum((jnp.cumsum(onehot, axis=0) - onehot) * onehot, axis=1)
    counts = jnp.sum(onehot, axis=0)
    padded = (counts + tm - 1) // tm * tm
    end_padded = jnp.cumsum(padded)
    start_padded = end_padded - padded
    dest = (start_padded[e_flat] + rank).astype(I32)
    rows = n_assign + N_EXPERTS * tm
    n_tiles = rows // tm
    flat_tok = jnp.arange(n_assign, dtype=I32) // TOP_K
    row_tok = jnp.zeros((rows,), I32).at[dest].set(flat_tok)
    tile_start = jnp.arange(n_tiles, dtype=I32) * tm
    tile_e = jnp.minimum(jnp.sum((tile_start[:, None] >= end_padded[None, :]).astype(I32), axis=1),
                         N_EXPERTS - 1).astype(I32)
    n_valid = (end_padded[-1:] // tm).astype(I32)
    xs = _gather_rows(x, row_tok, tm=tm)
    h = _moe_up(xs, wg, wu, tile_e, n_valid, tm=tm)
    yrows = _moe_down(h, wd, tile_e, n_valid, tm=tm)
    return _combine_ln(yrows, dest, x, wts, g, b, alpha)


def _split_w_in(w_in, d_model):
    attn_w = N_HEADS * HEAD_DIM
    kv_w = N_KV * HEAD_DIM
    idx_w = IDX_HEADS * IDX_DIM
    o = np.cumsum([0, attn_w, kv_w, kv_w, idx_w, IDX_DIM, IDX_HEADS, d_model, d_model, d_model])
    wq = w_in[:, o[0]:o[1]].astype(BF16)
    wk = w_in[:, o[1]:o[2]].astype(BF16)
    wv = w_in[:, o[2]:o[3]].astype(BF16)
    wqi = w_in[:, o[3]:o[4]].reshape(-1, IDX_HEADS, IDX_DIM)
    wqi = jnp.pad(wqi, ((0, 0), (0, 0), (0, LANES - IDX_DIM))).reshape(-1, IDX_HEADS * LANES).astype(BF16)
    wkiwi = jnp.pad(w_in[:, o[4]:o[6]], ((0, 0), (0, LANES - IDX_DIM - IDX_HEADS))).astype(BF16)
    wpx = w_in[:, o[6]:o[8]].astype(BF16)
    wgr = w_in[:, o[8]:o[9]].astype(BF16)
    wgates = w_in[:, o[9]:].astype(BF16)
    return wq, wk, wv, wqi, wkiwi, wpx, wgr, wgates


def _mixer(x_mm, tabs, w_in, w_out, pool_w, pool_scale, conv_w, conv_b, wa, ba, wx, bx, lam,
           batch, seq, d_model, tq):
    cos_a, sin_a, cos_i, sin_ia, sin_ib, cos_q, sin_q = tabs
    wq, wk, wv, wqi, wkiwi, wpx, wgr, wgates = _split_w_in(w_in, d_model)
    rope_a = ((cos_a, sin_a), (HEAD_DIM // 2,))
    rope_i = ((cos_i, sin_ia, sin_ib), (LANES - IDX_DIM // 2, IDX_DIM // 2))
    qh = _proj_heads(x_mm, wq, (cos_q, sin_q), rope_a[1], BF16)
    kh = _proj_heads(x_mm, wk, *rope_a, BF16)
    vh = _proj_heads(x_mm, wv, (), (), BF16)
    qih = _proj_heads(x_mm, wqi, *rope_i, BF16)
    kiwi = _proj_heads(x_mm, wkiwi, *rope_i, F32)
    px = _proj_plain(x_mm, wpx, None, F32)
    gg = _proj_plain(x_mm, wgr, "gelu", BF16)
    sg = _proj_plain(x_mm, wgates, "sigmoid", BF16)
    ya = _attention(qh, kh, vh, qih, kiwi, sg, batch, seq, tq=tq)
    yb = _pool_mixer(px, sg, pool_w.astype(BF16), pool_scale, batch, seq)
    yc = _lru_mixer(px, gg, sg, conv_w, conv_b, wa.astype(BF16), ba, wx.astype(BF16), bx, lam, batch, seq)
    mix_w = w_out.shape[0]
    return _out_proj(ya, yb, yc, w_out.astype(BF16).reshape(3, mix_w // 3, d_model))


def _pad_ff(w, axis, mult):
    ff = w.shape[axis]
    padn = (-ff) % mult
    if padn == 0:
        return w
    widths = [(0, 0)] * w.ndim
    widths[axis] = (0, padn)
    return jnp.pad(w, widths)


def kernel(x, positions, mix_w_in, mix_w_out, pool_w, pool_scale, conv_w, conv_b, lru_wa, lru_ba, lru_wx, lru_bx, lru_lam, ln_mix_g, ln_mix_b, ln_ffn_g, ln_ffn_b, dense_w_gate, dense_w_up, dense_w_down, moe_router, moe_w_gate, moe_w_up, moe_w_down):
    batch, seq, d_model = x.shape
    depth = mix_w_in.shape[0]
    alpha = np.float32((2 * depth) ** 0.25)
    n = batch * seq
    tabs = _rope_tables(positions)
    xf = x.reshape(n, d_model)
    x_mm = xf
    for layer in range(depth):
        m = _mixer(x_mm, tabs, mix_w_in[layer], mix_w_out[layer], pool_w[layer], pool_scale[layer],
                   conv_w[layer], conv_b[layer], lru_wa[layer], lru_ba[layer], lru_wx[layer],
                   lru_bx[layer], lru_lam[layer], batch, seq, d_model, tq=256)
        xf, xb = _add_ln(xf, m, ln_mix_g[layer], ln_mix_b[layer], alpha)
        j = layer // 2
        if layer % 2 == 0:
            wg = _pad_ff(dense_w_gate[j], 1, 512).astype(BF16)
            wu = _pad_ff(dense_w_up[j], 1, 512).astype(BF16)
            wd = _pad_ff(dense_w_down[j], 0, 512).astype(BF16)
            xf, x_mm = _ffn_dense(xb, xf, wg, wu, wd, ln_ffn_g[layer], ln_ffn_b[layer], alpha)
        else:
            xf = _moe_block(xf, moe_router[j], moe_w_gate[j], moe_w_up[j], moe_w_down[j],
                            ln_ffn_g[layer], ln_ffn_b[layer], alpha)
            x_mm = xf
    return xf.reshape(batch, seq, d_model)
```

```python
import functools

import jax
import jax.numpy as jnp
import numpy as np
from jax import lax
from jax.experimental import pallas as pl
from jax.experimental.pallas import tpu as pltpu

F32 = jnp.float32
BF16 = jnp.bfloat16
I32 = jnp.int32

LANES = 128
SUBLANES = 8
VMEM_LIMIT = 56 * 1024 * 1024

CHUNK = 64
N_HEADS = 16
HEAD_DIM = 128
N_KV = 4
HEADS_PER_KV = N_HEADS // N_KV
IDX_HEADS = 16
IDX_DIM = 64
TOPK_MAX = 256
ROPE_THETA = 10000.0
POOL_WINDOWS = (2, 4, 8, 16)
LRU_BLOCK = 128
CONV_WIDTH = 4
LRU_C = 8.0
N_EXPERTS = 8
TOP_K = 2
LN_EPS = 1e-5
LN_ROWS = 128
INT_MIN = -2 ** 31
NEG_BIG = -1e30


def _cparams(*sem):
    return pltpu.CompilerParams(dimension_semantics=sem, vmem_limit_bytes=VMEM_LIMIT)


def _sigmoid(x):
    return 0.5 * (1.0 + jnp.tanh(0.5 * x))


def _gelu_tanh(x):
    c = np.float32(np.sqrt(2.0 / np.pi))
    return 0.5 * x * (1.0 + jnp.tanh(c * (x + np.float32(0.044715) * (x * x * x))))


def _layer_norm(y, g, b):
    mu = jnp.mean(y, axis=-1, keepdims=True)
    d = y - mu
    var = jnp.mean(d * d, axis=-1, keepdims=True)
    return d * lax.rsqrt(var + LN_EPS) * g + b


def _rope_tab_kernel(pos_ref, inv_a_ref, inv_i_ref, sgn_a_ref, m_cos_ref, add_cos_ref,
                     m_sa_ref, m_sb_ref, cos_a_ref, sin_a_ref, cos_i_ref, sin_ia_ref, sin_ib_ref,
                     cos_q_ref, sin_q_ref):
    pos = pos_ref[...]
    ang_a = pos * inv_a_ref[...]
    cos_a = jnp.cos(ang_a)
    sin_a = jnp.sin(ang_a) * sgn_a_ref[...]
    cos_a_ref[...] = cos_a
    sin_a_ref[...] = sin_a
    q_scale = np.float32(HEAD_DIM ** -0.5 * np.log2(np.e))
    cos_q_ref[...] = cos_a * q_scale
    sin_q_ref[...] = sin_a * q_scale
    ang_i = pos * inv_i_ref[...]
    s_i = jnp.sin(ang_i)
    cos_i_ref[...] = jnp.cos(ang_i) * m_cos_ref[...] + add_cos_ref[...]
    sin_ia_ref[...] = s_i * m_sa_ref[...]
    sin_ib_ref[...] = s_i * m_sb_ref[...]


def _rope_tables(positions):
    n = positions.size
    pos = jnp.broadcast_to(positions.reshape(n, 1).astype(F32), (n, LANES))
    lane = np.arange(LANES)
    inv_a = (ROPE_THETA ** (-jnp.arange(0, HEAD_DIM, 2, dtype=F32) / HEAD_DIM))
    inv_i = (ROPE_THETA ** (-jnp.arange(0, IDX_DIM, 2, dtype=F32) / IDX_DIM))
    inv_a_row = jnp.concatenate([inv_a, inv_a])[None, :]
    inv_i_row = jnp.concatenate([inv_i, inv_i, jnp.zeros((LANES - IDX_DIM,), F32)])[None, :]
    sgn_a = jnp.asarray(np.where(lane < HEAD_DIM // 2, -1.0, 1.0), F32)[None, :]
    m_cos = jnp.asarray((lane < IDX_DIM).astype(np.float32))[None, :]
    wi_scale = (IDX_HEADS ** -0.5) * (IDX_DIM ** -0.5)
    add_cos = jnp.asarray(np.where((lane >= IDX_DIM) & (lane < IDX_DIM + IDX_HEADS), wi_scale, 0.0), F32)[None, :]
    m_sa = jnp.asarray(np.where(lane < IDX_DIM // 2, -1.0, 0.0), F32)[None, :]
    m_sb = jnp.asarray(np.where((lane >= IDX_DIM // 2) & (lane < IDX_DIM), 1.0, 0.0), F32)[None, :]
    tm = min(n, 1024)
    row = pl.BlockSpec((1, LANES), lambda i: (0, 0))
    tok = pl.BlockSpec((tm, LANES), lambda i: (i, 0))
    out = jax.ShapeDtypeStruct((n, LANES), F32)
    return pl.pallas_call(
        _rope_tab_kernel,
        out_shape=(out,) * 7,
        grid=(n // tm,),
        in_specs=[tok] + [row] * 7,
        out_specs=(tok,) * 7,
        compiler_params=_cparams("parallel"),
        name="rope_tables",
    )(pos, inv_a_row, inv_i_row, sgn_a, m_cos, add_cos, m_sa, m_sb)


def _proj_heads_kernel(x_ref, w_ref, *rest, shifts, heads):
    tabs, o_ref = rest[:-1], rest[-1]
    acc = jnp.dot(x_ref[...].astype(BF16), w_ref[...], preferred_element_type=F32)
    for h in range(heads):
        xh = acc[:, h * LANES:(h + 1) * LANES]
        if tabs:
            y = xh * tabs[0][...]
            for s, t in zip(shifts, tabs[1:]):
                y = y + pltpu.roll(xh, s, 1) * t[...]
        else:
            y = xh
        o_ref[h] = y.astype(o_ref.dtype)


def _proj_heads(x, w, tabs, shifts, out_dtype, *, tm=1024, heads_per_tile=8):
    n, d = x.shape
    h_total = w.shape[1] // LANES
    hp = min(heads_per_tile, h_total)
    tm = min(tm, n)
    tab_spec = pl.BlockSpec((tm, LANES), lambda i, j: (i, 0))
    return pl.pallas_call(
        functools.partial(_proj_heads_kernel, shifts=shifts, heads=hp),
        out_shape=jax.ShapeDtypeStruct((h_total, n, LANES), out_dtype),
        grid=(n // tm, h_total // hp),
        in_specs=[pl.BlockSpec((tm, d), lambda i, j: (i, 0)),
                  pl.BlockSpec((d, hp * LANES), lambda i, j: (0, j))] + [tab_spec] * len(tabs),
        out_specs=pl.BlockSpec((hp, tm, LANES), lambda i, j: (j, i, 0)),
        compiler_params=_cparams("parallel", "arbitrary"),
        name="proj_heads",
    )(x, w, *tabs)


def _proj_plain_kernel(x_ref, w_ref, o_ref, *, act):
    acc = jnp.dot(x_ref[...].astype(BF16), w_ref[...], preferred_element_type=F32)
    if act == "gelu":
        acc = _gelu_tanh(acc)
    elif act == "sigmoid":
        acc = _sigmoid(acc)
    o_ref[...] = acc.astype(o_ref.dtype)


def _proj_plain(x, w, act, out_dtype, *, tm=1024, tn=1024):
    n, d = x.shape
    c = w.shape[1]
    tm = min(tm, n)
    return pl.pallas_call(
        functools.partial(_proj_plain_kernel, act=act),
        out_shape=jax.ShapeDtypeStruct((n, c), out_dtype),
        grid=(n // tm, c // tn),
        in_specs=[pl.BlockSpec((tm, d), lambda i, j: (i, 0)),
                  pl.BlockSpec((d, tn), lambda i, j: (0, j))],
        out_specs=pl.BlockSpec((tm, tn), lambda i, j: (i, j)),
        compiler_params=_cparams("parallel", "arbitrary"),
        name="proj_plain",
    )(x, w)


def _key_to_float(key):
    return pltpu.bitcast(key ^ ((key >> 31) & 0x7FFFFFFF), F32)


def _attn_kernel(q_ref, k_ref, v_ref, qi_ref, ki_ref, wi_ref, gate_ref, o_ref,
                 wb_s, key_s, keyt_s, x_s, thr_s, xrow_s, *state, tq, topk, seq):
    m_s, acc_s = state[:N_KV], state[N_KV:]
    kb = tq
    n_sub = kb // LANES
    qt = pl.program_id(1)
    nkb = qt + 1
    n_hi = IDX_HEADS
    rows_g = HEADS_PER_KV * tq

    wi = wi_ref[...]
    for h in range(n_hi):
        wb_s[h] = jnp.broadcast_to(wi[:, IDX_DIM + h:IDX_DIM + h + 1], (tq, LANES))
    qi = qi_ref[...].reshape(n_hi * tq, LANES)
    q_row = qt * tq + lax.broadcasted_iota(I32, (tq, LANES), 0)
    limit = (q_row // CHUNK + 1) * CHUNK
    lane_pos = lax.broadcasted_iota(I32, (tq, LANES), 1)
    limit_t = ((qt * tq + lax.broadcasted_iota(I32, (1, tq), 1)) // CHUNK + 1) * CHUNK

    def score_body(j, carry):
        start = pl.multiple_of(j * kb, kb)
        ki_blk = ki_ref[pl.ds(start, kb), :].astype(BF16)
        s = lax.dot_general(qi, ki_blk, (((1,), (1,)), ((), ())), preferred_element_type=F32)
        parts = []
        for c in range(n_sub):
            sc = jnp.zeros((tq, LANES), F32)
            for h in range(n_hi):
                sh = s[h * tq:(h + 1) * tq, c * LANES:(c + 1) * LANES]
                sc = sc + wb_s[h] * jnp.maximum(sh, 0.0)
            sc = jnp.where(sc == 0.0, 0.0, sc)
            parts.append(sc)
            kpos = start + c * LANES + lane_pos
            key_s[j, :, c * LANES:(c + 1) * LANES] = jnp.where(kpos < limit, sc, -jnp.inf)
        sc_t = jnp.concatenate(parts, axis=1).T
        kpos_t = start + lax.broadcasted_iota(I32, (kb, tq), 0)
        keyt_s[j] = jnp.where(kpos_t < limit_t, sc_t, -jnp.inf)
        return carry

    lax.fori_loop(0, nkb, score_body, 0)

    kf = np.float32(topk)
    searched = limit_t > topk

    def count_keys(pred):
        def body(j, cnt):
            hit = jnp.where(pred(keyt_s[j], j), 1.0, 0.0)
            return cnt + jnp.sum(hit.reshape(kb // SUBLANES, SUBLANES, tq), axis=0)
        cnt = lax.fori_loop(0, nkb, body, jnp.zeros((SUBLANES, tq), F32))
        return jnp.sum(cnt, axis=0, keepdims=True)

    thr0 = jnp.where(count_keys(lambda s, j: s >= 0.0) >= kf, 0, INT_MIN).astype(I32)

    def thr_body(it, thr):
        cand = thr | jnp.left_shift(jnp.int32(1), 30 - it)
        cand_f = _key_to_float(cand)
        return jnp.where(count_keys(lambda s, j: s >= cand_f) >= kf, cand, thr)

    thr = _key_to_float(lax.fori_loop(0, 31, thr_body, thr0))
    thr = jnp.where(searched, thr, -jnp.inf)
    n_ge = count_keys(lambda s, j: s >= thr)
    need = kf - count_keys(lambda s, j: s > thr)
    xrow_s[...] = jnp.where(searched, seq, -1).astype(I32)
    tie_flag = jnp.max(jnp.where(searched & (n_ge > kf), 1.0, 0.0), axis=(0, 1), keepdims=True)

    @pl.when(tie_flag[0, 0] > 0.0)
    def _():
        nbits = max(int(seq - 1).bit_length(), 1)
        row_pos = lax.broadcasted_iota(I32, (kb, tq), 0)

        def x_body(it, xcut):
            cand = xcut | jnp.left_shift(jnp.int32(1), nbits - 1 - it)
            cnt = count_keys(lambda s, j: (s == thr) & (j * kb + row_pos < cand))
            return jnp.where(cnt < need, cand, xcut)

        xcut = lax.fori_loop(0, nbits, x_body, jnp.zeros((1, tq), I32))
        xrow_s[...] = jnp.where(searched, xcut, -1)

    def to_col(row_f32):
        return jnp.broadcast_to(row_f32, (LANES, tq)).T

    thr_s[...] = to_col(thr)
    x_s[...] = to_col(xrow_s[...].astype(F32)).astype(I32)
    rb = min(tq, 128)
    n_rc = tq // rb
    lane_rb = lax.broadcasted_iota(I32, (rb, LANES), 1)

    for g in range(N_KV):
        m_s[g][...] = jnp.full((rows_g, LANES), NEG_BIG, F32)
        acc_s[g][...] = jnp.zeros((rows_g, 2 * LANES), F32)
    ones_v = jnp.ones((kb, LANES), BF16)

    def attn_body(j, carry):
        start = pl.multiple_of(j * kb, kb)
        bias = []
        for rc in range(n_rc):
            thr_c = thr_s[rc * rb:(rc + 1) * rb, :]
            xcut_c = x_s[rc * rb:(rc + 1) * rb, :]
            parts = []
            for c in range(n_sub):
                kc = key_s[j, rc * rb:(rc + 1) * rb, c * LANES:(c + 1) * LANES]
                kpos = start + c * LANES + lane_rb
                sel = (kc > thr_c) | ((kc == thr_c) & (kpos <= xcut_c))
                parts.append(jnp.where(sel, 0.0, NEG_BIG))
            bias.append(jnp.concatenate(parts, axis=1))
        for g in range(N_KV):
            qg = q_ref[g * HEADS_PER_KV:(g + 1) * HEADS_PER_KV].reshape(rows_g, LANES)
            kg = k_ref[g, pl.ds(start, kb), :]
            vg = jnp.concatenate([v_ref[g, pl.ds(start, kb), :], ones_v], axis=1)
            lg_all = lax.dot_general(qg, kg, (((1,), (1,)), ((), ())), preferred_element_type=F32)
            m_prev_all = m_s[g][...]
            p_parts, a_parts, m_parts = [], [], []
            for r in range(HEADS_PER_KV):
                for rc in range(n_rc):
                    r0 = r * tq + rc * rb
                    lg = lg_all[r0:r0 + rb] + bias[rc]
                    m_prev = m_prev_all[r0:r0 + rb]
                    m_new = jnp.maximum(m_prev, jnp.max(lg, axis=1, keepdims=True))
                    p = jnp.exp2(lg - jnp.concatenate([m_new] * n_sub, axis=1))
                    p_parts.append(p.astype(BF16))
                    a_parts.append(jnp.exp2(m_prev - m_new))
                    m_parts.append(m_new)
            p_all = jnp.concatenate(p_parts, axis=0)
            alpha_all = jnp.concatenate(a_parts, axis=0)
            pv = jnp.dot(p_all, vg, preferred_element_type=F32)
            acc_s[g][...] = jnp.concatenate([alpha_all, alpha_all], axis=1) * acc_s[g][...] + pv
            m_s[g][...] = jnp.concatenate(m_parts, axis=0)
        return carry

    lax.fori_loop(0, nkb, attn_body, 0)

    for g in range(N_KV):
        acc = acc_s[g][...]
        og = acc[:, :LANES] / acc[:, LANES:]
        for r in range(HEADS_PER_KV):
            col = (g * HEADS_PER_KV + r) * LANES
            y = og[r * tq:(r + 1) * tq] * gate_ref[:, col:col + LANES].astype(F32)
            o_ref[:, col:col + LANES] = y.astype(o_ref.dtype)


def _attention(qh, kh, vh, qih, kiwi, sg, batch, seq, *, tq):
    n = batch * seq
    topk = min(TOPK_MAX, seq // 4)
    tq = min(tq, seq)
    nqt = seq // tq
    attn_w = N_HEADS * HEAD_DIM
    rows_g = HEADS_PER_KV * tq
    return pl.pallas_call(
        functools.partial(_attn_kernel, tq=tq, topk=topk, seq=seq),
        out_shape=jax.ShapeDtypeStruct((n, attn_w), BF16),
        grid=(batch, nqt),
        in_specs=[
            pl.BlockSpec((N_HEADS, tq, LANES), lambda b, t: (0, b * nqt + t, 0)),
            pl.BlockSpec((N_KV, seq, LANES), lambda b, t: (0, b, 0)),
            pl.BlockSpec((N_KV, seq, LANES), lambda b, t: (0, b, 0)),
            pl.BlockSpec((IDX_HEADS, tq, LANES), lambda b, t: (0, b * nqt + t, 0)),
            pl.BlockSpec((None, seq, LANES), lambda b, t: (0, b, 0)),
            pl.BlockSpec((None, tq, LANES), lambda b, t: (0, b * nqt + t, 0)),
            pl.BlockSpec((tq, attn_w), lambda b, t: (b * nqt + t, 0)),
        ],
        out_specs=pl.BlockSpec((tq, attn_w), lambda b, t: (b * nqt + t, 0)),
        scratch_shapes=[
            pltpu.VMEM((IDX_HEADS, tq, LANES), F32),
            pltpu.VMEM((nqt, tq, tq), F32),
            pltpu.VMEM((nqt, tq, tq), F32),
            pltpu.VMEM((tq, LANES), I32),
            pltpu.VMEM((tq, LANES), F32),
            pltpu.VMEM((1, tq), I32),
        ] + [pltpu.VMEM((rows_g, LANES), F32)] * N_KV + [pltpu.VMEM((rows_g, 2 * LANES), F32)] * N_KV,
        compiler_params=_cparams("parallel", "arbitrary"),
        name="sparse_attention",
    )(qh, kh, vh, qih, kiwi, kiwi, sg)


def _pool_kernel(p_ref, gate_ref, w_ref, scale_ref, o_ref, buf_a, buf_b, *, seq):
    pad = 16
    g = pl.program_id(1)
    p = p_ref[...]
    zeros = jnp.zeros((pad, p.shape[1]), F32)
    buf_a[0:pad, :] = zeros
    buf_b[0:pad, :] = zeros
    buf_a[pad:pad + seq, :] = p
    s2 = p + buf_a[pad - 1:pad - 1 + seq, :]
    buf_b[pad:pad + seq, :] = s2
    s4 = s2 + buf_b[pad - 2:pad - 2 + seq, :]
    buf_a[pad:pad + seq, :] = s4
    s8 = s4 + buf_a[pad - 4:pad - 4 + seq, :]
    buf_b[pad:pad + seq, :] = s8
    s16 = s8 + buf_b[pad - 8:pad - 8 + seq, :]
    t1 = (lax.broadcasted_iota(I32, p.shape, 0) + 1).astype(F32)
    win = jnp.where(g == 0, 2.0, jnp.where(g == 1, 4.0, jnp.where(g == 2, 8.0, 16.0))).astype(F32)
    total = jnp.where(g == 0, s2, jnp.where(g == 1, s4, jnp.where(g == 2, s8, s16)))
    mean = total / jnp.minimum(t1, win)
    diff = (mean - p).astype(BF16)
    y = jnp.dot(diff, w_ref[...], preferred_element_type=F32)
    o_ref[...] = (y * scale_ref[...] * gate_ref[...].astype(F32)).astype(o_ref.dtype)


def _pool_mixer(px, sg, pool_w, pool_scale, batch, seq):
    n = batch * seq
    width = pool_scale.shape[-1]
    ng = len(POOL_WINDOWS)
    cg = width // ng
    assert POOL_WINDOWS == (2, 4, 8, 16)
    return pl.pallas_call(
        functools.partial(_pool_kernel, seq=seq),
        out_shape=jax.ShapeDtypeStruct((n, width), BF16),
        grid=(batch, ng),
        in_specs=[
            pl.BlockSpec((seq, cg), lambda b, g: (b, g)),
            pl.BlockSpec((seq, cg), lambda b, g: (b, ng + g)),
            pl.BlockSpec((None, cg, cg), lambda b, g: (g, 0, 0)),
            pl.BlockSpec((1, cg), lambda b, g: (0, g)),
        ],
        out_specs=pl.BlockSpec((seq, cg), lambda b, g: (b, g)),
        scratch_shapes=[pltpu.VMEM((seq + 16, cg), F32), pltpu.VMEM((seq + 16, cg), F32)],
        compiler_params=_cparams("parallel", "arbitrary"),
        name="pool_mixer",
    )(px, sg, pool_w, pool_scale.reshape(1, width))


def _lru_kernel(x_ref, gr_ref, gate_ref, cw_ref, cb_ref, wa_ref, ba_ref, wx_ref, bx_ref, lam_ref,
                o_ref, a_s, b_s, *, seq, ct):
    x = x_ref[...]
    row = lax.broadcasted_iota(I32, (seq, ct), 0)
    xc = jnp.broadcast_to(cb_ref[...], (seq, ct))
    for tap in range(CONV_WIDTH):
        d = CONV_WIDTH - 1 - tap
        x_d = x if d == 0 else jnp.where(row >= d, pltpu.roll(x, d, 0), 0.0)
        xc = xc + x_d * cw_ref[tap:tap + 1, :]
    xcb = xc.astype(BF16)
    nb = ct // LRU_BLOCK
    r_parts, i_parts = [], []
    for blk in range(nb):
        xb = xcb[:, blk * LRU_BLOCK:(blk + 1) * LRU_BLOCK]
        r_parts.append(jnp.dot(xb, wa_ref[blk], preferred_element_type=F32))
        i_parts.append(jnp.dot(xb, wx_ref[blk], preferred_element_type=F32))
    r = _sigmoid(jnp.concatenate(r_parts, axis=1) + ba_ref[...])
    gi = _sigmoid(jnp.concatenate(i_parts, axis=1) + bx_ref[...])
    lam = lam_ref[...]
    softplus_neg_lam = jnp.log(1.0 + jnp.exp(-lam))
    log_a = -LRU_C * r * softplus_neg_lam
    a = jnp.exp(log_a)
    one_m_a2 = 1.0 - a * a
    root = jnp.where(one_m_a2 > 0.0, one_m_a2 * lax.rsqrt(one_m_a2), 0.0)
    b = root * (gi * xc)

    sub = row % SUBLANES
    for d in (1, 2, 4):
        keep = sub >= d
        a_sh = jnp.where(keep, pltpu.roll(a, d, 0), 1.0)
        b_sh = jnp.where(keep, pltpu.roll(b, d, 0), 0.0)
        b = a * b_sh + b
        a = a * a_sh
    a_s[...] = a
    b_s[...] = b

    def body(t, carry):
        r0 = pl.multiple_of(t * SUBLANES, SUBLANES)
        h = a_s[pl.ds(r0, SUBLANES), :] * carry + b_s[pl.ds(r0, SUBLANES), :]
        b_s[pl.ds(r0, SUBLANES), :] = h
        return jnp.broadcast_to(h[SUBLANES - 1:SUBLANES, :], (SUBLANES, ct))

    lax.fori_loop(0, seq // SUBLANES, body, jnp.zeros((SUBLANES, ct), F32), unroll=8)
    h = b_s[...]
    o_ref[...] = (h * gr_ref[...].astype(F32) * gate_ref[...].astype(F32)).astype(o_ref.dtype)


def _lru_mixer(px, gg, sg, conv_w, conv_b, wa, ba, wx, bx, lam, batch, seq, *, ct=256):
    n = batch * seq
    width = conv_b.shape[-1]
    nct = width // ct
    nb = ct // LRU_BLOCK
    row = lambda a: a.reshape(1, width)
    rspec = pl.BlockSpec((1, ct), lambda b, j: (0, j))
    return pl.pallas_call(
        functools.partial(_lru_kernel, seq=seq, ct=ct),
        out_shape=jax.ShapeDtypeStruct((n, width), BF16),
        grid=(batch, nct),
        in_specs=[
            pl.BlockSpec((seq, ct), lambda b, j: (b, nct + j)),
            pl.BlockSpec((seq, ct), lambda b, j: (b, j)),
            pl.BlockSpec((seq, ct), lambda b, j: (b, 2 * nct + j)),
            pl.BlockSpec((CONV_WIDTH, ct), lambda b, j: (0, j)),
            rspec,
            pl.BlockSpec((nb, LRU_BLOCK, LRU_BLOCK), lambda b, j: (j, 0, 0)),
            rspec,
            pl.BlockSpec((nb, LRU_BLOCK, LRU_BLOCK), lambda b, j: (j, 0, 0)),
            rspec,
            rspec,
        ],
        out_specs=pl.BlockSpec((seq, ct), lambda b, j: (b, j)),
        scratch_shapes=[pltpu.VMEM((seq, ct), F32),
                        pltpu.VMEM((seq, ct), F32)],
        compiler_params=_cparams("parallel", "arbitrary"),
        name="rglru_mixer",
    )(px, gg, sg, conv_w, row(conv_b), wa, row(ba), wx, row(bx), row(lam))


def _out_proj_kernel(ya_ref, yb_ref, yc_ref, w_ref, o_ref):
    acc = jnp.dot(ya_ref[...], w_ref[0], preferred_element_type=F32)
    acc = acc + jnp.dot(yb_ref[...], w_ref[1], preferred_element_type=F32)
    acc = acc + jnp.dot(yc_ref[...], w_ref[2], preferred_element_type=F32)
    o_ref[...] = acc


def _out_proj(ya, yb, yc, w3, *, tm=1024, tn=512):
    n, width = ya.shape
    d = w3.shape[-1]
    tm = min(tm, n)
    aspec = pl.BlockSpec((tm, width), lambda i, j: (i, 0))
    return pl.pallas_call(
        _out_proj_kernel,
        out_shape=jax.ShapeDtypeStruct((n, d), F32),
        grid=(n // tm, d // tn),
        in_specs=[aspec, aspec, aspec, pl.BlockSpec((3, width, tn), lambda i, j: (0, 0, j))],
        out_specs=pl.BlockSpec((tm, tn), lambda i, j: (i, j)),
        compiler_params=_cparams("parallel", "arbitrary"),
        name="out_proj",
    )(ya, yb, yc, w3)


def _add_ln_kernel(x_ref, m_ref, g_ref, b_ref, o_ref, ob_ref, *, alpha):
    y = alpha * x_ref[...] + m_ref[...]
    out = _layer_norm(y, g_ref[...], b_ref[...])
    o_ref[...] = out
    ob_ref[...] = out.astype(BF16)


def _add_ln(x, m, g, b, alpha, *, tm=256):
    n, d = x.shape
    tm = min(tm, n)
    tok = pl.BlockSpec((tm, d), lambda i: (i, 0))
    row = pl.BlockSpec((1, d), lambda i: (0, 0))
    return pl.pallas_call(
        functools.partial(_add_ln_kernel, alpha=alpha),
        out_shape=(jax.ShapeDtypeStruct((n, d), F32), jax.ShapeDtypeStruct((n, d), BF16)),
        grid=(n // tm,),
        in_specs=[tok, tok, row, row],
        out_specs=(tok, tok),
        compiler_params=_cparams("parallel"),
        name="add_layer_norm",
    )(x, m, g.reshape(1, d), b.reshape(1, d))


def _ffn_up_kernel(x_ref, wg_ref, wu_ref, o_ref):
    xb = x_ref[...]
    hg = jnp.dot(xb, wg_ref[...], preferred_element_type=F32)
    hu = jnp.dot(xb, wu_ref[...], preferred_element_type=F32)
    o_ref[...] = (hg * _sigmoid(hg) * hu).astype(o_ref.dtype)


def _ffn_down_ln_kernel(h_ref, w_ref, x_ref, g_ref, b_ref, o_ref, ob_ref, *, alpha):
    k = pl.program_id(1)
    y = jnp.dot(h_ref[...], w_ref[...], preferred_element_type=F32)

    @pl.when(k == 0)
    def _():
        o_ref[...] = y

    @pl.when(k > 0)
    def _():
        o_ref[...] += y

    @pl.when(k == pl.num_programs(1) - 1)
    def _():
        rc = min(LN_ROWS, o_ref.shape[0])

        def body(c, carry):
            rows = pl.ds(pl.multiple_of(c * rc, rc), rc)
            out = _layer_norm(alpha * x_ref[rows, :] + o_ref[rows, :], g_ref[...], b_ref[...])
            o_ref[rows, :] = out
            ob_ref[rows, :] = out.astype(BF16)
            return carry

        lax.fori_loop(0, o_ref.shape[0] // rc, body, 0)


def _ffn_dense(xb, x, wg, wu, wd, g, b, alpha, *, tm_up=1024, tn_up=512, tm_down=512, k_steps=4):
    n, d = x.shape
    ff = wg.shape[1]
    tm_up, tm_down = min(tm_up, n), min(tm_down, n)
    h = pl.pallas_call(
        _ffn_up_kernel,
        out_shape=jax.ShapeDtypeStruct((n, ff), BF16),
        grid=(n // tm_up, ff // tn_up),
        in_specs=[pl.BlockSpec((tm_up, d), lambda i, j: (i, 0)),
                  pl.BlockSpec((d, tn_up), lambda i, j: (0, j)),
                  pl.BlockSpec((d, tn_up), lambda i, j: (0, j))],
        out_specs=pl.BlockSpec((tm_up, tn_up), lambda i, j: (i, j)),
        compiler_params=_cparams("parallel", "arbitrary"),
        name="ffn_up",
    )(xb, wg, wu)
    tk = ff // k_steps
    tok = pl.BlockSpec((tm_down, d), lambda i, k: (i, 0))
    row = pl.BlockSpec((1, d), lambda i, k: (0, 0))
    return pl.pallas_call(
        functools.partial(_ffn_down_ln_kernel, alpha=alpha),
        out_shape=(jax.ShapeDtypeStruct((n, d), F32), jax.ShapeDtypeStruct((n, d), BF16)),
        grid=(n // tm_down, k_steps),
        in_specs=[pl.BlockSpec((tm_down, tk), lambda i, k: (i, k)),
                  pl.BlockSpec((tk, d), lambda i, k: (k, 0)),
                  tok, row, row],
        out_specs=(tok, tok),
        compiler_params=_cparams("parallel", "arbitrary"),
        name="ffn_down_ln",
    )(h, wd, x, g.reshape(1, d), b.reshape(1, d))


def _router_kernel(x_ref, w_ref, ids_ref, wts_ref):
    logits = jnp.dot(x_ref[...], w_ref[...], preferred_element_type=F32, precision=lax.Precision.HIGHEST)
    lane_i = lax.broadcasted_iota(I32, logits.shape, 1)
    lane = lane_i.astype(F32)
    logits = jnp.where(lane_i < N_EXPERTS, logits, -jnp.inf)
    m1 = jnp.max(logits, axis=1, keepdims=True)
    i1 = jnp.min(jnp.where(logits == m1, lane, float(LANES)), axis=1, keepdims=True)
    rest = jnp.where(lane == i1, -jnp.inf, logits)
    m2 = jnp.max(rest, axis=1, keepdims=True)
    i2 = jnp.min(jnp.where(rest == m2, lane, float(LANES)), axis=1, keepdims=True)
    e2 = jnp.exp(m2 - m1)
    w1 = 1.0 / (1.0 + e2)
    w2 = e2 / (1.0 + e2)
    ids_ref[...] = jnp.where(lane_i == 0, i1, jnp.where(lane_i == 1, i2, 0.0)).astype(I32)
    wts_ref[...] = jnp.where(lane_i == 0, w1, jnp.where(lane_i == 1, w2, 0.0))


def _router(x, w_router, *, tm=512):
    n, d = x.shape
    wpad = jnp.zeros((d, LANES), F32).at[:, :N_EXPERTS].set(w_router.astype(F32))
    tm = min(tm, n)
    tok = pl.BlockSpec((tm, LANES), lambda i: (i, 0))
    return pl.pallas_call(
        _router_kernel,
        out_shape=(jax.ShapeDtypeStruct((n, LANES), I32), jax.ShapeDtypeStruct((n, LANES), F32)),
        grid=(n // tm,),
        in_specs=[pl.BlockSpec((tm, d), lambda i: (i, 0)), pl.BlockSpec((d, LANES), lambda i: (0, 0))],
        out_specs=(tok, tok),
        compiler_params=_cparams("parallel"),
        name="moe_router",
    )(x, wpad)


def _gather_rows_kernel(tok_ref, x_hbm, o_ref, buf, sem, *, tm):
    i = pl.program_id(0)

    def issue(tile, slot):
        base = tile * tm

        def body(r, carry):
            t = tok_ref[base + r]
            pltpu.make_async_copy(x_hbm.at[pl.ds(t, 1), :], buf.at[slot, pl.ds(r, 1), :], sem.at[slot]).start()
            return carry

        lax.fori_loop(0, tm, body, 0, unroll=8)

    @pl.when(i == 0)
    def _():
        issue(0, 0)

    @pl.when(i + 1 < pl.num_programs(0))
    def _():
        issue(i + 1, (i + 1) % 2)

    slot = i % 2
    pltpu.make_async_copy(x_hbm.at[pl.ds(0, tm), :], buf.at[slot], sem.at[slot]).wait()
    o_ref[...] = buf[slot].astype(o_ref.dtype)


def _gather_rows(x, row_tok, *, tm=256):
    n, d = x.shape
    rows = row_tok.shape[0]
    return pl.pallas_call(
        functools.partial(_gather_rows_kernel, tm=tm),
        out_shape=jax.ShapeDtypeStruct((rows, d), BF16),
        grid_spec=pltpu.PrefetchScalarGridSpec(
            num_scalar_prefetch=1,
            grid=(rows // tm,),
            in_specs=[pl.BlockSpec(memory_space=pl.ANY)],
            out_specs=pl.BlockSpec((tm, d), lambda i, tok: (i, 0)),
            scratch_shapes=[pltpu.VMEM((2, tm, d), F32), pltpu.SemaphoreType.DMA((2,))],
        ),
        compiler_params=_cparams("arbitrary"),
        name="moe_gather_rows",
    )(row_tok, x)


def _expert_changed(te_ref, i):
    return (i == 0) | (te_ref[i] != te_ref[jnp.maximum(i - 1, 0)])


def _moe_up_kernel(te_ref, nv_ref, x_ref, wg_ref, wu_ref, o_ref, wg_s, wu_s):
    i = pl.program_id(1)

    @pl.when(_expert_changed(te_ref, i))
    def _():
        wg_s[...] = wg_ref[...].astype(BF16)
        wu_s[...] = wu_ref[...].astype(BF16)

    @pl.when(i < nv_ref[0])
    def _():
        xb = x_ref[...]
        hg = jnp.dot(xb, wg_s[...], preferred_element_type=F32)
        hu = jnp.dot(xb, wu_s[...], preferred_element_type=F32)
        o_ref[...] = (hg * _sigmoid(hg) * hu).astype(o_ref.dtype)

    @pl.when(i >= nv_ref[0])
    def _():
        o_ref[...] = jnp.zeros(o_ref.shape, o_ref.dtype)


def _moe_up(xs, wg, wu, tile_e, n_valid, *, tm, tf=1024):
    rows, d = xs.shape
    ff = wg.shape[-1]
    return pl.pallas_call(
        _moe_up_kernel,
        out_shape=jax.ShapeDtypeStruct((rows, ff), BF16),
        grid_spec=pltpu.PrefetchScalarGridSpec(
            num_scalar_prefetch=2,
            grid=(ff // tf, rows // tm),
            in_specs=[pl.BlockSpec((tm, d), lambda j, i, te, nv: (i, 0)),
                      pl.BlockSpec((None, d, tf), lambda j, i, te, nv: (te[i], 0, j)),
                      pl.BlockSpec((None, d, tf), lambda j, i, te, nv: (te[i], 0, j))],
            out_specs=pl.BlockSpec((tm, tf), lambda j, i, te, nv: (i, j)),
            scratch_shapes=[pltpu.VMEM((d, tf), BF16), pltpu.VMEM((d, tf), BF16)],
        ),
        compiler_params=_cparams("arbitrary", "arbitrary"),
        name="moe_up",
    )(tile_e, n_valid, xs, wg, wu)


def _moe_down_kernel(te_ref, nv_ref, h_ref, wd_ref, o_ref, wd_s):
    i = pl.program_id(1)

    @pl.when(_expert_changed(te_ref, i))
    def _():
        wd_s[...] = wd_ref[...].astype(BF16)

    @pl.when(i < nv_ref[0])
    def _():
        o_ref[...] = jnp.dot(h_ref[...], wd_s[...], preferred_element_type=F32)

    @pl.when(i >= nv_ref[0])
    def _():
        o_ref[...] = jnp.zeros(o_ref.shape, o_ref.dtype)


def _moe_down(h, wd, tile_e, n_valid, *, tm, tn=512):
    rows, ff = h.shape
    d = wd.shape[-1]
    return pl.pallas_call(
        _moe_down_kernel,
        out_shape=jax.ShapeDtypeStruct((rows, d), F32),
        grid_spec=pltpu.PrefetchScalarGridSpec(
            num_scalar_prefetch=2,
            grid=(d // tn, rows // tm),
            in_specs=[pl.BlockSpec((tm, ff), lambda j, i, te, nv: (i, 0)),
                      pl.BlockSpec((None, ff, tn), lambda j, i, te, nv: (te[i], 0, j))],
            out_specs=pl.BlockSpec((tm, tn), lambda j, i, te, nv: (i, j)),
            scratch_shapes=[pltpu.VMEM((ff, tn), BF16)],
        ),
        compiler_params=_cparams("arbitrary", "arbitrary"),
        name="moe_down",
    )(tile_e, n_valid, h, wd)


def _combine_ln_kernel(pos_ref, y_hbm, x_ref, wts_ref, g_ref, b_ref, o_ref, buf, sem, *, tm, alpha):
    i = pl.program_id(0)

    def issue(tile, slot):
        base = tile * tm

        def body(r, carry):
            for k in range(TOP_K):
                row = pos_ref[(base + r) * TOP_K + k]
                pltpu.make_async_copy(y_hbm.at[pl.ds(row, 1), :], buf.at[slot, k, pl.ds(r, 1), :],
                                      sem.at[slot]).start()
            return carry

        lax.fori_loop(0, tm, body, 0, unroll=8)

    @pl.when(i == 0)
    def _():
        issue(0, 0)

    @pl.when(i + 1 < pl.num_programs(0))
    def _():
        issue(i + 1, (i + 1) % 2)

    slot = i % 2
    for k in range(TOP_K):
        pltpu.make_async_copy(y_hbm.at[pl.ds(0, tm), :], buf.at[slot, k], sem.at[slot]).wait()
    wts = wts_ref[...]
    y = buf[slot, 0] * wts[:, 0:1] + buf[slot, 1] * wts[:, 1:2]
    o_ref[...] = _layer_norm(alpha * x_ref[...] + y, g_ref[...], b_ref[...])


def _combine_ln(yrows, pos, x, wts, g, b, alpha, *, tm=128):
    n, d = x.shape
    tm = min(tm, n)
    return pl.pallas_call(
        functools.partial(_combine_ln_kernel, tm=tm, alpha=alpha),
        out_shape=jax.ShapeDtypeStruct((n, d), F32),
        grid_spec=pltpu.PrefetchScalarGridSpec(
            num_scalar_prefetch=1,
            grid=(n // tm,),
            in_specs=[pl.BlockSpec(memory_space=pl.ANY),
                      pl.BlockSpec((tm, d), lambda i, pos: (i, 0)),
                      pl.BlockSpec((tm, LANES), lambda i, pos: (i, 0)),
                      pl.BlockSpec((1, d), lambda i, pos: (0, 0)),
                      pl.BlockSpec((1, d), lambda i, pos: (0, 0))],
            out_specs=pl.BlockSpec((tm, d), lambda i, pos: (i, 0)),
            scratch_shapes=[pltpu.VMEM((2, TOP_K, tm, d), F32), pltpu.SemaphoreType.DMA((2,))],
        ),
        compiler_params=_cparams("arbitrary"),
        name="moe_combine_ln",
    )(pos, yrows, x, wts, g.reshape(1, d), b.reshape(1, d))


def _moe_block(x, w_router, wg, wu, wd, g, b, alpha, *, tm=256):
    n, d = x.shape
    ids, wts = _router(x, w_router)
    e_flat = ids[:, :TOP_K].reshape(-1)
    n_assign = n * TOP_K
    onehot = (e_flat[:, None] == jnp.arange(N_EXPERTS, dtype=I32)[None, :]).astype(I32)
    rank = jnp.sum((jnp.cumsum(onehot, axis=0) - onehot) * onehot, axis=1)
    counts = jnp.sum(onehot, axis=0)
    padded = (counts + tm - 1) // tm * tm
    end_padded = jnp.cumsum(padded)
    start_padded = end_padded - padded
    dest = (start_padded[e_flat] + rank).astype(I32)
    rows = n_assign + N_EXPERTS * tm
    n_tiles = rows // tm
    flat_tok = jnp.arange(n_assign, dtype=I32) // TOP_K
    row_tok = jnp.zeros((rows,), I32).at[dest].set(flat_tok)
    tile_start = jnp.arange(n_tiles, dtype=I32) * tm
    tile_e = jnp.minimum(jnp.sum((tile_start[:, None] >= end_padded[None, :]).astype(I32), axis=1),
                         N_EXPERTS - 1).astype(I32)
    n_valid = (end_padded[-1:] // tm).astype(I32)
    xs = _gather_rows(x, row_tok, tm=tm)
    h = _moe_up(xs, wg, wu, tile_e, n_valid, tm=tm)
    yrows = _moe_down(h, wd, tile_e, n_valid, tm=tm)
    return _combine_ln(yrows, dest, x, wts, g, b, alpha)


def _split_w_in(w_in, d_model):
    attn_w = N_HEADS * HEAD_DIM
    kv_w = N_KV * HEAD_DIM
    idx_w = IDX_HEADS * IDX_DIM
    o = np.cumsum([0, attn_w, kv_w, kv_w, idx_w, IDX_DIM, IDX_HEADS, d_model, d_model, d_model])
    wq = w_in[:, o[0]:o[1]].astype(BF16)
    wk = w_in[:, o[1]:o[2]].astype(BF16)
    wv = w_in[:, o[2]:o[3]].astype(BF16)
    wqi = w_in[:, o[3]:o[4]].reshape(-1, IDX_HEADS, IDX_DIM)
    wqi = jnp.pad(wqi, ((0, 0), (0, 0), (0, LANES - IDX_DIM))).reshape(-1, IDX_HEADS * LANES).astype(BF16)
    wkiwi = jnp.pad(w_in[:, o[4]:o[6]], ((0, 0), (0, LANES - IDX_DIM - IDX_HEADS))).astype(BF16)
    wpx = w_in[:, o[6]:o[8]].astype(BF16)
    wgr = w_in[:, o[8]:o[9]].astype(BF16)
    wgates = w_in[:, o[9]:].astype(BF16)
    return wq, wk, wv, wqi, wkiwi, wpx, wgr, wgates


def _mixer(x_mm, tabs, w_in, w_out, pool_w, pool_scale, conv_w, conv_b, wa, ba, wx, bx, lam,
           batch, seq, d_model, tq):
    cos_a, sin_a, cos_i, sin_ia, sin_ib, cos_q, sin_q = tabs
    wq, wk, wv, wqi, wkiwi, wpx, wgr, wgates = _split_w_in(w_in, d_model)
    rope_a = ((cos_a, sin_a), (HEAD_DIM // 2,))
    rope_i = ((cos_i, sin_ia, sin_ib), (LANES - IDX_DIM // 2, IDX_DIM // 2))
    qh = _proj_heads(x_mm, wq, (cos_q, sin_q), rope_a[1], BF16)
    kh = _proj_heads(x_mm, wk, *rope_a, BF16)
    vh = _proj_heads(x_mm, wv, (), (), BF16)
    qih = _proj_heads(x_mm, wqi, *rope_i, BF16)
    kiwi = _proj_heads(x_mm, wkiwi, *rope_i, F32)
    px = _proj_plain(x_mm, wpx, None, F32)
    gg = _proj_plain(x_mm, wgr, "gelu", BF16)
    sg = _proj_plain(x_mm, wgates, "sigmoid", BF16)
    ya = _attention(qh, kh, vh, qih, kiwi, sg, batch, seq, tq=tq)
    yb = _pool_mixer(px, sg, pool_w.astype(BF16), pool_scale, batch, seq)
    yc = _lru_mixer(px, gg, sg, conv_w, conv_b, wa.astype(BF16), ba, wx.astype(BF16), bx, lam, batch, seq)
    mix_w = w_out.shape[0]
    return _out_proj(ya, yb, yc, w_out.astype(BF16).reshape(3, mix_w // 3, d_model))


def _pad_ff(w, axis, mult):
    ff = w.shape[axis]
    padn = (-ff) % mult
    if padn == 0:
        return w
    widths = [(0, 0)] * w.ndim
    widths[axis] = (0, padn)
    return jnp.pad(w, widths)


def kernel(x, positions, mix_w_in, mix_w_out, pool_w, pool_scale, conv_w, conv_b, lru_wa, lru_ba, lru_wx, lru_bx, lru_lam, ln_mix_g, ln_mix_b, ln_ffn_g, ln_ffn_b, dense_w_gate, dense_w_up, dense_w_down, moe_router, moe_w_gate, moe_w_up, moe_w_down):
    batch, seq, d_model = x.shape
    depth = mix_w_in.shape[0]
    alpha = np.float32((2 * depth) ** 0.25)
    n = batch * seq
    tabs = _rope_tables(positions)
    xf = x.reshape(n, d_model)
    x_mm = xf
    for layer in range(depth):
        m = _mixer(x_mm, tabs, mix_w_in[layer], mix_w_out[layer], pool_w[layer], pool_scale[layer],
                   conv_w[layer], conv_b[layer], lru_wa[layer], lru_ba[layer], lru_wx[layer],
                   lru_bx[layer], lru_lam[layer], batch, seq, d_model, tq=256)
        xf, xb = _add_ln(xf, m, ln_mix_g[layer], ln_mix_b[layer], alpha)
        j = layer // 2
        if layer % 2 == 0:
            wg = _pad_ff(dense_w_gate[j], 1, 512).astype(BF16)
            wu = _pad_ff(dense_w_up[j], 1, 512).astype(BF16)
            wd = _pad_ff(dense_w_down[j], 0, 512).astype(BF16)
            xf, x_mm = _ffn_dense(xb, xf, wg, wu, wd, ln_ffn_g[layer], ln_ffn_b[layer], alpha)
        else:
            xf = _moe_block(xf, moe_router[j], moe_w_gate[j], moe_w_up[j], moe_w_down[j],
                            ln_ffn_g[layer], ln_ffn_b[layer], alpha)
            x_mm = xf
    return xf.reshape(batch, seq, d_model)
```

```python
import functools

import jax
import jax.numpy as jnp
import numpy as np
from jax import lax
from jax.experimental import pallas as pl
from jax.experimental.pallas import tpu as pltpu

F32 = jnp.float32
BF16 = jnp.bfloat16
I32 = jnp.int32

LANES = 128
SUBLANES = 8
VMEM_LIMIT = 56 * 1024 * 1024

CHUNK = 64
N_HEADS = 16
HEAD_DIM = 128
N_KV = 4
HEADS_PER_KV = N_HEADS // N_KV
IDX_HEADS = 16
IDX_DIM = 64
TOPK_MAX = 256
ROPE_THETA = 10000.0
POOL_WINDOWS = (2, 4, 8, 16)
LRU_BLOCK = 128
CONV_WIDTH = 4
LRU_C = 8.0
N_EXPERTS = 8
TOP_K = 2
LN_EPS = 1e-5
LN_ROWS = 128
INT_MIN = -2 ** 31
NEG_BIG = -1e30


def _cparams(*sem):
    return pltpu.CompilerParams(dimension_semantics=sem, vmem_limit_bytes=VMEM_LIMIT)


def _sigmoid(x):
    return 0.5 * (1.0 + jnp.tanh(0.5 * x))


def _gelu_tanh(x):
    c = np.float32(np.sqrt(2.0 / np.pi))
    return 0.5 * x * (1.0 + jnp.tanh(c * (x + np.float32(0.044715) * (x * x * x))))


def _layer_norm(y, g, b):
    mu = jnp.mean(y, axis=-1, keepdims=True)
    d = y - mu
    var = jnp.mean(d * d, axis=-1, keepdims=True)
    return d * lax.rsqrt(var + LN_EPS) * g + b


def _rope_tab_kernel(pos_ref, inv_a_ref, inv_i_ref, sgn_a_ref, m_cos_ref, add_cos_ref,
                     m_sa_ref, m_sb_ref, cos_a_ref, sin_a_ref, cos_i_ref, sin_ia_ref, sin_ib_ref,
                     cos_q_ref, sin_q_ref, cos_iq_ref):
    pos = pos_ref[...]
    ang_a = pos * inv_a_ref[...]
    cos_a = jnp.cos(ang_a)
    sin_a = jnp.sin(ang_a) * sgn_a_ref[...]
    cos_a_ref[...] = cos_a
    sin_a_ref[...] = sin_a
    q_scale = np.float32(HEAD_DIM ** -0.5 * np.log2(np.e))
    cos_q_ref[...] = cos_a * q_scale
    sin_q_ref[...] = sin_a * q_scale
    ang_i = pos * inv_i_ref[...]
    s_i = jnp.sin(ang_i)
    cos_iq = jnp.cos(ang_i) * m_cos_ref[...]
    cos_iq_ref[...] = cos_iq
    cos_i_ref[...] = cos_iq + add_cos_ref[...]
    sin_ia_ref[...] = s_i * m_sa_ref[...]
    sin_ib_ref[...] = s_i * m_sb_ref[...]


def _rope_tables(positions):
    n = positions.size
    pos = jnp.broadcast_to(positions.reshape(n, 1).astype(F32), (n, LANES))
    lane = np.arange(LANES)
    inv_a = (ROPE_THETA ** (-jnp.arange(0, HEAD_DIM, 2, dtype=F32) / HEAD_DIM))
    inv_i = (ROPE_THETA ** (-jnp.arange(0, IDX_DIM, 2, dtype=F32) / IDX_DIM))
    inv_a_row = jnp.concatenate([inv_a, inv_a])[None, :]
    inv_i_row = jnp.concatenate([inv_i, inv_i, jnp.zeros((LANES - IDX_DIM,), F32)])[None, :]
    sgn_a = jnp.asarray(np.where(lane < HEAD_DIM // 2, -1.0, 1.0), F32)[None, :]
    m_cos = jnp.asarray((lane < IDX_DIM).astype(np.float32))[None, :]
    wi_scale = (IDX_HEADS ** -0.5) * (IDX_DIM ** -0.5)
    add_cos = jnp.asarray(np.where((lane >= IDX_DIM) & (lane < IDX_DIM + IDX_HEADS), wi_scale, 0.0), F32)[None, :]
    m_sa = jnp.asarray(np.where(lane < IDX_DIM // 2, -1.0, 0.0), F32)[None, :]
    m_sb = jnp.asarray(np.where((lane >= IDX_DIM // 2) & (lane < IDX_DIM), 1.0, 0.0), F32)[None, :]
    tm = min(n, 1024)
    row = pl.BlockSpec((1, LANES), lambda i: (0, 0))
    tok = pl.BlockSpec((tm, LANES), lambda i: (i, 0))
    out = jax.ShapeDtypeStruct((n, LANES), F32)
    return pl.pallas_call(
        _rope_tab_kernel,
        out_shape=(out,) * 8,
        grid=(n // tm,),
        in_specs=[tok] + [row] * 7,
        out_specs=(tok,) * 8,
        compiler_params=_cparams("parallel"),
        name="rope_tables",
    )(pos, inv_a_row, inv_i_row, sgn_a, m_cos, add_cos, m_sa, m_sb)


def _proj_heads_kernel(x_ref, w_ref, *rest, shifts, heads, head_width):
    tabs, o_ref = rest[:-1], rest[-1]
    acc = jnp.dot(x_ref[...].astype(BF16), w_ref[...], preferred_element_type=F32)
    per_slab = LANES // head_width
    for h in range(heads):
        xh = acc[:, (h // per_slab) * LANES:(h // per_slab + 1) * LANES]
        if h % per_slab:
            xh = pltpu.roll(xh, LANES - (h % per_slab) * head_width, 1)
        if tabs:
            y = xh * tabs[0][...]
            for s, t in zip(shifts, tabs[1:]):
                y = y + pltpu.roll(xh, s, 1) * t[...]
        else:
            y = xh
        o_ref[h] = y.astype(o_ref.dtype)


def _proj_heads(x, w, cols, tabs, shifts, out_dtype, *, head_width=LANES, tm=1024, tile_cols=1024):
    n, d = x.shape
    off, width = cols
    tn = min(tile_cols, width)
    hp = tn // head_width
    assert width % tn == 0 and off % tn == 0 and (head_width == LANES or tabs)
    tm = min(tm, n)
    tab_spec = pl.BlockSpec((tm, LANES), lambda i, j: (i, 0))
    return pl.pallas_call(
        functools.partial(_proj_heads_kernel, shifts=shifts, heads=hp, head_width=head_width),
        out_shape=jax.ShapeDtypeStruct((width // head_width, n, LANES), out_dtype),
        grid=(n // tm, width // tn),
        in_specs=[pl.BlockSpec((tm, d), lambda i, j: (i, 0)),
                  pl.BlockSpec((d, tn), lambda i, j: (0, off // tn + j))] + [tab_spec] * len(tabs),
        out_specs=pl.BlockSpec((hp, tm, LANES), lambda i, j: (j, i, 0)),
        compiler_params=_cparams("parallel", "arbitrary"),
        name="proj_heads",
    )(x, w, *tabs)


def _proj_plain_kernel(x_ref, w_ref, o_ref, *, act):
    acc = jnp.dot(x_ref[...].astype(BF16), w_ref[...], preferred_element_type=F32)
    if act == "gelu":
        acc = _gelu_tanh(acc)
    elif act == "sigmoid":
        acc = _sigmoid(acc)
    o_ref[...] = acc.astype(o_ref.dtype)


def _proj_plain(x, w, cols, act, out_dtype, *, tm=1024, tn=1024):
    n, d = x.shape
    off, width = cols
    assert width % tn == 0 and off % tn == 0
    tm = min(tm, n)
    return pl.pallas_call(
        functools.partial(_proj_plain_kernel, act=act),
        out_shape=jax.ShapeDtypeStruct((n, width), out_dtype),
        grid=(n // tm, width // tn),
        in_specs=[pl.BlockSpec((tm, d), lambda i, j: (i, 0)),
                  pl.BlockSpec((d, tn), lambda i, j: (0, off // tn + j))],
        out_specs=pl.BlockSpec((tm, tn), lambda i, j: (i, j)),
        compiler_params=_cparams("parallel", "arbitrary"),
        name="proj_plain",
    )(x, w)


def _key_to_float(key):
    return pltpu.bitcast(key ^ ((key >> 31) & 0x7FFFFFFF), F32)


def _attn_kernel(q_ref, k_ref, v_ref, qi_ref, ki_ref, wi_ref, gate_ref, o_ref,
                 wb_s, key_s, keyt_s, x_s, thr_s, xrow_s, *state, tq, topk, seq):
    m_s, acc_s = state[:N_KV], state[N_KV:]
    kb = tq
    n_sub = kb // LANES
    qt = pl.program_id(1)
    nkb = qt + 1
    n_hi = IDX_HEADS
    rows_g = HEADS_PER_KV * tq

    wi = wi_ref[...]
    for h in range(n_hi):
        wb_s[h] = jnp.broadcast_to(wi[:, IDX_DIM + h:IDX_DIM + h + 1], (tq, LANES))
    qi = qi_ref[...].reshape(n_hi * tq, LANES)
    q_row = qt * tq + lax.broadcasted_iota(I32, (tq, LANES), 0)
    limit = (q_row // CHUNK + 1) * CHUNK
    lane_pos = lax.broadcasted_iota(I32, (tq, LANES), 1)
    limit_t = ((qt * tq + lax.broadcasted_iota(I32, (1, tq), 1)) // CHUNK + 1) * CHUNK

    def score_body(j, carry):
        start = pl.multiple_of(j * kb, kb)
        ki_blk = ki_ref[pl.ds(start, kb), :].astype(BF16)
        s = lax.dot_general(qi, ki_blk, (((1,), (1,)), ((), ())), preferred_element_type=F32)
        parts = []
        for c in range(n_sub):
            sc = jnp.zeros((tq, LANES), F32)
            for h in range(n_hi):
                sh = s[h * tq:(h + 1) * tq, c * LANES:(c + 1) * LANES]
                sc = sc + wb_s[h] * jnp.maximum(sh, 0.0)
            sc = jnp.where(sc == 0.0, 0.0, sc)
            parts.append(sc)
            kpos = start + c * LANES + lane_pos
            key_s[j, :, c * LANES:(c + 1) * LANES] = jnp.where(kpos < limit, sc, -jnp.inf)
        sc_t = jnp.concatenate(parts, axis=1).T
        kpos_t = start + lax.broadcasted_iota(I32, (kb, tq), 0)
        keyt_s[j] = jnp.where(kpos_t < limit_t, sc_t, -jnp.inf)
        return carry

    lax.fori_loop(0, nkb, score_body, 0)

    kf = np.float32(topk)
    searched = limit_t > topk

    def count_keys(pred):
        def body(j, cnt):
            hit = jnp.where(pred(keyt_s[j], j), 1.0, 0.0)
            return cnt + jnp.sum(hit.reshape(kb // SUBLANES, SUBLANES, tq), axis=0)
        cnt = lax.fori_loop(0, nkb, body, jnp.zeros((SUBLANES, tq), F32))
        return jnp.sum(cnt, axis=0, keepdims=True)

    thr0 = jnp.where(count_keys(lambda s, j: s >= 0.0) >= kf, 0, INT_MIN).astype(I32)

    def thr_body(it, thr):
        cand = thr | jnp.left_shift(jnp.int32(1), 30 - it)
        cand_f = _key_to_float(cand)
        return jnp.where(count_keys(lambda s, j: s >= cand_f) >= kf, cand, thr)

    thr = _key_to_float(lax.fori_loop(0, 31, thr_body, thr0))
    thr = jnp.where(searched, thr, -jnp.inf)
    n_ge = count_keys(lambda s, j: s >= thr)
    need = kf - count_keys(lambda s, j: s > thr)
    xrow_s[...] = jnp.where(searched, seq, -1).astype(I32)
    tie_flag = jnp.max(jnp.where(searched & (n_ge > kf), 1.0, 0.0), axis=(0, 1), keepdims=True)

    @pl.when(tie_flag[0, 0] > 0.0)
    def _():
        nbits = max(int(seq - 1).bit_length(), 1)
        row_pos = lax.broadcasted_iota(I32, (kb, tq), 0)

        def x_body(it, xcut):
            cand = xcut | jnp.left_shift(jnp.int32(1), nbits - 1 - it)
            cnt = count_keys(lambda s, j: (s == thr) & (j * kb + row_pos < cand))
            return jnp.where(cnt < need, cand, xcut)

        xcut = lax.fori_loop(0, nbits, x_body, jnp.zeros((1, tq), I32))
        xrow_s[...] = jnp.where(searched, xcut, -1)

    def to_col(row_f32):
        return jnp.broadcast_to(row_f32, (LANES, tq)).T

    thr_s[...] = to_col(thr)
    x_s[...] = to_col(xrow_s[...].astype(F32)).astype(I32)
    rb = min(tq, 128)
    n_rc = tq // rb
    lane_rb = lax.broadcasted_iota(I32, (rb, LANES), 1)

    for g in range(N_KV):
        m_s[g][...] = jnp.full((rows_g, LANES), NEG_BIG, F32)
        acc_s[g][...] = jnp.zeros((rows_g, 2 * LANES), F32)
    ones_v = jnp.ones((kb, LANES), BF16)

    def attn_body(j, carry):
        start = pl.multiple_of(j * kb, kb)
        bias = []
        for rc in range(n_rc):
            thr_c = thr_s[rc * rb:(rc + 1) * rb, :]
            xcut_c = x_s[rc * rb:(rc + 1) * rb, :]
            parts = []
            for c in range(n_sub):
                kc = key_s[j, rc * rb:(rc + 1) * rb, c * LANES:(c + 1) * LANES]
                kpos = start + c * LANES + lane_rb
                sel = (kc > thr_c) | ((kc == thr_c) & (kpos <= xcut_c))
                parts.append(jnp.where(sel, 0.0, NEG_BIG))
            bias.append(jnp.concatenate(parts, axis=1))
        for g in range(N_KV):
            qg = q_ref[g * HEADS_PER_KV:(g + 1) * HEADS_PER_KV].reshape(rows_g, LANES)
            kg = k_ref[g, pl.ds(start, kb), :]
            vg = jnp.concatenate([v_ref[g, pl.ds(start, kb), :], ones_v], axis=1)
            lg_all = lax.dot_general(qg, kg, (((1,), (1,)), ((), ())), preferred_element_type=F32)
            m_prev_all = m_s[g][...]
            p_parts, a_parts, m_parts = [], [], []
            for r in range(HEADS_PER_KV):
                for rc in range(n_rc):
                    r0 = r * tq + rc * rb
                    lg = lg_all[r0:r0 + rb] + bias[rc]
                    m_prev = m_prev_all[r0:r0 + rb]
                    m_new = jnp.maximum(m_prev, jnp.max(lg, axis=1, keepdims=True))
                    p = jnp.exp2(lg - jnp.concatenate([m_new] * n_sub, axis=1))
                    p_parts.append(p.astype(BF16))
                    a_parts.append(jnp.exp2(m_prev - m_new))
                    m_parts.append(m_new)
            p_all = jnp.concatenate(p_parts, axis=0)
            alpha_all = jnp.concatenate(a_parts, axis=0)
            pv = jnp.dot(p_all, vg, preferred_element_type=F32)
            acc_s[g][...] = jnp.concatenate([alpha_all, alpha_all], axis=1) * acc_s[g][...] + pv
            m_s[g][...] = jnp.concatenate(m_parts, axis=0)
        return carry

    lax.fori_loop(0, nkb, attn_body, 0)

    for g in range(N_KV):
        acc = acc_s[g][...]
        og = acc[:, :LANES] / acc[:, LANES:]
        for r in range(HEADS_PER_KV):
            col = (g * HEADS_PER_KV + r) * LANES
            y = og[r * tq:(r + 1) * tq] * gate_ref[:, col:col + LANES].astype(F32)
            o_ref[:, col:col + LANES] = y.astype(o_ref.dtype)


def _attention(qh, kh, vh, qih, kiwi, sg, batch, seq, *, tq):
    n = batch * seq
    topk = min(TOPK_MAX, seq // 4)
    tq = min(tq, seq)
    nqt = seq // tq
    attn_w = N_HEADS * HEAD_DIM
    rows_g = HEADS_PER_KV * tq
    return pl.pallas_call(
        functools.partial(_attn_kernel, tq=tq, topk=topk, seq=seq),
        out_shape=jax.ShapeDtypeStruct((n, attn_w), BF16),
        grid=(batch, nqt),
        in_specs=[
            pl.BlockSpec((N_HEADS, tq, LANES), lambda b, t: (0, b * nqt + t, 0)),
            pl.BlockSpec((N_KV, seq, LANES), lambda b, t: (0, b, 0)),
            pl.BlockSpec((N_KV, seq, LANES), lambda b, t: (0, b, 0)),
            pl.BlockSpec((IDX_HEADS, tq, LANES), lambda b, t: (0, b * nqt + t, 0)),
            pl.BlockSpec((None, seq, LANES), lambda b, t: (0, b, 0)),
            pl.BlockSpec((None, tq, LANES), lambda b, t: (0, b * nqt + t, 0)),
            pl.BlockSpec((tq, attn_w), lambda b, t: (b * nqt + t, 0)),
        ],
        out_specs=pl.BlockSpec((tq, attn_w), lambda b, t: (b * nqt + t, 0)),
        scratch_shapes=[
            pltpu.VMEM((IDX_HEADS, tq, LANES), F32),
            pltpu.VMEM((nqt, tq, tq), F32),
            pltpu.VMEM((nqt, tq, tq), F32),
            pltpu.VMEM((tq, LANES), I32),
            pltpu.VMEM((tq, LANES), F32),
            pltpu.VMEM((1, tq), I32),
        ] + [pltpu.VMEM((rows_g, LANES), F32)] * N_KV + [pltpu.VMEM((rows_g, 2 * LANES), F32)] * N_KV,
        compiler_params=_cparams("parallel", "arbitrary"),
        name="sparse_attention",
    )(qh, kh, vh, qih, kiwi, kiwi, sg)


def _pool_kernel(p_ref, gate_ref, w_ref, scale_ref, o_ref, buf_a, buf_b, *, seq):
    pad = 16
    g = pl.program_id(1)
    p = p_ref[...]
    zeros = jnp.zeros((pad, p.shape[1]), F32)
    buf_a[0:pad, :] = zeros
    buf_b[0:pad, :] = zeros
    buf_a[pad:pad + seq, :] = p
    s2 = p + buf_a[pad - 1:pad - 1 + seq, :]
    buf_b[pad:pad + seq, :] = s2
    s4 = s2 + buf_b[pad - 2:pad - 2 + seq, :]
    buf_a[pad:pad + seq, :] = s4
    s8 = s4 + buf_a[pad - 4:pad - 4 + seq, :]
    buf_b[pad:pad + seq, :] = s8
    s16 = s8 + buf_b[pad - 8:pad - 8 + seq, :]
    t1 = (lax.broadcasted_iota(I32, p.shape, 0) + 1).astype(F32)
    win = jnp.where(g == 0, 2.0, jnp.where(g == 1, 4.0, jnp.where(g == 2, 8.0, 16.0))).astype(F32)
    total = jnp.where(g == 0, s2, jnp.where(g == 1, s4, jnp.where(g == 2, s8, s16)))
    mean = total / jnp.minimum(t1, win)
    diff = (mean - p).astype(BF16)
    y = jnp.dot(diff, w_ref[...], preferred_element_type=F32)
    o_ref[...] = (y * scale_ref[...] * gate_ref[...].astype(F32)).astype(o_ref.dtype)


def _pool_mixer(px, sg, pool_w, pool_scale, batch, seq):
    n = batch * seq
    width = pool_scale.shape[-1]
    ng = len(POOL_WINDOWS)
    cg = width // ng
    assert POOL_WINDOWS == (2, 4, 8, 16)
    return pl.pallas_call(
        functools.partial(_pool_kernel, seq=seq),
        out_shape=jax.ShapeDtypeStruct((n, width), BF16),
        grid=(batch, ng),
        in_specs=[
            pl.BlockSpec((seq, cg), lambda b, g: (b, g)),
            pl.BlockSpec((seq, cg), lambda b, g: (b, ng + g)),
            pl.BlockSpec((None, cg, cg), lambda b, g: (g, 0, 0)),
            pl.BlockSpec((1, cg), lambda b, g: (0, g)),
        ],
        out_specs=pl.BlockSpec((seq, cg), lambda b, g: (b, g)),
        scratch_shapes=[pltpu.VMEM((seq + 16, cg), F32), pltpu.VMEM((seq + 16, cg), F32)],
        compiler_params=_cparams("parallel", "arbitrary"),
        name="pool_mixer",
    )(px, sg, pool_w, pool_scale.reshape(1, width))


def _lru_kernel(x_ref, gr_ref, gate_ref, cw_ref, cb_ref, wa_ref, ba_ref, wx_ref, bx_ref, lam_ref,
                o_ref, a_s, b_s, *, seq, ct):
    x = x_ref[...]
    row = lax.broadcasted_iota(I32, (seq, ct), 0)
    xc = jnp.broadcast_to(cb_ref[...], (seq, ct))
    for tap in range(CONV_WIDTH):
        d = CONV_WIDTH - 1 - tap
        x_d = x if d == 0 else jnp.where(row >= d, pltpu.roll(x, d, 0), 0.0)
        xc = xc + x_d * cw_ref[tap:tap + 1, :]
    xcb = xc.astype(BF16)
    nb = ct // LRU_BLOCK
    r_parts, i_parts = [], []
    for blk in range(nb):
        xb = xcb[:, blk * LRU_BLOCK:(blk + 1) * LRU_BLOCK]
        r_parts.append(jnp.dot(xb, wa_ref[blk], preferred_element_type=F32))
        i_parts.append(jnp.dot(xb, wx_ref[blk], preferred_element_type=F32))
    r = _sigmoid(jnp.concatenate(r_parts, axis=1) + ba_ref[...])
    gi = _sigmoid(jnp.concatenate(i_parts, axis=1) + bx_ref[...])
    lam = lam_ref[...]
    softplus_neg_lam = jnp.log(1.0 + jnp.exp(-lam))
    log_a = -LRU_C * r * softplus_neg_lam
    a = jnp.exp(log_a)
    one_m_a2 = 1.0 - a * a
    root = jnp.where(one_m_a2 > 0.0, one_m_a2 * lax.rsqrt(one_m_a2), 0.0)
    b = root * (gi * xc)

    sub = row % SUBLANES
    for d in (1, 2, 4):
        keep = sub >= d
        a_sh = jnp.where(keep, pltpu.roll(a, d, 0), 1.0)
        b_sh = jnp.where(keep, pltpu.roll(b, d, 0), 0.0)
        b = a * b_sh + b
        a = a * a_sh
    a_s[...] = a
    b_s[...] = b

    def body(t, carry):
        r0 = pl.multiple_of(t * SUBLANES, SUBLANES)
        h = a_s[pl.ds(r0, SUBLANES), :] * carry + b_s[pl.ds(r0, SUBLANES), :]
        b_s[pl.ds(r0, SUBLANES), :] = h
        return jnp.broadcast_to(h[SUBLANES - 1:SUBLANES, :], (SUBLANES, ct))

    lax.fori_loop(0, seq // SUBLANES, body, jnp.zeros((SUBLANES, ct), F32), unroll=8)
    h = b_s[...]
    o_ref[...] = (h * gr_ref[...].astype(F32) * gate_ref[...].astype(F32)).astype(o_ref.dtype)


def _lru_mixer(px, gg, sg, conv_w, conv_b, wa, ba, wx, bx, lam, batch, seq, *, ct=256):
    n = batch * seq
    width = conv_b.shape[-1]
    nct = width // ct
    nb = ct // LRU_BLOCK
    row = lambda a: a.reshape(1, width)
    rspec = pl.BlockSpec((1, ct), lambda b, j: (0, j))
    return pl.pallas_call(
        functools.partial(_lru_kernel, seq=seq, ct=ct),
        out_shape=jax.ShapeDtypeStruct((n, width), BF16),
        grid=(batch, nct),
        in_specs=[
            pl.BlockSpec((seq, ct), lambda b, j: (b, nct + j)),
            pl.BlockSpec((seq, ct), lambda b, j: (b, j)),
            pl.BlockSpec((seq, ct), lambda b, j: (b, 2 * nct + j)),
            pl.BlockSpec((CONV_WIDTH, ct), lambda b, j: (0, j)),
            rspec,
            pl.BlockSpec((nb, LRU_BLOCK, LRU_BLOCK), lambda b, j: (j, 0, 0)),
            rspec,
            pl.BlockSpec((nb, LRU_BLOCK, LRU_BLOCK), lambda b, j: (j, 0, 0)),
            rspec,
            rspec,
        ],
        out_specs=pl.BlockSpec((seq, ct), lambda b, j: (b, j)),
        scratch_shapes=[pltpu.VMEM((seq, ct), F32),
                        pltpu.VMEM((seq, ct), F32)],
        compiler_params=_cparams("parallel", "arbitrary"),
        name="rglru_mixer",
    )(px, gg, sg, conv_w, row(conv_b), wa, row(ba), wx, row(bx), row(lam))


def _out_proj_ln_kernel(ya_ref, yb_ref, yc_ref, w_ref, x_ref, g_ref, b_ref, o_ref, ob_ref, *, alpha):
    acc = jnp.dot(ya_ref[...], w_ref[0], preferred_element_type=F32)
    acc = acc + jnp.dot(yb_ref[...], w_ref[1], preferred_element_type=F32)
    acc = acc + jnp.dot(yc_ref[...], w_ref[2], preferred_element_type=F32)
    out = _layer_norm(alpha * x_ref[...] + acc, g_ref[...], b_ref[...])
    o_ref[...] = out
    ob_ref[...] = out.astype(BF16)


def _out_proj_ln(ya, yb, yc, w3, x, g, b, alpha, *, tm=256):
    n, width = ya.shape
    d = w3.shape[-1]
    tm = min(tm, n)
    aspec = pl.BlockSpec((tm, width), lambda i: (i, 0))
    tok = pl.BlockSpec((tm, d), lambda i: (i, 0))
    row = pl.BlockSpec((1, d), lambda i: (0, 0))
    return pl.pallas_call(
        functools.partial(_out_proj_ln_kernel, alpha=alpha),
        out_shape=(jax.ShapeDtypeStruct((n, d), F32), jax.ShapeDtypeStruct((n, d), BF16)),
        grid=(n // tm,),
        in_specs=[aspec, aspec, aspec,
                  pl.BlockSpec((3, width, d), lambda i: (0, 0, 0), pipeline_mode=pl.Buffered(1)),
                  tok, row, row],
        out_specs=(tok, tok),
        compiler_params=_cparams("parallel"),
        name="out_proj_ln",
    )(ya, yb, yc, w3, x, g.reshape(1, d), b.reshape(1, d))


def _ffn_up_kernel(x_ref, wg_ref, wu_ref, o_ref):
    xb = x_ref[...]
    hg = jnp.dot(xb, wg_ref[...], preferred_element_type=F32)
    hu = jnp.dot(xb, wu_ref[...], preferred_element_type=F32)
    o_ref[...] = (hg * _sigmoid(hg) * hu).astype(o_ref.dtype)


def _ffn_down_ln_kernel(h_ref, w_ref, x_ref, g_ref, b_ref, o_ref, ob_ref, *, alpha):
    k = pl.program_id(1)
    y = jnp.dot(h_ref[...], w_ref[...], preferred_element_type=F32)

    @pl.when(k == 0)
    def _():
        o_ref[...] = y

    @pl.when(k > 0)
    def _():
        o_ref[...] += y

    @pl.when(k == pl.num_programs(1) - 1)
    def _():
        rc = min(LN_ROWS, o_ref.shape[0])

        def body(c, carry):
            rows = pl.ds(pl.multiple_of(c * rc, rc), rc)
            out = _layer_norm(alpha * x_ref[rows, :] + o_ref[rows, :], g_ref[...], b_ref[...])
            o_ref[rows, :] = out
            ob_ref[rows, :] = out.astype(BF16)
            return carry

        lax.fori_loop(0, o_ref.shape[0] // rc, body, 0)


def _ffn_dense(xb, x, wg, wu, wd, g, b, alpha, *, tm_up=1024, tn_up=512, tm_down=512, k_steps=4):
    n, d = x.shape
    ff = wg.shape[1]
    tm_up, tm_down = min(tm_up, n), min(tm_down, n)
    h = pl.pallas_call(
        _ffn_up_kernel,
        out_shape=jax.ShapeDtypeStruct((n, ff), BF16),
        grid=(n // tm_up, ff // tn_up),
        in_specs=[pl.BlockSpec((tm_up, d), lambda i, j: (i, 0)),
                  pl.BlockSpec((d, tn_up), lambda i, j: (0, j)),
                  pl.BlockSpec((d, tn_up), lambda i, j: (0, j))],
        out_specs=pl.BlockSpec((tm_up, tn_up), lambda i, j: (i, j)),
        compiler_params=_cparams("parallel", "arbitrary"),
        name="ffn_up",
    )(xb, wg, wu)
    tk = ff // k_steps
    tok = pl.BlockSpec((tm_down, d), lambda i, k: (i, 0))
    row = pl.BlockSpec((1, d), lambda i, k: (0, 0))
    return pl.pallas_call(
        functools.partial(_ffn_down_ln_kernel, alpha=alpha),
        out_shape=(jax.ShapeDtypeStruct((n, d), F32), jax.ShapeDtypeStruct((n, d), BF16)),
        grid=(n // tm_down, k_steps),
        in_specs=[pl.BlockSpec((tm_down, tk), lambda i, k: (i, k)),
                  pl.BlockSpec((tk, d), lambda i, k: (k, 0)),
                  tok, row, row],
        out_specs=(tok, tok),
        compiler_params=_cparams("parallel", "arbitrary"),
        name="ffn_down_ln",
    )(h, wd, x, g.reshape(1, d), b.reshape(1, d))


def _router_kernel(x_ref, w_ref, ids_ref, wts_ref):
    logits = jnp.dot(x_ref[...], w_ref[...], preferred_element_type=F32, precision=lax.Precision.HIGHEST)
    lane_i = lax.broadcasted_iota(I32, logits.shape, 1)
    lane = lane_i.astype(F32)
    logits = jnp.where(lane_i < N_EXPERTS, logits, -jnp.inf)
    m1 = jnp.max(logits, axis=1, keepdims=True)
    i1 = jnp.min(jnp.where(logits == m1, lane, float(LANES)), axis=1, keepdims=True)
    rest = jnp.where(lane == i1, -jnp.inf, logits)
    m2 = jnp.max(rest, axis=1, keepdims=True)
    i2 = jnp.min(jnp.where(rest == m2, lane, float(LANES)), axis=1, keepdims=True)
    e2 = jnp.exp(m2 - m1)
    w1 = 1.0 / (1.0 + e2)
    w2 = e2 / (1.0 + e2)
    ids_ref[...] = jnp.where(lane_i == 0, i1, jnp.where(lane_i == 1, i2, 0.0)).astype(I32)
    wts_ref[...] = jnp.where(lane_i == 0, w1, jnp.where(lane_i == 1, w2, 0.0))


def _router(x, w_router, *, tm=512):
    n, d = x.shape
    wpad = jnp.zeros((d, LANES), F32).at[:, :N_EXPERTS].set(w_router.astype(F32))
    tm = min(tm, n)
    tok = pl.BlockSpec((tm, LANES), lambda i: (i, 0))
    return pl.pallas_call(
        _router_kernel,
        out_shape=(jax.ShapeDtypeStruct((n, LANES), I32), jax.ShapeDtypeStruct((n, LANES), F32)),
        grid=(n // tm,),
        in_specs=[pl.BlockSpec((tm, d), lambda i: (i, 0)), pl.BlockSpec((d, LANES), lambda i: (0, 0))],
        out_specs=(tok, tok),
        compiler_params=_cparams("parallel"),
        name="moe_router",
    )(x, wpad)


def _gather_rows_kernel(tok_ref, x_hbm, o_ref, buf, sem, *, tm):
    i = pl.program_id(0)

    def issue(tile, slot):
        base = tile * tm

        def body(r, carry):
            t = tok_ref[base + r]
            pltpu.make_async_copy(x_hbm.at[pl.ds(t, 1), :], buf.at[slot, pl.ds(r, 1), :], sem.at[slot]).start()
            return carry

        lax.fori_loop(0, tm, body, 0, unroll=8)

    @pl.when(i == 0)
    def _():
        issue(0, 0)

    @pl.when(i + 1 < pl.num_programs(0))
    def _():
        issue(i + 1, (i + 1) % 2)

    slot = i % 2
    pltpu.make_async_copy(x_hbm.at[pl.ds(0, tm), :], buf.at[slot], sem.at[slot]).wait()
    o_ref[...] = buf[slot].astype(o_ref.dtype)


def _gather_rows(x, row_tok, *, tm=256):
    n, d = x.shape
    rows = row_tok.shape[0]
    return pl.pallas_call(
        functools.partial(_gather_rows_kernel, tm=tm),
        out_shape=jax.ShapeDtypeStruct((rows, d), BF16),
        grid_spec=pltpu.PrefetchScalarGridSpec(
            num_scalar_prefetch=1,
            grid=(rows // tm,),
            in_specs=[pl.BlockSpec(memory_space=pl.ANY)],
            out_specs=pl.BlockSpec((tm, d), lambda i, tok: (i, 0)),
            scratch_shapes=[pltpu.VMEM((2, tm, d), F32), pltpu.SemaphoreType.DMA((2,))],
        ),
        compiler_params=_cparams("arbitrary"),
        name="moe_gather_rows",
    )(row_tok, x)


def _expert_changed(te_ref, i):
    return (i == 0) | (te_ref[i] != te_ref[jnp.maximum(i - 1, 0)])


def _stream_expert_weights(j, i, n_j, col_tile, te_ref, run_ref, nxt_ref, meta_ref, w_hbm, stage, work, sem):
    @pl.when(_expert_changed(te_ref, i))
    def _():
        k = j * meta_ref[0] + run_ref[i]
        slot = k % 2

        def copies(e, jj, s):
            col = pl.ds(pl.multiple_of(jj * col_tile, col_tile), col_tile)
            return [pltpu.make_async_copy(w.at[e, :, col], st.at[s], sem.at[s, a])
                    for a, (w, st) in enumerate(zip(w_hbm, stage))]

        @pl.when(k == 0)
        def _():
            for c in copies(te_ref[i], j, slot):
                c.start()

        has_next_run = nxt_ref[i] >= 0
        e_next = jnp.where(has_next_run, nxt_ref[i], meta_ref[1])
        j_next = jnp.where(has_next_run, j, j + 1)

        @pl.when(j_next < n_j)
        def _():
            for c in copies(e_next, j_next, 1 - slot):
                c.start()

        for c in copies(te_ref[i], j, slot):
            c.wait()
        for st, wk in zip(stage, work):
            wk[...] = st[slot].astype(BF16)


def _moe_up_kernel(te_ref, nv_ref, run_ref, nxt_ref, meta_ref, x_ref, wg_hbm, wu_hbm, o_ref,
                   stage_g, stage_u, wg_s, wu_s, sem, *, tf):
    j, i = pl.program_id(0), pl.program_id(1)
    _stream_expert_weights(j, i, pl.num_programs(0), tf, te_ref, run_ref, nxt_ref, meta_ref,
                           (wg_hbm, wu_hbm), (stage_g, stage_u), (wg_s, wu_s), sem)

    @pl.when(i < nv_ref[0])
    def _():
        xb = x_ref[...]
        hg = jnp.dot(xb, wg_s[...], preferred_element_type=F32)
        hu = jnp.dot(xb, wu_s[...], preferred_element_type=F32)
        o_ref[...] = (hg * _sigmoid(hg) * hu).astype(o_ref.dtype)

    @pl.when(i >= nv_ref[0])
    def _():
        o_ref[...] = jnp.zeros(o_ref.shape, o_ref.dtype)


def _moe_up(xs, wg, wu, sched, *, tm, tf=1024):
    rows, d = xs.shape
    ff = wg.shape[-1]
    return pl.pallas_call(
        functools.partial(_moe_up_kernel, tf=tf),
        out_shape=jax.ShapeDtypeStruct((rows, ff), BF16),
        grid_spec=pltpu.PrefetchScalarGridSpec(
            num_scalar_prefetch=len(sched),
            grid=(ff // tf, rows // tm),
            in_specs=[pl.BlockSpec((tm, d), lambda j, i, *_: (i, 0)),
                      pl.BlockSpec(memory_space=pl.ANY),
                      pl.BlockSpec(memory_space=pl.ANY)],
            out_specs=pl.BlockSpec((tm, tf), lambda j, i, *_: (i, j)),
            scratch_shapes=[pltpu.VMEM((2, d, tf), F32), pltpu.VMEM((2, d, tf), F32),
                            pltpu.VMEM((d, tf), BF16), pltpu.VMEM((d, tf), BF16),
                            pltpu.SemaphoreType.DMA((2, 2))],
        ),
        compiler_params=_cparams("arbitrary", "arbitrary"),
        name="moe_up",
    )(*sched, xs, wg, wu)


def _moe_down_kernel(te_ref, nv_ref, run_ref, nxt_ref, meta_ref, h_ref, wd_hbm, o_ref, stage_d, wd_s, sem, *, tn):
    j, i = pl.program_id(0), pl.program_id(1)
    _stream_expert_weights(j, i, pl.num_programs(0), tn, te_ref, run_ref, nxt_ref, meta_ref,
                           (wd_hbm,), (stage_d,), (wd_s,), sem)

    @pl.when(i < nv_ref[0])
    def _():
        o_ref[...] = jnp.dot(h_ref[...], wd_s[...], preferred_element_type=F32)

    @pl.when(i >= nv_ref[0])
    def _():
        o_ref[...] = jnp.zeros(o_ref.shape, o_ref.dtype)


def _moe_down(h, wd, sched, *, tm, tn=512):
    rows, ff = h.shape
    d = wd.shape[-1]
    return pl.pallas_call(
        functools.partial(_moe_down_kernel, tn=tn),
        out_shape=jax.ShapeDtypeStruct((rows, d), F32),
        grid_spec=pltpu.PrefetchScalarGridSpec(
            num_scalar_prefetch=len(sched),
            grid=(d // tn, rows // tm),
            in_specs=[pl.BlockSpec((tm, ff), lambda j, i, *_: (i, 0)),
                      pl.BlockSpec(memory_space=pl.ANY)],
            out_specs=pl.BlockSpec((tm, tn), lambda j, i, *_: (i, j)),
            scratch_shapes=[pltpu.VMEM((2, ff, tn), F32), pltpu.VMEM((ff, tn), BF16),
                            pltpu.SemaphoreType.DMA((2, 1))],
        ),
        compiler_params=_cparams("arbitrary", "arbitrary"),
        name="moe_down",
    )(*sched, h, wd)


def _combine_ln_kernel(pos_ref, y_hbm, x_ref, wts_ref, g_ref, b_ref, o_ref, buf, sem, *, tm, alpha):
    i = pl.program_id(0)

    def issue(tile, slot):
        base = tile * tm

        def body(r, carry):
            for k in range(TOP_K):
                row = pos_ref[(base + r) * TOP_K + k]
                pltpu.make_async_copy(y_hbm.at[pl.ds(row, 1), :], buf.at[slot, k, pl.ds(r, 1), :],
                                      sem.at[slot]).start()
            return carry

        lax.fori_loop(0, tm, body, 0, unroll=8)

    @pl.when(i == 0)
    def _():
        issue(0, 0)

    @pl.when(i + 1 < pl.num_programs(0))
    def _():
        issue(i + 1, (i + 1) % 2)

    slot = i % 2
    for k in range(TOP_K):
        pltpu.make_async_copy(y_hbm.at[pl.ds(0, tm), :], buf.at[slot, k], sem.at[slot]).wait()
    wts = wts_ref[...]
    y = buf[slot, 0] * wts[:, 0:1] + buf[slot, 1] * wts[:, 1:2]
    o_ref[...] = _layer_norm(alpha * x_ref[...] + y, g_ref[...], b_ref[...])


def _combine_ln(yrows, pos, x, wts, g, b, alpha, *, tm=128):
    n, d = x.shape
    tm = min(tm, n)
    return pl.pallas_call(
        functools.partial(_combine_ln_kernel, tm=tm, alpha=alpha),
        out_shape=jax.ShapeDtypeStruct((n, d), F32),
        grid_spec=pltpu.PrefetchScalarGridSpec(
            num_scalar_prefetch=1,
            grid=(n // tm,),
            in_specs=[pl.BlockSpec(memory_space=pl.ANY),
                      pl.BlockSpec((tm, d), lambda i, pos: (i, 0)),
                      pl.BlockSpec((tm, LANES), lambda i, pos: (i, 0)),
                      pl.BlockSpec((1, d), lambda i, pos: (0, 0)),
                      pl.BlockSpec((1, d), lambda i, pos: (0, 0))],
            out_specs=pl.BlockSpec((tm, d), lambda i, pos: (i, 0)),
            scratch_shapes=[pltpu.VMEM((2, TOP_K, tm, d), F32), pltpu.SemaphoreType.DMA((2,))],
        ),
        compiler_params=_cparams("arbitrary"),
        name="moe_combine_ln",
    )(pos, yrows, x, wts, g.reshape(1, d), b.reshape(1, d))


def _moe_block(x, w_router, wg, wu, wd, g, b, alpha, *, tm=256):
    n, d = x.shape
    ids, wts = _router(x, w_router)
    e_flat = ids[:, :TOP_K].reshape(-1)
    n_assign = n * TOP_K
    onehot = (e_flat[:, None] == jnp.arange(N_EXPERTS, dtype=I32)[None, :]).astype(I32)
    rank = jnp.sum((jnp.cumsum(onehot, axis=0) - onehot) * onehot, axis=1)
    counts = jnp.sum(onehot, axis=0)
    padded = (counts + tm - 1) // tm * tm
    end_padded = jnp.cumsum(padded)
    start_padded = end_padded - padded
    dest = (start_padded[e_flat] + rank).astype(I32)
    rows = n_assign + N_EXPERTS * tm
    n_tiles = rows // tm
    flat_tok = jnp.arange(n_assign, dtype=I32) // TOP_K
    row_tok = jnp.zeros((rows,), I32).at[dest].set(flat_tok)
    tile_start = jnp.arange(n_tiles, dtype=I32) * tm
    tile_e = jnp.minimum(jnp.sum((tile_start[:, None] >= end_padded[None, :]).astype(I32), axis=1),
                         N_EXPERTS - 1).astype(I32)
    n_valid = (end_padded[-1:] // tm).astype(I32)
    tile_ix = jnp.arange(n_tiles, dtype=I32)
    tile_e = jnp.where(tile_ix < n_valid[0], tile_e, tile_e[jnp.maximum(n_valid[0] - 1, 0)])
    is_start = jnp.concatenate([jnp.ones((1,), I32), (tile_e[1:] != tile_e[:-1]).astype(I32)])
    run_id = (jnp.cumsum(is_start) - 1).astype(I32)
    larger = jnp.where(tile_e[None, :] > tile_e[:, None], tile_e[None, :], N_EXPERTS)
    nxt_e = jnp.min(larger, axis=1)
    nxt_e = jnp.where(nxt_e < N_EXPERTS, nxt_e, -1).astype(I32)
    meta = jnp.stack([run_id[-1] + 1, tile_e[0]]).astype(I32)
    sched = (tile_e, n_valid, run_id, nxt_e, meta)
    xs = _gather_rows(x, row_tok, tm=tm)
    h = _moe_up(xs, wg, wu, sched, tm=tm)
    yrows = _moe_down(h, wd, sched, tm=tm)
    return _combine_ln(yrows, dest, x, wts, g, b, alpha)


def _pack_w_in(w_in, d_model):
    attn_w = N_HEADS * HEAD_DIM
    kv_w = N_KV * HEAD_DIM
    idx_w = IDX_HEADS * IDX_DIM
    o = np.cumsum([0, attn_w, kv_w, kv_w, idx_w, IDX_DIM, IDX_HEADS, d_model, d_model, d_model])
    kiwi = jnp.pad(w_in[:, o[4]:o[6]], ((0, 0), (0, LANES - IDX_DIM - IDX_HEADS)))
    parts = [("q", w_in[:, o[0]:o[1]]), ("px", w_in[:, o[6]:o[8]]), ("gr", w_in[:, o[8]:o[9]]),
             ("gates", w_in[:, o[9]:]), ("qi", w_in[:, o[3]:o[4]]), ("k", w_in[:, o[1]:o[2]]),
             ("v", w_in[:, o[2]:o[3]]), ("kiwi", kiwi)]
    cols, off = {}, 0
    for name, w in parts:
        cols[name] = (off, w.shape[1])
        off += w.shape[1]
    return jnp.concatenate([w for _, w in parts], axis=1).astype(BF16), cols


def _mixer(x_mm, tabs, w_in, pool_w, pool_scale, conv_w, conv_b, wa, ba, wx, bx, lam,
           batch, seq, d_model, tq):
    cos_a, sin_a, cos_i, sin_ia, sin_ib, cos_q, sin_q, cos_iq = tabs
    wcat, cols = _pack_w_in(w_in, d_model)
    rope_a = ((cos_a, sin_a), (HEAD_DIM // 2,))
    rope_i = (LANES - IDX_DIM // 2, IDX_DIM // 2)
    qh = _proj_heads(x_mm, wcat, cols["q"], (cos_q, sin_q), rope_a[1], BF16)
    kh = _proj_heads(x_mm, wcat, cols["k"], *rope_a, BF16)
    vh = _proj_heads(x_mm, wcat, cols["v"], (), (), BF16)
    qih = _proj_heads(x_mm, wcat, cols["qi"], (cos_iq, sin_ia, sin_ib), rope_i, BF16, head_width=IDX_DIM)
    kiwi = _proj_heads(x_mm, wcat, cols["kiwi"], (cos_i, sin_ia, sin_ib), rope_i, F32)
    px = _proj_plain(x_mm, wcat, cols["px"], None, F32)
    gg = _proj_plain(x_mm, wcat, cols["gr"], "gelu", BF16)
    sg = _proj_plain(x_mm, wcat, cols["gates"], "sigmoid", BF16)
    ya = _attention(qh, kh, vh, qih, kiwi, sg, batch, seq, tq=tq)
    yb = _pool_mixer(px, sg, pool_w.astype(BF16), pool_scale, batch, seq)
    yc = _lru_mixer(px, gg, sg, conv_w, conv_b, wa.astype(BF16), ba, wx.astype(BF16), bx, lam, batch, seq)
    return ya, yb, yc


def _pad_ff(w, axis, mult):
    ff = w.shape[axis]
    padn = (-ff) % mult
    if padn == 0:
        return w
    widths = [(0, 0)] * w.ndim
    widths[axis] = (0, padn)
    return jnp.pad(w, widths)


def kernel(x, positions, mix_w_in, mix_w_out, pool_w, pool_scale, conv_w, conv_b, lru_wa, lru_ba, lru_wx, lru_bx, lru_lam, ln_mix_g, ln_mix_b, ln_ffn_g, ln_ffn_b, dense_w_gate, dense_w_up, dense_w_down, moe_router, moe_w_gate, moe_w_up, moe_w_down):
    batch, seq, d_model = x.shape
    depth = mix_w_in.shape[0]
    alpha = np.float32((2 * depth) ** 0.25)
    n = batch * seq
    tabs = _rope_tables(positions)
    xf = x.reshape(n, d_model)
    x_mm = xf
    for layer in range(depth):
        ya, yb, yc = _mixer(x_mm, tabs, mix_w_in[layer], pool_w[layer], pool_scale[layer],
                            conv_w[layer], conv_b[layer], lru_wa[layer], lru_ba[layer], lru_wx[layer],
                            lru_bx[layer], lru_lam[layer], batch, seq, d_model, tq=256)
        w_out3 = mix_w_out[layer].astype(BF16).reshape(3, -1, d_model)
        xf, xb = _out_proj_ln(ya, yb, yc, w_out3, xf, ln_mix_g[layer], ln_mix_b[layer], alpha)
        j = layer // 2
        if layer % 2 == 0:
            wg = _pad_ff(dense_w_gate[j].astype(BF16), 1, 512)
            wu = _pad_ff(dense_w_up[j].astype(BF16), 1, 512)
            wd = _pad_ff(dense_w_down[j].astype(BF16), 0, 512)
            xf, x_mm = _ffn_dense(xb, xf, wg, wu, wd, ln_ffn_g[layer], ln_ffn_b[layer], alpha)
        else:
            xf = _moe_block(xf, moe_router[j], moe_w_gate[j], moe_w_up[j], moe_w_down[j],
                            ln_ffn_g[layer], ln_ffn_b[layer], alpha)
            x_mm = xf
    return xf.reshape(batch, seq, d_model)
```

```python
import functools

import jax
import jax.numpy as jnp
import numpy as np
from jax import lax
from jax.experimental import pallas as pl
from jax.experimental.pallas import tpu as pltpu

F32 = jnp.float32
BF16 = jnp.bfloat16
I32 = jnp.int32

LANES = 128
SUBLANES = 8
VMEM_LIMIT = 56 * 1024 * 1024

CHUNK = 64
N_HEADS = 16
HEAD_DIM = 128
N_KV = 4
HEADS_PER_KV = N_HEADS // N_KV
IDX_HEADS = 16
IDX_DIM = 64
TOPK_MAX = 256
ROPE_THETA = 10000.0
POOL_WINDOWS = (2, 4, 8, 16)
LRU_BLOCK = 128
CONV_WIDTH = 4
LRU_C = 8.0
N_EXPERTS = 8
TOP_K = 2
LN_EPS = 1e-5
LN_ROWS = 128
INT_MIN = -2 ** 31
NEG_BIG = -1e30


def _cparams(*sem):
    return pltpu.CompilerParams(dimension_semantics=sem, vmem_limit_bytes=VMEM_LIMIT)


def _sigmoid(x):
    return 0.5 * (1.0 + jnp.tanh(0.5 * x))


def _gelu_tanh(x):
    c = np.float32(np.sqrt(2.0 / np.pi))
    return 0.5 * x * (1.0 + jnp.tanh(c * (x + np.float32(0.044715) * (x * x * x))))


def _layer_norm(y, g, b):
    mu = jnp.mean(y, axis=-1, keepdims=True)
    d = y - mu
    var = jnp.mean(d * d, axis=-1, keepdims=True)
    return d * lax.rsqrt(var + LN_EPS) * g + b


def _rope_tab_kernel(pos_ref, inv_a_ref, inv_i_ref, sgn_a_ref, m_cos_ref, add_cos_ref,
                     m_sa_ref, m_sb_ref, cos_a_ref, sin_a_ref, cos_i_ref, sin_ia_ref, sin_ib_ref,
                     cos_q_ref, sin_q_ref, cos_iq_ref):
    pos = pos_ref[...]
    ang_a = pos * inv_a_ref[...]
    cos_a = jnp.cos(ang_a)
    sin_a = jnp.sin(ang_a) * sgn_a_ref[...]
    cos_a_ref[...] = cos_a
    sin_a_ref[...] = sin_a
    q_scale = np.float32(HEAD_DIM ** -0.5 * np.log2(np.e))
    cos_q_ref[...] = cos_a * q_scale
    sin_q_ref[...] = sin_a * q_scale
    ang_i = pos * inv_i_ref[...]
    s_i = jnp.sin(ang_i)
    cos_iq = jnp.cos(ang_i) * m_cos_ref[...]
    cos_iq_ref[...] = cos_iq
    cos_i_ref[...] = cos_iq + add_cos_ref[...]
    sin_ia_ref[...] = s_i * m_sa_ref[...]
    sin_ib_ref[...] = s_i * m_sb_ref[...]


def _rope_tables(positions):
    n = positions.size
    pos = jnp.broadcast_to(positions.reshape(n, 1).astype(F32), (n, LANES))
    lane = np.arange(LANES)
    inv_a = (ROPE_THETA ** (-jnp.arange(0, HEAD_DIM, 2, dtype=F32) / HEAD_DIM))
    inv_i = (ROPE_THETA ** (-jnp.arange(0, IDX_DIM, 2, dtype=F32) / IDX_DIM))
    inv_a_row = jnp.concatenate([inv_a, inv_a])[None, :]
    inv_i_row = jnp.concatenate([inv_i, inv_i, jnp.zeros((LANES - IDX_DIM,), F32)])[None, :]
    sgn_a = jnp.asarray(np.where(lane < HEAD_DIM // 2, -1.0, 1.0), F32)[None, :]
    m_cos = jnp.asarray((lane < IDX_DIM).astype(np.float32))[None, :]
    wi_scale = (IDX_HEADS ** -0.5) * (IDX_DIM ** -0.5)
    add_cos = jnp.asarray(np.where((lane >= IDX_DIM) & (lane < IDX_DIM + IDX_HEADS), wi_scale, 0.0), F32)[None, :]
    m_sa = jnp.asarray(np.where(lane < IDX_DIM // 2, -1.0, 0.0), F32)[None, :]
    m_sb = jnp.asarray(np.where((lane >= IDX_DIM // 2) & (lane < IDX_DIM), 1.0, 0.0), F32)[None, :]
    tm = min(n, 1024)
    row = pl.BlockSpec((1, LANES), lambda i: (0, 0))
    tok = pl.BlockSpec((tm, LANES), lambda i: (i, 0))
    out = jax.ShapeDtypeStruct((n, LANES), F32)
    return pl.pallas_call(
        _rope_tab_kernel,
        out_shape=(out,) * 8,
        grid=(n // tm,),
        in_specs=[tok] + [row] * 7,
        out_specs=(tok,) * 8,
        compiler_params=_cparams("parallel"),
        name="rope_tables",
    )(pos, inv_a_row, inv_i_row, sgn_a, m_cos, add_cos, m_sa, m_sb)


def _proj_heads_kernel(x_ref, w_ref, *rest, shifts, heads, head_width):
    tabs, o_ref = rest[:-1], rest[-1]
    acc = jnp.dot(x_ref[...].astype(BF16), w_ref[...], preferred_element_type=F32)
    per_slab = LANES // head_width
    for h in range(heads):
        xh = acc[:, (h // per_slab) * LANES:(h // per_slab + 1) * LANES]
        if h % per_slab:
            xh = pltpu.roll(xh, LANES - (h % per_slab) * head_width, 1)
        if tabs:
            y = xh * tabs[0][...]
            for s, t in zip(shifts, tabs[1:]):
                y = y + pltpu.roll(xh, s, 1) * t[...]
        else:
            y = xh
        o_ref[h] = y.astype(o_ref.dtype)


def _proj_heads(x, w, cols, tabs, shifts, out_dtype, *, head_width=LANES, tm=1024, tile_cols=1024):
    n, d = x.shape
    off, width = cols
    tn = min(tile_cols, width)
    hp = tn // head_width
    assert width % tn == 0 and off % tn == 0 and (head_width == LANES or tabs)
    tm = min(tm, n)
    tab_spec = pl.BlockSpec((tm, LANES), lambda i, j: (i, 0))
    return pl.pallas_call(
        functools.partial(_proj_heads_kernel, shifts=shifts, heads=hp, head_width=head_width),
        out_shape=jax.ShapeDtypeStruct((width // head_width, n, LANES), out_dtype),
        grid=(n // tm, width // tn),
        in_specs=[pl.BlockSpec((tm, d), lambda i, j: (i, 0)),
                  pl.BlockSpec((d, tn), lambda i, j: (0, off // tn + j))] + [tab_spec] * len(tabs),
        out_specs=pl.BlockSpec((hp, tm, LANES), lambda i, j: (j, i, 0)),
        compiler_params=_cparams("parallel", "arbitrary"),
        name="proj_heads",
    )(x, w, *tabs)


def _proj_plain_kernel(x_ref, w_ref, o_ref, *, act):
    acc = jnp.dot(x_ref[...].astype(BF16), w_ref[...], preferred_element_type=F32)
    if act == "gelu":
        acc = _gelu_tanh(acc)
    elif act == "sigmoid":
        acc = _sigmoid(acc)
    o_ref[...] = acc.astype(o_ref.dtype)


def _proj_plain(x, w, cols, act, out_dtype, *, tm=1024, tn=1024):
    n, d = x.shape
    off, width = cols
    assert width % tn == 0 and off % tn == 0
    tm = min(tm, n)
    return pl.pallas_call(
        functools.partial(_proj_plain_kernel, act=act),
        out_shape=jax.ShapeDtypeStruct((n, width), out_dtype),
        grid=(n // tm, width // tn),
        in_specs=[pl.BlockSpec((tm, d), lambda i, j: (i, 0)),
                  pl.BlockSpec((d, tn), lambda i, j: (0, off // tn + j))],
        out_specs=pl.BlockSpec((tm, tn), lambda i, j: (i, j)),
        compiler_params=_cparams("parallel", "arbitrary"),
        name="proj_plain",
    )(x, w)


def _key_to_float(key):
    return pltpu.bitcast(key ^ ((key >> 31) & 0x7FFFFFFF), F32)


def _attn_kernel(q_ref, k_ref, v_ref, qi_ref, ki_ref, wi_ref, gate_ref, o_ref,
                 wb_s, key_s, keyt_s, x_s, thr_s, xrow_s, *state, tq, topk, seq):
    m_s, acc_s = state[:N_KV], state[N_KV:]
    kb = tq
    n_sub = kb // LANES
    qt = pl.program_id(1)
    nkb = qt + 1
    n_hi = IDX_HEADS
    rows_g = HEADS_PER_KV * tq

    wi = wi_ref[...]
    for h in range(n_hi):
        wb_s[h] = jnp.broadcast_to(wi[:, IDX_DIM + h:IDX_DIM + h + 1], (tq, LANES))
    qi = qi_ref[...].reshape(n_hi * tq, LANES)
    q_row = qt * tq + lax.broadcasted_iota(I32, (tq, LANES), 0)
    limit = (q_row // CHUNK + 1) * CHUNK
    lane_pos = lax.broadcasted_iota(I32, (tq, LANES), 1)
    limit_t = ((qt * tq + lax.broadcasted_iota(I32, (1, tq), 1)) // CHUNK + 1) * CHUNK

    def score_body(j, carry):
        start = pl.multiple_of(j * kb, kb)
        ki_blk = ki_ref[pl.ds(start, kb), :].astype(BF16)
        s = lax.dot_general(qi, ki_blk, (((1,), (1,)), ((), ())), preferred_element_type=F32)
        parts = []
        for c in range(n_sub):
            sc = jnp.zeros((tq, LANES), F32)
            for h in range(n_hi):
                sh = s[h * tq:(h + 1) * tq, c * LANES:(c + 1) * LANES]
                sc = sc + wb_s[h] * jnp.maximum(sh, 0.0)
            sc = jnp.where(sc == 0.0, 0.0, sc)
            parts.append(sc)
            kpos = start + c * LANES + lane_pos
            key_s[j, :, c * LANES:(c + 1) * LANES] = jnp.where(kpos < limit, sc, -jnp.inf)
        sc_t = jnp.concatenate(parts, axis=1).T
        kpos_t = start + lax.broadcasted_iota(I32, (kb, tq), 0)
        keyt_s[j] = jnp.where(kpos_t < limit_t, sc_t, -jnp.inf)
        return carry

    lax.fori_loop(0, nkb, score_body, 0)

    kf = np.float32(topk)
    searched = limit_t > topk

    def count_keys(pred):
        def body(j, cnt):
            hit = jnp.where(pred(keyt_s[j], j), 1.0, 0.0)
            return cnt + jnp.sum(hit.reshape(kb // SUBLANES, SUBLANES, tq), axis=0)
        cnt = lax.fori_loop(0, nkb, body, jnp.zeros((SUBLANES, tq), F32))
        return jnp.sum(cnt, axis=0, keepdims=True)

    thr0 = jnp.where(count_keys(lambda s, j: s >= 0.0) >= kf, 0, INT_MIN).astype(I32)

    def thr_body(it, thr):
        cand = thr | jnp.left_shift(jnp.int32(1), 30 - it)
        cand_f = _key_to_float(cand)
        return jnp.where(count_keys(lambda s, j: s >= cand_f) >= kf, cand, thr)

    thr = _key_to_float(lax.fori_loop(0, 31, thr_body, thr0))
    thr = jnp.where(searched, thr, -jnp.inf)
    n_ge = count_keys(lambda s, j: s >= thr)
    need = kf - count_keys(lambda s, j: s > thr)
    xrow_s[...] = jnp.where(searched, seq, -1).astype(I32)
    tie_flag = jnp.max(jnp.where(searched & (n_ge > kf), 1.0, 0.0), axis=(0, 1), keepdims=True)

    @pl.when(tie_flag[0, 0] > 0.0)
    def _():
        nbits = max(int(seq - 1).bit_length(), 1)
        row_pos = lax.broadcasted_iota(I32, (kb, tq), 0)

        def x_body(it, xcut):
            cand = xcut | jnp.left_shift(jnp.int32(1), nbits - 1 - it)
            cnt = count_keys(lambda s, j: (s == thr) & (j * kb + row_pos < cand))
            return jnp.where(cnt < need, cand, xcut)

        xcut = lax.fori_loop(0, nbits, x_body, jnp.zeros((1, tq), I32))
        xrow_s[...] = jnp.where(searched, xcut, -1)

    def to_col(row_f32):
        return jnp.broadcast_to(row_f32, (LANES, tq)).T

    thr_s[...] = to_col(thr)
    x_s[...] = to_col(xrow_s[...].astype(F32)).astype(I32)
    rb = min(tq, 128)
    n_rc = tq // rb
    lane_rb = lax.broadcasted_iota(I32, (rb, LANES), 1)

    for g in range(N_KV):
        m_s[g][...] = jnp.full((rows_g, LANES), NEG_BIG, F32)
        acc_s[g][...] = jnp.zeros((rows_g, 2 * LANES), F32)
    ones_v = jnp.ones((kb, LANES), BF16)

    def attn_body(j, carry):
        start = pl.multiple_of(j * kb, kb)
        bias = []
        for rc in range(n_rc):
            thr_c = thr_s[rc * rb:(rc + 1) * rb, :]
            xcut_c = x_s[rc * rb:(rc + 1) * rb, :]
            parts = []
            for c in range(n_sub):
                kc = key_s[j, rc * rb:(rc + 1) * rb, c * LANES:(c + 1) * LANES]
                kpos = start + c * LANES + lane_rb
                sel = (kc > thr_c) | ((kc == thr_c) & (kpos <= xcut_c))
                parts.append(jnp.where(sel, 0.0, NEG_BIG))
            bias.append(jnp.concatenate(parts, axis=1))
        for g in range(N_KV):
            qg = q_ref[g * HEADS_PER_KV:(g + 1) * HEADS_PER_KV].reshape(rows_g, LANES)
            kg = k_ref[g, pl.ds(start, kb), :]
            vg = jnp.concatenate([v_ref[g, pl.ds(start, kb), :], ones_v], axis=1)
            lg_all = lax.dot_general(qg, kg, (((1,), (1,)), ((), ())), preferred_element_type=F32)
            m_prev_all = m_s[g][...]
            p_parts, a_parts, m_parts = [], [], []
            for r in range(HEADS_PER_KV):
                for rc in range(n_rc):
                    r0 = r * tq + rc * rb
                    lg = lg_all[r0:r0 + rb] + bias[rc]
                    m_prev = m_prev_all[r0:r0 + rb]
                    m_new = jnp.maximum(m_prev, jnp.max(lg, axis=1, keepdims=True))
                    p = jnp.exp2(lg - jnp.concatenate([m_new] * n_sub, axis=1))
                    p_parts.append(p.astype(BF16))
                    a_parts.append(jnp.exp2(m_prev - m_new))
                    m_parts.append(m_new)
            p_all = jnp.concatenate(p_parts, axis=0)
            alpha_all = jnp.concatenate(a_parts, axis=0)
            pv = jnp.dot(p_all, vg, preferred_element_type=F32)
            acc_s[g][...] = jnp.concatenate([alpha_all, alpha_all], axis=1) * acc_s[g][...] + pv
            m_s[g][...] = jnp.concatenate(m_parts, axis=0)
        return carry

    lax.fori_loop(0, nkb, attn_body, 0)

    for g in range(N_KV):
        acc = acc_s[g][...]
        og = acc[:, :LANES] / acc[:, LANES:]
        for r in range(HEADS_PER_KV):
            col = (g * HEADS_PER_KV + r) * LANES
            y = og[r * tq:(r + 1) * tq] * gate_ref[:, col:col + LANES].astype(F32)
            o_ref[:, col:col + LANES] = y.astype(o_ref.dtype)


def _attention(qh, kh, vh, qih, kiwi, sg, batch, seq, *, tq):
    n = batch * seq
    topk = min(TOPK_MAX, seq // 4)
    tq = min(tq, seq)
    nqt = seq // tq
    attn_w = N_HEADS * HEAD_DIM
    rows_g = HEADS_PER_KV * tq
    return pl.pallas_call(
        functools.partial(_attn_kernel, tq=tq, topk=topk, seq=seq),
        out_shape=jax.ShapeDtypeStruct((n, attn_w), BF16),
        grid=(batch, nqt),
        in_specs=[
            pl.BlockSpec((N_HEADS, tq, LANES), lambda b, t: (0, b * nqt + t, 0)),
            pl.BlockSpec((N_KV, seq, LANES), lambda b, t: (0, b, 0)),
            pl.BlockSpec((N_KV, seq, LANES), lambda b, t: (0, b, 0)),
            pl.BlockSpec((IDX_HEADS, tq, LANES), lambda b, t: (0, b * nqt + t, 0)),
            pl.BlockSpec((None, seq, LANES), lambda b, t: (0, b, 0)),
            pl.BlockSpec((None, tq, LANES), lambda b, t: (0, b * nqt + t, 0)),
            pl.BlockSpec((tq, attn_w), lambda b, t: (b * nqt + t, 0)),
        ],
        out_specs=pl.BlockSpec((tq, attn_w), lambda b, t: (b * nqt + t, 0)),
        scratch_shapes=[
            pltpu.VMEM((IDX_HEADS, tq, LANES), F32),
            pltpu.VMEM((nqt, tq, tq), F32),
            pltpu.VMEM((nqt, tq, tq), F32),
            pltpu.VMEM((tq, LANES), I32),
            pltpu.VMEM((tq, LANES), F32),
            pltpu.VMEM((1, tq), I32),
        ] + [pltpu.VMEM((rows_g, LANES), F32)] * N_KV + [pltpu.VMEM((rows_g, 2 * LANES), F32)] * N_KV,
        compiler_params=_cparams("parallel", "arbitrary"),
        name="sparse_attention",
    )(qh, kh, vh, qih, kiwi, kiwi, sg)


def _pool_kernel(p_ref, gate_ref, w_ref, scale_ref, o_ref, buf_a, buf_b, *, seq):
    pad = 16
    g = pl.program_id(1)
    p = p_ref[...]
    zeros = jnp.zeros((pad, p.shape[1]), F32)
    buf_a[0:pad, :] = zeros
    buf_b[0:pad, :] = zeros
    buf_a[pad:pad + seq, :] = p
    s2 = p + buf_a[pad - 1:pad - 1 + seq, :]
    buf_b[pad:pad + seq, :] = s2
    s4 = s2 + buf_b[pad - 2:pad - 2 + seq, :]
    buf_a[pad:pad + seq, :] = s4
    s8 = s4 + buf_a[pad - 4:pad - 4 + seq, :]
    buf_b[pad:pad + seq, :] = s8
    s16 = s8 + buf_b[pad - 8:pad - 8 + seq, :]
    t1 = (lax.broadcasted_iota(I32, p.shape, 0) + 1).astype(F32)
    win = jnp.where(g == 0, 2.0, jnp.where(g == 1, 4.0, jnp.where(g == 2, 8.0, 16.0))).astype(F32)
    total = jnp.where(g == 0, s2, jnp.where(g == 1, s4, jnp.where(g == 2, s8, s16)))
    mean = total / jnp.minimum(t1, win)
    diff = (mean - p).astype(BF16)
    y = jnp.dot(diff, w_ref[...], preferred_element_type=F32)
    o_ref[...] = (y * scale_ref[...] * gate_ref[...].astype(F32)).astype(o_ref.dtype)


def _pool_mixer(px, sg, pool_w, pool_scale, batch, seq):
    n = batch * seq
    width = pool_scale.shape[-1]
    ng = len(POOL_WINDOWS)
    cg = width // ng
    assert POOL_WINDOWS == (2, 4, 8, 16)
    return pl.pallas_call(
        functools.partial(_pool_kernel, seq=seq),
        out_shape=jax.ShapeDtypeStruct((n, width), BF16),
        grid=(batch, ng),
        in_specs=[
            pl.BlockSpec((seq, cg), lambda b, g: (b, g)),
            pl.BlockSpec((seq, cg), lambda b, g: (b, ng + g)),
            pl.BlockSpec((None, cg, cg), lambda b, g: (g, 0, 0)),
            pl.BlockSpec((1, cg), lambda b, g: (0, g)),
        ],
        out_specs=pl.BlockSpec((seq, cg), lambda b, g: (b, g)),
        scratch_shapes=[pltpu.VMEM((seq + 16, cg), F32), pltpu.VMEM((seq + 16, cg), F32)],
        compiler_params=_cparams("parallel", "arbitrary"),
        name="pool_mixer",
    )(px, sg, pool_w, pool_scale.reshape(1, width))


def _lru_kernel(x_ref, gr_ref, gate_ref, cw_ref, cb_ref, wa_ref, ba_ref, wx_ref, bx_ref, lam_ref,
                o_ref, a_s, b_s, *, seq, ct):
    x = x_ref[...]
    row = lax.broadcasted_iota(I32, (seq, ct), 0)
    xc = jnp.broadcast_to(cb_ref[...], (seq, ct))
    for tap in range(CONV_WIDTH):
        d = CONV_WIDTH - 1 - tap
        x_d = x if d == 0 else jnp.where(row >= d, pltpu.roll(x, d, 0), 0.0)
        xc = xc + x_d * cw_ref[tap:tap + 1, :]
    xcb = xc.astype(BF16)
    nb = ct // LRU_BLOCK
    r_parts, i_parts = [], []
    for blk in range(nb):
        xb = xcb[:, blk * LRU_BLOCK:(blk + 1) * LRU_BLOCK]
        r_parts.append(jnp.dot(xb, wa_ref[blk], preferred_element_type=F32))
        i_parts.append(jnp.dot(xb, wx_ref[blk], preferred_element_type=F32))
    r = _sigmoid(jnp.concatenate(r_parts, axis=1) + ba_ref[...])
    gi = _sigmoid(jnp.concatenate(i_parts, axis=1) + bx_ref[...])
    lam = lam_ref[...]
    softplus_neg_lam = jnp.log(1.0 + jnp.exp(-lam))
    log_a = -LRU_C * r * softplus_neg_lam
    a = jnp.exp(log_a)
    one_m_a2 = 1.0 - a * a
    root = jnp.where(one_m_a2 > 0.0, one_m_a2 * lax.rsqrt(one_m_a2), 0.0)
    b = root * (gi * xc)

    tiles = (seq // SUBLANES, SUBLANES, ct)
    a = a.reshape(tiles)
    b = b.reshape(tiles)
    sub = lax.broadcasted_iota(I32, tiles, 1)
    for d in (1, 2, 4):
        keep = sub >= d
        a_sh = jnp.where(keep, pltpu.roll(a, d, 1), 1.0)
        b_sh = jnp.where(keep, pltpu.roll(b, d, 1), 0.0)
        b = a * b_sh + b
        a = a * a_sh
    a_s[...] = a.reshape(seq, ct)
    b_s[...] = b.reshape(seq, ct)

    def body(t, carry):
        r0 = pl.multiple_of(t * SUBLANES, SUBLANES)
        h = a_s[pl.ds(r0, SUBLANES), :] * carry + b_s[pl.ds(r0, SUBLANES), :]
        b_s[pl.ds(r0, SUBLANES), :] = h
        return jnp.broadcast_to(h[SUBLANES - 1:SUBLANES, :], (SUBLANES, ct))

    lax.fori_loop(0, seq // SUBLANES, body, jnp.zeros((SUBLANES, ct), F32), unroll=8)
    h = b_s[...]
    o_ref[...] = (h * gr_ref[...].astype(F32) * gate_ref[...].astype(F32)).astype(o_ref.dtype)


def _lru_mixer(px, gg, sg, conv_w, conv_b, wa, ba, wx, bx, lam, batch, seq, *, ct=256):
    n = batch * seq
    width = conv_b.shape[-1]
    nct = width // ct
    nb = ct // LRU_BLOCK
    row = lambda a: a.reshape(1, width)
    rspec = pl.BlockSpec((1, ct), lambda b, j: (0, j))
    return pl.pallas_call(
        functools.partial(_lru_kernel, seq=seq, ct=ct),
        out_shape=jax.ShapeDtypeStruct((n, width), BF16),
        grid=(batch, nct),
        in_specs=[
            pl.BlockSpec((seq, ct), lambda b, j: (b, nct + j)),
            pl.BlockSpec((seq, ct), lambda b, j: (b, j)),
            pl.BlockSpec((seq, ct), lambda b, j: (b, 2 * nct + j)),
            pl.BlockSpec((CONV_WIDTH, ct), lambda b, j: (0, j)),
            rspec,
            pl.BlockSpec((nb, LRU_BLOCK, LRU_BLOCK), lambda b, j: (j, 0, 0)),
            rspec,
            pl.BlockSpec((nb, LRU_BLOCK, LRU_BLOCK), lambda b, j: (j, 0, 0)),
            rspec,
            rspec,
        ],
        out_specs=pl.BlockSpec((seq, ct), lambda b, j: (b, j)),
        scratch_shapes=[pltpu.VMEM((seq, ct), F32),
                        pltpu.VMEM((seq, ct), F32)],
        compiler_params=_cparams("parallel", "arbitrary"),
        name="rglru_mixer",
    )(px, gg, sg, conv_w, row(conv_b), wa, row(ba), wx, row(bx), row(lam))


def _out_proj_ln_kernel(ya_ref, yb_ref, yc_ref, w_ref, x_ref, g_ref, b_ref, o_ref, ob_ref, *, alpha):
    acc = jnp.dot(ya_ref[...], w_ref[0], preferred_element_type=F32)
    acc = acc + jnp.dot(yb_ref[...], w_ref[1], preferred_element_type=F32)
    acc = acc + jnp.dot(yc_ref[...], w_ref[2], preferred_element_type=F32)
    out = _layer_norm(alpha * x_ref[...] + acc, g_ref[...], b_ref[...])
    o_ref[...] = out
    ob_ref[...] = out.astype(BF16)


def _out_proj_ln(ya, yb, yc, w3, x, g, b, alpha, *, tm=256):
    n, width = ya.shape
    d = w3.shape[-1]
    tm = min(tm, n)
    aspec = pl.BlockSpec((tm, width), lambda i: (i, 0))
    tok = pl.BlockSpec((tm, d), lambda i: (i, 0))
    row = pl.BlockSpec((1, d), lambda i: (0, 0))
    return pl.pallas_call(
        functools.partial(_out_proj_ln_kernel, alpha=alpha),
        out_shape=(jax.ShapeDtypeStruct((n, d), F32), jax.ShapeDtypeStruct((n, d), BF16)),
        grid=(n // tm,),
        in_specs=[aspec, aspec, aspec,
                  pl.BlockSpec((3, width, d), lambda i: (0, 0, 0), pipeline_mode=pl.Buffered(1)),
                  tok, row, row],
        out_specs=(tok, tok),
        compiler_params=_cparams("parallel"),
        name="out_proj_ln",
    )(ya, yb, yc, w3, x, g.reshape(1, d), b.reshape(1, d))


def _ffn_up_kernel(x_ref, wg_ref, wu_ref, o_ref):
    xb = x_ref[...]
    hg = jnp.dot(xb, wg_ref[...], preferred_element_type=F32)
    hu = jnp.dot(xb, wu_ref[...], preferred_element_type=F32)
    o_ref[...] = (hg * _sigmoid(hg) * hu).astype(o_ref.dtype)


def _ffn_down_ln_kernel(h_ref, w_ref, x_ref, g_ref, b_ref, o_ref, ob_ref, *, alpha):
    y = jnp.dot(h_ref[...], w_ref[...], preferred_element_type=F32)
    out = _layer_norm(alpha * x_ref[...] + y, g_ref[...], b_ref[...])
    o_ref[...] = out
    ob_ref[...] = out.astype(BF16)


def _ffn_dense(xb, x, wg, wu, wd, g, b, alpha, *, tm_up=1024, tn_up=512, tm_down=256):
    n, d = x.shape
    ff = wg.shape[1]
    ff_real = wd.shape[0]
    assert ff_real % LANES == 0
    tm_up, tm_down = min(tm_up, n), min(tm_down, n)
    h = pl.pallas_call(
        _ffn_up_kernel,
        out_shape=jax.ShapeDtypeStruct((n, ff), BF16),
        grid=(n // tm_up, ff // tn_up),
        in_specs=[pl.BlockSpec((tm_up, d), lambda i, j: (i, 0)),
                  pl.BlockSpec((d, tn_up), lambda i, j: (0, j)),
                  pl.BlockSpec((d, tn_up), lambda i, j: (0, j))],
        out_specs=pl.BlockSpec((tm_up, tn_up), lambda i, j: (i, j)),
        compiler_params=_cparams("parallel", "arbitrary"),
        name="ffn_up",
    )(xb, wg, wu)
    tok = pl.BlockSpec((tm_down, d), lambda i: (i, 0))
    row = pl.BlockSpec((1, d), lambda i: (0, 0))
    return pl.pallas_call(
        functools.partial(_ffn_down_ln_kernel, alpha=alpha),
        out_shape=(jax.ShapeDtypeStruct((n, d), F32), jax.ShapeDtypeStruct((n, d), BF16)),
        grid=(n // tm_down,),
        in_specs=[pl.BlockSpec((tm_down, ff_real), lambda i: (i, 0)),
                  pl.BlockSpec((ff_real, d), lambda i: (0, 0), pipeline_mode=pl.Buffered(1)),
                  tok, row, row],
        out_specs=(tok, tok),
        compiler_params=_cparams("parallel"),
        name="ffn_down_ln",
    )(h, wd, x, g.reshape(1, d), b.reshape(1, d))


def _router_kernel(x_ref, w_ref, ids_ref, wts_ref):
    logits = jnp.dot(x_ref[...], w_ref[...], preferred_element_type=F32, precision=lax.Precision.HIGHEST)
    lane_i = lax.broadcasted_iota(I32, logits.shape, 1)
    lane = lane_i.astype(F32)
    logits = jnp.where(lane_i < N_EXPERTS, logits, -jnp.inf)
    m1 = jnp.max(logits, axis=1, keepdims=True)
    i1 = jnp.min(jnp.where(logits == m1, lane, float(LANES)), axis=1, keepdims=True)
    rest = jnp.where(lane == i1, -jnp.inf, logits)
    m2 = jnp.max(rest, axis=1, keepdims=True)
    i2 = jnp.min(jnp.where(rest == m2, lane, float(LANES)), axis=1, keepdims=True)
    e2 = jnp.exp(m2 - m1)
    w1 = 1.0 / (1.0 + e2)
    w2 = e2 / (1.0 + e2)
    ids_ref[...] = jnp.where(lane_i == 0, i1, jnp.where(lane_i == 1, i2, 0.0)).astype(I32)
    wts_ref[...] = jnp.where(lane_i == 0, w1, jnp.where(lane_i == 1, w2, 0.0))


def _router(x, w_router, *, tm=512):
    n, d = x.shape
    wpad = jnp.zeros((d, LANES), F32).at[:, :N_EXPERTS].set(w_router.astype(F32))
    tm = min(tm, n)
    tok = pl.BlockSpec((tm, LANES), lambda i: (i, 0))
    return pl.pallas_call(
        _router_kernel,
        out_shape=(jax.ShapeDtypeStruct((n, LANES), I32), jax.ShapeDtypeStruct((n, LANES), F32)),
        grid=(n // tm,),
        in_specs=[pl.BlockSpec((tm, d), lambda i: (i, 0)), pl.BlockSpec((d, LANES), lambda i: (0, 0))],
        out_specs=(tok, tok),
        compiler_params=_cparams("parallel"),
        name="moe_router",
    )(x, wpad)


def _gather_rows_kernel(tok_ref, x_hbm, o_ref, buf, sem, *, tm):
    i = pl.program_id(0)

    def issue(tile, slot):
        base = tile * tm

        def body(r, carry):
            t = tok_ref[base + r]
            pltpu.make_async_copy(x_hbm.at[pl.ds(t, 1), :], buf.at[slot, pl.ds(r, 1), :], sem.at[slot]).start()
            return carry

        lax.fori_loop(0, tm, body, 0, unroll=8)

    @pl.when(i == 0)
    def _():
        issue(0, 0)

    @pl.when(i + 1 < pl.num_programs(0))
    def _():
        issue(i + 1, (i + 1) % 2)

    slot = i % 2
    pltpu.make_async_copy(x_hbm.at[pl.ds(0, tm), :], buf.at[slot], sem.at[slot]).wait()
    o_ref[...] = buf[slot].astype(o_ref.dtype)


def _gather_rows(x, row_tok, *, tm=256):
    n, d = x.shape
    rows = row_tok.shape[0]
    return pl.pallas_call(
        functools.partial(_gather_rows_kernel, tm=tm),
        out_shape=jax.ShapeDtypeStruct((rows, d), BF16),
        grid_spec=pltpu.PrefetchScalarGridSpec(
            num_scalar_prefetch=1,
            grid=(rows // tm,),
            in_specs=[pl.BlockSpec(memory_space=pl.ANY)],
            out_specs=pl.BlockSpec((tm, d), lambda i, tok: (i, 0)),
            scratch_shapes=[pltpu.VMEM((2, tm, d), F32), pltpu.SemaphoreType.DMA((2,))],
        ),
        compiler_params=_cparams("arbitrary"),
        name="moe_gather_rows",
    )(row_tok, x)


def _expert_changed(te_ref, i):
    return (i == 0) | (te_ref[i] != te_ref[jnp.maximum(i - 1, 0)])


def _stream_expert_weights(j, i, n_j, col_tile, te_ref, run_ref, nxt_ref, meta_ref, w_hbm, stage, work, sem):
    @pl.when(_expert_changed(te_ref, i))
    def _():
        k = j * meta_ref[0] + run_ref[i]
        slot = k % 2

        def copies(e, jj, s):
            col = pl.ds(pl.multiple_of(jj * col_tile, col_tile), col_tile)
            return [pltpu.make_async_copy(w.at[e, :, col], st.at[s], sem.at[s, a])
                    for a, (w, st) in enumerate(zip(w_hbm, stage))]

        @pl.when(k == 0)
        def _():
            for c in copies(te_ref[i], j, slot):
                c.start()

        has_next_run = nxt_ref[i] >= 0
        e_next = jnp.where(has_next_run, nxt_ref[i], meta_ref[1])
        j_next = jnp.where(has_next_run, j, j + 1)

        @pl.when(j_next < n_j)
        def _():
            for c in copies(e_next, j_next, 1 - slot):
                c.start()

        for c in copies(te_ref[i], j, slot):
            c.wait()
        for st, wk in zip(stage, work):
            wk[...] = st[slot].astype(BF16)


def _moe_up_kernel(te_ref, nv_ref, run_ref, nxt_ref, meta_ref, x_ref, wg_hbm, wu_hbm, o_ref,
                   stage_g, stage_u, wg_s, wu_s, sem, *, tf):
    j, i = pl.program_id(0), pl.program_id(1)
    _stream_expert_weights(j, i, pl.num_programs(0), tf, te_ref, run_ref, nxt_ref, meta_ref,
                           (wg_hbm, wu_hbm), (stage_g, stage_u), (wg_s, wu_s), sem)

    @pl.when(i < nv_ref[0])
    def _():
        xb = x_ref[...]
        hg = jnp.dot(xb, wg_s[...], preferred_element_type=F32)
        hu = jnp.dot(xb, wu_s[...], preferred_element_type=F32)
        o_ref[...] = (hg * _sigmoid(hg) * hu).astype(o_ref.dtype)

    @pl.when(i >= nv_ref[0])
    def _():
        o_ref[...] = jnp.zeros(o_ref.shape, o_ref.dtype)


def _moe_up(xs, wg, wu, sched, *, tm, tf=1024):
    rows, d = xs.shape
    ff = wg.shape[-1]
    return pl.pallas_call(
        functools.partial(_moe_up_kernel, tf=tf),
        out_shape=jax.ShapeDtypeStruct((rows, ff), BF16),
        grid_spec=pltpu.PrefetchScalarGridSpec(
            num_scalar_prefetch=len(sched),
            grid=(ff // tf, rows // tm),
            in_specs=[pl.BlockSpec((tm, d), lambda j, i, *_: (i, 0)),
                      pl.BlockSpec(memory_space=pl.ANY),
                      pl.BlockSpec(memory_space=pl.ANY)],
            out_specs=pl.BlockSpec((tm, tf), lambda j, i, *_: (i, j)),
            scratch_shapes=[pltpu.VMEM((2, d, tf), F32), pltpu.VMEM((2, d, tf), F32),
                            pltpu.VMEM((d, tf), BF16), pltpu.VMEM((d, tf), BF16),
                            pltpu.SemaphoreType.DMA((2, 2))],
        ),
        compiler_params=_cparams("arbitrary", "arbitrary"),
        name="moe_up",
    )(*sched, xs, wg, wu)


def _moe_down_kernel(te_ref, nv_ref, run_ref, nxt_ref, meta_ref, h_ref, wd_hbm, o_ref, stage_d, wd_s, sem, *, tn):
    j, i = pl.program_id(0), pl.program_id(1)
    _stream_expert_weights(j, i, pl.num_programs(0), tn, te_ref, run_ref, nxt_ref, meta_ref,
                           (wd_hbm,), (stage_d,), (wd_s,), sem)

    @pl.when(i < nv_ref[0])
    def _():
        o_ref[...] = jnp.dot(h_ref[...], wd_s[...], preferred_element_type=F32)

    @pl.when(i >= nv_ref[0])
    def _():
        o_ref[...] = jnp.zeros(o_ref.shape, o_ref.dtype)


def _moe_down(h, wd, sched, *, tm, tn=512):
    rows, ff = h.shape
    d = wd.shape[-1]
    return pl.pallas_call(
        functools.partial(_moe_down_kernel, tn=tn),
        out_shape=jax.ShapeDtypeStruct((rows, d), F32),
        grid_spec=pltpu.PrefetchScalarGridSpec(
            num_scalar_prefetch=len(sched),
            grid=(d // tn, rows // tm),
            in_specs=[pl.BlockSpec((tm, ff), lambda j, i, *_: (i, 0)),
                      pl.BlockSpec(memory_space=pl.ANY)],
            out_specs=pl.BlockSpec((tm, tn), lambda j, i, *_: (i, j)),
            scratch_shapes=[pltpu.VMEM((2, ff, tn), F32), pltpu.VMEM((ff, tn), BF16),
                            pltpu.SemaphoreType.DMA((2, 1))],
        ),
        compiler_params=_cparams("arbitrary", "arbitrary"),
        name="moe_down",
    )(*sched, h, wd)


def _combine_ln_kernel(pos_ref, y_hbm, x_ref, wts_ref, g_ref, b_ref, o_ref, buf, sem, *, tm, alpha):
    i = pl.program_id(0)

    def issue(tile, slot):
        base = tile * tm

        def body(r, carry):
            for k in range(TOP_K):
                row = pos_ref[(base + r) * TOP_K + k]
                pltpu.make_async_copy(y_hbm.at[pl.ds(row, 1), :], buf.at[slot, k, pl.ds(r, 1), :],
                                      sem.at[slot]).start()
            return carry

        lax.fori_loop(0, tm, body, 0, unroll=8)

    @pl.when(i == 0)
    def _():
        issue(0, 0)

    @pl.when(i + 1 < pl.num_programs(0))
    def _():
        issue(i + 1, (i + 1) % 2)

    slot = i % 2
    for k in range(TOP_K):
        pltpu.make_async_copy(y_hbm.at[pl.ds(0, tm), :], buf.at[slot, k], sem.at[slot]).wait()
    wts = wts_ref[...]
    y = buf[slot, 0] * wts[:, 0:1] + buf[slot, 1] * wts[:, 1:2]
    o_ref[...] = _layer_norm(alpha * x_ref[...] + y, g_ref[...], b_ref[...])


def _combine_ln(yrows, pos, x, wts, g, b, alpha, *, tm=128):
    n, d = x.shape
    tm = min(tm, n)
    return pl.pallas_call(
        functools.partial(_combine_ln_kernel, tm=tm, alpha=alpha),
        out_shape=jax.ShapeDtypeStruct((n, d), F32),
        grid_spec=pltpu.PrefetchScalarGridSpec(
            num_scalar_prefetch=1,
            grid=(n // tm,),
            in_specs=[pl.BlockSpec(memory_space=pl.ANY),
                      pl.BlockSpec((tm, d), lambda i, pos: (i, 0)),
                      pl.BlockSpec((tm, LANES), lambda i, pos: (i, 0)),
                      pl.BlockSpec((1, d), lambda i, pos: (0, 0)),
                      pl.BlockSpec((1, d), lambda i, pos: (0, 0))],
            out_specs=pl.BlockSpec((tm, d), lambda i, pos: (i, 0)),
            scratch_shapes=[pltpu.VMEM((2, TOP_K, tm, d), F32), pltpu.SemaphoreType.DMA((2,))],
        ),
        compiler_params=_cparams("arbitrary"),
        name="moe_combine_ln",
    )(pos, yrows, x, wts, g.reshape(1, d), b.reshape(1, d))


def _moe_block(x, w_router, wg, wu, wd, g, b, alpha, *, tm=256):
    n, d = x.shape
    ids, wts = _router(x, w_router)
    e_flat = ids[:, :TOP_K].reshape(-1)
    n_assign = n * TOP_K
    onehot = (e_flat[:, None] == jnp.arange(N_EXPERTS, dtype=I32)[None, :]).astype(I32)
    rank = jnp.sum((jnp.cumsum(onehot, axis=0) - onehot) * onehot, axis=1)
    counts = jnp.sum(onehot, axis=0)
    padded = (counts + tm - 1) // tm * tm
    end_padded = jnp.cumsum(padded)
    start_padded = end_padded - padded
    dest = (start_padded[e_flat] + rank).astype(I32)
    rows = n_assign + N_EXPERTS * tm
    n_tiles = rows // tm
    flat_tok = jnp.arange(n_assign, dtype=I32) // TOP_K
    row_tok = jnp.zeros((rows,), I32).at[dest].set(flat_tok)
    tile_start = jnp.arange(n_tiles, dtype=I32) * tm
    tile_e = jnp.minimum(jnp.sum((tile_start[:, None] >= end_padded[None, :]).astype(I32), axis=1),
                         N_EXPERTS - 1).astype(I32)
    n_valid = (end_padded[-1:] // tm).astype(I32)
    tile_ix = jnp.arange(n_tiles, dtype=I32)
    tile_e = jnp.where(tile_ix < n_valid[0], tile_e, tile_e[jnp.maximum(n_valid[0] - 1, 0)])
    is_start = jnp.concatenate([jnp.ones((1,), I32), (tile_e[1:] != tile_e[:-1]).astype(I32)])
    run_id = (jnp.cumsum(is_start) - 1).astype(I32)
    larger = jnp.where(tile_e[None, :] > tile_e[:, None], tile_e[None, :], N_EXPERTS)
    nxt_e = jnp.min(larger, axis=1)
    nxt_e = jnp.where(nxt_e < N_EXPERTS, nxt_e, -1).astype(I32)
    meta = jnp.stack([run_id[-1] + 1, tile_e[0]]).astype(I32)
    sched = (tile_e, n_valid, run_id, nxt_e, meta)
    xs = _gather_rows(x, row_tok, tm=tm)
    h = _moe_up(xs, wg, wu, sched, tm=tm)
    yrows = _moe_down(h, wd, sched, tm=tm)
    return _combine_ln(yrows, dest, x, wts, g, b, alpha)


def _pack_w_in(w_in, d_model):
    attn_w = N_HEADS * HEAD_DIM
    kv_w = N_KV * HEAD_DIM
    idx_w = IDX_HEADS * IDX_DIM
    o = np.cumsum([0, attn_w, kv_w, kv_w, idx_w, IDX_DIM, IDX_HEADS, d_model, d_model, d_model])
    kiwi = jnp.pad(w_in[:, o[4]:o[6]], ((0, 0), (0, LANES - IDX_DIM - IDX_HEADS)))
    parts = [("q", w_in[:, o[0]:o[1]]), ("px", w_in[:, o[6]:o[8]]), ("gr", w_in[:, o[8]:o[9]]),
             ("gates", w_in[:, o[9]:]), ("qi", w_in[:, o[3]:o[4]]), ("k", w_in[:, o[1]:o[2]]),
             ("v", w_in[:, o[2]:o[3]]), ("kiwi", kiwi)]
    cols, off = {}, 0
    for name, w in parts:
        cols[name] = (off, w.shape[1])
        off += w.shape[1]
    return jnp.concatenate([w for _, w in parts], axis=1).astype(BF16), cols


def _mixer(x_mm, tabs, w_in, pool_w, pool_scale, conv_w, conv_b, wa, ba, wx, bx, lam,
           batch, seq, d_model, tq):
    cos_a, sin_a, cos_i, sin_ia, sin_ib, cos_q, sin_q, cos_iq = tabs
    wcat, cols = _pack_w_in(w_in, d_model)
    rope_a = ((cos_a, sin_a), (HEAD_DIM // 2,))
    rope_i = (LANES - IDX_DIM // 2, IDX_DIM // 2)
    qh = _proj_heads(x_mm, wcat, cols["q"], (cos_q, sin_q), rope_a[1], BF16)
    kh = _proj_heads(x_mm, wcat, cols["k"], *rope_a, BF16)
    vh = _proj_heads(x_mm, wcat, cols["v"], (), (), BF16)
    qih = _proj_heads(x_mm, wcat, cols["qi"], (cos_iq, sin_ia, sin_ib), rope_i, BF16, head_width=IDX_DIM)
    kiwi = _proj_heads(x_mm, wcat, cols["kiwi"], (cos_i, sin_ia, sin_ib), rope_i, F32)
    px = _proj_plain(x_mm, wcat, cols["px"], None, F32)
    gg = _proj_plain(x_mm, wcat, cols["gr"], "gelu", BF16)
    sg = _proj_plain(x_mm, wcat, cols["gates"], "sigmoid", BF16)
    ya = _attention(qh, kh, vh, qih, kiwi, sg, batch, seq, tq=tq)
    yb = _pool_mixer(px, sg, pool_w.astype(BF16), pool_scale, batch, seq)
    yc = _lru_mixer(px, gg, sg, conv_w, conv_b, wa.astype(BF16), ba, wx.astype(BF16), bx, lam, batch, seq)
    return ya, yb, yc


def _pad_ff(w, axis, mult):
    ff = w.shape[axis]
    padn = (-ff) % mult
    if padn == 0:
        return w
    widths = [(0, 0)] * w.ndim
    widths[axis] = (0, padn)
    return jnp.pad(w, widths)


def kernel(x, positions, mix_w_in, mix_w_out, pool_w, pool_scale, conv_w, conv_b, lru_wa, lru_ba, lru_wx, lru_bx, lru_lam, ln_mix_g, ln_mix_b, ln_ffn_g, ln_ffn_b, dense_w_gate, dense_w_up, dense_w_down, moe_router, moe_w_gate, moe_w_up, moe_w_down):
    batch, seq, d_model = x.shape
    depth = mix_w_in.shape[0]
    alpha = np.float32((2 * depth) ** 0.25)
    n = batch * seq
    tabs = _rope_tables(positions)
    xf = x.reshape(n, d_model)
    x_mm = xf
    for layer in range(depth):
        ya, yb, yc = _mixer(x_mm, tabs, mix_w_in[layer], pool_w[layer], pool_scale[layer],
                            conv_w[layer], conv_b[layer], lru_wa[layer], lru_ba[layer], lru_wx[layer],
                            lru_bx[layer], lru_lam[layer], batch, seq, d_model, tq=256)
        w_out3 = mix_w_out[layer].astype(BF16).reshape(3, -1, d_model)
        xf, xb = _out_proj_ln(ya, yb, yc, w_out3, xf, ln_mix_g[layer], ln_mix_b[layer], alpha)
        j = layer // 2
        if layer % 2 == 0:
            wg = _pad_ff(dense_w_gate[j].astype(BF16), 1, 512)
            wu = _pad_ff(dense_w_up[j].astype(BF16), 1, 512)
            wd = dense_w_down[j].astype(BF16)
            xf, x_mm = _ffn_dense(xb, xf, wg, wu, wd, ln_ffn_g[layer], ln_ffn_b[layer], alpha)
        else:
            xf = _moe_block(xf, moe_router[j], moe_w_gate[j], moe_w_up[j], moe_w_down[j],
                            ln_ffn_g[layer], ln_ffn_b[layer], alpha)
            x_mm = xf
    return xf.reshape(batch, seq, d_model)
```

```python
import functools

import jax
import jax.numpy as jnp
import numpy as np
from jax import lax
from jax.experimental import pallas as pl
from jax.experimental.pallas import tpu as pltpu

F32 = jnp.float32
BF16 = jnp.bfloat16
I32 = jnp.int32

LANES = 128
SUBLANES = 8
VMEM_LIMIT = 56 * 1024 * 1024

CHUNK = 64
N_HEADS = 16
HEAD_DIM = 128
N_KV = 4
HEADS_PER_KV = N_HEADS // N_KV
IDX_HEADS = 16
IDX_DIM = 64
TOPK_MAX = 256
ROPE_THETA = 10000.0
POOL_WINDOWS = (2, 4, 8, 16)
LRU_BLOCK = 128
CONV_WIDTH = 4
LRU_C = 8.0
N_EXPERTS = 8
TOP_K = 2
LN_EPS = 1e-5
LN_ROWS = 128
INT_MIN = -2 ** 31
NEG_BIG = -1e30


def _cparams(*sem):
    return pltpu.CompilerParams(dimension_semantics=sem, vmem_limit_bytes=VMEM_LIMIT)


def _sigmoid(x):
    return 0.5 * (1.0 + jnp.tanh(0.5 * x))


def _gelu_tanh(x):
    c = np.float32(np.sqrt(2.0 / np.pi))
    return 0.5 * x * (1.0 + jnp.tanh(c * (x + np.float32(0.044715) * (x * x * x))))


def _layer_norm(y, g, b):
    mu = jnp.mean(y, axis=-1, keepdims=True)
    d = y - mu
    var = jnp.mean(d * d, axis=-1, keepdims=True)
    return d * lax.rsqrt(var + LN_EPS) * g + b


def _rope_tab_kernel(pos_ref, inv_a_ref, inv_i_ref, sgn_a_ref, m_cos_ref, add_cos_ref,
                     m_sa_ref, m_sb_ref, cos_a_ref, sin_a_ref, cos_i_ref, sin_ia_ref, sin_ib_ref,
                     cos_q_ref, sin_q_ref, cos_iq_ref):
    pos = pos_ref[...]
    ang_a = pos * inv_a_ref[...]
    cos_a = jnp.cos(ang_a)
    sin_a = jnp.sin(ang_a) * sgn_a_ref[...]
    cos_a_ref[...] = cos_a
    sin_a_ref[...] = sin_a
    q_scale = np.float32(HEAD_DIM ** -0.5 * np.log2(np.e))
    cos_q_ref[...] = cos_a * q_scale
    sin_q_ref[...] = sin_a * q_scale
    ang_i = pos * inv_i_ref[...]
    s_i = jnp.sin(ang_i)
    cos_iq = jnp.cos(ang_i) * m_cos_ref[...]
    cos_iq_ref[...] = cos_iq
    cos_i_ref[...] = cos_iq + add_cos_ref[...]
    sin_ia_ref[...] = s_i * m_sa_ref[...]
    sin_ib_ref[...] = s_i * m_sb_ref[...]


def _rope_tables(positions):
    n = positions.size
    pos = jnp.broadcast_to(positions.reshape(n, 1).astype(F32), (n, LANES))
    lane = np.arange(LANES)
    inv_a = (ROPE_THETA ** (-jnp.arange(0, HEAD_DIM, 2, dtype=F32) / HEAD_DIM))
    inv_i = (ROPE_THETA ** (-jnp.arange(0, IDX_DIM, 2, dtype=F32) / IDX_DIM))
    inv_a_row = jnp.concatenate([inv_a, inv_a])[None, :]
    inv_i_row = jnp.concatenate([inv_i, inv_i, jnp.zeros((LANES - IDX_DIM,), F32)])[None, :]
    sgn_a = jnp.asarray(np.where(lane < HEAD_DIM // 2, -1.0, 1.0), F32)[None, :]
    m_cos = jnp.asarray((lane < IDX_DIM).astype(np.float32))[None, :]
    wi_scale = (IDX_HEADS ** -0.5) * (IDX_DIM ** -0.5)
    add_cos = jnp.asarray(np.where((lane >= IDX_DIM) & (lane < IDX_DIM + IDX_HEADS), wi_scale, 0.0), F32)[None, :]
    m_sa = jnp.asarray(np.where(lane < IDX_DIM // 2, -1.0, 0.0), F32)[None, :]
    m_sb = jnp.asarray(np.where((lane >= IDX_DIM // 2) & (lane < IDX_DIM), 1.0, 0.0), F32)[None, :]
    tm = min(n, 1024)
    row = pl.BlockSpec((1, LANES), lambda i: (0, 0))
    tok = pl.BlockSpec((tm, LANES), lambda i: (i, 0))
    out = jax.ShapeDtypeStruct((n, LANES), F32)
    return pl.pallas_call(
        _rope_tab_kernel,
        out_shape=(out,) * 8,
        grid=(n // tm,),
        in_specs=[tok] + [row] * 7,
        out_specs=(tok,) * 8,
        compiler_params=_cparams("parallel"),
        name="rope_tables",
    )(pos, inv_a_row, inv_i_row, sgn_a, m_cos, add_cos, m_sa, m_sb)


def _proj_heads_kernel(x_ref, w_ref, *rest, shifts, heads, head_width):
    tabs, o_ref, w_s = rest[:-2], rest[-2], rest[-1]

    @pl.when(pl.program_id(1) == 0)
    def _():
        w_s[...] = w_ref[...].astype(BF16)

    acc = jnp.dot(x_ref[...], w_s[...], preferred_element_type=F32)
    per_slab = LANES // head_width
    for h in range(heads):
        xh = acc[:, (h // per_slab) * LANES:(h // per_slab + 1) * LANES]
        if h % per_slab:
            xh = pltpu.roll(xh, LANES - (h % per_slab) * head_width, 1)
        if tabs:
            y = xh * tabs[0][...]
            for s, t in zip(shifts, tabs[1:]):
                y = y + pltpu.roll(xh, s, 1) * t[...]
        else:
            y = xh
        o_ref[h] = y.astype(o_ref.dtype)


def _proj_heads(x, w, cols, tabs, shifts, out_dtype, *, head_width=LANES, tm=1024, tile_cols=1024):
    n, d = x.shape
    off, width = cols
    tn = min(tile_cols, width)
    hp = tn // head_width
    assert width % tn == 0 and off % tn == 0 and (head_width == LANES or tabs)
    tm = min(tm, n)
    tab_spec = pl.BlockSpec((tm, LANES), lambda j, i: (i, 0))
    return pl.pallas_call(
        functools.partial(_proj_heads_kernel, shifts=shifts, heads=hp, head_width=head_width),
        out_shape=jax.ShapeDtypeStruct((width // head_width, n, LANES), out_dtype),
        grid=(width // tn, n // tm),
        in_specs=[pl.BlockSpec((tm, d), lambda j, i: (i, 0)),
                  pl.BlockSpec((d, tn), lambda j, i: (0, off // tn + j))] + [tab_spec] * len(tabs),
        out_specs=pl.BlockSpec((hp, tm, LANES), lambda j, i: (j, i, 0)),
        scratch_shapes=[pltpu.VMEM((d, tn), BF16)],
        compiler_params=_cparams("arbitrary", "arbitrary"),
        name="proj_heads",
    )(x, w, *tabs)


def _proj_shifted_kernel(x_ref, w_ref, wn_ref, o_ref, w_s, wn_s, *, act, lane_shift):
    @pl.when(pl.program_id(1) == 0)
    def _():
        w_s[...] = w_ref[...].astype(BF16)
        wn_s[...] = jnp.zeros(wn_s.shape, BF16)
        wn_s[:, :lane_shift] = wn_ref[:, :lane_shift].astype(BF16)

    x = x_ref[...]
    res = jnp.dot(x, w_s[...], preferred_element_type=F32)
    res_next = jnp.dot(x, wn_s[...], preferred_element_type=F32)
    n_slab = res.shape[1] // LANES
    rolled = [pltpu.roll(res[:, s * LANES:(s + 1) * LANES], LANES - lane_shift, 1) for s in range(n_slab)]
    rolled.append(pltpu.roll(res_next, LANES - lane_shift, 1))
    from_own = lax.broadcasted_iota(I32, (res.shape[0], LANES), 1) < LANES - lane_shift
    for s in range(n_slab):
        y = jnp.where(from_own, rolled[s], rolled[s + 1])
        if act == "gelu":
            y = _gelu_tanh(y)
        elif act == "sigmoid":
            y = _sigmoid(y)
        o_ref[:, s * LANES:(s + 1) * LANES] = y.astype(o_ref.dtype)


def _proj_shifted(x, w, cols, act, out_dtype, *, tm=1024, tn=1024):
    n, d = x.shape
    off, width = cols
    lane_shift = off % LANES
    base = off - lane_shift
    assert lane_shift and base % tn == 0 and width % tn == 0
    tm = min(tm, n)
    return pl.pallas_call(
        functools.partial(_proj_shifted_kernel, act=act, lane_shift=lane_shift),
        out_shape=jax.ShapeDtypeStruct((n, width), out_dtype),
        grid=(width // tn, n // tm),
        in_specs=[pl.BlockSpec((tm, d), lambda j, i: (i, 0)),
                  pl.BlockSpec((d, tn), lambda j, i: (0, base // tn + j)),
                  pl.BlockSpec((d, LANES), lambda j, i: (0, (base + tn) // LANES + j * (tn // LANES)))],
        out_specs=pl.BlockSpec((tm, tn), lambda j, i: (i, j)),
        scratch_shapes=[pltpu.VMEM((d, tn), BF16), pltpu.VMEM((d, LANES), BF16)],
        compiler_params=_cparams("arbitrary", "arbitrary"),
        name="proj_shifted",
    )(x, w, w)


def _key_to_float(key):
    return pltpu.bitcast(key ^ ((key >> 31) & 0x7FFFFFFF), F32)


def _attn_kernel(q_ref, k_ref, v_ref, qi_ref, ki_ref, wi_ref, gate_ref, o_ref,
                 wb_s, key_s, keyt_s, x_s, thr_s, xrow_s, *state, tq, topk, seq):
    m_s, acc_s = state[:N_KV], state[N_KV:]
    kb = tq
    n_sub = kb // LANES
    qt = pl.program_id(1)
    nkb = qt + 1
    n_hi = IDX_HEADS
    rows_g = HEADS_PER_KV * tq

    wi = wi_ref[...]
    for h in range(n_hi):
        wb_s[h] = jnp.broadcast_to(wi[:, IDX_DIM + h:IDX_DIM + h + 1], (tq, LANES))
    qi = qi_ref[...].reshape(n_hi * tq, LANES)
    q_row = qt * tq + lax.broadcasted_iota(I32, (tq, LANES), 0)
    limit = (q_row // CHUNK + 1) * CHUNK
    lane_pos = lax.broadcasted_iota(I32, (tq, LANES), 1)
    limit_t = ((qt * tq + lax.broadcasted_iota(I32, (1, tq), 1)) // CHUNK + 1) * CHUNK

    def score_body(j, carry):
        start = pl.multiple_of(j * kb, kb)
        ki_blk = ki_ref[pl.ds(start, kb), :].astype(BF16)
        s = lax.dot_general(qi, ki_blk, (((1,), (1,)), ((), ())), preferred_element_type=F32)
        parts = []
        for c in range(n_sub):
            sc = jnp.zeros((tq, LANES), F32)
            for h in range(n_hi):
                sh = s[h * tq:(h + 1) * tq, c * LANES:(c + 1) * LANES]
                sc = sc + wb_s[h] * jnp.maximum(sh, 0.0)
            sc = jnp.where(sc == 0.0, 0.0, sc)
            parts.append(sc)
            kpos = start + c * LANES + lane_pos
            key_s[j, :, c * LANES:(c + 1) * LANES] = jnp.where(kpos < limit, sc, -jnp.inf)
        sc_t = jnp.concatenate(parts, axis=1).T
        kpos_t = start + lax.broadcasted_iota(I32, (kb, tq), 0)
        keyt_s[j] = jnp.where(kpos_t < limit_t, sc_t, -jnp.inf)
        return carry

    lax.fori_loop(0, nkb, score_body, 0)

    kf = np.float32(topk)
    searched = limit_t > topk

    def count_keys(pred):
        def body(j, cnt):
            hit = jnp.where(pred(keyt_s[j], j), 1.0, 0.0)
            return cnt + jnp.sum(hit.reshape(kb // SUBLANES, SUBLANES, tq), axis=0)
        cnt = lax.fori_loop(0, nkb, body, jnp.zeros((SUBLANES, tq), F32))
        return jnp.sum(cnt, axis=0, keepdims=True)

    thr0 = jnp.where(count_keys(lambda s, j: s >= 0.0) >= kf, 0, INT_MIN).astype(I32)

    def thr_body(it, thr):
        cand = thr | jnp.left_shift(jnp.int32(1), 30 - it)
        cand_f = _key_to_float(cand)
        return jnp.where(count_keys(lambda s, j: s >= cand_f) >= kf, cand, thr)

    thr = _key_to_float(lax.fori_loop(0, 31, thr_body, thr0))
    thr = jnp.where(searched, thr, -jnp.inf)
    n_ge = count_keys(lambda s, j: s >= thr)
    need = kf - count_keys(lambda s, j: s > thr)
    xrow_s[...] = jnp.where(searched, seq, -1).astype(I32)
    tie_flag = jnp.max(jnp.where(searched & (n_ge > kf), 1.0, 0.0), axis=(0, 1), keepdims=True)

    @pl.when(tie_flag[0, 0] > 0.0)
    def _():
        nbits = max(int(seq - 1).bit_length(), 1)
        row_pos = lax.broadcasted_iota(I32, (kb, tq), 0)

        def x_body(it, xcut):
            cand = xcut | jnp.left_shift(jnp.int32(1), nbits - 1 - it)
            cnt = count_keys(lambda s, j: (s == thr) & (j * kb + row_pos < cand))
            return jnp.where(cnt < need, cand, xcut)

        xcut = lax.fori_loop(0, nbits, x_body, jnp.zeros((1, tq), I32))
        xrow_s[...] = jnp.where(searched, xcut, -1)

    def to_col(row_f32):
        return jnp.broadcast_to(row_f32, (LANES, tq)).T

    thr_s[...] = to_col(thr)
    x_s[...] = to_col(xrow_s[...].astype(F32)).astype(I32)
    rb = min(tq, 128)
    n_rc = tq // rb
    lane_rb = lax.broadcasted_iota(I32, (rb, LANES), 1)

    for g in range(N_KV):
        m_s[g][...] = jnp.full((rows_g, LANES), NEG_BIG, F32)
        acc_s[g][...] = jnp.zeros((rows_g, 2 * LANES), F32)
    ones_v = jnp.ones((kb, LANES), BF16)

    def attn_body(j, carry):
        start = pl.multiple_of(j * kb, kb)
        bias = []
        for rc in range(n_rc):
            thr_c = thr_s[rc * rb:(rc + 1) * rb, :]
            xcut_c = x_s[rc * rb:(rc + 1) * rb, :]
            parts = []
            for c in range(n_sub):
                kc = key_s[j, rc * rb:(rc + 1) * rb, c * LANES:(c + 1) * LANES]
                kpos = start + c * LANES + lane_rb
                sel = (kc > thr_c) | ((kc == thr_c) & (kpos <= xcut_c))
                parts.append(jnp.where(sel, 0.0, NEG_BIG))
            bias.append(jnp.concatenate(parts, axis=1))
        for g in range(N_KV):
            qg = q_ref[g * HEADS_PER_KV:(g + 1) * HEADS_PER_KV].reshape(rows_g, LANES)
            kg = k_ref[g, pl.ds(start, kb), :]
            vg = jnp.concatenate([v_ref[g, pl.ds(start, kb), :], ones_v], axis=1)
            lg_all = lax.dot_general(qg, kg, (((1,), (1,)), ((), ())), preferred_element_type=F32)
            m_prev_all = m_s[g][...]
            p_parts, a_parts, m_parts = [], [], []
            for r in range(HEADS_PER_KV):
                for rc in range(n_rc):
                    r0 = r * tq + rc * rb
                    lg = lg_all[r0:r0 + rb] + bias[rc]
                    m_prev = m_prev_all[r0:r0 + rb]
                    m_new = jnp.maximum(m_prev, jnp.max(lg, axis=1, keepdims=True))
                    p = jnp.exp2(lg - jnp.concatenate([m_new] * n_sub, axis=1))
                    p_parts.append(p.astype(BF16))
                    a_parts.append(jnp.exp2(m_prev - m_new))
                    m_parts.append(m_new)
            p_all = jnp.concatenate(p_parts, axis=0)
            alpha_all = jnp.concatenate(a_parts, axis=0)
            pv = jnp.dot(p_all, vg, preferred_element_type=F32)
            acc_s[g][...] = jnp.concatenate([alpha_all, alpha_all], axis=1) * acc_s[g][...] + pv
            m_s[g][...] = jnp.concatenate(m_parts, axis=0)
        return carry

    lax.fori_loop(0, nkb, attn_body, 0)

    for g in range(N_KV):
        acc = acc_s[g][...]
        og = acc[:, :LANES] / acc[:, LANES:]
        for r in range(HEADS_PER_KV):
            col = (g * HEADS_PER_KV + r) * LANES
            y = og[r * tq:(r + 1) * tq] * gate_ref[:, col:col + LANES].astype(F32)
            o_ref[:, col:col + LANES] = y.astype(o_ref.dtype)


def _attention(qh, kh, vh, qih, kiwi, sg, batch, seq, *, tq):
    n = batch * seq
    topk = min(TOPK_MAX, seq // 4)
    tq = min(tq, seq)
    nqt = seq // tq
    attn_w = N_HEADS * HEAD_DIM
    rows_g = HEADS_PER_KV * tq
    return pl.pallas_call(
        functools.partial(_attn_kernel, tq=tq, topk=topk, seq=seq),
        out_shape=jax.ShapeDtypeStruct((n, attn_w), BF16),
        grid=(batch, nqt),
        in_specs=[
            pl.BlockSpec((N_HEADS, tq, LANES), lambda b, t: (0, b * nqt + t, 0)),
            pl.BlockSpec((N_KV, seq, LANES), lambda b, t: (0, b, 0)),
            pl.BlockSpec((N_KV, seq, LANES), lambda b, t: (0, b, 0)),
            pl.BlockSpec((IDX_HEADS, tq, LANES), lambda b, t: (0, b * nqt + t, 0)),
            pl.BlockSpec((None, seq, LANES), lambda b, t: (0, b, 0)),
            pl.BlockSpec((None, tq, LANES), lambda b, t: (0, b * nqt + t, 0)),
            pl.BlockSpec((tq, attn_w), lambda b, t: (b * nqt + t, 0)),
        ],
        out_specs=pl.BlockSpec((tq, attn_w), lambda b, t: (b * nqt + t, 0)),
        scratch_shapes=[
            pltpu.VMEM((IDX_HEADS, tq, LANES), F32),
            pltpu.VMEM((nqt, tq, tq), F32),
            pltpu.VMEM((nqt, tq, tq), F32),
            pltpu.VMEM((tq, LANES), I32),
            pltpu.VMEM((tq, LANES), F32),
            pltpu.VMEM((1, tq), I32),
        ] + [pltpu.VMEM((rows_g, LANES), F32)] * N_KV + [pltpu.VMEM((rows_g, 2 * LANES), F32)] * N_KV,
        compiler_params=_cparams("parallel", "arbitrary"),
        name="sparse_attention",
    )(qh, kh, vh, qih, kiwi, kiwi, sg)


def _pool_kernel(p_ref, gate_ref, w_ref, scale_ref, o_ref, buf_a, buf_b, *, seq):
    pad = 16
    g = pl.program_id(1)
    p = p_ref[...]
    zeros = jnp.zeros((pad, p.shape[1]), F32)
    buf_a[0:pad, :] = zeros
    buf_b[0:pad, :] = zeros
    buf_a[pad:pad + seq, :] = p
    s2 = p + buf_a[pad - 1:pad - 1 + seq, :]
    buf_b[pad:pad + seq, :] = s2
    s4 = s2 + buf_b[pad - 2:pad - 2 + seq, :]
    buf_a[pad:pad + seq, :] = s4
    s8 = s4 + buf_a[pad - 4:pad - 4 + seq, :]
    buf_b[pad:pad + seq, :] = s8
    s16 = s8 + buf_b[pad - 8:pad - 8 + seq, :]
    t1 = (lax.broadcasted_iota(I32, p.shape, 0) + 1).astype(F32)
    win = jnp.where(g == 0, 2.0, jnp.where(g == 1, 4.0, jnp.where(g == 2, 8.0, 16.0))).astype(F32)
    total = jnp.where(g == 0, s2, jnp.where(g == 1, s4, jnp.where(g == 2, s8, s16)))
    mean = total / jnp.minimum(t1, win)
    diff = (mean - p).astype(BF16)
    y = jnp.dot(diff, w_ref[...], preferred_element_type=F32)
    o_ref[...] = (y * scale_ref[...] * gate_ref[...].astype(F32)).astype(o_ref.dtype)


def _pool_mixer(px, sg, pool_w, pool_scale, batch, seq):
    n = batch * seq
    width = pool_scale.shape[-1]
    ng = len(POOL_WINDOWS)
    cg = width // ng
    assert POOL_WINDOWS == (2, 4, 8, 16)
    return pl.pallas_call(
        functools.partial(_pool_kernel, seq=seq),
        out_shape=jax.ShapeDtypeStruct((n, width), BF16),
        grid=(batch, ng),
        in_specs=[
            pl.BlockSpec((seq, cg), lambda b, g: (b, g)),
            pl.BlockSpec((seq, cg), lambda b, g: (b, ng + g)),
            pl.BlockSpec((None, cg, cg), lambda b, g: (g, 0, 0)),
            pl.BlockSpec((1, cg), lambda b, g: (0, g)),
        ],
        out_specs=pl.BlockSpec((seq, cg), lambda b, g: (b, g)),
        scratch_shapes=[pltpu.VMEM((seq + 16, cg), F32), pltpu.VMEM((seq + 16, cg), F32)],
        compiler_params=_cparams("parallel", "arbitrary"),
        name="pool_mixer",
    )(px, sg, pool_w, pool_scale.reshape(1, width))


def _lru_kernel(x_ref, gr_ref, gate_ref, cw_ref, cb_ref, wa_ref, ba_ref, wx_ref, bx_ref, lam_ref,
                o_ref, a_s, b_s, *, seq, ct):
    x = x_ref[...]
    row = lax.broadcasted_iota(I32, (seq, ct), 0)
    xc = jnp.broadcast_to(cb_ref[...], (seq, ct))
    for tap in range(CONV_WIDTH):
        d = CONV_WIDTH - 1 - tap
        x_d = x if d == 0 else jnp.where(row >= d, pltpu.roll(x, d, 0), 0.0)
        xc = xc + x_d * cw_ref[tap:tap + 1, :]
    xcb = xc.astype(BF16)
    nb = ct // LRU_BLOCK
    r_parts, i_parts = [], []
    for blk in range(nb):
        xb = xcb[:, blk * LRU_BLOCK:(blk + 1) * LRU_BLOCK]
        r_parts.append(jnp.dot(xb, wa_ref[blk], preferred_element_type=F32))
        i_parts.append(jnp.dot(xb, wx_ref[blk], preferred_element_type=F32))
    r = _sigmoid(jnp.concatenate(r_parts, axis=1) + ba_ref[...])
    gi = _sigmoid(jnp.concatenate(i_parts, axis=1) + bx_ref[...])
    lam = lam_ref[...]
    softplus_neg_lam = jnp.log(1.0 + jnp.exp(-lam))
    log_a = -LRU_C * r * softplus_neg_lam
    a = jnp.exp(log_a)
    one_m_a2 = 1.0 - a * a
    root = jnp.where(one_m_a2 > 0.0, one_m_a2 * lax.rsqrt(one_m_a2), 0.0)
    b = root * (gi * xc)

    tiles = (seq // SUBLANES, SUBLANES, ct)
    a = a.reshape(tiles)
    b = b.reshape(tiles)
    sub = lax.broadcasted_iota(I32, tiles, 1)
    for d in (1, 2, 4):
        keep = sub >= d
        a_sh = jnp.where(keep, pltpu.roll(a, d, 1), 1.0)
        b_sh = jnp.where(keep, pltpu.roll(b, d, 1), 0.0)
        b = a * b_sh + b
        a = a * a_sh
    a_s[...] = a.reshape(seq, ct)
    b_s[...] = b.reshape(seq, ct)

    def body(t, carry):
        r0 = pl.multiple_of(t * SUBLANES, SUBLANES)
        h = a_s[pl.ds(r0, SUBLANES), :] * carry + b_s[pl.ds(r0, SUBLANES), :]
        b_s[pl.ds(r0, SUBLANES), :] = h
        return jnp.broadcast_to(h[SUBLANES - 1:SUBLANES, :], (SUBLANES, ct))

    lax.fori_loop(0, seq // SUBLANES, body, jnp.zeros((SUBLANES, ct), F32), unroll=8)
    h = b_s[...]
    o_ref[...] = (h * gr_ref[...].astype(F32) * gate_ref[...].astype(F32)).astype(o_ref.dtype)


def _lru_mixer(px, gg, sg, conv_w, conv_b, wa, ba, wx, bx, lam, batch, seq, *, ct=256):
    n = batch * seq
    width = conv_b.shape[-1]
    nct = width // ct
    nb = ct // LRU_BLOCK
    row = lambda a: a.reshape(1, width)
    rspec = pl.BlockSpec((1, ct), lambda b, j: (0, j))
    return pl.pallas_call(
        functools.partial(_lru_kernel, seq=seq, ct=ct),
        out_shape=jax.ShapeDtypeStruct((n, width), BF16),
        grid=(batch, nct),
        in_specs=[
            pl.BlockSpec((seq, ct), lambda b, j: (b, nct + j)),
            pl.BlockSpec((seq, ct), lambda b, j: (b, j)),
            pl.BlockSpec((seq, ct), lambda b, j: (b, 2 * nct + j)),
            pl.BlockSpec((CONV_WIDTH, ct), lambda b, j: (0, j)),
            rspec,
            pl.BlockSpec((nb, LRU_BLOCK, LRU_BLOCK), lambda b, j: (j, 0, 0)),
            rspec,
            pl.BlockSpec((nb, LRU_BLOCK, LRU_BLOCK), lambda b, j: (j, 0, 0)),
            rspec,
            rspec,
        ],
        out_specs=pl.BlockSpec((seq, ct), lambda b, j: (b, j)),
        scratch_shapes=[pltpu.VMEM((seq, ct), F32),
                        pltpu.VMEM((seq, ct), F32)],
        compiler_params=_cparams("parallel", "arbitrary"),
        name="rglru_mixer",
    )(px, gg, sg, conv_w, row(conv_b), wa, row(ba), wx, row(bx), row(lam))


def _out_proj_ln_kernel(ya_ref, yb_ref, yc_ref, w_ref, x_ref, g_ref, b_ref, o_ref, ob_ref, *, alpha):
    acc = jnp.dot(ya_ref[...], w_ref[0], preferred_element_type=F32)
    acc = acc + jnp.dot(yb_ref[...], w_ref[1], preferred_element_type=F32)
    acc = acc + jnp.dot(yc_ref[...], w_ref[2], preferred_element_type=F32)
    out = _layer_norm(alpha * x_ref[...] + acc, g_ref[...], b_ref[...])
    o_ref[...] = out
    ob_ref[...] = out.astype(BF16)


def _out_proj_ln(ya, yb, yc, w3, x, g, b, alpha, *, tm=256):
    n, width = ya.shape
    d = w3.shape[-1]
    tm = min(tm, n)
    aspec = pl.BlockSpec((tm, width), lambda i: (i, 0))
    tok = pl.BlockSpec((tm, d), lambda i: (i, 0))
    row = pl.BlockSpec((1, d), lambda i: (0, 0))
    return pl.pallas_call(
        functools.partial(_out_proj_ln_kernel, alpha=alpha),
        out_shape=(jax.ShapeDtypeStruct((n, d), F32), jax.ShapeDtypeStruct((n, d), BF16)),
        grid=(n // tm,),
        in_specs=[aspec, aspec, aspec,
                  pl.BlockSpec((3, width, d), lambda i: (0, 0, 0), pipeline_mode=pl.Buffered(1)),
                  tok, row, row],
        out_specs=(tok, tok),
        compiler_params=_cparams("parallel"),
        name="out_proj_ln",
    )(ya, yb, yc, w3, x, g.reshape(1, d), b.reshape(1, d))


def _ffn_up_kernel(x_ref, wg_ref, wu_ref, o_ref):
    xb = x_ref[...]
    hg = jnp.dot(xb, wg_ref[...], preferred_element_type=F32)
    hu = jnp.dot(xb, wu_ref[...], preferred_element_type=F32)
    o_ref[...] = (hg * _sigmoid(hg) * hu).astype(o_ref.dtype)


def _ffn_down_ln_kernel(h_ref, w_ref, x_ref, g_ref, b_ref, o_ref, ob_ref, *, alpha):
    y = jnp.dot(h_ref[...], w_ref[...], preferred_element_type=F32)
    out = _layer_norm(alpha * x_ref[...] + y, g_ref[...], b_ref[...])
    o_ref[...] = out
    ob_ref[...] = out.astype(BF16)


def _ffn_dense(xb, x, wg, wu, wd, g, b, alpha, *, tm_up=1024, tn_up=512, tm_down=256):
    n, d = x.shape
    ff = wg.shape[1]
    ff_real = wd.shape[0]
    assert ff_real % LANES == 0
    tm_up, tm_down = min(tm_up, n), min(tm_down, n)
    h = pl.pallas_call(
        _ffn_up_kernel,
        out_shape=jax.ShapeDtypeStruct((n, ff), BF16),
        grid=(n // tm_up, ff // tn_up),
        in_specs=[pl.BlockSpec((tm_up, d), lambda i, j: (i, 0)),
                  pl.BlockSpec((d, tn_up), lambda i, j: (0, j)),
                  pl.BlockSpec((d, tn_up), lambda i, j: (0, j))],
        out_specs=pl.BlockSpec((tm_up, tn_up), lambda i, j: (i, j)),
        compiler_params=_cparams("parallel", "arbitrary"),
        name="ffn_up",
    )(xb, wg, wu)
    tok = pl.BlockSpec((tm_down, d), lambda i: (i, 0))
    row = pl.BlockSpec((1, d), lambda i: (0, 0))
    return pl.pallas_call(
        functools.partial(_ffn_down_ln_kernel, alpha=alpha),
        out_shape=(jax.ShapeDtypeStruct((n, d), F32), jax.ShapeDtypeStruct((n, d), BF16)),
        grid=(n // tm_down,),
        in_specs=[pl.BlockSpec((tm_down, ff_real), lambda i: (i, 0)),
                  pl.BlockSpec((ff_real, d), lambda i: (0, 0), pipeline_mode=pl.Buffered(1)),
                  tok, row, row],
        out_specs=(tok, tok),
        compiler_params=_cparams("parallel"),
        name="ffn_down_ln",
    )(h, wd, x, g.reshape(1, d), b.reshape(1, d))


def _router_kernel(x_ref, w_ref, ids_ref, wts_ref):
    logits = jnp.dot(x_ref[...], w_ref[...], preferred_element_type=F32, precision=lax.Precision.HIGHEST)
    lane_i = lax.broadcasted_iota(I32, logits.shape, 1)
    lane = lane_i.astype(F32)
    logits = jnp.where(lane_i < N_EXPERTS, logits, -jnp.inf)
    m1 = jnp.max(logits, axis=1, keepdims=True)
    i1 = jnp.min(jnp.where(logits == m1, lane, float(LANES)), axis=1, keepdims=True)
    rest = jnp.where(lane == i1, -jnp.inf, logits)
    m2 = jnp.max(rest, axis=1, keepdims=True)
    i2 = jnp.min(jnp.where(rest == m2, lane, float(LANES)), axis=1, keepdims=True)
    e2 = jnp.exp(m2 - m1)
    w1 = 1.0 / (1.0 + e2)
    w2 = e2 / (1.0 + e2)
    ids_ref[...] = jnp.where(lane_i == 0, i1, jnp.where(lane_i == 1, i2, 0.0)).astype(I32)
    wts_ref[...] = jnp.where(lane_i == 0, w1, jnp.where(lane_i == 1, w2, 0.0))


def _router(x, w_router, *, tm=512):
    n, d = x.shape
    wpad = jnp.zeros((d, LANES), F32).at[:, :N_EXPERTS].set(w_router.astype(F32))
    tm = min(tm, n)
    tok = pl.BlockSpec((tm, LANES), lambda i: (i, 0))
    return pl.pallas_call(
        _router_kernel,
        out_shape=(jax.ShapeDtypeStruct((n, LANES), I32), jax.ShapeDtypeStruct((n, LANES), F32)),
        grid=(n // tm,),
        in_specs=[pl.BlockSpec((tm, d), lambda i: (i, 0)), pl.BlockSpec((d, LANES), lambda i: (0, 0))],
        out_specs=(tok, tok),
        compiler_params=_cparams("parallel"),
        name="moe_router",
    )(x, wpad)


def _gather_rows_kernel(tok_ref, x_hbm, o_ref, buf, sem, *, tm):
    i = pl.program_id(0)

    def issue(tile, slot):
        base = tile * tm

        def body(r, carry):
            t = tok_ref[base + r]
            pltpu.make_async_copy(x_hbm.at[pl.ds(t, 1), :], buf.at[slot, pl.ds(r, 1), :], sem.at[slot]).start()
            return carry

        lax.fori_loop(0, tm, body, 0, unroll=8)

    @pl.when(i == 0)
    def _():
        issue(0, 0)

    @pl.when(i + 1 < pl.num_programs(0))
    def _():
        issue(i + 1, (i + 1) % 2)

    slot = i % 2
    pltpu.make_async_copy(x_hbm.at[pl.ds(0, tm), :], buf.at[slot], sem.at[slot]).wait()
    o_ref[...] = buf[slot].astype(o_ref.dtype)


def _gather_rows(x, row_tok, *, tm=256):
    n, d = x.shape
    rows = row_tok.shape[0]
    return pl.pallas_call(
        functools.partial(_gather_rows_kernel, tm=tm),
        out_shape=jax.ShapeDtypeStruct((rows, d), BF16),
        grid_spec=pltpu.PrefetchScalarGridSpec(
            num_scalar_prefetch=1,
            grid=(rows // tm,),
            in_specs=[pl.BlockSpec(memory_space=pl.ANY)],
            out_specs=pl.BlockSpec((tm, d), lambda i, tok: (i, 0)),
            scratch_shapes=[pltpu.VMEM((2, tm, d), F32), pltpu.SemaphoreType.DMA((2,))],
        ),
        compiler_params=_cparams("arbitrary"),
        name="moe_gather_rows",
    )(row_tok, x)


def _expert_changed(te_ref, i):
    return (i == 0) | (te_ref[i] != te_ref[jnp.maximum(i - 1, 0)])


def _stream_expert_weights(j, i, n_j, col_tile, te_ref, run_ref, nxt_ref, meta_ref, w_hbm, stage, work, sem):
    @pl.when(_expert_changed(te_ref, i))
    def _():
        k = j * meta_ref[0] + run_ref[i]
        slot = k % 2

        def copies(e, jj, s):
            col = pl.ds(pl.multiple_of(jj * col_tile, col_tile), col_tile)
            return [pltpu.make_async_copy(w.at[e, :, col], st.at[s], sem.at[s, a])
                    for a, (w, st) in enumerate(zip(w_hbm, stage))]

        @pl.when(k == 0)
        def _():
            for c in copies(te_ref[i], j, slot):
                c.start()

        has_next_run = nxt_ref[i] >= 0
        e_next = jnp.where(has_next_run, nxt_ref[i], meta_ref[1])
        j_next = jnp.where(has_next_run, j, j + 1)

        @pl.when(j_next < n_j)
        def _():
            for c in copies(e_next, j_next, 1 - slot):
                c.start()

        for c in copies(te_ref[i], j, slot):
            c.wait()
        for st, wk in zip(stage, work):
            wk[...] = st[slot].astype(BF16)


def _moe_up_kernel(te_ref, nv_ref, run_ref, nxt_ref, meta_ref, x_ref, wg_hbm, wu_hbm, o_ref,
                   stage_g, stage_u, wg_s, wu_s, sem, *, tf):
    j, i = pl.program_id(0), pl.program_id(1)
    _stream_expert_weights(j, i, pl.num_programs(0), tf, te_ref, run_ref, nxt_ref, meta_ref,
                           (wg_hbm, wu_hbm), (stage_g, stage_u), (wg_s, wu_s), sem)

    @pl.when(i < nv_ref[0])
    def _():
        xb = x_ref[...]
        hg = jnp.dot(xb, wg_s[...], preferred_element_type=F32)
        hu = jnp.dot(xb, wu_s[...], preferred_element_type=F32)
        o_ref[...] = (hg * _sigmoid(hg) * hu).astype(o_ref.dtype)

    @pl.when(i >= nv_ref[0])
    def _():
        o_ref[...] = jnp.zeros(o_ref.shape, o_ref.dtype)


def _moe_up(xs, wg, wu, sched, *, tm, tf=1024):
    rows, d = xs.shape
    ff = wg.shape[-1]
    return pl.pallas_call(
        functools.partial(_moe_up_kernel, tf=tf),
        out_shape=jax.ShapeDtypeStruct((rows, ff), BF16),
        grid_spec=pltpu.PrefetchScalarGridSpec(
            num_scalar_prefetch=len(sched),
            grid=(ff // tf, rows // tm),
            in_specs=[pl.BlockSpec((tm, d), lambda j, i, *_: (i, 0)),
                      pl.BlockSpec(memory_space=pl.ANY),
                      pl.BlockSpec(memory_space=pl.ANY)],
            out_specs=pl.BlockSpec((tm, tf), lambda j, i, *_: (i, j)),
            scratch_shapes=[pltpu.VMEM((2, d, tf), F32), pltpu.VMEM((2, d, tf), F32),
                            pltpu.VMEM((d, tf), BF16), pltpu.VMEM((d, tf), BF16),
                            pltpu.SemaphoreType.DMA((2, 2))],
        ),
        compiler_params=_cparams("arbitrary", "arbitrary"),
        name="moe_up",
    )(*sched, xs, wg, wu)


def _moe_down_kernel(te_ref, nv_ref, run_ref, nxt_ref, meta_ref, h_ref, wd_hbm, o_ref, stage_d, wd_s, sem, *, tn):
    j, i = pl.program_id(0), pl.program_id(1)
    _stream_expert_weights(j, i, pl.num_programs(0), tn, te_ref, run_ref, nxt_ref, meta_ref,
                           (wd_hbm,), (stage_d,), (wd_s,), sem)

    @pl.when(i < nv_ref[0])
    def _():
        o_ref[...] = jnp.dot(h_ref[...], wd_s[...], preferred_element_type=F32)

    @pl.when(i >= nv_ref[0])
    def _():
        o_ref[...] = jnp.zeros(o_ref.shape, o_ref.dtype)


def _moe_down(h, wd, sched, *, tm, tn=512):
    rows, ff = h.shape
    d = wd.shape[-1]
    return pl.pallas_call(
        functools.partial(_moe_down_kernel, tn=tn),
        out_shape=jax.ShapeDtypeStruct((rows, d), F32),
        grid_spec=pltpu.PrefetchScalarGridSpec(
            num_scalar_prefetch=len(sched),
            grid=(d // tn, rows // tm),
            in_specs=[pl.BlockSpec((tm, ff), lambda j, i, *_: (i, 0)),
                      pl.BlockSpec(memory_space=pl.ANY)],
            out_specs=pl.BlockSpec((tm, tn), lambda j, i, *_: (i, j)),
            scratch_shapes=[pltpu.VMEM((2, ff, tn), F32), pltpu.VMEM((ff, tn), BF16),
                            pltpu.SemaphoreType.DMA((2, 1))],
        ),
        compiler_params=_cparams("arbitrary", "arbitrary"),
        name="moe_down",
    )(*sched, h, wd)


def _combine_ln_kernel(pos_ref, y_hbm, x_ref, wts_ref, g_ref, b_ref, o_ref, buf, sem, *, tm, alpha):
    i = pl.program_id(0)

    def issue(tile, slot):
        base = tile * tm

        def body(r, carry):
            for k in range(TOP_K):
                row = pos_ref[(base + r) * TOP_K + k]
                pltpu.make_async_copy(y_hbm.at[pl.ds(row, 1), :], buf.at[slot, k, pl.ds(r, 1), :],
                                      sem.at[slot]).start()
            return carry

        lax.fori_loop(0, tm, body, 0, unroll=8)

    @pl.when(i == 0)
    def _():
        issue(0, 0)

    @pl.when(i + 1 < pl.num_programs(0))
    def _():
        issue(i + 1, (i + 1) % 2)

    slot = i % 2
    for k in range(TOP_K):
        pltpu.make_async_copy(y_hbm.at[pl.ds(0, tm), :], buf.at[slot, k], sem.at[slot]).wait()
    wts = wts_ref[...]
    y = buf[slot, 0] * wts[:, 0:1] + buf[slot, 1] * wts[:, 1:2]
    o_ref[...] = _layer_norm(alpha * x_ref[...] + y, g_ref[...], b_ref[...])


def _combine_ln(yrows, pos, x, wts, g, b, alpha, *, tm=128):
    n, d = x.shape
    tm = min(tm, n)
    return pl.pallas_call(
        functools.partial(_combine_ln_kernel, tm=tm, alpha=alpha),
        out_shape=jax.ShapeDtypeStruct((n, d), F32),
        grid_spec=pltpu.PrefetchScalarGridSpec(
            num_scalar_prefetch=1,
            grid=(n // tm,),
            in_specs=[pl.BlockSpec(memory_space=pl.ANY),
                      pl.BlockSpec((tm, d), lambda i, pos: (i, 0)),
                      pl.BlockSpec((tm, LANES), lambda i, pos: (i, 0)),
                      pl.BlockSpec((1, d), lambda i, pos: (0, 0)),
                      pl.BlockSpec((1, d), lambda i, pos: (0, 0))],
            out_specs=pl.BlockSpec((tm, d), lambda i, pos: (i, 0)),
            scratch_shapes=[pltpu.VMEM((2, TOP_K, tm, d), F32), pltpu.SemaphoreType.DMA((2,))],
        ),
        compiler_params=_cparams("arbitrary"),
        name="moe_combine_ln",
    )(pos, yrows, x, wts, g.reshape(1, d), b.reshape(1, d))


def _moe_block(x, w_router, wg, wu, wd, g, b, alpha, *, tm=256):
    n, d = x.shape
    ids, wts = _router(x, w_router)
    e_flat = ids[:, :TOP_K].reshape(-1)
    n_assign = n * TOP_K
    onehot = (e_flat[:, None] == jnp.arange(N_EXPERTS, dtype=I32)[None, :]).astype(I32)
    rank = jnp.sum((jnp.cumsum(onehot, axis=0) - onehot) * onehot, axis=1)
    counts = jnp.sum(onehot, axis=0)
    padded = (counts + tm - 1) // tm * tm
    end_padded = jnp.cumsum(padded)
    start_padded = end_padded - padded
    dest = (start_padded[e_flat] + rank).astype(I32)
    rows = n_assign + N_EXPERTS * tm
    n_tiles = rows // tm
    flat_tok = jnp.arange(n_assign, dtype=I32) // TOP_K
    row_tok = jnp.zeros((rows,), I32).at[dest].set(flat_tok)
    tile_start = jnp.arange(n_tiles, dtype=I32) * tm
    tile_e = jnp.minimum(jnp.sum((tile_start[:, None] >= end_padded[None, :]).astype(I32), axis=1),
                         N_EXPERTS - 1).astype(I32)
    n_valid = (end_padded[-1:] // tm).astype(I32)
    tile_ix = jnp.arange(n_tiles, dtype=I32)
    tile_e = jnp.where(tile_ix < n_valid[0], tile_e, tile_e[jnp.maximum(n_valid[0] - 1, 0)])
    is_start = jnp.concatenate([jnp.ones((1,), I32), (tile_e[1:] != tile_e[:-1]).astype(I32)])
    run_id = (jnp.cumsum(is_start) - 1).astype(I32)
    larger = jnp.where(tile_e[None, :] > tile_e[:, None], tile_e[None, :], N_EXPERTS)
    nxt_e = jnp.min(larger, axis=1)
    nxt_e = jnp.where(nxt_e < N_EXPERTS, nxt_e, -1).astype(I32)
    meta = jnp.stack([run_id[-1] + 1, tile_e[0]]).astype(I32)
    sched = (tile_e, n_valid, run_id, nxt_e, meta)
    xs = _gather_rows(x, row_tok, tm=tm)
    h = _moe_up(xs, wg, wu, sched, tm=tm)
    yrows = _moe_down(h, wd, sched, tm=tm)
    return _combine_ln(yrows, dest, x, wts, g, b, alpha)


def _w_in_columns(d_model):
    attn_w = N_HEADS * HEAD_DIM
    kv_w = N_KV * HEAD_DIM
    o = np.cumsum([0, attn_w, kv_w, kv_w, IDX_HEADS * IDX_DIM, IDX_DIM, IDX_HEADS, d_model, d_model, d_model])
    o = [int(v) for v in o]
    return {"q": (o[0], attn_w), "k": (o[1], kv_w), "v": (o[2], kv_w), "qi": (o[3], o[4] - o[3]),
            "kiwi": (o[4], LANES),
            "px": (o[6], 2 * d_model), "gr": (o[8], d_model), "gates": (o[9], 3 * d_model)}


def _mixer(x_mm, tabs, w_in, pool_w, pool_scale, conv_w, conv_b, wa, ba, wx, bx, lam,
           batch, seq, d_model, tq):
    cos_a, sin_a, cos_i, sin_ia, sin_ib, cos_q, sin_q, cos_iq = tabs
    cols = _w_in_columns(d_model)
    rope_a = ((cos_a, sin_a), (HEAD_DIM // 2,))
    rope_i = (LANES - IDX_DIM // 2, IDX_DIM // 2)
    qh = _proj_heads(x_mm, w_in, cols["q"], (cos_q, sin_q), rope_a[1], BF16)
    kh = _proj_heads(x_mm, w_in, cols["k"], *rope_a, BF16)
    vh = _proj_heads(x_mm, w_in, cols["v"], (), (), BF16)
    qih = _proj_heads(x_mm, w_in, cols["qi"], (cos_iq, sin_ia, sin_ib), rope_i, BF16, head_width=IDX_DIM)
    kiwi = _proj_heads(x_mm, w_in, cols["kiwi"], (cos_i, sin_ia, sin_ib), rope_i, F32)
    px = _proj_shifted(x_mm, w_in, cols["px"], None, F32)
    gg = _proj_shifted(x_mm, w_in, cols["gr"], "gelu", BF16)
    sg = _proj_shifted(x_mm, w_in, cols["gates"], "sigmoid", BF16)
    ya = _attention(qh, kh, vh, qih, kiwi, sg, batch, seq, tq=tq)
    yb = _pool_mixer(px, sg, pool_w.astype(BF16), pool_scale, batch, seq)
    yc = _lru_mixer(px, gg, sg, conv_w, conv_b, wa.astype(BF16), ba, wx.astype(BF16), bx, lam, batch, seq)
    return ya, yb, yc


def _pad_ff(w, axis, mult):
    ff = w.shape[axis]
    padn = (-ff) % mult
    if padn == 0:
        return w
    widths = [(0, 0)] * w.ndim
    widths[axis] = (0, padn)
    return jnp.pad(w, widths)


def kernel(x, positions, mix_w_in, mix_w_out, pool_w, pool_scale, conv_w, conv_b, lru_wa, lru_ba, lru_wx, lru_bx, lru_lam, ln_mix_g, ln_mix_b, ln_ffn_g, ln_ffn_b, dense_w_gate, dense_w_up, dense_w_down, moe_router, moe_w_gate, moe_w_up, moe_w_down):
    batch, seq, d_model = x.shape
    depth = mix_w_in.shape[0]
    alpha = np.float32((2 * depth) ** 0.25)
    n = batch * seq
    tabs = _rope_tables(positions)
    xf = x.reshape(n, d_model)
    x_mm = xf.astype(BF16)
    for layer in range(depth):
        ya, yb, yc = _mixer(x_mm, tabs, mix_w_in[layer], pool_w[layer], pool_scale[layer],
                            conv_w[layer], conv_b[layer], lru_wa[layer], lru_ba[layer], lru_wx[layer],
                            lru_bx[layer], lru_lam[layer], batch, seq, d_model, tq=256)
        w_out3 = mix_w_out[layer].astype(BF16).reshape(3, -1, d_model)
        xf, xb = _out_proj_ln(ya, yb, yc, w_out3, xf, ln_mix_g[layer], ln_mix_b[layer], alpha)
        j = layer // 2
        if layer % 2 == 0:
            wg = _pad_ff(dense_w_gate[j].astype(BF16), 1, 512)
            wu = _pad_ff(dense_w_up[j].astype(BF16), 1, 512)
            wd = dense_w_down[j].astype(BF16)
            xf, x_mm = _ffn_dense(xb, xf, wg, wu, wd, ln_ffn_g[layer], ln_ffn_b[layer], alpha)
        else:
            xf = _moe_block(xf, moe_router[j], moe_w_gate[j], moe_w_up[j], moe_w_down[j],
                            ln_ffn_g[layer], ln_ffn_b[layer], alpha)
            x_mm = xf.astype(BF16)
    return xf.reshape(batch, seq, d_model)
```

```python
import functools

import jax
import jax.numpy as jnp
import numpy as np
from jax import lax
from jax.experimental import pallas as pl
from jax.experimental.pallas import tpu as pltpu

F32 = jnp.float32
BF16 = jnp.bfloat16
I32 = jnp.int32

LANES = 128
SUBLANES = 8
VMEM_LIMIT = 56 * 1024 * 1024

CHUNK = 64
N_HEADS = 16
HEAD_DIM = 128
N_KV = 4
HEADS_PER_KV = N_HEADS // N_KV
IDX_HEADS = 16
IDX_DIM = 64
TOPK_MAX = 256
ROPE_THETA = 10000.0
POOL_WINDOWS = (2, 4, 8, 16)
LRU_BLOCK = 128
CONV_WIDTH = 4
LRU_C = 8.0
N_EXPERTS = 8
TOP_K = 2
LN_EPS = 1e-5
LN_ROWS = 128
INT_MIN = -2 ** 31
NEG_BIG = -1e30


def _cparams(*sem):
    return pltpu.CompilerParams(dimension_semantics=sem, vmem_limit_bytes=VMEM_LIMIT)


def _sigmoid(x):
    return 0.5 * (1.0 + jnp.tanh(0.5 * x))


def _gelu_tanh(x):
    c = np.float32(np.sqrt(2.0 / np.pi))
    return 0.5 * x * (1.0 + jnp.tanh(c * (x + np.float32(0.044715) * (x * x * x))))


def _layer_norm(y, g, b):
    mu = jnp.mean(y, axis=-1, keepdims=True)
    d = y - mu
    var = jnp.mean(d * d, axis=-1, keepdims=True)
    return d * lax.rsqrt(var + LN_EPS) * g + b


def _rope_tab_kernel(pos_ref, inv_a_ref, inv_i_ref, sgn_a_ref, m_cos_ref, add_cos_ref,
                     m_sa_ref, m_sb_ref, cos_a_ref, sin_a_ref, cos_i_ref, sin_ia_ref, sin_ib_ref,
                     cos_q_ref, sin_q_ref, cos_iq_ref):
    pos = pos_ref[...]
    ang_a = pos * inv_a_ref[...]
    cos_a = jnp.cos(ang_a)
    sin_a = jnp.sin(ang_a) * sgn_a_ref[...]
    cos_a_ref[...] = cos_a
    sin_a_ref[...] = sin_a
    q_scale = np.float32(HEAD_DIM ** -0.5 * np.log2(np.e))
    cos_q_ref[...] = cos_a * q_scale
    sin_q_ref[...] = sin_a * q_scale
    ang_i = pos * inv_i_ref[...]
    s_i = jnp.sin(ang_i)
    cos_iq = jnp.cos(ang_i) * m_cos_ref[...]
    cos_iq_ref[...] = cos_iq
    cos_i_ref[...] = cos_iq + add_cos_ref[...]
    sin_ia_ref[...] = s_i * m_sa_ref[...]
    sin_ib_ref[...] = s_i * m_sb_ref[...]


def _rope_tables(positions):
    n = positions.size
    pos = jnp.broadcast_to(positions.reshape(n, 1).astype(F32), (n, LANES))
    lane = np.arange(LANES)
    inv_a = (ROPE_THETA ** (-jnp.arange(0, HEAD_DIM, 2, dtype=F32) / HEAD_DIM))
    inv_i = (ROPE_THETA ** (-jnp.arange(0, IDX_DIM, 2, dtype=F32) / IDX_DIM))
    inv_a_row = jnp.concatenate([inv_a, inv_a])[None, :]
    inv_i_row = jnp.concatenate([inv_i, inv_i, jnp.zeros((LANES - IDX_DIM,), F32)])[None, :]
    sgn_a = jnp.asarray(np.where(lane < HEAD_DIM // 2, -1.0, 1.0), F32)[None, :]
    m_cos = jnp.asarray((lane < IDX_DIM).astype(np.float32))[None, :]
    wi_scale = (IDX_HEADS ** -0.5) * (IDX_DIM ** -0.5)
    add_cos = jnp.asarray(np.where((lane >= IDX_DIM) & (lane < IDX_DIM + IDX_HEADS), wi_scale, 0.0), F32)[None, :]
    m_sa = jnp.asarray(np.where(lane < IDX_DIM // 2, -1.0, 0.0), F32)[None, :]
    m_sb = jnp.asarray(np.where((lane >= IDX_DIM // 2) & (lane < IDX_DIM), 1.0, 0.0), F32)[None, :]
    tm = min(n, 1024)
    row = pl.BlockSpec((1, LANES), lambda i: (0, 0))
    tok = pl.BlockSpec((tm, LANES), lambda i: (i, 0))
    out = jax.ShapeDtypeStruct((n, LANES), F32)
    return pl.pallas_call(
        _rope_tab_kernel,
        out_shape=(out,) * 8,
        grid=(n // tm,),
        in_specs=[tok] + [row] * 7,
        out_specs=(tok,) * 8,
        compiler_params=_cparams("parallel"),
        name="rope_tables",
    )(pos, inv_a_row, inv_i_row, sgn_a, m_cos, add_cos, m_sa, m_sb)


def _proj_heads_kernel(x_ref, w_ref, *rest, shifts, heads, head_width):
    tabs, o_ref, w_s = rest[:-2], rest[-2], rest[-1]

    @pl.when(pl.program_id(1) == 0)
    def _():
        w_s[...] = w_ref[...].astype(BF16)

    acc = jnp.dot(x_ref[...].astype(BF16), w_s[...], preferred_element_type=F32)
    per_slab = LANES // head_width
    for h in range(heads):
        xh = acc[:, (h // per_slab) * LANES:(h // per_slab + 1) * LANES]
        if h % per_slab:
            xh = pltpu.roll(xh, LANES - (h % per_slab) * head_width, 1)
        if tabs:
            y = xh * tabs[0][...]
            for s, t in zip(shifts, tabs[1:]):
                y = y + pltpu.roll(xh, s, 1) * t[...]
        else:
            y = xh
        o_ref[h] = y.astype(o_ref.dtype)


def _proj_heads(x, w, layer, cols, tabs, shifts, out_dtype, *, head_width=LANES, tm=1024, tile_cols=1024):
    n, d = x.shape
    off, width = cols
    tn = min(tile_cols, width)
    hp = tn // head_width
    assert width % tn == 0 and off % tn == 0 and (head_width == LANES or tabs)
    tm = min(tm, n)
    tab_spec = pl.BlockSpec((tm, LANES), lambda j, i: (i, 0))
    return pl.pallas_call(
        functools.partial(_proj_heads_kernel, shifts=shifts, heads=hp, head_width=head_width),
        out_shape=jax.ShapeDtypeStruct((width // head_width, n, LANES), out_dtype),
        grid=(width // tn, n // tm),
        in_specs=[pl.BlockSpec((tm, d), lambda j, i: (i, 0)),
                  pl.BlockSpec((None, d, tn), lambda j, i: (layer, 0, off // tn + j))] + [tab_spec] * len(tabs),
        out_specs=pl.BlockSpec((hp, tm, LANES), lambda j, i: (j, i, 0)),
        scratch_shapes=[pltpu.VMEM((d, tn), BF16)],
        compiler_params=_cparams("arbitrary", "arbitrary"),
        name="proj_heads",
    )(x, w, *tabs)


def _realign_kernel(w_ref, wn_ref, o_ref, nxt_s, *, lane_shift):
    nxt_s[...] = jnp.zeros(nxt_s.shape, F32)
    nxt_s[:, :lane_shift] = wn_ref[:, :lane_shift]
    n_slab = o_ref.shape[1] // LANES
    rolled = [pltpu.roll(w_ref[:, s * LANES:(s + 1) * LANES], LANES - lane_shift, 1) for s in range(n_slab)]
    rolled.append(pltpu.roll(nxt_s[...], LANES - lane_shift, 1))
    from_own = lax.broadcasted_iota(I32, (o_ref.shape[0], LANES), 1) < LANES - lane_shift
    for s in range(n_slab):
        o_ref[:, s * LANES:(s + 1) * LANES] = jnp.where(from_own, rolled[s], rolled[s + 1]).astype(o_ref.dtype)


def _realign_columns(w, cols, *, tn=1024):
    layers, d, _ = w.shape
    off, width = cols
    lane_shift = off % LANES
    base = off - lane_shift
    assert lane_shift and base % tn == 0 and width % tn == 0
    return pl.pallas_call(
        functools.partial(_realign_kernel, lane_shift=lane_shift),
        out_shape=jax.ShapeDtypeStruct((layers, d, width), BF16),
        grid=(layers, width // tn),
        in_specs=[pl.BlockSpec((None, d, tn), lambda l, j: (l, 0, base // tn + j)),
                  pl.BlockSpec((None, d, LANES), lambda l, j: (l, 0, (base + tn) // LANES + j * (tn // LANES)))],
        out_specs=pl.BlockSpec((None, d, tn), lambda l, j: (l, 0, j)),
        scratch_shapes=[pltpu.VMEM((d, LANES), F32)],
        compiler_params=_cparams("parallel", "arbitrary"),
        name="realign_columns",
    )(w, w)


def _proj_plain_kernel(x_ref, w_ref, o_ref, *, act):
    acc = jnp.dot(x_ref[...].astype(BF16), w_ref[...], preferred_element_type=F32)
    if act == "gelu":
        acc = _gelu_tanh(acc)
    elif act == "sigmoid":
        acc = _sigmoid(acc)
    o_ref[...] = acc.astype(o_ref.dtype)


def _proj_plain(x, w, layer, cols, act, out_dtype, *, tm=1024, tn=1024):
    n, d = x.shape
    off, width = cols
    assert width % tn == 0 and off % tn == 0
    tm = min(tm, n)
    return pl.pallas_call(
        functools.partial(_proj_plain_kernel, act=act),
        out_shape=jax.ShapeDtypeStruct((n, width), out_dtype),
        grid=(n // tm, width // tn),
        in_specs=[pl.BlockSpec((tm, d), lambda i, j: (i, 0)),
                  pl.BlockSpec((None, d, tn), lambda i, j: (layer, 0, off // tn + j))],
        out_specs=pl.BlockSpec((tm, tn), lambda i, j: (i, j)),
        compiler_params=_cparams("parallel", "arbitrary"),
        name="proj_plain",
    )(x, w)


def _key_to_float(key):
    return pltpu.bitcast(key ^ ((key >> 31) & 0x7FFFFFFF), F32)


def _attn_kernel(q_ref, k_ref, v_ref, qi_ref, ki_ref, wi_ref, gate_ref, o_ref,
                 wb_s, key_s, keyt_s, x_s, thr_s, xrow_s, *state, tq, topk, seq):
    m_s, acc_s = state[:N_KV], state[N_KV:]
    kb = tq
    n_sub = kb // LANES
    qt = pl.program_id(1)
    nkb = qt + 1
    n_hi = IDX_HEADS
    rows_g = HEADS_PER_KV * tq

    wi = wi_ref[...]
    for h in range(n_hi):
        wb_s[h] = jnp.broadcast_to(wi[:, IDX_DIM + h:IDX_DIM + h + 1], (tq, LANES))
    qi = qi_ref[...].reshape(n_hi * tq, LANES)
    q_row = qt * tq + lax.broadcasted_iota(I32, (tq, LANES), 0)
    limit = (q_row // CHUNK + 1) * CHUNK
    lane_pos = lax.broadcasted_iota(I32, (tq, LANES), 1)
    limit_t = ((qt * tq + lax.broadcasted_iota(I32, (1, tq), 1)) // CHUNK + 1) * CHUNK

    def score_body(j, carry):
        start = pl.multiple_of(j * kb, kb)
        ki_blk = ki_ref[pl.ds(start, kb), :].astype(BF16)
        s = lax.dot_general(qi, ki_blk, (((1,), (1,)), ((), ())), preferred_element_type=F32)
        parts = []
        for c in range(n_sub):
            sc = jnp.zeros((tq, LANES), F32)
            for h in range(n_hi):
                sh = s[h * tq:(h + 1) * tq, c * LANES:(c + 1) * LANES]
                sc = sc + wb_s[h] * jnp.maximum(sh, 0.0)
            sc = jnp.where(sc == 0.0, 0.0, sc)
            parts.append(sc)
            kpos = start + c * LANES + lane_pos
            key_s[j, :, c * LANES:(c + 1) * LANES] = jnp.where(kpos < limit, sc, -jnp.inf)
        sc_t = jnp.concatenate(parts, axis=1).T
        kpos_t = start + lax.broadcasted_iota(I32, (kb, tq), 0)
        keyt_s[j] = jnp.where(kpos_t < limit_t, sc_t, -jnp.inf)
        return carry

    lax.fori_loop(0, nkb, score_body, 0)

    kf = np.float32(topk)
    searched = limit_t > topk

    def count_keys(pred):
        def body(j, cnt):
            hit = jnp.where(pred(keyt_s[j], j), 1.0, 0.0)
            return cnt + jnp.sum(hit.reshape(kb // SUBLANES, SUBLANES, tq), axis=0)
        cnt = lax.fori_loop(0, nkb, body, jnp.zeros((SUBLANES, tq), F32))
        return jnp.sum(cnt, axis=0, keepdims=True)

    thr0 = jnp.where(count_keys(lambda s, j: s >= 0.0) >= kf, 0, INT_MIN).astype(I32)

    def thr_body(it, thr):
        cand = thr | jnp.left_shift(jnp.int32(1), 30 - it)
        cand_f = _key_to_float(cand)
        return jnp.where(count_keys(lambda s, j: s >= cand_f) >= kf, cand, thr)

    thr = _key_to_float(lax.fori_loop(0, 31, thr_body, thr0))
    thr = jnp.where(searched, thr, -jnp.inf)
    n_ge = count_keys(lambda s, j: s >= thr)
    need = kf - count_keys(lambda s, j: s > thr)
    xrow_s[...] = jnp.where(searched, seq, -1).astype(I32)
    tie_flag = jnp.max(jnp.where(searched & (n_ge > kf), 1.0, 0.0), axis=(0, 1), keepdims=True)

    @pl.when(tie_flag[0, 0] > 0.0)
    def _():
        nbits = max(int(seq - 1).bit_length(), 1)
        row_pos = lax.broadcasted_iota(I32, (kb, tq), 0)

        def x_body(it, xcut):
            cand = xcut | jnp.left_shift(jnp.int32(1), nbits - 1 - it)
            cnt = count_keys(lambda s, j: (s == thr) & (j * kb + row_pos < cand))
            return jnp.where(cnt < need, cand, xcut)

        xcut = lax.fori_loop(0, nbits, x_body, jnp.zeros((1, tq), I32))
        xrow_s[...] = jnp.where(searched, xcut, -1)

    def to_col(row_f32):
        return jnp.broadcast_to(row_f32, (LANES, tq)).T

    thr_s[...] = to_col(thr)
    x_s[...] = to_col(xrow_s[...].astype(F32)).astype(I32)
    rb = min(tq, 128)
    n_rc = tq // rb
    lane_rb = lax.broadcasted_iota(I32, (rb, LANES), 1)

    for g in range(N_KV):
        m_s[g][...] = jnp.full((rows_g, LANES), NEG_BIG, F32)
        acc_s[g][...] = jnp.zeros((rows_g, 2 * LANES), F32)
    ones_v = jnp.ones((kb, LANES), BF16)

    def attn_body(j, carry):
        start = pl.multiple_of(j * kb, kb)
        bias = []
        for rc in range(n_rc):
            thr_c = thr_s[rc * rb:(rc + 1) * rb, :]
            xcut_c = x_s[rc * rb:(rc + 1) * rb, :]
            parts = []
            for c in range(n_sub):
                kc = key_s[j, rc * rb:(rc + 1) * rb, c * LANES:(c + 1) * LANES]
                kpos = start + c * LANES + lane_rb
                sel = (kc > thr_c) | ((kc == thr_c) & (kpos <= xcut_c))
                parts.append(jnp.where(sel, 0.0, NEG_BIG))
            bias.append(jnp.concatenate(parts, axis=1))
        for g in range(N_KV):
            qg = q_ref[g * HEADS_PER_KV:(g + 1) * HEADS_PER_KV].reshape(rows_g, LANES)
            kg = k_ref[g, pl.ds(start, kb), :]
            vg = jnp.concatenate([v_ref[g, pl.ds(start, kb), :], ones_v], axis=1)
            lg_all = lax.dot_general(qg, kg, (((1,), (1,)), ((), ())), preferred_element_type=F32)
            m_prev_all = m_s[g][...]
            p_parts, a_parts, m_parts = [], [], []
            for r in range(HEADS_PER_KV):
                for rc in range(n_rc):
                    r0 = r * tq + rc * rb
                    lg = lg_all[r0:r0 + rb] + bias[rc]
                    m_prev = m_prev_all[r0:r0 + rb]
                    m_new = jnp.maximum(m_prev, jnp.max(lg, axis=1, keepdims=True))
                    p = jnp.exp2(lg - jnp.concatenate([m_new] * n_sub, axis=1))
                    p_parts.append(p.astype(BF16))
                    a_parts.append(jnp.exp2(m_prev - m_new))
                    m_parts.append(m_new)
            p_all = jnp.concatenate(p_parts, axis=0)
            alpha_all = jnp.concatenate(a_parts, axis=0)
            pv = jnp.dot(p_all, vg, preferred_element_type=F32)
            acc_s[g][...] = jnp.concatenate([alpha_all, alpha_all], axis=1) * acc_s[g][...] + pv
            m_s[g][...] = jnp.concatenate(m_parts, axis=0)
        return carry

    lax.fori_loop(0, nkb, attn_body, 0)

    for g in range(N_KV):
        acc = acc_s[g][...]
        og = acc[:, :LANES] / acc[:, LANES:]
        for r in range(HEADS_PER_KV):
            col = (g * HEADS_PER_KV + r) * LANES
            y = og[r * tq:(r + 1) * tq] * gate_ref[:, col:col + LANES].astype(F32)
            o_ref[:, col:col + LANES] = y.astype(o_ref.dtype)


def _attention(qh, kh, vh, qih, kiwi, sg, batch, seq, *, tq):
    n = batch * seq
    topk = min(TOPK_MAX, seq // 4)
    tq = min(tq, seq)
    nqt = seq // tq
    attn_w = N_HEADS * HEAD_DIM
    rows_g = HEADS_PER_KV * tq
    return pl.pallas_call(
        functools.partial(_attn_kernel, tq=tq, topk=topk, seq=seq),
        out_shape=jax.ShapeDtypeStruct((n, attn_w), BF16),
        grid=(batch, nqt),
        in_specs=[
            pl.BlockSpec((N_HEADS, tq, LANES), lambda b, t: (0, b * nqt + t, 0)),
            pl.BlockSpec((N_KV, seq, LANES), lambda b, t: (0, b, 0)),
            pl.BlockSpec((N_KV, seq, LANES), lambda b, t: (0, b, 0)),
            pl.BlockSpec((IDX_HEADS, tq, LANES), lambda b, t: (0, b * nqt + t, 0)),
            pl.BlockSpec((None, seq, LANES), lambda b, t: (0, b, 0)),
            pl.BlockSpec((None, tq, LANES), lambda b, t: (0, b * nqt + t, 0)),
            pl.BlockSpec((tq, attn_w), lambda b, t: (b * nqt + t, 0)),
        ],
        out_specs=pl.BlockSpec((tq, attn_w), lambda b, t: (b * nqt + t, 0)),
        scratch_shapes=[
            pltpu.VMEM((IDX_HEADS, tq, LANES), F32),
            pltpu.VMEM((nqt, tq, tq), F32),
            pltpu.VMEM((nqt, tq, tq), F32),
            pltpu.VMEM((tq, LANES), I32),
            pltpu.VMEM((tq, LANES), F32),
            pltpu.VMEM((1, tq), I32),
        ] + [pltpu.VMEM((rows_g, LANES), F32)] * N_KV + [pltpu.VMEM((rows_g, 2 * LANES), F32)] * N_KV,
        compiler_params=_cparams("parallel", "arbitrary"),
        name="sparse_attention",
    )(qh, kh, vh, qih, kiwi, kiwi, sg)


def _pool_kernel(p_ref, gate_ref, w_ref, scale_ref, o_ref, buf_a, buf_b, *, seq):
    pad = 16
    g = pl.program_id(1)
    p = p_ref[...]
    zeros = jnp.zeros((pad, p.shape[1]), F32)
    buf_a[0:pad, :] = zeros
    buf_b[0:pad, :] = zeros
    buf_a[pad:pad + seq, :] = p
    s2 = p + buf_a[pad - 1:pad - 1 + seq, :]
    buf_b[pad:pad + seq, :] = s2
    s4 = s2 + buf_b[pad - 2:pad - 2 + seq, :]
    buf_a[pad:pad + seq, :] = s4
    s8 = s4 + buf_a[pad - 4:pad - 4 + seq, :]
    buf_b[pad:pad + seq, :] = s8
    s16 = s8 + buf_b[pad - 8:pad - 8 + seq, :]
    t1 = (lax.broadcasted_iota(I32, p.shape, 0) + 1).astype(F32)
    win = jnp.where(g == 0, 2.0, jnp.where(g == 1, 4.0, jnp.where(g == 2, 8.0, 16.0))).astype(F32)
    total = jnp.where(g == 0, s2, jnp.where(g == 1, s4, jnp.where(g == 2, s8, s16)))
    mean = total / jnp.minimum(t1, win)
    diff = (mean - p).astype(BF16)
    y = jnp.dot(diff, w_ref[...], preferred_element_type=F32)
    o_ref[...] = (y * scale_ref[...] * gate_ref[...].astype(F32)).astype(o_ref.dtype)


def _pool_mixer(px, sg, pool_w, pool_scale, batch, seq):
    n = batch * seq
    width = pool_scale.shape[-1]
    ng = len(POOL_WINDOWS)
    cg = width // ng
    assert POOL_WINDOWS == (2, 4, 8, 16)
    return pl.pallas_call(
        functools.partial(_pool_kernel, seq=seq),
        out_shape=jax.ShapeDtypeStruct((n, width), BF16),
        grid=(batch, ng),
        in_specs=[
            pl.BlockSpec((seq, cg), lambda b, g: (b, g)),
            pl.BlockSpec((seq, cg), lambda b, g: (b, ng + g)),
            pl.BlockSpec((None, cg, cg), lambda b, g: (g, 0, 0)),
            pl.BlockSpec((1, cg), lambda b, g: (0, g)),
        ],
        out_specs=pl.BlockSpec((seq, cg), lambda b, g: (b, g)),
        scratch_shapes=[pltpu.VMEM((seq + 16, cg), F32), pltpu.VMEM((seq + 16, cg), F32)],
        compiler_params=_cparams("parallel", "arbitrary"),
        name="pool_mixer",
    )(px, sg, pool_w, pool_scale.reshape(1, width))


def _lru_kernel(x_ref, gr_ref, gate_ref, cw_ref, cb_ref, wa_ref, ba_ref, wx_ref, bx_ref, lam_ref,
                o_ref, a_s, b_s, *, seq, ct):
    x = x_ref[...]
    row = lax.broadcasted_iota(I32, (seq, ct), 0)
    xc = jnp.broadcast_to(cb_ref[...], (seq, ct))
    for tap in range(CONV_WIDTH):
        d = CONV_WIDTH - 1 - tap
        x_d = x if d == 0 else jnp.where(row >= d, pltpu.roll(x, d, 0), 0.0)
        xc = xc + x_d * cw_ref[tap:tap + 1, :]
    xcb = xc.astype(BF16)
    nb = ct // LRU_BLOCK
    r_parts, i_parts = [], []
    for blk in range(nb):
        xb = xcb[:, blk * LRU_BLOCK:(blk + 1) * LRU_BLOCK]
        r_parts.append(jnp.dot(xb, wa_ref[blk], preferred_element_type=F32))
        i_parts.append(jnp.dot(xb, wx_ref[blk], preferred_element_type=F32))
    r = _sigmoid(jnp.concatenate(r_parts, axis=1) + ba_ref[...])
    gi = _sigmoid(jnp.concatenate(i_parts, axis=1) + bx_ref[...])
    lam = lam_ref[...]
    softplus_neg_lam = jnp.log(1.0 + jnp.exp(-lam))
    log_a = -LRU_C * r * softplus_neg_lam
    a = jnp.exp(log_a)
    one_m_a2 = 1.0 - a * a
    root = jnp.where(one_m_a2 > 0.0, one_m_a2 * lax.rsqrt(one_m_a2), 0.0)
    b = root * (gi * xc)

    tiles = (seq // SUBLANES, SUBLANES, ct)
    a = a.reshape(tiles)
    b = b.reshape(tiles)
    sub = lax.broadcasted_iota(I32, tiles, 1)
    for d in (1, 2, 4):
        keep = sub >= d
        a_sh = jnp.where(keep, pltpu.roll(a, d, 1), 1.0)
        b_sh = jnp.where(keep, pltpu.roll(b, d, 1), 0.0)
        b = a * b_sh + b
        a = a * a_sh
    a_s[...] = a.reshape(seq, ct)
    b_s[...] = b.reshape(seq, ct)

    def body(t, carry):
        r0 = pl.multiple_of(t * SUBLANES, SUBLANES)
        h = a_s[pl.ds(r0, SUBLANES), :] * carry + b_s[pl.ds(r0, SUBLANES), :]
        b_s[pl.ds(r0, SUBLANES), :] = h
        return jnp.broadcast_to(h[SUBLANES - 1:SUBLANES, :], (SUBLANES, ct))

    lax.fori_loop(0, seq // SUBLANES, body, jnp.zeros((SUBLANES, ct), F32), unroll=8)
    h = b_s[...]
    o_ref[...] = (h * gr_ref[...].astype(F32) * gate_ref[...].astype(F32)).astype(o_ref.dtype)


def _lru_mixer(px, gg, sg, conv_w, conv_b, wa, ba, wx, bx, lam, batch, seq, *, ct=256):
    n = batch * seq
    width = conv_b.shape[-1]
    nct = width // ct
    nb = ct // LRU_BLOCK
    row = lambda a: a.reshape(1, width)
    rspec = pl.BlockSpec((1, ct), lambda b, j: (0, j))
    return pl.pallas_call(
        functools.partial(_lru_kernel, seq=seq, ct=ct),
        out_shape=jax.ShapeDtypeStruct((n, width), BF16),
        grid=(batch, nct),
        in_specs=[
            pl.BlockSpec((seq, ct), lambda b, j: (b, nct + j)),
            pl.BlockSpec((seq, ct), lambda b, j: (b, j)),
            pl.BlockSpec((seq, ct), lambda b, j: (b, 2 * nct + j)),
            pl.BlockSpec((CONV_WIDTH, ct), lambda b, j: (0, j)),
            rspec,
            pl.BlockSpec((nb, LRU_BLOCK, LRU_BLOCK), lambda b, j: (j, 0, 0)),
            rspec,
            pl.BlockSpec((nb, LRU_BLOCK, LRU_BLOCK), lambda b, j: (j, 0, 0)),
            rspec,
            rspec,
        ],
        out_specs=pl.BlockSpec((seq, ct), lambda b, j: (b, j)),
        scratch_shapes=[pltpu.VMEM((seq, ct), F32),
                        pltpu.VMEM((seq, ct), F32)],
        compiler_params=_cparams("parallel", "arbitrary"),
        name="rglru_mixer",
    )(px, gg, sg, conv_w, row(conv_b), wa, row(ba), wx, row(bx), row(lam))


def _out_proj_ln_kernel(ya_ref, yb_ref, yc_ref, w_ref, x_ref, g_ref, b_ref, o_ref, ob_ref, *, alpha):
    acc = jnp.dot(ya_ref[...], w_ref[0], preferred_element_type=F32)
    acc = acc + jnp.dot(yb_ref[...], w_ref[1], preferred_element_type=F32)
    acc = acc + jnp.dot(yc_ref[...], w_ref[2], preferred_element_type=F32)
    out = _layer_norm(alpha * x_ref[...] + acc, g_ref[...], b_ref[...])
    o_ref[...] = out
    ob_ref[...] = out.astype(BF16)


def _out_proj_ln(ya, yb, yc, w3, x, g, b, alpha, *, tm=256):
    n, width = ya.shape
    d = w3.shape[-1]
    tm = min(tm, n)
    aspec = pl.BlockSpec((tm, width), lambda i: (i, 0))
    tok = pl.BlockSpec((tm, d), lambda i: (i, 0))
    row = pl.BlockSpec((1, d), lambda i: (0, 0))
    return pl.pallas_call(
        functools.partial(_out_proj_ln_kernel, alpha=alpha),
        out_shape=(jax.ShapeDtypeStruct((n, d), F32), jax.ShapeDtypeStruct((n, d), BF16)),
        grid=(n // tm,),
        in_specs=[aspec, aspec, aspec,
                  pl.BlockSpec((3, width, d), lambda i: (0, 0, 0), pipeline_mode=pl.Buffered(1)),
                  tok, row, row],
        out_specs=(tok, tok),
        compiler_params=_cparams("parallel"),
        name="out_proj_ln",
    )(ya, yb, yc, w3, x, g.reshape(1, d), b.reshape(1, d))


def _ffn_up_kernel(x_ref, wg_ref, wu_ref, o_ref):
    xb = x_ref[...]
    hg = jnp.dot(xb, wg_ref[...], preferred_element_type=F32)
    hu = jnp.dot(xb, wu_ref[...], preferred_element_type=F32)
    o_ref[...] = (hg * _sigmoid(hg) * hu).astype(o_ref.dtype)


def _ffn_down_ln_kernel(h_ref, w_ref, x_ref, g_ref, b_ref, o_ref, ob_ref, *, alpha):
    y = jnp.dot(h_ref[...], w_ref[...], preferred_element_type=F32)
    out = _layer_norm(alpha * x_ref[...] + y, g_ref[...], b_ref[...])
    o_ref[...] = out
    ob_ref[...] = out.astype(BF16)


def _ffn_dense(xb, x, wg, wu, wd, g, b, alpha, *, tm_up=1024, tn_up=512, tm_down=256):
    n, d = x.shape
    ff = wg.shape[1]
    ff_real = wd.shape[0]
    assert ff_real % LANES == 0
    tm_up, tm_down = min(tm_up, n), min(tm_down, n)
    h = pl.pallas_call(
        _ffn_up_kernel,
        out_shape=jax.ShapeDtypeStruct((n, ff), BF16),
        grid=(n // tm_up, ff // tn_up),
        in_specs=[pl.BlockSpec((tm_up, d), lambda i, j: (i, 0)),
                  pl.BlockSpec((d, tn_up), lambda i, j: (0, j)),
                  pl.BlockSpec((d, tn_up), lambda i, j: (0, j))],
        out_specs=pl.BlockSpec((tm_up, tn_up), lambda i, j: (i, j)),
        compiler_params=_cparams("parallel", "arbitrary"),
        name="ffn_up",
    )(xb, wg, wu)
    tok = pl.BlockSpec((tm_down, d), lambda i: (i, 0))
    row = pl.BlockSpec((1, d), lambda i: (0, 0))
    return pl.pallas_call(
        functools.partial(_ffn_down_ln_kernel, alpha=alpha),
        out_shape=(jax.ShapeDtypeStruct((n, d), F32), jax.ShapeDtypeStruct((n, d), BF16)),
        grid=(n // tm_down,),
        in_specs=[pl.BlockSpec((tm_down, ff_real), lambda i: (i, 0)),
                  pl.BlockSpec((ff_real, d), lambda i: (0, 0), pipeline_mode=pl.Buffered(1)),
                  tok, row, row],
        out_specs=(tok, tok),
        compiler_params=_cparams("parallel"),
        name="ffn_down_ln",
    )(h, wd, x, g.reshape(1, d), b.reshape(1, d))


def _router_kernel(x_ref, w_ref, ids_ref, wts_ref):
    logits = jnp.dot(x_ref[...], w_ref[...], preferred_element_type=F32, precision=lax.Precision.HIGHEST)
    lane_i = lax.broadcasted_iota(I32, logits.shape, 1)
    lane = lane_i.astype(F32)
    logits = jnp.where(lane_i < N_EXPERTS, logits, -jnp.inf)
    m1 = jnp.max(logits, axis=1, keepdims=True)
    i1 = jnp.min(jnp.where(logits == m1, lane, float(LANES)), axis=1, keepdims=True)
    rest = jnp.where(lane == i1, -jnp.inf, logits)
    m2 = jnp.max(rest, axis=1, keepdims=True)
    i2 = jnp.min(jnp.where(rest == m2, lane, float(LANES)), axis=1, keepdims=True)
    e2 = jnp.exp(m2 - m1)
    w1 = 1.0 / (1.0 + e2)
    w2 = e2 / (1.0 + e2)
    ids_ref[...] = jnp.where(lane_i == 0, i1, jnp.where(lane_i == 1, i2, 0.0)).astype(I32)
    wts_ref[...] = jnp.where(lane_i == 0, w1, jnp.where(lane_i == 1, w2, 0.0))


def _router(x, w_router, *, tm=512):
    n, d = x.shape
    wpad = jnp.zeros((d, LANES), F32).at[:, :N_EXPERTS].set(w_router.astype(F32))
    tm = min(tm, n)
    tok = pl.BlockSpec((tm, LANES), lambda i: (i, 0))
    return pl.pallas_call(
        _router_kernel,
        out_shape=(jax.ShapeDtypeStruct((n, LANES), I32), jax.ShapeDtypeStruct((n, LANES), F32)),
        grid=(n // tm,),
        in_specs=[pl.BlockSpec((tm, d), lambda i: (i, 0)), pl.BlockSpec((d, LANES), lambda i: (0, 0))],
        out_specs=(tok, tok),
        compiler_params=_cparams("parallel"),
        name="moe_router",
    )(x, wpad)


def _gather_rows_kernel(tok_ref, x_hbm, o_ref, buf, sem, *, tm):
    i = pl.program_id(0)

    def issue(tile, slot):
        base = tile * tm

        def body(r, carry):
            t = tok_ref[base + r]
            pltpu.make_async_copy(x_hbm.at[pl.ds(t, 1), :], buf.at[slot, pl.ds(r, 1), :], sem.at[slot]).start()
            return carry

        lax.fori_loop(0, tm, body, 0, unroll=8)

    @pl.when(i == 0)
    def _():
        issue(0, 0)

    @pl.when(i + 1 < pl.num_programs(0))
    def _():
        issue(i + 1, (i + 1) % 2)

    slot = i % 2
    pltpu.make_async_copy(x_hbm.at[pl.ds(0, tm), :], buf.at[slot], sem.at[slot]).wait()
    o_ref[...] = buf[slot].astype(o_ref.dtype)


def _gather_rows(x, row_tok, *, tm=256):
    n, d = x.shape
    rows = row_tok.shape[0]
    return pl.pallas_call(
        functools.partial(_gather_rows_kernel, tm=tm),
        out_shape=jax.ShapeDtypeStruct((rows, d), BF16),
        grid_spec=pltpu.PrefetchScalarGridSpec(
            num_scalar_prefetch=1,
            grid=(rows // tm,),
            in_specs=[pl.BlockSpec(memory_space=pl.ANY)],
            out_specs=pl.BlockSpec((tm, d), lambda i, tok: (i, 0)),
            scratch_shapes=[pltpu.VMEM((2, tm, d), F32), pltpu.SemaphoreType.DMA((2,))],
        ),
        compiler_params=_cparams("arbitrary"),
        name="moe_gather_rows",
    )(row_tok, x)


def _expert_changed(te_ref, i):
    return (i == 0) | (te_ref[i] != te_ref[jnp.maximum(i - 1, 0)])


def _stream_expert_weights(j, i, n_j, col_tile, te_ref, run_ref, nxt_ref, meta_ref, w_hbm, stage, work, sem):
    @pl.when(_expert_changed(te_ref, i))
    def _():
        k = j * meta_ref[0] + run_ref[i]
        slot = k % 2

        def copies(e, jj, s):
            col = pl.ds(pl.multiple_of(jj * col_tile, col_tile), col_tile)
            return [pltpu.make_async_copy(w.at[e, :, col], st.at[s], sem.at[s, a])
                    for a, (w, st) in enumerate(zip(w_hbm, stage))]

        @pl.when(k == 0)
        def _():
            for c in copies(te_ref[i], j, slot):
                c.start()

        has_next_run = nxt_ref[i] >= 0
        e_next = jnp.where(has_next_run, nxt_ref[i], meta_ref[1])
        j_next = jnp.where(has_next_run, j, j + 1)

        @pl.when(j_next < n_j)
        def _():
            for c in copies(e_next, j_next, 1 - slot):
                c.start()

        for c in copies(te_ref[i], j, slot):
            c.wait()
        for st, wk in zip(stage, work):
            wk[...] = st[slot].astype(BF16)


def _moe_up_kernel(te_ref, nv_ref, run_ref, nxt_ref, meta_ref, x_ref, wg_hbm, wu_hbm, o_ref,
                   stage_g, stage_u, wg_s, wu_s, sem, *, tf):
    j, i = pl.program_id(0), pl.program_id(1)
    _stream_expert_weights(j, i, pl.num_programs(0), tf, te_ref, run_ref, nxt_ref, meta_ref,
                           (wg_hbm, wu_hbm), (stage_g, stage_u), (wg_s, wu_s), sem)

    @pl.when(i < nv_ref[0])
    def _():
        xb = x_ref[...]
        hg = jnp.dot(xb, wg_s[...], preferred_element_type=F32)
        hu = jnp.dot(xb, wu_s[...], preferred_element_type=F32)
        o_ref[...] = (hg * _sigmoid(hg) * hu).astype(o_ref.dtype)

    @pl.when(i >= nv_ref[0])
    def _():
        o_ref[...] = jnp.zeros(o_ref.shape, o_ref.dtype)


def _moe_up(xs, wg, wu, sched, *, tm, tf=1024):
    rows, d = xs.shape
    ff = wg.shape[-1]
    return pl.pallas_call(
        functools.partial(_moe_up_kernel, tf=tf),
        out_shape=jax.ShapeDtypeStruct((rows, ff), BF16),
        grid_spec=pltpu.PrefetchScalarGridSpec(
            num_scalar_prefetch=len(sched),
            grid=(ff // tf, rows // tm),
            in_specs=[pl.BlockSpec((tm, d), lambda j, i, *_: (i, 0)),
                      pl.BlockSpec(memory_space=pl.ANY),
                      pl.BlockSpec(memory_space=pl.ANY)],
            out_specs=pl.BlockSpec((tm, tf), lambda j, i, *_: (i, j)),
            scratch_shapes=[pltpu.VMEM((2, d, tf), F32), pltpu.VMEM((2, d, tf), F32),
                            pltpu.VMEM((d, tf), BF16), pltpu.VMEM((d, tf), BF16),
                            pltpu.SemaphoreType.DMA((2, 2))],
        ),
        compiler_params=_cparams("arbitrary", "arbitrary"),
        name="moe_up",
    )(*sched, xs, wg, wu)


def _moe_down_kernel(te_ref, nv_ref, run_ref, nxt_ref, meta_ref, h_ref, wd_hbm, o_ref, stage_d, wd_s, sem, *, tn):
    j, i = pl.program_id(0), pl.program_id(1)
    _stream_expert_weights(j, i, pl.num_programs(0), tn, te_ref, run_ref, nxt_ref, meta_ref,
                           (wd_hbm,), (stage_d,), (wd_s,), sem)

    @pl.when(i < nv_ref[0])
    def _():
        o_ref[...] = jnp.dot(h_ref[...], wd_s[...], preferred_element_type=F32)

    @pl.when(i >= nv_ref[0])
    def _():
        o_ref[...] = jnp.zeros(o_ref.shape, o_ref.dtype)


def _moe_down(h, wd, sched, *, tm, tn=512):
    rows, ff = h.shape
    d = wd.shape[-1]
    return pl.pallas_call(
        functools.partial(_moe_down_kernel, tn=tn),
        out_shape=jax.ShapeDtypeStruct((rows, d), F32),
        grid_spec=pltpu.PrefetchScalarGridSpec(
            num_scalar_prefetch=len(sched),
            grid=(d // tn, rows // tm),
            in_specs=[pl.BlockSpec((tm, ff), lambda j, i, *_: (i, 0)),
                      pl.BlockSpec(memory_space=pl.ANY)],
            out_specs=pl.BlockSpec((tm, tn), lambda j, i, *_: (i, j)),
            scratch_shapes=[pltpu.VMEM((2, ff, tn), F32), pltpu.VMEM((ff, tn), BF16),
                            pltpu.SemaphoreType.DMA((2, 1))],
        ),
        compiler_params=_cparams("arbitrary", "arbitrary"),
        name="moe_down",
    )(*sched, h, wd)


def _combine_ln_kernel(pos_ref, y_hbm, x_ref, wts_ref, g_ref, b_ref, o_ref, buf, sem, *, tm, alpha):
    i = pl.program_id(0)

    def issue(tile, slot):
        base = tile * tm

        def body(r, carry):
            for k in range(TOP_K):
                row = pos_ref[(base + r) * TOP_K + k]
                pltpu.make_async_copy(y_hbm.at[pl.ds(row, 1), :], buf.at[slot, k, pl.ds(r, 1), :],
                                      sem.at[slot]).start()
            return carry

        lax.fori_loop(0, tm, body, 0, unroll=8)

    @pl.when(i == 0)
    def _():
        issue(0, 0)

    @pl.when(i + 1 < pl.num_programs(0))
    def _():
        issue(i + 1, (i + 1) % 2)

    slot = i % 2
    for k in range(TOP_K):
        pltpu.make_async_copy(y_hbm.at[pl.ds(0, tm), :], buf.at[slot, k], sem.at[slot]).wait()
    wts = wts_ref[...]
    y = buf[slot, 0] * wts[:, 0:1] + buf[slot, 1] * wts[:, 1:2]
    o_ref[...] = _layer_norm(alpha * x_ref[...] + y, g_ref[...], b_ref[...])


def _combine_ln(yrows, pos, x, wts, g, b, alpha, *, tm=128):
    n, d = x.shape
    tm = min(tm, n)
    return pl.pallas_call(
        functools.partial(_combine_ln_kernel, tm=tm, alpha=alpha),
        out_shape=jax.ShapeDtypeStruct((n, d), F32),
        grid_spec=pltpu.PrefetchScalarGridSpec(
            num_scalar_prefetch=1,
            grid=(n // tm,),
            in_specs=[pl.BlockSpec(memory_space=pl.ANY),
                      pl.BlockSpec((tm, d), lambda i, pos: (i, 0)),
                      pl.BlockSpec((tm, LANES), lambda i, pos: (i, 0)),
                      pl.BlockSpec((1, d), lambda i, pos: (0, 0)),
                      pl.BlockSpec((1, d), lambda i, pos: (0, 0))],
            out_specs=pl.BlockSpec((tm, d), lambda i, pos: (i, 0)),
            scratch_shapes=[pltpu.VMEM((2, TOP_K, tm, d), F32), pltpu.SemaphoreType.DMA((2,))],
        ),
        compiler_params=_cparams("arbitrary"),
        name="moe_combine_ln",
    )(pos, yrows, x, wts, g.reshape(1, d), b.reshape(1, d))


def _moe_block(x, w_router, wg, wu, wd, g, b, alpha, *, tm=256):
    n, d = x.shape
    ids, wts = _router(x, w_router)
    e_flat = ids[:, :TOP_K].reshape(-1)
    n_assign = n * TOP_K
    onehot = (e_flat[:, None] == jnp.arange(N_EXPERTS, dtype=I32)[None, :]).astype(I32)
    rank = jnp.sum((jnp.cumsum(onehot, axis=0) - onehot) * onehot, axis=1)
    counts = jnp.sum(onehot, axis=0)
    padded = (counts + tm - 1) // tm * tm
    end_padded = jnp.cumsum(padded)
    start_padded = end_padded - padded
    dest = (start_padded[e_flat] + rank).astype(I32)
    rows = n_assign + N_EXPERTS * tm
    n_tiles = rows // tm
    flat_tok = jnp.arange(n_assign, dtype=I32) // TOP_K
    row_tok = jnp.zeros((rows,), I32).at[dest].set(flat_tok)
    tile_start = jnp.arange(n_tiles, dtype=I32) * tm
    tile_e = jnp.minimum(jnp.sum((tile_start[:, None] >= end_padded[None, :]).astype(I32), axis=1),
                         N_EXPERTS - 1).astype(I32)
    n_valid = (end_padded[-1:] // tm).astype(I32)
    tile_ix = jnp.arange(n_tiles, dtype=I32)
    tile_e = jnp.where(tile_ix < n_valid[0], tile_e, tile_e[jnp.maximum(n_valid[0] - 1, 0)])
    is_start = jnp.concatenate([jnp.ones((1,), I32), (tile_e[1:] != tile_e[:-1]).astype(I32)])
    run_id = (jnp.cumsum(is_start) - 1).astype(I32)
    larger = jnp.where(tile_e[None, :] > tile_e[:, None], tile_e[None, :], N_EXPERTS)
    nxt_e = jnp.min(larger, axis=1)
    nxt_e = jnp.where(nxt_e < N_EXPERTS, nxt_e, -1).astype(I32)
    meta = jnp.stack([run_id[-1] + 1, tile_e[0]]).astype(I32)
    sched = (tile_e, n_valid, run_id, nxt_e, meta)
    xs = _gather_rows(x, row_tok, tm=tm)
    h = _moe_up(xs, wg, wu, sched, tm=tm)
    yrows = _moe_down(h, wd, sched, tm=tm)
    return _combine_ln(yrows, dest, x, wts, g, b, alpha)


def _w_in_columns(d_model):
    attn_w = N_HEADS * HEAD_DIM
    kv_w = N_KV * HEAD_DIM
    o = np.cumsum([0, attn_w, kv_w, kv_w, IDX_HEADS * IDX_DIM, IDX_DIM, IDX_HEADS, d_model, d_model, d_model])
    o = [int(v) for v in o]
    return {"q": (o[0], attn_w), "k": (o[1], kv_w), "v": (o[2], kv_w), "qi": (o[3], o[4] - o[3]),
            "kiwi": (o[4], LANES),
            "tail": (o[6], 6 * d_model)}


def _mixer(x_mm, tabs, w_in, w_tail, layer, pool_w, pool_scale, conv_w, conv_b, wa, ba, wx, bx, lam,
           batch, seq, d_model, tq):
    cos_a, sin_a, cos_i, sin_ia, sin_ib, cos_q, sin_q, cos_iq = tabs
    cols = _w_in_columns(d_model)
    rope_a = ((cos_a, sin_a), (HEAD_DIM // 2,))
    rope_i = (LANES - IDX_DIM // 2, IDX_DIM // 2)
    qh = _proj_heads(x_mm, w_in, layer, cols["q"], (cos_q, sin_q), rope_a[1], BF16)
    kh = _proj_heads(x_mm, w_in, layer, cols["k"], *rope_a, BF16)
    vh = _proj_heads(x_mm, w_in, layer, cols["v"], (), (), BF16)
    qih = _proj_heads(x_mm, w_in, layer, cols["qi"], (cos_iq, sin_ia, sin_ib), rope_i, BF16, head_width=IDX_DIM)
    kiwi = _proj_heads(x_mm, w_in, layer, cols["kiwi"], (cos_i, sin_ia, sin_ib), rope_i, F32)
    px = _proj_plain(x_mm, w_tail, layer, (0, 2 * d_model), None, F32)
    gg = _proj_plain(x_mm, w_tail, layer, (2 * d_model, d_model), "gelu", BF16)
    sg = _proj_plain(x_mm, w_tail, layer, (3 * d_model, 3 * d_model), "sigmoid", BF16)
    ya = _attention(qh, kh, vh, qih, kiwi, sg, batch, seq, tq=tq)
    yb = _pool_mixer(px, sg, pool_w.astype(BF16), pool_scale, batch, seq)
    yc = _lru_mixer(px, gg, sg, conv_w, conv_b, wa.astype(BF16), ba, wx.astype(BF16), bx, lam, batch, seq)
    return ya, yb, yc


def _pad_ff(w, axis, mult):
    ff = w.shape[axis]
    padn = (-ff) % mult
    if padn == 0:
        return w
    widths = [(0, 0)] * w.ndim
    widths[axis] = (0, padn)
    return jnp.pad(w, widths)


def kernel(x, positions, mix_w_in, mix_w_out, pool_w, pool_scale, conv_w, conv_b, lru_wa, lru_ba, lru_wx, lru_bx, lru_lam, ln_mix_g, ln_mix_b, ln_ffn_g, ln_ffn_b, dense_w_gate, dense_w_up, dense_w_down, moe_router, moe_w_gate, moe_w_up, moe_w_down):
    batch, seq, d_model = x.shape
    depth = mix_w_in.shape[0]
    alpha = np.float32((2 * depth) ** 0.25)
    n = batch * seq
    tabs = _rope_tables(positions)
    xf = x.reshape(n, d_model)
    x_mm = xf
    w_tail = _realign_columns(mix_w_in, _w_in_columns(d_model)["tail"])
    for layer in range(depth):
        ya, yb, yc = _mixer(x_mm, tabs, mix_w_in, w_tail, layer, pool_w[layer], pool_scale[layer],
                            conv_w[layer], conv_b[layer], lru_wa[layer], lru_ba[layer], lru_wx[layer],
                            lru_bx[layer], lru_lam[layer], batch, seq, d_model, tq=256)
        w_out3 = mix_w_out[layer].astype(BF16).reshape(3, -1, d_model)
        xf, xb = _out_proj_ln(ya, yb, yc, w_out3, xf, ln_mix_g[layer], ln_mix_b[layer], alpha)
        j = layer // 2
        if layer % 2 == 0:
            wg = _pad_ff(dense_w_gate[j].astype(BF16), 1, 512)
            wu = _pad_ff(dense_w_up[j].astype(BF16), 1, 512)
            wd = dense_w_down[j].astype(BF16)
            xf, x_mm = _ffn_dense(xb, xf, wg, wu, wd, ln_ffn_g[layer], ln_ffn_b[layer], alpha)
        else:
            xf = _moe_block(xf, moe_router[j], moe_w_gate[j], moe_w_up[j], moe_w_down[j],
                            ln_ffn_g[layer], ln_ffn_b[layer], alpha)
            x_mm = xf
    return xf.reshape(batch, seq, d_model)
```

```python
import functools

import jax
import jax.numpy as jnp
import numpy as np
from jax import lax
from jax.experimental import pallas as pl
from jax.experimental.pallas import tpu as pltpu

F32 = jnp.float32
BF16 = jnp.bfloat16
I32 = jnp.int32

LANES = 128
SUBLANES = 8
VMEM_LIMIT = 56 * 1024 * 1024

CHUNK = 64
N_HEADS = 16
HEAD_DIM = 128
N_KV = 4
HEADS_PER_KV = N_HEADS // N_KV
IDX_HEADS = 16
IDX_DIM = 64
TOPK_MAX = 256
ROPE_THETA = 10000.0
POOL_WINDOWS = (2, 4, 8, 16)
LRU_BLOCK = 128
CONV_WIDTH = 4
LRU_C = 8.0
N_EXPERTS = 8
TOP_K = 2
LN_EPS = 1e-5
LN_ROWS = 128
INT_MIN = -2 ** 31
NEG_BIG = -1e30


def _cparams(*sem):
    return pltpu.CompilerParams(dimension_semantics=sem, vmem_limit_bytes=VMEM_LIMIT)


def _sigmoid(x):
    return 0.5 * (1.0 + jnp.tanh(0.5 * x))


def _gelu_tanh(x):
    c = np.float32(np.sqrt(2.0 / np.pi))
    return 0.5 * x * (1.0 + jnp.tanh(c * (x + np.float32(0.044715) * (x * x * x))))


def _layer_norm(y, g, b):
    mu = jnp.mean(y, axis=-1, keepdims=True)
    d = y - mu
    var = jnp.mean(d * d, axis=-1, keepdims=True)
    return d * lax.rsqrt(var + LN_EPS) * g + b


def _rope_tab_kernel(pos_ref, inv_a_ref, inv_i_ref, sgn_a_ref, m_cos_ref, add_cos_ref,
                     m_sa_ref, m_sb_ref, cos_a_ref, sin_a_ref, cos_i_ref, sin_ia_ref, sin_ib_ref,
                     cos_q_ref, sin_q_ref, cos_iq_ref):
    pos = pos_ref[...]
    ang_a = pos * inv_a_ref[...]
    cos_a = jnp.cos(ang_a)
    sin_a = jnp.sin(ang_a) * sgn_a_ref[...]
    cos_a_ref[...] = cos_a
    sin_a_ref[...] = sin_a
    q_scale = np.float32(HEAD_DIM ** -0.5 * np.log2(np.e))
    cos_q_ref[...] = cos_a * q_scale
    sin_q_ref[...] = sin_a * q_scale
    ang_i = pos * inv_i_ref[...]
    s_i = jnp.sin(ang_i)
    cos_iq = jnp.cos(ang_i) * m_cos_ref[...]
    cos_iq_ref[...] = cos_iq
    cos_i_ref[...] = cos_iq + add_cos_ref[...]
    sin_ia_ref[...] = s_i * m_sa_ref[...]
    sin_ib_ref[...] = s_i * m_sb_ref[...]


def _rope_tables(positions):
    n = positions.size
    pos = jnp.broadcast_to(positions.reshape(n, 1).astype(F32), (n, LANES))
    lane = np.arange(LANES)
    inv_a = (ROPE_THETA ** (-jnp.arange(0, HEAD_DIM, 2, dtype=F32) / HEAD_DIM))
    inv_i = (ROPE_THETA ** (-jnp.arange(0, IDX_DIM, 2, dtype=F32) / IDX_DIM))
    inv_a_row = jnp.concatenate([inv_a, inv_a])[None, :]
    inv_i_row = jnp.concatenate([inv_i, inv_i, jnp.zeros((LANES - IDX_DIM,), F32)])[None, :]
    sgn_a = jnp.asarray(np.where(lane < HEAD_DIM // 2, -1.0, 1.0), F32)[None, :]
    m_cos = jnp.asarray((lane < IDX_DIM).astype(np.float32))[None, :]
    wi_scale = (IDX_HEADS ** -0.5) * (IDX_DIM ** -0.5)
    add_cos = jnp.asarray(np.where((lane >= IDX_DIM) & (lane < IDX_DIM + IDX_HEADS), wi_scale, 0.0), F32)[None, :]
    m_sa = jnp.asarray(np.where(lane < IDX_DIM // 2, -1.0, 0.0), F32)[None, :]
    m_sb = jnp.asarray(np.where((lane >= IDX_DIM // 2) & (lane < IDX_DIM), 1.0, 0.0), F32)[None, :]
    tm = min(n, 1024)
    row = pl.BlockSpec((1, LANES), lambda i: (0, 0))
    tok = pl.BlockSpec((tm, LANES), lambda i: (i, 0))
    out = jax.ShapeDtypeStruct((n, LANES), F32)
    return pl.pallas_call(
        _rope_tab_kernel,
        out_shape=(out,) * 8,
        grid=(n // tm,),
        in_specs=[tok] + [row] * 7,
        out_specs=(tok,) * 8,
        compiler_params=_cparams("parallel"),
        name="rope_tables",
    )(pos, inv_a_row, inv_i_row, sgn_a, m_cos, add_cos, m_sa, m_sb)


def _stream_weight_tile(wt_hbm, stage, w_s, sem, *, layer, off, tn):
    j = pl.program_id(0)

    def copy(jj, slot):
        rows = pl.ds(pl.multiple_of(off + jj * tn, SUBLANES), tn)
        return pltpu.make_async_copy(wt_hbm.at[layer, rows, :], stage.at[slot], sem.at[slot])

    @pl.when(pl.program_id(1) == 0)
    def _():
        slot = j % 2

        @pl.when(j == 0)
        def _():
            copy(0, 0).start()

        @pl.when(j + 1 < pl.num_programs(0))
        def _():
            copy(j + 1, 1 - slot).start()

        copy(j, slot).wait()
        w_s[...] = stage[slot].astype(BF16)


def _dot_nt(x, w):
    return lax.dot_general(x, w, (((1,), (1,)), ((), ())), preferred_element_type=F32)


def _proj_heads_kernel(x_ref, wt_hbm, *rest, layer, off, tn, shifts, heads, head_width):
    tabs, o_ref, stage, w_s, sem = rest[:-4], rest[-4], rest[-3], rest[-2], rest[-1]
    _stream_weight_tile(wt_hbm, stage, w_s, sem, layer=layer, off=off, tn=tn)
    acc = _dot_nt(x_ref[...].astype(BF16), w_s[...])
    per_slab = LANES // head_width
    for h in range(heads):
        xh = acc[:, (h // per_slab) * LANES:(h // per_slab + 1) * LANES]
        if h % per_slab:
            xh = pltpu.roll(xh, LANES - (h % per_slab) * head_width, 1)
        if tabs:
            y = xh * tabs[0][...]
            for s, t in zip(shifts, tabs[1:]):
                y = y + pltpu.roll(xh, s, 1) * t[...]
        else:
            y = xh
        o_ref[h] = y.astype(o_ref.dtype)


def _weight_stream_scratch(tn, d):
    return [pltpu.VMEM((2, tn, d), F32), pltpu.VMEM((tn, d), BF16), pltpu.SemaphoreType.DMA((2,))]


def _proj_heads(x, wt, layer, cols, tabs, shifts, out_dtype, *, head_width=LANES, tm=1024, tile_cols=1024):
    n, d = x.shape
    off, width = cols
    tn = min(tile_cols, width)
    hp = tn // head_width
    assert width % tn == 0 and off % SUBLANES == 0 and (head_width == LANES or tabs)
    tm = min(tm, n)
    tab_spec = pl.BlockSpec((tm, LANES), lambda j, i: (i, 0))
    return pl.pallas_call(
        functools.partial(_proj_heads_kernel, layer=layer, off=off, tn=tn, shifts=shifts, heads=hp,
                          head_width=head_width),
        out_shape=jax.ShapeDtypeStruct((width // head_width, n, LANES), out_dtype),
        grid=(width // tn, n // tm),
        in_specs=[pl.BlockSpec((tm, d), lambda j, i: (i, 0)),
                  pl.BlockSpec(memory_space=pl.ANY)] + [tab_spec] * len(tabs),
        out_specs=pl.BlockSpec((hp, tm, LANES), lambda j, i: (j, i, 0)),
        scratch_shapes=_weight_stream_scratch(tn, d),
        compiler_params=_cparams("arbitrary", "arbitrary"),
        name="proj_heads",
    )(x, wt, *tabs)


def _proj_plain_kernel(x_ref, wt_hbm, o_ref, stage, w_s, sem, *, layer, off, tn, act):
    _stream_weight_tile(wt_hbm, stage, w_s, sem, layer=layer, off=off, tn=tn)
    acc = _dot_nt(x_ref[...].astype(BF16), w_s[...])
    if act == "gelu":
        acc = _gelu_tanh(acc)
    elif act == "sigmoid":
        acc = _sigmoid(acc)
    o_ref[...] = acc.astype(o_ref.dtype)


def _proj_plain(x, wt, layer, cols, act, out_dtype, *, tm=1024, tn=1024):
    n, d = x.shape
    off, width = cols
    assert width % tn == 0 and off % SUBLANES == 0
    tm = min(tm, n)
    return pl.pallas_call(
        functools.partial(_proj_plain_kernel, layer=layer, off=off, tn=tn, act=act),
        out_shape=jax.ShapeDtypeStruct((n, width), out_dtype),
        grid=(width // tn, n // tm),
        in_specs=[pl.BlockSpec((tm, d), lambda j, i: (i, 0)),
                  pl.BlockSpec(memory_space=pl.ANY)],
        out_specs=pl.BlockSpec((tm, tn), lambda j, i: (i, j)),
        scratch_shapes=_weight_stream_scratch(tn, d),
        compiler_params=_cparams("arbitrary", "arbitrary"),
        name="proj_plain",
    )(x, wt)


def _key_to_float(key):
    return pltpu.bitcast(key ^ ((key >> 31) & 0x7FFFFFFF), F32)


def _attn_kernel(q_ref, k_ref, v_ref, qi_ref, ki_ref, wi_ref, gate_ref, o_ref,
                 wb_s, key_s, keyt_s, x_s, thr_s, xrow_s, *state, tq, topk, seq):
    m_s, acc_s = state[:N_KV], state[N_KV:]
    kb = tq
    n_sub = kb // LANES
    qt = pl.program_id(1)
    nkb = qt + 1
    n_hi = IDX_HEADS
    rows_g = HEADS_PER_KV * tq

    wi = wi_ref[...]
    for h in range(n_hi):
        wb_s[h] = jnp.broadcast_to(wi[:, IDX_DIM + h:IDX_DIM + h + 1], (tq, LANES))
    qi = qi_ref[...].reshape(n_hi * tq, LANES)
    q_row = qt * tq + lax.broadcasted_iota(I32, (tq, LANES), 0)
    limit = (q_row // CHUNK + 1) * CHUNK
    lane_pos = lax.broadcasted_iota(I32, (tq, LANES), 1)
    limit_t = ((qt * tq + lax.broadcasted_iota(I32, (1, tq), 1)) // CHUNK + 1) * CHUNK

    def score_body(j, carry):
        start = pl.multiple_of(j * kb, kb)
        ki_blk = ki_ref[pl.ds(start, kb), :].astype(BF16)
        s = lax.dot_general(qi, ki_blk, (((1,), (1,)), ((), ())), preferred_element_type=F32)
        parts = []
        for c in range(n_sub):
            sc = jnp.zeros((tq, LANES), F32)
            for h in range(n_hi):
                sh = s[h * tq:(h + 1) * tq, c * LANES:(c + 1) * LANES]
                sc = sc + wb_s[h] * jnp.maximum(sh, 0.0)
            sc = jnp.where(sc == 0.0, 0.0, sc)
            parts.append(sc)
            kpos = start + c * LANES + lane_pos
            key_s[j, :, c * LANES:(c + 1) * LANES] = jnp.where(kpos < limit, sc, -jnp.inf)
        sc_t = jnp.concatenate(parts, axis=1).T
        kpos_t = start + lax.broadcasted_iota(I32, (kb, tq), 0)
        keyt_s[j] = jnp.where(kpos_t < limit_t, sc_t, -jnp.inf)
        return carry

    lax.fori_loop(0, nkb, score_body, 0)

    kf = np.float32(topk)
    searched = limit_t > topk

    def count_keys(pred):
        def body(j, cnt):
            hit = jnp.where(pred(keyt_s[j], j), 1.0, 0.0)
            return cnt + jnp.sum(hit.reshape(kb // SUBLANES, SUBLANES, tq), axis=0)
        cnt = lax.fori_loop(0, nkb, body, jnp.zeros((SUBLANES, tq), F32))
        return jnp.sum(cnt, axis=0, keepdims=True)

    thr0 = jnp.where(count_keys(lambda s, j: s >= 0.0) >= kf, 0, INT_MIN).astype(I32)

    def thr_body(it, thr):
        cand = thr | jnp.left_shift(jnp.int32(1), 30 - it)
        cand_f = _key_to_float(cand)
        return jnp.where(count_keys(lambda s, j: s >= cand_f) >= kf, cand, thr)

    thr = _key_to_float(lax.fori_loop(0, 31, thr_body, thr0))
    thr = jnp.where(searched, thr, -jnp.inf)
    n_ge = count_keys(lambda s, j: s >= thr)
    need = kf - count_keys(lambda s, j: s > thr)
    xrow_s[...] = jnp.where(searched, seq, -1).astype(I32)
    tie_flag = jnp.max(jnp.where(searched & (n_ge > kf), 1.0, 0.0), axis=(0, 1), keepdims=True)

    @pl.when(tie_flag[0, 0] > 0.0)
    def _():
        nbits = max(int(seq - 1).bit_length(), 1)
        row_pos = lax.broadcasted_iota(I32, (kb, tq), 0)

        def x_body(it, xcut):
            cand = xcut | jnp.left_shift(jnp.int32(1), nbits - 1 - it)
            cnt = count_keys(lambda s, j: (s == thr) & (j * kb + row_pos < cand))
            return jnp.where(cnt < need, cand, xcut)

        xcut = lax.fori_loop(0, nbits, x_body, jnp.zeros((1, tq), I32))
        xrow_s[...] = jnp.where(searched, xcut, -1)

    def to_col(row_f32):
        return jnp.broadcast_to(row_f32, (LANES, tq)).T

    thr_s[...] = to_col(thr)
    x_s[...] = to_col(xrow_s[...].astype(F32)).astype(I32)
    rb = min(tq, 128)
    n_rc = tq // rb
    lane_rb = lax.broadcasted_iota(I32, (rb, LANES), 1)

    for g in range(N_KV):
        m_s[g][...] = jnp.full((rows_g, LANES), NEG_BIG, F32)
        acc_s[g][...] = jnp.zeros((rows_g, 2 * LANES), F32)
    ones_v = jnp.ones((kb, LANES), BF16)

    def attn_body(j, carry):
        start = pl.multiple_of(j * kb, kb)
        bias = []
        for rc in range(n_rc):
            thr_c = thr_s[rc * rb:(rc + 1) * rb, :]
            xcut_c = x_s[rc * rb:(rc + 1) * rb, :]
            parts = []
            for c in range(n_sub):
                kc = key_s[j, rc * rb:(rc + 1) * rb, c * LANES:(c + 1) * LANES]
                kpos = start + c * LANES + lane_rb
                sel = (kc > thr_c) | ((kc == thr_c) & (kpos <= xcut_c))
                parts.append(jnp.where(sel, 0.0, NEG_BIG))
            bias.append(jnp.concatenate(parts, axis=1))
        for g in range(N_KV):
            qg = q_ref[g * HEADS_PER_KV:(g + 1) * HEADS_PER_KV].reshape(rows_g, LANES)
            kg = k_ref[g, pl.ds(start, kb), :]
            vg = jnp.concatenate([v_ref[g, pl.ds(start, kb), :], ones_v], axis=1)
            lg_all = lax.dot_general(qg, kg, (((1,), (1,)), ((), ())), preferred_element_type=F32)
            m_prev_all = m_s[g][...]
            p_parts, a_parts, m_parts = [], [], []
            for r in range(HEADS_PER_KV):
                for rc in range(n_rc):
                    r0 = r * tq + rc * rb
                    lg = lg_all[r0:r0 + rb] + bias[rc]
                    m_prev = m_prev_all[r0:r0 + rb]
                    m_new = jnp.maximum(m_prev, jnp.max(lg, axis=1, keepdims=True))
                    p = jnp.exp2(lg - jnp.concatenate([m_new] * n_sub, axis=1))
                    p_parts.append(p.astype(BF16))
                    a_parts.append(jnp.exp2(m_prev - m_new))
                    m_parts.append(m_new)
            p_all = jnp.concatenate(p_parts, axis=0)
            alpha_all = jnp.concatenate(a_parts, axis=0)
            pv = jnp.dot(p_all, vg, preferred_element_type=F32)
            acc_s[g][...] = jnp.concatenate([alpha_all, alpha_all], axis=1) * acc_s[g][...] + pv
            m_s[g][...] = jnp.concatenate(m_parts, axis=0)
        return carry

    lax.fori_loop(0, nkb, attn_body, 0)

    for g in range(N_KV):
        acc = acc_s[g][...]
        og = acc[:, :LANES] / acc[:, LANES:]
        for r in range(HEADS_PER_KV):
            col = (g * HEADS_PER_KV + r) * LANES
            y = og[r * tq:(r + 1) * tq] * gate_ref[:, col:col + LANES].astype(F32)
            o_ref[:, col:col + LANES] = y.astype(o_ref.dtype)


def _attention(qh, kh, vh, qih, kiwi, sg, batch, seq, *, tq):
    n = batch * seq
    topk = min(TOPK_MAX, seq // 4)
    tq = min(tq, seq)
    nqt = seq // tq
    attn_w = N_HEADS * HEAD_DIM
    rows_g = HEADS_PER_KV * tq
    return pl.pallas_call(
        functools.partial(_attn_kernel, tq=tq, topk=topk, seq=seq),
        out_shape=jax.ShapeDtypeStruct((n, attn_w), BF16),
        grid=(batch, nqt),
        in_specs=[
            pl.BlockSpec((N_HEADS, tq, LANES), lambda b, t: (0, b * nqt + t, 0)),
            pl.BlockSpec((N_KV, seq, LANES), lambda b, t: (0, b, 0)),
            pl.BlockSpec((N_KV, seq, LANES), lambda b, t: (0, b, 0)),
            pl.BlockSpec((IDX_HEADS, tq, LANES), lambda b, t: (0, b * nqt + t, 0)),
            pl.BlockSpec((None, seq, LANES), lambda b, t: (0, b, 0)),
            pl.BlockSpec((None, tq, LANES), lambda b, t: (0, b * nqt + t, 0)),
            pl.BlockSpec((tq, attn_w), lambda b, t: (b * nqt + t, 0)),
        ],
        out_specs=pl.BlockSpec((tq, attn_w), lambda b, t: (b * nqt + t, 0)),
        scratch_shapes=[
            pltpu.VMEM((IDX_HEADS, tq, LANES), F32),
            pltpu.VMEM((nqt, tq, tq), F32),
            pltpu.VMEM((nqt, tq, tq), F32),
            pltpu.VMEM((tq, LANES), I32),
            pltpu.VMEM((tq, LANES), F32),
            pltpu.VMEM((1, tq), I32),
        ] + [pltpu.VMEM((rows_g, LANES), F32)] * N_KV + [pltpu.VMEM((rows_g, 2 * LANES), F32)] * N_KV,
        compiler_params=_cparams("parallel", "arbitrary"),
        name="sparse_attention",
    )(qh, kh, vh, qih, kiwi, kiwi, sg)


def _pool_kernel(p_ref, gate_ref, w_ref, scale_ref, o_ref, buf_a, buf_b, *, seq):
    pad = 16
    g = pl.program_id(1)
    p = p_ref[...]
    zeros = jnp.zeros((pad, p.shape[1]), F32)
    buf_a[0:pad, :] = zeros
    buf_b[0:pad, :] = zeros
    buf_a[pad:pad + seq, :] = p
    s2 = p + buf_a[pad - 1:pad - 1 + seq, :]
    buf_b[pad:pad + seq, :] = s2
    s4 = s2 + buf_b[pad - 2:pad - 2 + seq, :]
    buf_a[pad:pad + seq, :] = s4
    s8 = s4 + buf_a[pad - 4:pad - 4 + seq, :]
    buf_b[pad:pad + seq, :] = s8
    s16 = s8 + buf_b[pad - 8:pad - 8 + seq, :]
    t1 = (lax.broadcasted_iota(I32, p.shape, 0) + 1).astype(F32)
    win = jnp.where(g == 0, 2.0, jnp.where(g == 1, 4.0, jnp.where(g == 2, 8.0, 16.0))).astype(F32)
    total = jnp.where(g == 0, s2, jnp.where(g == 1, s4, jnp.where(g == 2, s8, s16)))
    mean = total / jnp.minimum(t1, win)
    diff = (mean - p).astype(BF16)
    y = jnp.dot(diff, w_ref[...], preferred_element_type=F32)
    o_ref[...] = (y * scale_ref[...] * gate_ref[...].astype(F32)).astype(o_ref.dtype)


def _pool_mixer(px, sg, pool_w, pool_scale, batch, seq):
    n = batch * seq
    width = pool_scale.shape[-1]
    ng = len(POOL_WINDOWS)
    cg = width // ng
    assert POOL_WINDOWS == (2, 4, 8, 16)
    return pl.pallas_call(
        functools.partial(_pool_kernel, seq=seq),
        out_shape=jax.ShapeDtypeStruct((n, width), BF16),
        grid=(batch, ng),
        in_specs=[
            pl.BlockSpec((seq, cg), lambda b, g: (b, g)),
            pl.BlockSpec((seq, cg), lambda b, g: (b, ng + g)),
            pl.BlockSpec((None, cg, cg), lambda b, g: (g, 0, 0)),
            pl.BlockSpec((1, cg), lambda b, g: (0, g)),
        ],
        out_specs=pl.BlockSpec((seq, cg), lambda b, g: (b, g)),
        scratch_shapes=[pltpu.VMEM((seq + 16, cg), F32), pltpu.VMEM((seq + 16, cg), F32)],
        compiler_params=_cparams("parallel", "arbitrary"),
        name="pool_mixer",
    )(px, sg, pool_w, pool_scale.reshape(1, width))


def _lru_kernel(x_ref, gr_ref, gate_ref, cw_ref, cb_ref, wa_ref, ba_ref, wx_ref, bx_ref, lam_ref,
                o_ref, a_s, b_s, *, seq, ct):
    x = x_ref[...]
    row = lax.broadcasted_iota(I32, (seq, ct), 0)
    xc = jnp.broadcast_to(cb_ref[...], (seq, ct))
    for tap in range(CONV_WIDTH):
        d = CONV_WIDTH - 1 - tap
        x_d = x if d == 0 else jnp.where(row >= d, pltpu.roll(x, d, 0), 0.0)
        xc = xc + x_d * cw_ref[tap:tap + 1, :]
    xcb = xc.astype(BF16)
    nb = ct // LRU_BLOCK
    r_parts, i_parts = [], []
    for blk in range(nb):
        xb = xcb[:, blk * LRU_BLOCK:(blk + 1) * LRU_BLOCK]
        r_parts.append(jnp.dot(xb, wa_ref[blk], preferred_element_type=F32))
        i_parts.append(jnp.dot(xb, wx_ref[blk], preferred_element_type=F32))
    r = _sigmoid(jnp.concatenate(r_parts, axis=1) + ba_ref[...])
    gi = _sigmoid(jnp.concatenate(i_parts, axis=1) + bx_ref[...])
    lam = lam_ref[...]
    softplus_neg_lam = jnp.log(1.0 + jnp.exp(-lam))
    log_a = -LRU_C * r * softplus_neg_lam
    a = jnp.exp(log_a)
    one_m_a2 = 1.0 - a * a
    root = jnp.where(one_m_a2 > 0.0, one_m_a2 * lax.rsqrt(one_m_a2), 0.0)
    b = root * (gi * xc)

    tiles = (seq // SUBLANES, SUBLANES, ct)
    a = a.reshape(tiles)
    b = b.reshape(tiles)
    sub = lax.broadcasted_iota(I32, tiles, 1)
    for d in (1, 2, 4):
        keep = sub >= d
        a_sh = jnp.where(keep, pltpu.roll(a, d, 1), 1.0)
        b_sh = jnp.where(keep, pltpu.roll(b, d, 1), 0.0)
        b = a * b_sh + b
        a = a * a_sh
    a_s[...] = a.reshape(seq, ct)
    b_s[...] = b.reshape(seq, ct)

    def body(t, carry):
        r0 = pl.multiple_of(t * SUBLANES, SUBLANES)
        h = a_s[pl.ds(r0, SUBLANES), :] * carry + b_s[pl.ds(r0, SUBLANES), :]
        b_s[pl.ds(r0, SUBLANES), :] = h
        return jnp.broadcast_to(h[SUBLANES - 1:SUBLANES, :], (SUBLANES, ct))

    lax.fori_loop(0, seq // SUBLANES, body, jnp.zeros((SUBLANES, ct), F32), unroll=8)
    h = b_s[...]
    o_ref[...] = (h * gr_ref[...].astype(F32) * gate_ref[...].astype(F32)).astype(o_ref.dtype)


def _lru_mixer(px, gg, sg, conv_w, conv_b, wa, ba, wx, bx, lam, batch, seq, *, ct=256):
    n = batch * seq
    width = conv_b.shape[-1]
    nct = width // ct
    nb = ct // LRU_BLOCK
    row = lambda a: a.reshape(1, width)
    rspec = pl.BlockSpec((1, ct), lambda b, j: (0, j))
    return pl.pallas_call(
        functools.partial(_lru_kernel, seq=seq, ct=ct),
        out_shape=jax.ShapeDtypeStruct((n, width), BF16),
        grid=(batch, nct),
        in_specs=[
            pl.BlockSpec((seq, ct), lambda b, j: (b, nct + j)),
            pl.BlockSpec((seq, ct), lambda b, j: (b, j)),
            pl.BlockSpec((seq, ct), lambda b, j: (b, 2 * nct + j)),
            pl.BlockSpec((CONV_WIDTH, ct), lambda b, j: (0, j)),
            rspec,
            pl.BlockSpec((nb, LRU_BLOCK, LRU_BLOCK), lambda b, j: (j, 0, 0)),
            rspec,
            pl.BlockSpec((nb, LRU_BLOCK, LRU_BLOCK), lambda b, j: (j, 0, 0)),
            rspec,
            rspec,
        ],
        out_specs=pl.BlockSpec((seq, ct), lambda b, j: (b, j)),
        scratch_shapes=[pltpu.VMEM((seq, ct), F32),
                        pltpu.VMEM((seq, ct), F32)],
        compiler_params=_cparams("parallel", "arbitrary"),
        name="rglru_mixer",
    )(px, gg, sg, conv_w, row(conv_b), wa, row(ba), wx, row(bx), row(lam))


def _out_proj_ln_kernel(ya_ref, yb_ref, yc_ref, w_ref, x_ref, g_ref, b_ref, o_ref, ob_ref, *, alpha):
    acc = jnp.dot(ya_ref[...], w_ref[0], preferred_element_type=F32)
    acc = acc + jnp.dot(yb_ref[...], w_ref[1], preferred_element_type=F32)
    acc = acc + jnp.dot(yc_ref[...], w_ref[2], preferred_element_type=F32)
    out = _layer_norm(alpha * x_ref[...] + acc, g_ref[...], b_ref[...])
    o_ref[...] = out
    ob_ref[...] = out.astype(BF16)


def _out_proj_ln(ya, yb, yc, w3, x, g, b, alpha, *, tm=256):
    n, width = ya.shape
    d = w3.shape[-1]
    tm = min(tm, n)
    aspec = pl.BlockSpec((tm, width), lambda i: (i, 0))
    tok = pl.BlockSpec((tm, d), lambda i: (i, 0))
    row = pl.BlockSpec((1, d), lambda i: (0, 0))
    return pl.pallas_call(
        functools.partial(_out_proj_ln_kernel, alpha=alpha),
        out_shape=(jax.ShapeDtypeStruct((n, d), F32), jax.ShapeDtypeStruct((n, d), BF16)),
        grid=(n // tm,),
        in_specs=[aspec, aspec, aspec,
                  pl.BlockSpec((3, width, d), lambda i: (0, 0, 0), pipeline_mode=pl.Buffered(1)),
                  tok, row, row],
        out_specs=(tok, tok),
        compiler_params=_cparams("parallel"),
        name="out_proj_ln",
    )(ya, yb, yc, w3, x, g.reshape(1, d), b.reshape(1, d))


def _ffn_up_kernel(x_ref, wg_ref, wu_ref, o_ref):
    xb = x_ref[...]
    hg = jnp.dot(xb, wg_ref[...], preferred_element_type=F32)
    hu = jnp.dot(xb, wu_ref[...], preferred_element_type=F32)
    o_ref[...] = (hg * _sigmoid(hg) * hu).astype(o_ref.dtype)


def _ffn_down_ln_kernel(h_ref, w_ref, x_ref, g_ref, b_ref, o_ref, ob_ref, *, alpha):
    y = jnp.dot(h_ref[...], w_ref[...], preferred_element_type=F32)
    out = _layer_norm(alpha * x_ref[...] + y, g_ref[...], b_ref[...])
    o_ref[...] = out
    ob_ref[...] = out.astype(BF16)


def _ffn_dense(xb, x, wg, wu, wd, g, b, alpha, *, tm_up=1024, tn_up=512, tm_down=256):
    n, d = x.shape
    ff = wg.shape[1]
    ff_real = wd.shape[0]
    assert ff_real % LANES == 0
    tm_up, tm_down = min(tm_up, n), min(tm_down, n)
    h = pl.pallas_call(
        _ffn_up_kernel,
        out_shape=jax.ShapeDtypeStruct((n, ff), BF16),
        grid=(n // tm_up, ff // tn_up),
        in_specs=[pl.BlockSpec((tm_up, d), lambda i, j: (i, 0)),
                  pl.BlockSpec((d, tn_up), lambda i, j: (0, j)),
                  pl.BlockSpec((d, tn_up), lambda i, j: (0, j))],
        out_specs=pl.BlockSpec((tm_up, tn_up), lambda i, j: (i, j)),
        compiler_params=_cparams("parallel", "arbitrary"),
        name="ffn_up",
    )(xb, wg, wu)
    tok = pl.BlockSpec((tm_down, d), lambda i: (i, 0))
    row = pl.BlockSpec((1, d), lambda i: (0, 0))
    return pl.pallas_call(
        functools.partial(_ffn_down_ln_kernel, alpha=alpha),
        out_shape=(jax.ShapeDtypeStruct((n, d), F32), jax.ShapeDtypeStruct((n, d), BF16)),
        grid=(n // tm_down,),
        in_specs=[pl.BlockSpec((tm_down, ff_real), lambda i: (i, 0)),
                  pl.BlockSpec((ff_real, d), lambda i: (0, 0), pipeline_mode=pl.Buffered(1)),
                  tok, row, row],
        out_specs=(tok, tok),
        compiler_params=_cparams("parallel"),
        name="ffn_down_ln",
    )(h, wd, x, g.reshape(1, d), b.reshape(1, d))


def _router_kernel(x_ref, w_ref, ids_ref, wts_ref):
    logits = jnp.dot(x_ref[...], w_ref[...], preferred_element_type=F32, precision=lax.Precision.HIGHEST)
    lane_i = lax.broadcasted_iota(I32, logits.shape, 1)
    lane = lane_i.astype(F32)
    logits = jnp.where(lane_i < N_EXPERTS, logits, -jnp.inf)
    m1 = jnp.max(logits, axis=1, keepdims=True)
    i1 = jnp.min(jnp.where(logits == m1, lane, float(LANES)), axis=1, keepdims=True)
    rest = jnp.where(lane == i1, -jnp.inf, logits)
    m2 = jnp.max(rest, axis=1, keepdims=True)
    i2 = jnp.min(jnp.where(rest == m2, lane, float(LANES)), axis=1, keepdims=True)
    e2 = jnp.exp(m2 - m1)
    w1 = 1.0 / (1.0 + e2)
    w2 = e2 / (1.0 + e2)
    ids_ref[...] = jnp.where(lane_i == 0, i1, jnp.where(lane_i == 1, i2, 0.0)).astype(I32)
    wts_ref[...] = jnp.where(lane_i == 0, w1, jnp.where(lane_i == 1, w2, 0.0))


def _router(x, w_router, *, tm=512):
    n, d = x.shape
    wpad = jnp.zeros((d, LANES), F32).at[:, :N_EXPERTS].set(w_router.astype(F32))
    tm = min(tm, n)
    tok = pl.BlockSpec((tm, LANES), lambda i: (i, 0))
    return pl.pallas_call(
        _router_kernel,
        out_shape=(jax.ShapeDtypeStruct((n, LANES), I32), jax.ShapeDtypeStruct((n, LANES), F32)),
        grid=(n // tm,),
        in_specs=[pl.BlockSpec((tm, d), lambda i: (i, 0)), pl.BlockSpec((d, LANES), lambda i: (0, 0))],
        out_specs=(tok, tok),
        compiler_params=_cparams("parallel"),
        name="moe_router",
    )(x, wpad)


def _gather_rows_kernel(tok_ref, x_hbm, o_ref, buf, sem, *, tm):
    i = pl.program_id(0)

    def issue(tile, slot):
        base = tile * tm

        def body(r, carry):
            t = tok_ref[base + r]
            pltpu.make_async_copy(x_hbm.at[pl.ds(t, 1), :], buf.at[slot, pl.ds(r, 1), :], sem.at[slot]).start()
            return carry

        lax.fori_loop(0, tm, body, 0, unroll=8)

    @pl.when(i == 0)
    def _():
        issue(0, 0)

    @pl.when(i + 1 < pl.num_programs(0))
    def _():
        issue(i + 1, (i + 1) % 2)

    slot = i % 2
    pltpu.make_async_copy(x_hbm.at[pl.ds(0, tm), :], buf.at[slot], sem.at[slot]).wait()
    o_ref[...] = buf[slot].astype(o_ref.dtype)


def _gather_rows(x, row_tok, *, tm=256):
    n, d = x.shape
    rows = row_tok.shape[0]
    return pl.pallas_call(
        functools.partial(_gather_rows_kernel, tm=tm),
        out_shape=jax.ShapeDtypeStruct((rows, d), BF16),
        grid_spec=pltpu.PrefetchScalarGridSpec(
            num_scalar_prefetch=1,
            grid=(rows // tm,),
            in_specs=[pl.BlockSpec(memory_space=pl.ANY)],
            out_specs=pl.BlockSpec((tm, d), lambda i, tok: (i, 0)),
            scratch_shapes=[pltpu.VMEM((2, tm, d), F32), pltpu.SemaphoreType.DMA((2,))],
        ),
        compiler_params=_cparams("arbitrary"),
        name="moe_gather_rows",
    )(row_tok, x)


def _expert_changed(te_ref, i):
    return (i == 0) | (te_ref[i] != te_ref[jnp.maximum(i - 1, 0)])


def _stream_expert_weights(j, i, n_j, col_tile, te_ref, run_ref, nxt_ref, meta_ref, w_hbm, stage, work, sem):
    @pl.when(_expert_changed(te_ref, i))
    def _():
        k = j * meta_ref[0] + run_ref[i]
        slot = k % 2

        def copies(e, jj, s):
            col = pl.ds(pl.multiple_of(jj * col_tile, col_tile), col_tile)
            return [pltpu.make_async_copy(w.at[e, :, col], st.at[s], sem.at[s, a])
                    for a, (w, st) in enumerate(zip(w_hbm, stage))]

        @pl.when(k == 0)
        def _():
            for c in copies(te_ref[i], j, slot):
                c.start()

        has_next_run = nxt_ref[i] >= 0
        e_next = jnp.where(has_next_run, nxt_ref[i], meta_ref[1])
        j_next = jnp.where(has_next_run, j, j + 1)

        @pl.when(j_next < n_j)
        def _():
            for c in copies(e_next, j_next, 1 - slot):
                c.start()

        for c in copies(te_ref[i], j, slot):
            c.wait()
        for st, wk in zip(stage, work):
            wk[...] = st[slot].astype(BF16)


def _moe_up_kernel(te_ref, nv_ref, run_ref, nxt_ref, meta_ref, x_ref, wg_hbm, wu_hbm, o_ref,
                   stage_g, stage_u, wg_s, wu_s, sem, *, tf):
    j, i = pl.program_id(0), pl.program_id(1)
    _stream_expert_weights(j, i, pl.num_programs(0), tf, te_ref, run_ref, nxt_ref, meta_ref,
                           (wg_hbm, wu_hbm), (stage_g, stage_u), (wg_s, wu_s), sem)

    @pl.when(i < nv_ref[0])
    def _():
        xb = x_ref[...]
        hg = jnp.dot(xb, wg_s[...], preferred_element_type=F32)
        hu = jnp.dot(xb, wu_s[...], preferred_element_type=F32)
        o_ref[...] = (hg * _sigmoid(hg) * hu).astype(o_ref.dtype)

    @pl.when(i >= nv_ref[0])
    def _():
        o_ref[...] = jnp.zeros(o_ref.shape, o_ref.dtype)


def _moe_up(xs, wg, wu, sched, *, tm, tf=1024):
    rows, d = xs.shape
    ff = wg.shape[-1]
    return pl.pallas_call(
        functools.partial(_moe_up_kernel, tf=tf),
        out_shape=jax.ShapeDtypeStruct((rows, ff), BF16),
        grid_spec=pltpu.PrefetchScalarGridSpec(
            num_scalar_prefetch=len(sched),
            grid=(ff // tf, rows // tm),
            in_specs=[pl.BlockSpec((tm, d), lambda j, i, *_: (i, 0)),
                      pl.BlockSpec(memory_space=pl.ANY),
                      pl.BlockSpec(memory_space=pl.ANY)],
            out_specs=pl.BlockSpec((tm, tf), lambda j, i, *_: (i, j)),
            scratch_shapes=[pltpu.VMEM((2, d, tf), F32), pltpu.VMEM((2, d, tf), F32),
                            pltpu.VMEM((d, tf), BF16), pltpu.VMEM((d, tf), BF16),
                            pltpu.SemaphoreType.DMA((2, 2))],
        ),
        compiler_params=_cparams("arbitrary", "arbitrary"),
        name="moe_up",
    )(*sched, xs, wg, wu)


def _moe_down_kernel(te_ref, nv_ref, run_ref, nxt_ref, meta_ref, h_ref, wd_hbm, o_ref, stage_d, wd_s, sem, *, tn):
    j, i = pl.program_id(0), pl.program_id(1)
    _stream_expert_weights(j, i, pl.num_programs(0), tn, te_ref, run_ref, nxt_ref, meta_ref,
                           (wd_hbm,), (stage_d,), (wd_s,), sem)

    @pl.when(i < nv_ref[0])
    def _():
        o_ref[...] = jnp.dot(h_ref[...], wd_s[...], preferred_element_type=F32)

    @pl.when(i >= nv_ref[0])
    def _():
        o_ref[...] = jnp.zeros(o_ref.shape, o_ref.dtype)


def _moe_down(h, wd, sched, *, tm, tn=512):
    rows, ff = h.shape
    d = wd.shape[-1]
    return pl.pallas_call(
        functools.partial(_moe_down_kernel, tn=tn),
        out_shape=jax.ShapeDtypeStruct((rows, d), F32),
        grid_spec=pltpu.PrefetchScalarGridSpec(
            num_scalar_prefetch=len(sched),
            grid=(d // tn, rows // tm),
            in_specs=[pl.BlockSpec((tm, ff), lambda j, i, *_: (i, 0)),
                      pl.BlockSpec(memory_space=pl.ANY)],
            out_specs=pl.BlockSpec((tm, tn), lambda j, i, *_: (i, j)),
            scratch_shapes=[pltpu.VMEM((2, ff, tn), F32), pltpu.VMEM((ff, tn), BF16),
                            pltpu.SemaphoreType.DMA((2, 1))],
        ),
        compiler_params=_cparams("arbitrary", "arbitrary"),
        name="moe_down",
    )(*sched, h, wd)


def _combine_ln_kernel(pos_ref, y_hbm, x_ref, wts_ref, g_ref, b_ref, o_ref, buf, sem, *, tm, alpha):
    i = pl.program_id(0)

    def issue(tile, slot):
        base = tile * tm

        def body(r, carry):
            for k in range(TOP_K):
                row = pos_ref[(base + r) * TOP_K + k]
                pltpu.make_async_copy(y_hbm.at[pl.ds(row, 1), :], buf.at[slot, k, pl.ds(r, 1), :],
                                      sem.at[slot]).start()
            return carry

        lax.fori_loop(0, tm, body, 0, unroll=8)

    @pl.when(i == 0)
    def _():
        issue(0, 0)

    @pl.when(i + 1 < pl.num_programs(0))
    def _():
        issue(i + 1, (i + 1) % 2)

    slot = i % 2
    for k in range(TOP_K):
        pltpu.make_async_copy(y_hbm.at[pl.ds(0, tm), :], buf.at[slot, k], sem.at[slot]).wait()
    wts = wts_ref[...]
    y = buf[slot, 0] * wts[:, 0:1] + buf[slot, 1] * wts[:, 1:2]
    o_ref[...] = _layer_norm(alpha * x_ref[...] + y, g_ref[...], b_ref[...])


def _combine_ln(yrows, pos, x, wts, g, b, alpha, *, tm=128):
    n, d = x.shape
    tm = min(tm, n)
    return pl.pallas_call(
        functools.partial(_combine_ln_kernel, tm=tm, alpha=alpha),
        out_shape=jax.ShapeDtypeStruct((n, d), F32),
        grid_spec=pltpu.PrefetchScalarGridSpec(
            num_scalar_prefetch=1,
            grid=(n // tm,),
            in_specs=[pl.BlockSpec(memory_space=pl.ANY),
                      pl.BlockSpec((tm, d), lambda i, pos: (i, 0)),
                      pl.BlockSpec((tm, LANES), lambda i, pos: (i, 0)),
                      pl.BlockSpec((1, d), lambda i, pos: (0, 0)),
                      pl.BlockSpec((1, d), lambda i, pos: (0, 0))],
            out_specs=pl.BlockSpec((tm, d), lambda i, pos: (i, 0)),
            scratch_shapes=[pltpu.VMEM((2, TOP_K, tm, d), F32), pltpu.SemaphoreType.DMA((2,))],
        ),
        compiler_params=_cparams("arbitrary"),
        name="moe_combine_ln",
    )(pos, yrows, x, wts, g.reshape(1, d), b.reshape(1, d))


def _moe_block(x, w_router, wg, wu, wd, g, b, alpha, *, tm=256):
    n, d = x.shape
    ids, wts = _router(x, w_router)
    e_flat = ids[:, :TOP_K].reshape(-1)
    n_assign = n * TOP_K
    onehot = (e_flat[:, None] == jnp.arange(N_EXPERTS, dtype=I32)[None, :]).astype(I32)
    rank = jnp.sum((jnp.cumsum(onehot, axis=0) - onehot) * onehot, axis=1)
    counts = jnp.sum(onehot, axis=0)
    padded = (counts + tm - 1) // tm * tm
    end_padded = jnp.cumsum(padded)
    start_padded = end_padded - padded
    dest = (start_padded[e_flat] + rank).astype(I32)
    rows = n_assign + N_EXPERTS * tm
    n_tiles = rows // tm
    flat_tok = jnp.arange(n_assign, dtype=I32) // TOP_K
    row_tok = jnp.zeros((rows,), I32).at[dest].set(flat_tok)
    tile_start = jnp.arange(n_tiles, dtype=I32) * tm
    tile_e = jnp.minimum(jnp.sum((tile_start[:, None] >= end_padded[None, :]).astype(I32), axis=1),
                         N_EXPERTS - 1).astype(I32)
    n_valid = (end_padded[-1:] // tm).astype(I32)
    tile_ix = jnp.arange(n_tiles, dtype=I32)
    tile_e = jnp.where(tile_ix < n_valid[0], tile_e, tile_e[jnp.maximum(n_valid[0] - 1, 0)])
    is_start = jnp.concatenate([jnp.ones((1,), I32), (tile_e[1:] != tile_e[:-1]).astype(I32)])
    run_id = (jnp.cumsum(is_start) - 1).astype(I32)
    larger = jnp.where(tile_e[None, :] > tile_e[:, None], tile_e[None, :], N_EXPERTS)
    nxt_e = jnp.min(larger, axis=1)
    nxt_e = jnp.where(nxt_e < N_EXPERTS, nxt_e, -1).astype(I32)
    meta = jnp.stack([run_id[-1] + 1, tile_e[0]]).astype(I32)
    sched = (tile_e, n_valid, run_id, nxt_e, meta)
    xs = _gather_rows(x, row_tok, tm=tm)
    h = _moe_up(xs, wg, wu, sched, tm=tm)
    yrows = _moe_down(h, wd, sched, tm=tm)
    return _combine_ln(yrows, dest, x, wts, g, b, alpha)


def _w_in_columns(d_model):
    attn_w = N_HEADS * HEAD_DIM
    kv_w = N_KV * HEAD_DIM
    o = np.cumsum([0, attn_w, kv_w, kv_w, IDX_HEADS * IDX_DIM, IDX_DIM, IDX_HEADS, d_model, d_model, d_model])
    o = [int(v) for v in o]
    return {"q": (o[0], attn_w), "k": (o[1], kv_w), "v": (o[2], kv_w), "qi": (o[3], o[4] - o[3]),
            "kiwi": (o[4], LANES),
            "px": (o[6], 2 * d_model), "gr": (o[8], d_model), "gates": (o[9], 3 * d_model)}


def _mixer(x_mm, tabs, wt_in, layer, pool_w, pool_scale, conv_w, conv_b, wa, ba, wx, bx, lam,
           batch, seq, d_model, tq):
    cos_a, sin_a, cos_i, sin_ia, sin_ib, cos_q, sin_q, cos_iq = tabs
    cols = _w_in_columns(d_model)
    rope_a = ((cos_a, sin_a), (HEAD_DIM // 2,))
    rope_i = (LANES - IDX_DIM // 2, IDX_DIM // 2)
    qh = _proj_heads(x_mm, wt_in, layer, cols["q"], (cos_q, sin_q), rope_a[1], BF16)
    kh = _proj_heads(x_mm, wt_in, layer, cols["k"], *rope_a, BF16)
    vh = _proj_heads(x_mm, wt_in, layer, cols["v"], (), (), BF16)
    qih = _proj_heads(x_mm, wt_in, layer, cols["qi"], (cos_iq, sin_ia, sin_ib), rope_i, BF16, head_width=IDX_DIM)
    kiwi = _proj_heads(x_mm, wt_in, layer, cols["kiwi"], (cos_i, sin_ia, sin_ib), rope_i, F32)
    px = _proj_plain(x_mm, wt_in, layer, cols["px"], None, F32)
    gg = _proj_plain(x_mm, wt_in, layer, cols["gr"], "gelu", BF16)
    sg = _proj_plain(x_mm, wt_in, layer, cols["gates"], "sigmoid", BF16)
    ya = _attention(qh, kh, vh, qih, kiwi, sg, batch, seq, tq=tq)
    yb = _pool_mixer(px, sg, pool_w.astype(BF16), pool_scale, batch, seq)
    yc = _lru_mixer(px, gg, sg, conv_w, conv_b, wa.astype(BF16), ba, wx.astype(BF16), bx, lam, batch, seq)
    return ya, yb, yc


def _pad_ff(w, axis, mult):
    ff = w.shape[axis]
    padn = (-ff) % mult
    if padn == 0:
        return w
    widths = [(0, 0)] * w.ndim
    widths[axis] = (0, padn)
    return jnp.pad(w, widths)


def kernel(x, positions, mix_w_in, mix_w_out, pool_w, pool_scale, conv_w, conv_b, lru_wa, lru_ba, lru_wx, lru_bx, lru_lam, ln_mix_g, ln_mix_b, ln_ffn_g, ln_ffn_b, dense_w_gate, dense_w_up, dense_w_down, moe_router, moe_w_gate, moe_w_up, moe_w_down):
    batch, seq, d_model = x.shape
    depth = mix_w_in.shape[0]
    alpha = np.float32((2 * depth) ** 0.25)
    n = batch * seq
    tabs = _rope_tables(positions)
    xf = x.reshape(n, d_model)
    x_mm = xf
    wt_in = jnp.swapaxes(mix_w_in, 1, 2)
    for layer in range(depth):
        ya, yb, yc = _mixer(x_mm, tabs, wt_in, layer, pool_w[layer], pool_scale[layer],
                            conv_w[layer], conv_b[layer], lru_wa[layer], lru_ba[layer], lru_wx[layer],
                            lru_bx[layer], lru_lam[layer], batch, seq, d_model, tq=256)
        w_out3 = mix_w_out[layer].astype(BF16).reshape(3, -1, d_model)
        xf, xb = _out_proj_ln(ya, yb, yc, w_out3, xf, ln_mix_g[layer], ln_mix_b[layer], alpha)
        j = layer // 2
        if layer % 2 == 0:
            wg = _pad_ff(dense_w_gate[j].astype(BF16), 1, 512)
            wu = _pad_ff(dense_w_up[j].astype(BF16), 1, 512)
            wd = dense_w_down[j].astype(BF16)
            xf, x_mm = _ffn_dense(xb, xf, wg, wu, wd, ln_ffn_g[layer], ln_ffn_b[layer], alpha)
        else:
            xf = _moe_block(xf, moe_router[j], moe_w_gate[j], moe_w_up[j], moe_w_down[j],
                            ln_ffn_g[layer], ln_ffn_b[layer], alpha)
            x_mm = xf
    return xf.reshape(batch, seq, d_model)
```

```python
import functools

import jax
import jax.numpy as jnp
import numpy as np
from jax import lax
from jax.experimental import pallas as pl
from jax.experimental.pallas import tpu as pltpu

F32 = jnp.float32
BF16 = jnp.bfloat16
I32 = jnp.int32

LANES = 128
SUBLANES = 8
VMEM_LIMIT = 56 * 1024 * 1024

CHUNK = 64
N_HEADS = 16
HEAD_DIM = 128
N_KV = 4
HEADS_PER_KV = N_HEADS // N_KV
IDX_HEADS = 16
IDX_DIM = 64
TOPK_MAX = 256
ROPE_THETA = 10000.0
POOL_WINDOWS = (2, 4, 8, 16)
LRU_BLOCK = 128
CONV_WIDTH = 4
LRU_C = 8.0
N_EXPERTS = 8
TOP_K = 2
LN_EPS = 1e-5
LN_ROWS = 128
INT_MIN = -2 ** 31
NEG_BIG = -1e30


def _cparams(*sem):
    return pltpu.CompilerParams(dimension_semantics=sem, vmem_limit_bytes=VMEM_LIMIT)


def _sigmoid(x):
    return 0.5 * (1.0 + jnp.tanh(0.5 * x))


def _gelu_tanh(x):
    c = np.float32(np.sqrt(2.0 / np.pi))
    return 0.5 * x * (1.0 + jnp.tanh(c * (x + np.float32(0.044715) * (x * x * x))))


def _layer_norm(y, g, b):
    mu = jnp.mean(y, axis=-1, keepdims=True)
    d = y - mu
    var = jnp.mean(d * d, axis=-1, keepdims=True)
    return d * lax.rsqrt(var + LN_EPS) * g + b


def _rope_tab_kernel(pos_ref, inv_a_ref, inv_i_ref, sgn_a_ref, m_cos_ref, add_cos_ref,
                     m_sa_ref, m_sb_ref, cos_a_ref, sin_a_ref, cos_i_ref, sin_ia_ref, sin_ib_ref,
                     cos_q_ref, sin_q_ref, cos_iq_ref):
    pos = pos_ref[...]
    ang_a = pos * inv_a_ref[...]
    cos_a = jnp.cos(ang_a)
    sin_a = jnp.sin(ang_a) * sgn_a_ref[...]
    cos_a_ref[...] = cos_a
    sin_a_ref[...] = sin_a
    q_scale = np.float32(HEAD_DIM ** -0.5 * np.log2(np.e))
    cos_q_ref[...] = cos_a * q_scale
    sin_q_ref[...] = sin_a * q_scale
    ang_i = pos * inv_i_ref[...]
    s_i = jnp.sin(ang_i)
    cos_iq = jnp.cos(ang_i) * m_cos_ref[...]
    cos_iq_ref[...] = cos_iq
    cos_i_ref[...] = cos_iq + add_cos_ref[...]
    sin_ia_ref[...] = s_i * m_sa_ref[...]
    sin_ib_ref[...] = s_i * m_sb_ref[...]


def _rope_tables(positions):
    n = positions.size
    pos = jnp.broadcast_to(positions.reshape(n, 1).astype(F32), (n, LANES))
    lane = np.arange(LANES)
    inv_a = (ROPE_THETA ** (-jnp.arange(0, HEAD_DIM, 2, dtype=F32) / HEAD_DIM))
    inv_i = (ROPE_THETA ** (-jnp.arange(0, IDX_DIM, 2, dtype=F32) / IDX_DIM))
    inv_a_row = jnp.concatenate([inv_a, inv_a])[None, :]
    inv_i_row = jnp.concatenate([inv_i, inv_i, jnp.zeros((LANES - IDX_DIM,), F32)])[None, :]
    sgn_a = jnp.asarray(np.where(lane < HEAD_DIM // 2, -1.0, 1.0), F32)[None, :]
    m_cos = jnp.asarray((lane < IDX_DIM).astype(np.float32))[None, :]
    wi_scale = (IDX_HEADS ** -0.5) * (IDX_DIM ** -0.5)
    add_cos = jnp.asarray(np.where((lane >= IDX_DIM) & (lane < IDX_DIM + IDX_HEADS), wi_scale, 0.0), F32)[None, :]
    m_sa = jnp.asarray(np.where(lane < IDX_DIM // 2, -1.0, 0.0), F32)[None, :]
    m_sb = jnp.asarray(np.where((lane >= IDX_DIM // 2) & (lane < IDX_DIM), 1.0, 0.0), F32)[None, :]
    tm = min(n, 1024)
    row = pl.BlockSpec((1, LANES), lambda i: (0, 0))
    tok = pl.BlockSpec((tm, LANES), lambda i: (i, 0))
    out = jax.ShapeDtypeStruct((n, LANES), F32)
    return pl.pallas_call(
        _rope_tab_kernel,
        out_shape=(out,) * 8,
        grid=(n // tm,),
        in_specs=[tok] + [row] * 7,
        out_specs=(tok,) * 8,
        compiler_params=_cparams("parallel"),
        name="rope_tables",
    )(pos, inv_a_row, inv_i_row, sgn_a, m_cos, add_cos, m_sa, m_sb)


def _stream_weight_tile(wt_hbm, stage, w_s, sem, *, layer, off, tn):
    j = pl.program_id(0)

    def copy(jj, slot):
        rows = pl.ds(pl.multiple_of(off + jj * tn, SUBLANES), tn)
        return pltpu.make_async_copy(wt_hbm.at[layer, rows, :], stage.at[slot], sem.at[slot])

    @pl.when(pl.program_id(1) == 0)
    def _():
        slot = j % 2

        @pl.when(j == 0)
        def _():
            copy(0, 0).start()

        @pl.when(j + 1 < pl.num_programs(0))
        def _():
            copy(j + 1, 1 - slot).start()

        copy(j, slot).wait()
        w_s[...] = stage[slot].astype(BF16)


def _dot_nt(x, w):
    return lax.dot_general(x, w, (((1,), (1,)), ((), ())), preferred_element_type=F32)


def _proj_heads_kernel(x_ref, wt_hbm, *rest, layer, off, tn, shifts, heads, head_width):
    tabs, o_ref, stage, w_s, sem = rest[:-4], rest[-4], rest[-3], rest[-2], rest[-1]
    _stream_weight_tile(wt_hbm, stage, w_s, sem, layer=layer, off=off, tn=tn)
    acc = _dot_nt(x_ref[...].astype(BF16), w_s[...])
    per_slab = LANES // head_width
    for h in range(heads):
        xh = acc[:, (h // per_slab) * LANES:(h // per_slab + 1) * LANES]
        if h % per_slab:
            xh = pltpu.roll(xh, LANES - (h % per_slab) * head_width, 1)
        if tabs:
            y = xh * tabs[0][...]
            for s, t in zip(shifts, tabs[1:]):
                y = y + pltpu.roll(xh, s, 1) * t[...]
        else:
            y = xh
        o_ref[h] = y.astype(o_ref.dtype)


def _weight_stream_scratch(tn, d):
    return [pltpu.VMEM((2, tn, d), F32), pltpu.VMEM((tn, d), BF16), pltpu.SemaphoreType.DMA((2,))]


def _proj_heads(x, wt, layer, cols, tabs, shifts, out_dtype, *, head_width=LANES, tm=1024, tile_cols=1024):
    n, d = x.shape
    off, width = cols
    tn = min(tile_cols, width)
    hp = tn // head_width
    assert width % tn == 0 and off % SUBLANES == 0 and (head_width == LANES or tabs)
    tm = min(tm, n)
    tab_spec = pl.BlockSpec((tm, LANES), lambda j, i: (i, 0))
    return pl.pallas_call(
        functools.partial(_proj_heads_kernel, layer=layer, off=off, tn=tn, shifts=shifts, heads=hp,
                          head_width=head_width),
        out_shape=jax.ShapeDtypeStruct((width // head_width, n, LANES), out_dtype),
        grid=(width // tn, n // tm),
        in_specs=[pl.BlockSpec((tm, d), lambda j, i: (i, 0)),
                  pl.BlockSpec(memory_space=pl.ANY)] + [tab_spec] * len(tabs),
        out_specs=pl.BlockSpec((hp, tm, LANES), lambda j, i: (j, i, 0)),
        scratch_shapes=_weight_stream_scratch(tn, d),
        compiler_params=_cparams("arbitrary", "arbitrary"),
        name="proj_heads",
    )(x, wt, *tabs)


def _proj_plain_kernel(x_ref, wt_hbm, o_ref, stage, w_s, sem, *, layer, off, tn, act):
    _stream_weight_tile(wt_hbm, stage, w_s, sem, layer=layer, off=off, tn=tn)
    acc = _dot_nt(x_ref[...].astype(BF16), w_s[...])
    if act == "gelu":
        acc = _gelu_tanh(acc)
    elif act == "sigmoid":
        acc = _sigmoid(acc)
    o_ref[...] = acc.astype(o_ref.dtype)


def _proj_plain(x, wt, layer, cols, act, out_dtype, *, tm=1024, tn=1024):
    n, d = x.shape
    off, width = cols
    assert width % tn == 0 and off % SUBLANES == 0
    tm = min(tm, n)
    return pl.pallas_call(
        functools.partial(_proj_plain_kernel, layer=layer, off=off, tn=tn, act=act),
        out_shape=jax.ShapeDtypeStruct((n, width), out_dtype),
        grid=(width // tn, n // tm),
        in_specs=[pl.BlockSpec((tm, d), lambda j, i: (i, 0)),
                  pl.BlockSpec(memory_space=pl.ANY)],
        out_specs=pl.BlockSpec((tm, tn), lambda j, i: (i, j)),
        scratch_shapes=_weight_stream_scratch(tn, d),
        compiler_params=_cparams("arbitrary", "arbitrary"),
        name="proj_plain",
    )(x, wt)


def _key_to_float(key):
    return pltpu.bitcast(key ^ ((key >> 31) & 0x7FFFFFFF), F32)


def _attn_kernel(q_ref, k_ref, v_ref, qi_ref, ki_ref, wi_ref, gate_ref, o_ref,
                 wb_s, key_s, keyt_s, x_s, thr_s, xrow_s, *state, tq, topk, seq):
    m_s, acc_s = state[:N_KV], state[N_KV:]
    kb = tq
    n_sub = kb // LANES
    qt = pl.program_id(1)
    nkb = qt + 1
    n_hi = IDX_HEADS
    rows_g = HEADS_PER_KV * tq

    wi = wi_ref[...]
    for h in range(n_hi):
        wb_s[h] = jnp.broadcast_to(wi[:, IDX_DIM + h:IDX_DIM + h + 1], (tq, LANES))
    qi = qi_ref[...].reshape(n_hi * tq, LANES)
    q_row = qt * tq + lax.broadcasted_iota(I32, (tq, LANES), 0)
    limit = (q_row // CHUNK + 1) * CHUNK
    lane_pos = lax.broadcasted_iota(I32, (tq, LANES), 1)
    limit_t = ((qt * tq + lax.broadcasted_iota(I32, (1, tq), 1)) // CHUNK + 1) * CHUNK

    def score_body(j, carry):
        start = pl.multiple_of(j * kb, kb)
        ki_blk = ki_ref[pl.ds(start, kb), :].astype(BF16)
        s = lax.dot_general(qi, ki_blk, (((1,), (1,)), ((), ())), preferred_element_type=F32)
        parts = []
        for c in range(n_sub):
            sc = jnp.zeros((tq, LANES), F32)
            for h in range(n_hi):
                sh = s[h * tq:(h + 1) * tq, c * LANES:(c + 1) * LANES]
                sc = sc + wb_s[h] * jnp.maximum(sh, 0.0)
            sc = jnp.where(sc == 0.0, 0.0, sc)
            parts.append(sc)
            kpos = start + c * LANES + lane_pos
            key_s[j, :, c * LANES:(c + 1) * LANES] = jnp.where(kpos < limit, sc, -jnp.inf)
        sc_t = jnp.concatenate(parts, axis=1).T
        kpos_t = start + lax.broadcasted_iota(I32, (kb, tq), 0)
        keyt_s[j] = jnp.where(kpos_t < limit_t, sc_t, -jnp.inf)
        return carry

    lax.fori_loop(0, nkb, score_body, 0)

    kf = np.float32(topk)
    searched = limit_t > topk

    def count_keys(pred):
        def body(j, cnt):
            hit = jnp.where(pred(keyt_s[j], j), 1.0, 0.0)
            return cnt + jnp.sum(hit.reshape(kb // SUBLANES, SUBLANES, tq), axis=0)
        cnt = lax.fori_loop(0, nkb, body, jnp.zeros((SUBLANES, tq), F32))
        return jnp.sum(cnt, axis=0, keepdims=True)

    thr0 = jnp.where(count_keys(lambda s, j: s >= 0.0) >= kf, 0, INT_MIN).astype(I32)

    def thr_body(it, thr):
        cand = thr | jnp.left_shift(jnp.int32(1), 30 - it)
        cand_f = _key_to_float(cand)
        return jnp.where(count_keys(lambda s, j: s >= cand_f) >= kf, cand, thr)

    thr = _key_to_float(lax.fori_loop(0, 31, thr_body, thr0))
    thr = jnp.where(searched, thr, -jnp.inf)
    n_ge = count_keys(lambda s, j: s >= thr)
    need = kf - count_keys(lambda s, j: s > thr)
    xrow_s[...] = jnp.where(searched, seq, -1).astype(I32)
    tie_flag = jnp.max(jnp.where(searched & (n_ge > kf), 1.0, 0.0), axis=(0, 1), keepdims=True)

    @pl.when(tie_flag[0, 0] > 0.0)
    def _():
        nbits = max(int(seq - 1).bit_length(), 1)
        row_pos = lax.broadcasted_iota(I32, (kb, tq), 0)

        def x_body(it, xcut):
            cand = xcut | jnp.left_shift(jnp.int32(1), nbits - 1 - it)
            cnt = count_keys(lambda s, j: (s == thr) & (j * kb + row_pos < cand))
            return jnp.where(cnt < need, cand, xcut)

        xcut = lax.fori_loop(0, nbits, x_body, jnp.zeros((1, tq), I32))
        xrow_s[...] = jnp.where(searched, xcut, -1)

    def to_col(row_f32):
        return jnp.broadcast_to(row_f32, (LANES, tq)).T

    thr_s[...] = to_col(thr)
    x_s[...] = to_col(xrow_s[...].astype(F32)).astype(I32)
    rb = min(tq, 128)
    n_rc = tq // rb
    lane_rb = lax.broadcasted_iota(I32, (rb, LANES), 1)

    for g in range(N_KV):
        m_s[g][...] = jnp.full((rows_g, LANES), NEG_BIG, F32)
        acc_s[g][...] = jnp.zeros((rows_g, 2 * LANES), F32)
    ones_v = jnp.ones((kb, LANES), BF16)

    def attn_body(j, carry):
        start = pl.multiple_of(j * kb, kb)
        bias = []
        for rc in range(n_rc):
            thr_c = thr_s[rc * rb:(rc + 1) * rb, :]
            xcut_c = x_s[rc * rb:(rc + 1) * rb, :]
            parts = []
            for c in range(n_sub):
                kc = key_s[j, rc * rb:(rc + 1) * rb, c * LANES:(c + 1) * LANES]
                kpos = start + c * LANES + lane_rb
                sel = (kc > thr_c) | ((kc == thr_c) & (kpos <= xcut_c))
                parts.append(jnp.where(sel, 0.0, NEG_BIG))
            bias.append(jnp.concatenate(parts, axis=1))
        for g in range(N_KV):
            qg = q_ref[g * HEADS_PER_KV:(g + 1) * HEADS_PER_KV].reshape(rows_g, LANES)
            kg = k_ref[g, pl.ds(start, kb), :]
            vg = jnp.concatenate([v_ref[g, pl.ds(start, kb), :], ones_v], axis=1)
            lg_all = lax.dot_general(qg, kg, (((1,), (1,)), ((), ())), preferred_element_type=F32)
            m_prev_all = m_s[g][...]
            p_parts, a_parts, m_parts = [], [], []
            for r in range(HEADS_PER_KV):
                for rc in range(n_rc):
                    r0 = r * tq + rc * rb
                    lg = lg_all[r0:r0 + rb] + bias[rc]
                    m_prev = m_prev_all[r0:r0 + rb]
                    m_new = jnp.maximum(m_prev, jnp.max(lg, axis=1, keepdims=True))
                    p = jnp.exp2(lg - jnp.concatenate([m_new] * n_sub, axis=1))
                    p_parts.append(p.astype(BF16))
                    a_parts.append(jnp.exp2(m_prev - m_new))
                    m_parts.append(m_new)
            p_all = jnp.concatenate(p_parts, axis=0)
            alpha_all = jnp.concatenate(a_parts, axis=0)
            pv = jnp.dot(p_all, vg, preferred_element_type=F32)
            acc_s[g][...] = jnp.concatenate([alpha_all, alpha_all], axis=1) * acc_s[g][...] + pv
            m_s[g][...] = jnp.concatenate(m_parts, axis=0)
        return carry

    lax.fori_loop(0, nkb, attn_body, 0)

    for g in range(N_KV):
        acc = acc_s[g][...]
        og = acc[:, :LANES] / acc[:, LANES:]
        for r in range(HEADS_PER_KV):
            col = (g * HEADS_PER_KV + r) * LANES
            y = og[r * tq:(r + 1) * tq] * gate_ref[:, col:col + LANES].astype(F32)
            o_ref[:, col:col + LANES] = y.astype(o_ref.dtype)


def _attention(qh, kh, vh, qih, kiwi, sg, batch, seq, *, tq):
    n = batch * seq
    topk = min(TOPK_MAX, seq // 4)
    tq = min(tq, seq)
    nqt = seq // tq
    attn_w = N_HEADS * HEAD_DIM
    rows_g = HEADS_PER_KV * tq
    return pl.pallas_call(
        functools.partial(_attn_kernel, tq=tq, topk=topk, seq=seq),
        out_shape=jax.ShapeDtypeStruct((n, attn_w), BF16),
        grid=(batch, nqt),
        in_specs=[
            pl.BlockSpec((N_HEADS, tq, LANES), lambda b, t: (0, b * nqt + t, 0)),
            pl.BlockSpec((N_KV, seq, LANES), lambda b, t: (0, b, 0)),
            pl.BlockSpec((N_KV, seq, LANES), lambda b, t: (0, b, 0)),
            pl.BlockSpec((IDX_HEADS, tq, LANES), lambda b, t: (0, b * nqt + t, 0)),
            pl.BlockSpec((None, seq, LANES), lambda b, t: (0, b, 0)),
            pl.BlockSpec((None, tq, LANES), lambda b, t: (0, b * nqt + t, 0)),
            pl.BlockSpec((tq, attn_w), lambda b, t: (b * nqt + t, 0)),
        ],
        out_specs=pl.BlockSpec((tq, attn_w), lambda b, t: (b * nqt + t, 0)),
        scratch_shapes=[
            pltpu.VMEM((IDX_HEADS, tq, LANES), F32),
            pltpu.VMEM((nqt, tq, tq), F32),
            pltpu.VMEM((nqt, tq, tq), F32),
            pltpu.VMEM((tq, LANES), I32),
            pltpu.VMEM((tq, LANES), F32),
            pltpu.VMEM((1, tq), I32),
        ] + [pltpu.VMEM((rows_g, LANES), F32)] * N_KV + [pltpu.VMEM((rows_g, 2 * LANES), F32)] * N_KV,
        compiler_params=_cparams("parallel", "arbitrary"),
        name="sparse_attention",
    )(qh, kh, vh, qih, kiwi, kiwi, sg)


def _pool_kernel(p_ref, gate_ref, w_ref, scale_ref, o_ref, buf_a, buf_b, *, seq):
    pad = 16
    g = pl.program_id(1)
    p = p_ref[...]
    zeros = jnp.zeros((pad, p.shape[1]), F32)
    buf_a[0:pad, :] = zeros
    buf_b[0:pad, :] = zeros
    buf_a[pad:pad + seq, :] = p
    s2 = p + buf_a[pad - 1:pad - 1 + seq, :]
    buf_b[pad:pad + seq, :] = s2
    s4 = s2 + buf_b[pad - 2:pad - 2 + seq, :]
    buf_a[pad:pad + seq, :] = s4
    s8 = s4 + buf_a[pad - 4:pad - 4 + seq, :]
    buf_b[pad:pad + seq, :] = s8
    s16 = s8 + buf_b[pad - 8:pad - 8 + seq, :]
    t1 = (lax.broadcasted_iota(I32, p.shape, 0) + 1).astype(F32)
    win = jnp.where(g == 0, 2.0, jnp.where(g == 1, 4.0, jnp.where(g == 2, 8.0, 16.0))).astype(F32)
    total = jnp.where(g == 0, s2, jnp.where(g == 1, s4, jnp.where(g == 2, s8, s16)))
    mean = total / jnp.minimum(t1, win)
    diff = (mean - p).astype(BF16)
    y = jnp.dot(diff, w_ref[...], preferred_element_type=F32)
    o_ref[...] = (y * scale_ref[...] * gate_ref[...].astype(F32)).astype(o_ref.dtype)


def _pool_mixer(px, sg, pool_w, pool_scale, batch, seq):
    n = batch * seq
    width = pool_scale.shape[-1]
    ng = len(POOL_WINDOWS)
    cg = width // ng
    assert POOL_WINDOWS == (2, 4, 8, 16)
    return pl.pallas_call(
        functools.partial(_pool_kernel, seq=seq),
        out_shape=jax.ShapeDtypeStruct((n, width), BF16),
        grid=(batch, ng),
        in_specs=[
            pl.BlockSpec((seq, cg), lambda b, g: (b, g)),
            pl.BlockSpec((seq, cg), lambda b, g: (b, ng + g)),
            pl.BlockSpec((None, cg, cg), lambda b, g: (g, 0, 0)),
            pl.BlockSpec((1, cg), lambda b, g: (0, g)),
        ],
        out_specs=pl.BlockSpec((seq, cg), lambda b, g: (b, g)),
        scratch_shapes=[pltpu.VMEM((seq + 16, cg), F32), pltpu.VMEM((seq + 16, cg), F32)],
        compiler_params=_cparams("parallel", "arbitrary"),
        name="pool_mixer",
    )(px, sg, pool_w, pool_scale.reshape(1, width))


def _lru_kernel(x_ref, gr_ref, gate_ref, cw_ref, cb_ref, wa_ref, ba_ref, wx_ref, bx_ref, lam_ref,
                o_ref, a_s, b_s, *, seq, ct):
    x = x_ref[...]
    row = lax.broadcasted_iota(I32, (seq, ct), 0)
    xc = jnp.broadcast_to(cb_ref[...], (seq, ct))
    for tap in range(CONV_WIDTH):
        d = CONV_WIDTH - 1 - tap
        x_d = x if d == 0 else jnp.where(row >= d, pltpu.roll(x, d, 0), 0.0)
        xc = xc + x_d * cw_ref[tap:tap + 1, :]
    xcb = xc.astype(BF16)
    nb = ct // LRU_BLOCK
    r_parts, i_parts = [], []
    for blk in range(nb):
        xb = xcb[:, blk * LRU_BLOCK:(blk + 1) * LRU_BLOCK]
        r_parts.append(jnp.dot(xb, wa_ref[blk], preferred_element_type=F32))
        i_parts.append(jnp.dot(xb, wx_ref[blk], preferred_element_type=F32))
    r = _sigmoid(jnp.concatenate(r_parts, axis=1) + ba_ref[...])
    gi = _sigmoid(jnp.concatenate(i_parts, axis=1) + bx_ref[...])
    lam = lam_ref[...]
    softplus_neg_lam = jnp.log(1.0 + jnp.exp(-lam))
    log_a = -LRU_C * r * softplus_neg_lam
    a = jnp.exp(log_a)
    one_m_a2 = 1.0 - a * a
    root = jnp.where(one_m_a2 > 0.0, one_m_a2 * lax.rsqrt(one_m_a2), 0.0)
    b = root * (gi * xc)

    tiles = (seq // SUBLANES, SUBLANES, ct)
    a = a.reshape(tiles)
    b = b.reshape(tiles)
    sub = lax.broadcasted_iota(I32, tiles, 1)
    for d in (1, 2, 4):
        keep = sub >= d
        a_sh = jnp.where(keep, pltpu.roll(a, d, 1), 1.0)
        b_sh = jnp.where(keep, pltpu.roll(b, d, 1), 0.0)
        b = a * b_sh + b
        a = a * a_sh
    a_s[...] = a.reshape(seq, ct)
    b_s[...] = b.reshape(seq, ct)

    def body(t, carry):
        r0 = pl.multiple_of(t * SUBLANES, SUBLANES)
        h = a_s[pl.ds(r0, SUBLANES), :] * carry + b_s[pl.ds(r0, SUBLANES), :]
        b_s[pl.ds(r0, SUBLANES), :] = h
        return jnp.broadcast_to(h[SUBLANES - 1:SUBLANES, :], (SUBLANES, ct))

    lax.fori_loop(0, seq // SUBLANES, body, jnp.zeros((SUBLANES, ct), F32), unroll=8)
    h = b_s[...]
    o_ref[...] = (h * gr_ref[...].astype(F32) * gate_ref[...].astype(F32)).astype(o_ref.dtype)


def _lru_mixer(px, gg, sg, conv_w, conv_b, wa, ba, wx, bx, lam, batch, seq, *, ct=256):
    n = batch * seq
    width = conv_b.shape[-1]
    nct = width // ct
    nb = ct // LRU_BLOCK
    row = lambda a: a.reshape(1, width)
    rspec = pl.BlockSpec((1, ct), lambda b, j: (0, j))
    return pl.pallas_call(
        functools.partial(_lru_kernel, seq=seq, ct=ct),
        out_shape=jax.ShapeDtypeStruct((n, width), BF16),
        grid=(batch, nct),
        in_specs=[
            pl.BlockSpec((seq, ct), lambda b, j: (b, nct + j)),
            pl.BlockSpec((seq, ct), lambda b, j: (b, j)),
            pl.BlockSpec((seq, ct), lambda b, j: (b, 2 * nct + j)),
            pl.BlockSpec((CONV_WIDTH, ct), lambda b, j: (0, j)),
            rspec,
            pl.BlockSpec((nb, LRU_BLOCK, LRU_BLOCK), lambda b, j: (j, 0, 0)),
            rspec,
            pl.BlockSpec((nb, LRU_BLOCK, LRU_BLOCK), lambda b, j: (j, 0, 0)),
            rspec,
            rspec,
        ],
        out_specs=pl.BlockSpec((seq, ct), lambda b, j: (b, j)),
        scratch_shapes=[pltpu.VMEM((seq, ct), F32),
                        pltpu.VMEM((seq, ct), F32)],
        compiler_params=_cparams("parallel", "arbitrary"),
        name="rglru_mixer",
    )(px, gg, sg, conv_w, row(conv_b), wa, row(ba), wx, row(bx), row(lam))


def _out_proj_ln_kernel(ya_ref, yb_ref, yc_ref, w_ref, x_ref, g_ref, b_ref, o_ref, ob_ref, *, alpha):
    acc = jnp.dot(ya_ref[...], w_ref[0], preferred_element_type=F32)
    acc = acc + jnp.dot(yb_ref[...], w_ref[1], preferred_element_type=F32)
    acc = acc + jnp.dot(yc_ref[...], w_ref[2], preferred_element_type=F32)
    out = _layer_norm(alpha * x_ref[...] + acc, g_ref[...], b_ref[...])
    o_ref[...] = out
    ob_ref[...] = out.astype(BF16)


def _out_proj_ln(ya, yb, yc, w3, x, g, b, alpha, *, tm=256):
    n, width = ya.shape
    d = w3.shape[-1]
    tm = min(tm, n)
    aspec = pl.BlockSpec((tm, width), lambda i: (i, 0))
    tok = pl.BlockSpec((tm, d), lambda i: (i, 0))
    row = pl.BlockSpec((1, d), lambda i: (0, 0))
    return pl.pallas_call(
        functools.partial(_out_proj_ln_kernel, alpha=alpha),
        out_shape=(jax.ShapeDtypeStruct((n, d), F32), jax.ShapeDtypeStruct((n, d), BF16)),
        grid=(n // tm,),
        in_specs=[aspec, aspec, aspec,
                  pl.BlockSpec((3, width, d), lambda i: (0, 0, 0), pipeline_mode=pl.Buffered(1)),
                  tok, row, row],
        out_specs=(tok, tok),
        compiler_params=_cparams("parallel"),
        name="out_proj_ln",
    )(ya, yb, yc, w3, x, g.reshape(1, d), b.reshape(1, d))


def _ffn_up_kernel(x_ref, wg_hbm, wu_hbm, o_ref, stage_g, stage_u, wg_s, wu_s, sem, *, layer, tn, ff):
    j, i = pl.program_id(0), pl.program_id(1)
    nj = pl.num_programs(0)
    n_tiles = -(-ff // tn)
    tail = ff - (n_tiles - 1) * tn

    def on_tile(jj, slot, fn):
        def run(width):
            col = pl.ds(pl.multiple_of(jj * tn, LANES), width)
            for a, (w, st) in enumerate(((wg_hbm, stage_g), (wu_hbm, stage_u))):
                fn(pltpu.make_async_copy(w.at[layer, :, col], st.at[slot, :, pl.ds(0, width)], sem.at[slot, a]))

        @pl.when(jj < nj - 1)
        def _():
            run(tn)

        @pl.when(jj == nj - 1)
        def _():
            run(tail)

    @pl.when(i == 0)
    def _():
        slot = j % 2

        @pl.when(j == 0)
        def _():
            on_tile(0, 0, lambda c: c.start())

        @pl.when(j + 1 < nj)
        def _():
            on_tile(j + 1, 1 - slot, lambda c: c.start())

        on_tile(j, slot, lambda c: c.wait())
        wg_s[...] = stage_g[slot].astype(BF16)
        wu_s[...] = stage_u[slot].astype(BF16)
        if tail < tn:
            @pl.when(j == nj - 1)
            def _():
                wg_s[:, tail:] = jnp.zeros((wg_s.shape[0], tn - tail), BF16)
                wu_s[:, tail:] = jnp.zeros((wu_s.shape[0], tn - tail), BF16)

    xb = x_ref[...]
    hg = jnp.dot(xb, wg_s[...], preferred_element_type=F32)
    hu = jnp.dot(xb, wu_s[...], preferred_element_type=F32)
    o_ref[...] = (hg * _sigmoid(hg) * hu).astype(o_ref.dtype)


def _ffn_down_ln_kernel(h_ref, w_ref, x_ref, g_ref, b_ref, o_ref, ob_ref, *, alpha):
    y = jnp.dot(h_ref[...], w_ref[...], preferred_element_type=F32)
    out = _layer_norm(alpha * x_ref[...] + y, g_ref[...], b_ref[...])
    o_ref[...] = out
    ob_ref[...] = out.astype(BF16)


def _ffn_dense(xb, x, wg, wu, layer, wd, g, b, alpha, *, tm_up=1024, tn_up=512, tm_down=256):
    n, d = x.shape
    ff_real = wd.shape[0]
    n_tiles = -(-ff_real // tn_up)
    assert ff_real % LANES == 0 and (ff_real % tn_up == 0 or n_tiles >= 3)
    tm_up, tm_down = min(tm_up, n), min(tm_down, n)
    h = pl.pallas_call(
        functools.partial(_ffn_up_kernel, layer=layer, tn=tn_up, ff=ff_real),
        out_shape=jax.ShapeDtypeStruct((n, n_tiles * tn_up), BF16),
        grid=(n_tiles, n // tm_up),
        in_specs=[pl.BlockSpec((tm_up, d), lambda j, i: (i, 0)),
                  pl.BlockSpec(memory_space=pl.ANY),
                  pl.BlockSpec(memory_space=pl.ANY)],
        out_specs=pl.BlockSpec((tm_up, tn_up), lambda j, i: (i, j)),
        scratch_shapes=[pltpu.VMEM((2, d, tn_up), F32), pltpu.VMEM((2, d, tn_up), F32),
                        pltpu.VMEM((d, tn_up), BF16), pltpu.VMEM((d, tn_up), BF16),
                        pltpu.SemaphoreType.DMA((2, 2))],
        compiler_params=_cparams("arbitrary", "arbitrary"),
        name="ffn_up",
    )(xb, wg, wu)
    tok = pl.BlockSpec((tm_down, d), lambda i: (i, 0))
    row = pl.BlockSpec((1, d), lambda i: (0, 0))
    return pl.pallas_call(
        functools.partial(_ffn_down_ln_kernel, alpha=alpha),
        out_shape=(jax.ShapeDtypeStruct((n, d), F32), jax.ShapeDtypeStruct((n, d), BF16)),
        grid=(n // tm_down,),
        in_specs=[pl.BlockSpec((tm_down, ff_real), lambda i: (i, 0)),
                  pl.BlockSpec((ff_real, d), lambda i: (0, 0), pipeline_mode=pl.Buffered(1)),
                  tok, row, row],
        out_specs=(tok, tok),
        compiler_params=_cparams("parallel"),
        name="ffn_down_ln",
    )(h, wd, x, g.reshape(1, d), b.reshape(1, d))


def _router_kernel(x_ref, w_ref, ids_ref, wts_ref):
    logits = jnp.dot(x_ref[...], w_ref[...], preferred_element_type=F32, precision=lax.Precision.HIGHEST)
    lane_i = lax.broadcasted_iota(I32, logits.shape, 1)
    lane = lane_i.astype(F32)
    logits = jnp.where(lane_i < N_EXPERTS, logits, -jnp.inf)
    m1 = jnp.max(logits, axis=1, keepdims=True)
    i1 = jnp.min(jnp.where(logits == m1, lane, float(LANES)), axis=1, keepdims=True)
    rest = jnp.where(lane == i1, -jnp.inf, logits)
    m2 = jnp.max(rest, axis=1, keepdims=True)
    i2 = jnp.min(jnp.where(rest == m2, lane, float(LANES)), axis=1, keepdims=True)
    e2 = jnp.exp(m2 - m1)
    w1 = 1.0 / (1.0 + e2)
    w2 = e2 / (1.0 + e2)
    ids_ref[...] = jnp.where(lane_i == 0, i1, jnp.where(lane_i == 1, i2, 0.0)).astype(I32)
    wts_ref[...] = jnp.where(lane_i == 0, w1, jnp.where(lane_i == 1, w2, 0.0))


def _router(x, w_router, *, tm=512):
    n, d = x.shape
    wpad = jnp.zeros((d, LANES), F32).at[:, :N_EXPERTS].set(w_router.astype(F32))
    tm = min(tm, n)
    tok = pl.BlockSpec((tm, LANES), lambda i: (i, 0))
    return pl.pallas_call(
        _router_kernel,
        out_shape=(jax.ShapeDtypeStruct((n, LANES), I32), jax.ShapeDtypeStruct((n, LANES), F32)),
        grid=(n // tm,),
        in_specs=[pl.BlockSpec((tm, d), lambda i: (i, 0)), pl.BlockSpec((d, LANES), lambda i: (0, 0))],
        out_specs=(tok, tok),
        compiler_params=_cparams("parallel"),
        name="moe_router",
    )(x, wpad)


def _gather_rows_kernel(tok_ref, x_hbm, o_ref, buf, sem, *, tm):
    i = pl.program_id(0)

    def issue(tile, slot):
        base = tile * tm

        def body(r, carry):
            t = tok_ref[base + r]
            pltpu.make_async_copy(x_hbm.at[pl.ds(t, 1), :], buf.at[slot, pl.ds(r, 1), :], sem.at[slot]).start()
            return carry

        lax.fori_loop(0, tm, body, 0, unroll=8)

    @pl.when(i == 0)
    def _():
        issue(0, 0)

    @pl.when(i + 1 < pl.num_programs(0))
    def _():
        issue(i + 1, (i + 1) % 2)

    slot = i % 2
    pltpu.make_async_copy(x_hbm.at[pl.ds(0, tm), :], buf.at[slot], sem.at[slot]).wait()
    o_ref[...] = buf[slot].astype(o_ref.dtype)


def _gather_rows(x, row_tok, *, tm=256):
    n, d = x.shape
    rows = row_tok.shape[0]
    return pl.pallas_call(
        functools.partial(_gather_rows_kernel, tm=tm),
        out_shape=jax.ShapeDtypeStruct((rows, d), BF16),
        grid_spec=pltpu.PrefetchScalarGridSpec(
            num_scalar_prefetch=1,
            grid=(rows // tm,),
            in_specs=[pl.BlockSpec(memory_space=pl.ANY)],
            out_specs=pl.BlockSpec((tm, d), lambda i, tok: (i, 0)),
            scratch_shapes=[pltpu.VMEM((2, tm, d), F32), pltpu.SemaphoreType.DMA((2,))],
        ),
        compiler_params=_cparams("arbitrary"),
        name="moe_gather_rows",
    )(row_tok, x)


def _expert_changed(te_ref, i):
    return (i == 0) | (te_ref[i] != te_ref[jnp.maximum(i - 1, 0)])


def _stream_expert_weights(j, i, n_j, col_tile, te_ref, run_ref, nxt_ref, meta_ref, w_hbm, stage, work, sem):
    @pl.when(_expert_changed(te_ref, i))
    def _():
        k = j * meta_ref[0] + run_ref[i]
        slot = k % 2

        def copies(e, jj, s):
            col = pl.ds(pl.multiple_of(jj * col_tile, col_tile), col_tile)
            return [pltpu.make_async_copy(w.at[e, :, col], st.at[s], sem.at[s, a])
                    for a, (w, st) in enumerate(zip(w_hbm, stage))]

        @pl.when(k == 0)
        def _():
            for c in copies(te_ref[i], j, slot):
                c.start()

        has_next_run = nxt_ref[i] >= 0
        e_next = jnp.where(has_next_run, nxt_ref[i], meta_ref[1])
        j_next = jnp.where(has_next_run, j, j + 1)

        @pl.when(j_next < n_j)
        def _():
            for c in copies(e_next, j_next, 1 - slot):
                c.start()

        for c in copies(te_ref[i], j, slot):
            c.wait()
        for st, wk in zip(stage, work):
            wk[...] = st[slot].astype(BF16)


def _moe_up_kernel(te_ref, nv_ref, run_ref, nxt_ref, meta_ref, x_ref, wg_hbm, wu_hbm, o_ref,
                   stage_g, stage_u, wg_s, wu_s, sem, *, tf):
    j, i = pl.program_id(0), pl.program_id(1)
    _stream_expert_weights(j, i, pl.num_programs(0), tf, te_ref, run_ref, nxt_ref, meta_ref,
                           (wg_hbm, wu_hbm), (stage_g, stage_u), (wg_s, wu_s), sem)

    @pl.when(i < nv_ref[0])
    def _():
        xb = x_ref[...]
        hg = jnp.dot(xb, wg_s[...], preferred_element_type=F32)
        hu = jnp.dot(xb, wu_s[...], preferred_element_type=F32)
        o_ref[...] = (hg * _sigmoid(hg) * hu).astype(o_ref.dtype)

    @pl.when(i >= nv_ref[0])
    def _():
        o_ref[...] = jnp.zeros(o_ref.shape, o_ref.dtype)


def _moe_up(xs, wg, wu, sched, *, tm, tf=1024):
    rows, d = xs.shape
    ff = wg.shape[-1]
    return pl.pallas_call(
        functools.partial(_moe_up_kernel, tf=tf),
        out_shape=jax.ShapeDtypeStruct((rows, ff), BF16),
        grid_spec=pltpu.PrefetchScalarGridSpec(
            num_scalar_prefetch=len(sched),
            grid=(ff // tf, rows // tm),
            in_specs=[pl.BlockSpec((tm, d), lambda j, i, *_: (i, 0)),
                      pl.BlockSpec(memory_space=pl.ANY),
                      pl.BlockSpec(memory_space=pl.ANY)],
            out_specs=pl.BlockSpec((tm, tf), lambda j, i, *_: (i, j)),
            scratch_shapes=[pltpu.VMEM((2, d, tf), F32), pltpu.VMEM((2, d, tf), F32),
                            pltpu.VMEM((d, tf), BF16), pltpu.VMEM((d, tf), BF16),
                            pltpu.SemaphoreType.DMA((2, 2))],
        ),
        compiler_params=_cparams("arbitrary", "arbitrary"),
        name="moe_up",
    )(*sched, xs, wg, wu)


def _moe_down_kernel(te_ref, nv_ref, run_ref, nxt_ref, meta_ref, h_ref, wd_hbm, o_ref, stage_d, wd_s, sem, *, tn):
    j, i = pl.program_id(0), pl.program_id(1)
    _stream_expert_weights(j, i, pl.num_programs(0), tn, te_ref, run_ref, nxt_ref, meta_ref,
                           (wd_hbm,), (stage_d,), (wd_s,), sem)

    @pl.when(i < nv_ref[0])
    def _():
        o_ref[...] = jnp.dot(h_ref[...], wd_s[...], preferred_element_type=F32)

    @pl.when(i >= nv_ref[0])
    def _():
        o_ref[...] = jnp.zeros(o_ref.shape, o_ref.dtype)


def _moe_down(h, wd, sched, *, tm, tn=512):
    rows, ff = h.shape
    d = wd.shape[-1]
    return pl.pallas_call(
        functools.partial(_moe_down_kernel, tn=tn),
        out_shape=jax.ShapeDtypeStruct((rows, d), F32),
        grid_spec=pltpu.PrefetchScalarGridSpec(
            num_scalar_prefetch=len(sched),
            grid=(d // tn, rows // tm),
            in_specs=[pl.BlockSpec((tm, ff), lambda j, i, *_: (i, 0)),
                      pl.BlockSpec(memory_space=pl.ANY)],
            out_specs=pl.BlockSpec((tm, tn), lambda j, i, *_: (i, j)),
            scratch_shapes=[pltpu.VMEM((2, ff, tn), F32), pltpu.VMEM((ff, tn), BF16),
                            pltpu.SemaphoreType.DMA((2, 1))],
        ),
        compiler_params=_cparams("arbitrary", "arbitrary"),
        name="moe_down",
    )(*sched, h, wd)


def _combine_ln_kernel(pos_ref, y_hbm, x_ref, wts_ref, g_ref, b_ref, o_ref, buf, sem, *, tm, alpha):
    i = pl.program_id(0)

    def issue(tile, slot):
        base = tile * tm

        def body(r, carry):
            for k in range(TOP_K):
                row = pos_ref[(base + r) * TOP_K + k]
                pltpu.make_async_copy(y_hbm.at[pl.ds(row, 1), :], buf.at[slot, k, pl.ds(r, 1), :],
                                      sem.at[slot]).start()
            return carry

        lax.fori_loop(0, tm, body, 0, unroll=8)

    @pl.when(i == 0)
    def _():
        issue(0, 0)

    @pl.when(i + 1 < pl.num_programs(0))
    def _():
        issue(i + 1, (i + 1) % 2)

    slot = i % 2
    for k in range(TOP_K):
        pltpu.make_async_copy(y_hbm.at[pl.ds(0, tm), :], buf.at[slot, k], sem.at[slot]).wait()
    wts = wts_ref[...]
    y = buf[slot, 0] * wts[:, 0:1] + buf[slot, 1] * wts[:, 1:2]
    o_ref[...] = _layer_norm(alpha * x_ref[...] + y, g_ref[...], b_ref[...])


def _combine_ln(yrows, pos, x, wts, g, b, alpha, *, tm=128):
    n, d = x.shape
    tm = min(tm, n)
    return pl.pallas_call(
        functools.partial(_combine_ln_kernel, tm=tm, alpha=alpha),
        out_shape=jax.ShapeDtypeStruct((n, d), F32),
        grid_spec=pltpu.PrefetchScalarGridSpec(
            num_scalar_prefetch=1,
            grid=(n // tm,),
            in_specs=[pl.BlockSpec(memory_space=pl.ANY),
                      pl.BlockSpec((tm, d), lambda i, pos: (i, 0)),
                      pl.BlockSpec((tm, LANES), lambda i, pos: (i, 0)),
                      pl.BlockSpec((1, d), lambda i, pos: (0, 0)),
                      pl.BlockSpec((1, d), lambda i, pos: (0, 0))],
            out_specs=pl.BlockSpec((tm, d), lambda i, pos: (i, 0)),
            scratch_shapes=[pltpu.VMEM((2, TOP_K, tm, d), F32), pltpu.SemaphoreType.DMA((2,))],
        ),
        compiler_params=_cparams("arbitrary"),
        name="moe_combine_ln",
    )(pos, yrows, x, wts, g.reshape(1, d), b.reshape(1, d))


def _moe_block(x, w_router, wg, wu, wd, g, b, alpha, *, tm=256):
    n, d = x.shape
    ids, wts = _router(x, w_router)
    e_flat = ids[:, :TOP_K].reshape(-1)
    n_assign = n * TOP_K
    onehot = (e_flat[:, None] == jnp.arange(N_EXPERTS, dtype=I32)[None, :]).astype(I32)
    rank = jnp.sum((jnp.cumsum(onehot, axis=0) - onehot) * onehot, axis=1)
    counts = jnp.sum(onehot, axis=0)
    padded = (counts + tm - 1) // tm * tm
    end_padded = jnp.cumsum(padded)
    start_padded = end_padded - padded
    dest = (start_padded[e_flat] + rank).astype(I32)
    rows = n_assign + N_EXPERTS * tm
    n_tiles = rows // tm
    flat_tok = jnp.arange(n_assign, dtype=I32) // TOP_K
    row_tok = jnp.zeros((rows,), I32).at[dest].set(flat_tok)
    tile_start = jnp.arange(n_tiles, dtype=I32) * tm
    tile_e = jnp.minimum(jnp.sum((tile_start[:, None] >= end_padded[None, :]).astype(I32), axis=1),
                         N_EXPERTS - 1).astype(I32)
    n_valid = (end_padded[-1:] // tm).astype(I32)
    tile_ix = jnp.arange(n_tiles, dtype=I32)
    tile_e = jnp.where(tile_ix < n_valid[0], tile_e, tile_e[jnp.maximum(n_valid[0] - 1, 0)])
    is_start = jnp.concatenate([jnp.ones((1,), I32), (tile_e[1:] != tile_e[:-1]).astype(I32)])
    run_id = (jnp.cumsum(is_start) - 1).astype(I32)
    larger = jnp.where(tile_e[None, :] > tile_e[:, None], tile_e[None, :], N_EXPERTS)
    nxt_e = jnp.min(larger, axis=1)
    nxt_e = jnp.where(nxt_e < N_EXPERTS, nxt_e, -1).astype(I32)
    meta = jnp.stack([run_id[-1] + 1, tile_e[0]]).astype(I32)
    sched = (tile_e, n_valid, run_id, nxt_e, meta)
    xs = _gather_rows(x, row_tok, tm=tm)
    h = _moe_up(xs, wg, wu, sched, tm=tm)
    yrows = _moe_down(h, wd, sched, tm=tm)
    return _combine_ln(yrows, dest, x, wts, g, b, alpha)


def _w_in_columns(d_model):
    attn_w = N_HEADS * HEAD_DIM
    kv_w = N_KV * HEAD_DIM
    o = np.cumsum([0, attn_w, kv_w, kv_w, IDX_HEADS * IDX_DIM, IDX_DIM, IDX_HEADS, d_model, d_model, d_model])
    o = [int(v) for v in o]
    return {"q": (o[0], attn_w), "k": (o[1], kv_w), "v": (o[2], kv_w), "qi": (o[3], o[4] - o[3]),
            "kiwi": (o[4], LANES),
            "px": (o[6], 2 * d_model), "gr": (o[8], d_model), "gates": (o[9], 3 * d_model)}


def _mixer(x_mm, tabs, wt_in, layer, pool_w, pool_scale, conv_w, conv_b, wa, ba, wx, bx, lam,
           batch, seq, d_model, tq):
    cos_a, sin_a, cos_i, sin_ia, sin_ib, cos_q, sin_q, cos_iq = tabs
    cols = _w_in_columns(d_model)
    rope_a = ((cos_a, sin_a), (HEAD_DIM // 2,))
    rope_i = (LANES - IDX_DIM // 2, IDX_DIM // 2)
    qh = _proj_heads(x_mm, wt_in, layer, cols["q"], (cos_q, sin_q), rope_a[1], BF16)
    kh = _proj_heads(x_mm, wt_in, layer, cols["k"], *rope_a, BF16)
    vh = _proj_heads(x_mm, wt_in, layer, cols["v"], (), (), BF16)
    qih = _proj_heads(x_mm, wt_in, layer, cols["qi"], (cos_iq, sin_ia, sin_ib), rope_i, BF16, head_width=IDX_DIM)
    kiwi = _proj_heads(x_mm, wt_in, layer, cols["kiwi"], (cos_i, sin_ia, sin_ib), rope_i, F32)
    px = _proj_plain(x_mm, wt_in, layer, cols["px"], None, F32)
    gg = _proj_plain(x_mm, wt_in, layer, cols["gr"], "gelu", BF16)
    sg = _proj_plain(x_mm, wt_in, layer, cols["gates"], "sigmoid", BF16)
    ya = _attention(qh, kh, vh, qih, kiwi, sg, batch, seq, tq=tq)
    yb = _pool_mixer(px, sg, pool_w.astype(BF16), pool_scale, batch, seq)
    yc = _lru_mixer(px, gg, sg, conv_w, conv_b, wa.astype(BF16), ba, wx.astype(BF16), bx, lam, batch, seq)
    return ya, yb, yc


def kernel(x, positions, mix_w_in, mix_w_out, pool_w, pool_scale, conv_w, conv_b, lru_wa, lru_ba, lru_wx, lru_bx, lru_lam, ln_mix_g, ln_mix_b, ln_ffn_g, ln_ffn_b, dense_w_gate, dense_w_up, dense_w_down, moe_router, moe_w_gate, moe_w_up, moe_w_down):
    batch, seq, d_model = x.shape
    depth = mix_w_in.shape[0]
    alpha = np.float32((2 * depth) ** 0.25)
    n = batch * seq
    tabs = _rope_tables(positions)
    xf = x.reshape(n, d_model)
    x_mm = xf
    wt_in = jnp.swapaxes(mix_w_in, 1, 2)
    for layer in range(depth):
        ya, yb, yc = _mixer(x_mm, tabs, wt_in, layer, pool_w[layer], pool_scale[layer],
                            conv_w[layer], conv_b[layer], lru_wa[layer], lru_ba[layer], lru_wx[layer],
                            lru_bx[layer], lru_lam[layer], batch, seq, d_model, tq=256)
        w_out3 = mix_w_out[layer].astype(BF16).reshape(3, -1, d_model)
        xf, xb = _out_proj_ln(ya, yb, yc, w_out3, xf, ln_mix_g[layer], ln_mix_b[layer], alpha)
        j = layer // 2
        if layer % 2 == 0:
            wd = dense_w_down[j].astype(BF16)
            xf, x_mm = _ffn_dense(xb, xf, dense_w_gate, dense_w_up, j, wd,
                                  ln_ffn_g[layer], ln_ffn_b[layer], alpha)
        else:
            xf = _moe_block(xf, moe_router[j], moe_w_gate[j], moe_w_up[j], moe_w_down[j],
                            ln_ffn_g[layer], ln_ffn_b[layer], alpha)
            x_mm = xf
    return xf.reshape(batch, seq, d_model)
```

```python
import functools

import jax
import jax.numpy as jnp
import numpy as np
from jax import lax
from jax.experimental import pallas as pl
from jax.experimental.pallas import tpu as pltpu

F32 = jnp.float32
BF16 = jnp.bfloat16
I32 = jnp.int32

LANES = 128
SUBLANES = 8
VMEM_LIMIT = 56 * 1024 * 1024

CHUNK = 64
N_HEADS = 16
HEAD_DIM = 128
N_KV = 4
HEADS_PER_KV = N_HEADS // N_KV
IDX_HEADS = 16
IDX_DIM = 64
TOPK_MAX = 256
ROPE_THETA = 10000.0
POOL_WINDOWS = (2, 4, 8, 16)
LRU_BLOCK = 128
CONV_WIDTH = 4
LRU_C = 8.0
N_EXPERTS = 8
TOP_K = 2
LN_EPS = 1e-5
LN_ROWS = 128
INT_MIN = -2 ** 31
NEG_BIG = -1e30


def _cparams(*sem):
    return pltpu.CompilerParams(dimension_semantics=sem, vmem_limit_bytes=VMEM_LIMIT)


def _sigmoid(x):
    return 0.5 * (1.0 + jnp.tanh(0.5 * x))


def _gelu_tanh(x):
    c = np.float32(np.sqrt(2.0 / np.pi))
    return 0.5 * x * (1.0 + jnp.tanh(c * (x + np.float32(0.044715) * (x * x * x))))


def _layer_norm(y, g, b):
    mu = jnp.mean(y, axis=-1, keepdims=True)
    d = y - mu
    var = jnp.mean(d * d, axis=-1, keepdims=True)
    return d * lax.rsqrt(var + LN_EPS) * g + b


def _rope_tab_kernel(pos_ref, inv_a_ref, inv_i_ref, sgn_a_ref, m_cos_ref, add_cos_ref,
                     m_sa_ref, m_sb_ref, cos_a_ref, sin_a_ref, cos_i_ref, sin_ia_ref, sin_ib_ref,
                     cos_q_ref, sin_q_ref, cos_iq_ref):
    pos = pos_ref[...]
    ang_a = pos * inv_a_ref[...]
    cos_a = jnp.cos(ang_a)
    sin_a = jnp.sin(ang_a) * sgn_a_ref[...]
    cos_a_ref[...] = cos_a
    sin_a_ref[...] = sin_a
    q_scale = np.float32(HEAD_DIM ** -0.5 * np.log2(np.e))
    cos_q_ref[...] = cos_a * q_scale
    sin_q_ref[...] = sin_a * q_scale
    ang_i = pos * inv_i_ref[...]
    s_i = jnp.sin(ang_i)
    cos_iq = jnp.cos(ang_i) * m_cos_ref[...]
    cos_iq_ref[...] = cos_iq
    cos_i_ref[...] = cos_iq + add_cos_ref[...]
    sin_ia_ref[...] = s_i * m_sa_ref[...]
    sin_ib_ref[...] = s_i * m_sb_ref[...]


def _rope_tables(positions):
    n = positions.size
    pos = jnp.broadcast_to(positions.reshape(n, 1).astype(F32), (n, LANES))
    lane = np.arange(LANES)
    inv_a = (ROPE_THETA ** (-jnp.arange(0, HEAD_DIM, 2, dtype=F32) / HEAD_DIM))
    inv_i = (ROPE_THETA ** (-jnp.arange(0, IDX_DIM, 2, dtype=F32) / IDX_DIM))
    inv_a_row = jnp.concatenate([inv_a, inv_a])[None, :]
    inv_i_row = jnp.concatenate([inv_i, inv_i, jnp.zeros((LANES - IDX_DIM,), F32)])[None, :]
    sgn_a = jnp.asarray(np.where(lane < HEAD_DIM // 2, -1.0, 1.0), F32)[None, :]
    m_cos = jnp.asarray((lane < IDX_DIM).astype(np.float32))[None, :]
    wi_scale = (IDX_HEADS ** -0.5) * (IDX_DIM ** -0.5)
    add_cos = jnp.asarray(np.where((lane >= IDX_DIM) & (lane < IDX_DIM + IDX_HEADS), wi_scale, 0.0), F32)[None, :]
    m_sa = jnp.asarray(np.where(lane < IDX_DIM // 2, -1.0, 0.0), F32)[None, :]
    m_sb = jnp.asarray(np.where((lane >= IDX_DIM // 2) & (lane < IDX_DIM), 1.0, 0.0), F32)[None, :]
    tm = min(n, 1024)
    row = pl.BlockSpec((1, LANES), lambda i: (0, 0))
    tok = pl.BlockSpec((tm, LANES), lambda i: (i, 0))
    out = jax.ShapeDtypeStruct((n, LANES), F32)
    return pl.pallas_call(
        _rope_tab_kernel,
        out_shape=(out,) * 8,
        grid=(n // tm,),
        in_specs=[tok] + [row] * 7,
        out_specs=(tok,) * 8,
        compiler_params=_cparams("parallel"),
        name="rope_tables",
    )(pos, inv_a_row, inv_i_row, sgn_a, m_cos, add_cos, m_sa, m_sb)


def _stream_weight_tile(wt_hbm, stage, w_s, sem, *, layer, off, tn):
    j = pl.program_id(0)

    def copy(jj, slot):
        rows = pl.ds(pl.multiple_of(off + jj * tn, SUBLANES), tn)
        return pltpu.make_async_copy(wt_hbm.at[layer, rows, :], stage.at[slot], sem.at[slot])

    @pl.when(pl.program_id(1) == 0)
    def _():
        slot = j % 2

        @pl.when(j == 0)
        def _():
            copy(0, 0).start()

        @pl.when(j + 1 < pl.num_programs(0))
        def _():
            copy(j + 1, 1 - slot).start()

        copy(j, slot).wait()
        w_s[...] = stage[slot].astype(BF16)


def _dot_nt(x, w):
    return lax.dot_general(x, w, (((1,), (1,)), ((), ())), preferred_element_type=F32)


def _proj_heads_kernel(x_ref, wt_hbm, *rest, layer, off, tn, shifts, heads, head_width):
    tabs, o_ref, stage, w_s, sem = rest[:-4], rest[-4], rest[-3], rest[-2], rest[-1]
    _stream_weight_tile(wt_hbm, stage, w_s, sem, layer=layer, off=off, tn=tn)
    acc = _dot_nt(x_ref[...].astype(BF16), w_s[...])
    per_slab = LANES // head_width
    for h in range(heads):
        xh = acc[:, (h // per_slab) * LANES:(h // per_slab + 1) * LANES]
        if h % per_slab:
            xh = pltpu.roll(xh, LANES - (h % per_slab) * head_width, 1)
        if tabs:
            y = xh * tabs[0][...]
            for s, t in zip(shifts, tabs[1:]):
                y = y + pltpu.roll(xh, s, 1) * t[...]
        else:
            y = xh
        o_ref[h] = y.astype(o_ref.dtype)


def _weight_stream_scratch(tn, d):
    return [pltpu.VMEM((2, tn, d), F32), pltpu.VMEM((tn, d), BF16), pltpu.SemaphoreType.DMA((2,))]


def _proj_heads(x, wt, layer, cols, tabs, shifts, out_dtype, *, head_width=LANES, tm=1024, tile_cols=1024):
    n, d = x.shape
    off, width = cols
    tn = min(tile_cols, width)
    hp = tn // head_width
    assert width % tn == 0 and off % SUBLANES == 0 and (head_width == LANES or tabs)
    tm = min(tm, n)
    tab_spec = pl.BlockSpec((tm, LANES), lambda j, i: (i, 0))
    return pl.pallas_call(
        functools.partial(_proj_heads_kernel, layer=layer, off=off, tn=tn, shifts=shifts, heads=hp,
                          head_width=head_width),
        out_shape=jax.ShapeDtypeStruct((width // head_width, n, LANES), out_dtype),
        grid=(width // tn, n // tm),
        in_specs=[pl.BlockSpec((tm, d), lambda j, i: (i, 0)),
                  pl.BlockSpec(memory_space=pl.ANY)] + [tab_spec] * len(tabs),
        out_specs=pl.BlockSpec((hp, tm, LANES), lambda j, i: (j, i, 0)),
        scratch_shapes=_weight_stream_scratch(tn, d),
        compiler_params=_cparams("arbitrary", "arbitrary"),
        name="proj_heads",
    )(x, wt, *tabs)


def _proj_plain_kernel(x_ref, wt_hbm, o_ref, stage, w_s, sem, *, layer, off, tn, act):
    _stream_weight_tile(wt_hbm, stage, w_s, sem, layer=layer, off=off, tn=tn)
    acc = _dot_nt(x_ref[...].astype(BF16), w_s[...])
    if act == "gelu":
        acc = _gelu_tanh(acc)
    elif act == "sigmoid":
        acc = _sigmoid(acc)
    o_ref[...] = acc.astype(o_ref.dtype)


def _proj_plain(x, wt, layer, cols, act, out_dtype, *, tm=1024, tn=1024):
    n, d = x.shape
    off, width = cols
    assert width % tn == 0 and off % SUBLANES == 0
    tm = min(tm, n)
    return pl.pallas_call(
        functools.partial(_proj_plain_kernel, layer=layer, off=off, tn=tn, act=act),
        out_shape=jax.ShapeDtypeStruct((n, width), out_dtype),
        grid=(width // tn, n // tm),
        in_specs=[pl.BlockSpec((tm, d), lambda j, i: (i, 0)),
                  pl.BlockSpec(memory_space=pl.ANY)],
        out_specs=pl.BlockSpec((tm, tn), lambda j, i: (i, j)),
        scratch_shapes=_weight_stream_scratch(tn, d),
        compiler_params=_cparams("arbitrary", "arbitrary"),
        name="proj_plain",
    )(x, wt)


def _key_to_float(key):
    return pltpu.bitcast(key ^ ((key >> 31) & 0x7FFFFFFF), F32)


def _attn_kernel(q_ref, k_ref, v_ref, qi_ref, ki_ref, wi_ref, gate_ref, o_ref,
                 wb_s, key_s, keyt_s, x_s, thr_s, xrow_s, *state, tq, topk, seq):
    m_s, acc_s = state[:N_KV], state[N_KV:]
    kb = tq
    n_sub = kb // LANES
    qt = pl.program_id(1)
    nkb = qt + 1
    n_hi = IDX_HEADS
    rows_g = HEADS_PER_KV * tq

    wi = wi_ref[...]
    for h in range(n_hi):
        wb_s[h] = jnp.broadcast_to(wi[:, IDX_DIM + h:IDX_DIM + h + 1], (tq, LANES))
    qi = qi_ref[...].reshape(n_hi * tq, LANES)
    q_row = qt * tq + lax.broadcasted_iota(I32, (tq, LANES), 0)
    limit = (q_row // CHUNK + 1) * CHUNK
    lane_pos = lax.broadcasted_iota(I32, (tq, LANES), 1)
    limit_t = ((qt * tq + lax.broadcasted_iota(I32, (1, tq), 1)) // CHUNK + 1) * CHUNK

    def score_body(j, carry):
        start = pl.multiple_of(j * kb, kb)
        ki_blk = ki_ref[pl.ds(start, kb), :].astype(BF16)
        s = lax.dot_general(qi, ki_blk, (((1,), (1,)), ((), ())), preferred_element_type=F32)
        parts = []
        for c in range(n_sub):
            sc = jnp.zeros((tq, LANES), F32)
            for h in range(n_hi):
                sh = s[h * tq:(h + 1) * tq, c * LANES:(c + 1) * LANES]
                sc = sc + wb_s[h] * jnp.maximum(sh, 0.0)
            sc = jnp.where(sc == 0.0, 0.0, sc)
            parts.append(sc)
            kpos = start + c * LANES + lane_pos
            key_s[j, :, c * LANES:(c + 1) * LANES] = jnp.where(kpos < limit, sc, -jnp.inf)
        sc_t = jnp.concatenate(parts, axis=1).T
        kpos_t = start + lax.broadcasted_iota(I32, (kb, tq), 0)
        keyt_s[j] = jnp.where(kpos_t < limit_t, sc_t, -jnp.inf)
        return carry

    lax.fori_loop(0, nkb, score_body, 0)

    kf = np.float32(topk)
    searched = limit_t > topk

    def count_keys(pred):
        def body(j, cnt):
            hit = jnp.where(pred(keyt_s[j], j), 1.0, 0.0)
            return cnt + jnp.sum(hit.reshape(kb // SUBLANES, SUBLANES, tq), axis=0)
        cnt = lax.fori_loop(0, nkb, body, jnp.zeros((SUBLANES, tq), F32))
        return jnp.sum(cnt, axis=0, keepdims=True)

    thr0 = jnp.where(count_keys(lambda s, j: s >= 0.0) >= kf, 0, INT_MIN).astype(I32)

    def thr_body(it, thr):
        cand = thr | jnp.left_shift(jnp.int32(1), 30 - it)
        cand_f = _key_to_float(cand)
        return jnp.where(count_keys(lambda s, j: s >= cand_f) >= kf, cand, thr)

    thr = _key_to_float(lax.fori_loop(0, 31, thr_body, thr0))
    thr = jnp.where(searched, thr, -jnp.inf)
    n_ge = count_keys(lambda s, j: s >= thr)
    need = kf - count_keys(lambda s, j: s > thr)
    xrow_s[...] = jnp.where(searched, seq, -1).astype(I32)
    tie_flag = jnp.max(jnp.where(searched & (n_ge > kf), 1.0, 0.0), axis=(0, 1), keepdims=True)

    @pl.when(tie_flag[0, 0] > 0.0)
    def _():
        nbits = max(int(seq - 1).bit_length(), 1)
        row_pos = lax.broadcasted_iota(I32, (kb, tq), 0)

        def x_body(it, xcut):
            cand = xcut | jnp.left_shift(jnp.int32(1), nbits - 1 - it)
            cnt = count_keys(lambda s, j: (s == thr) & (j * kb + row_pos < cand))
            return jnp.where(cnt < need, cand, xcut)

        xcut = lax.fori_loop(0, nbits, x_body, jnp.zeros((1, tq), I32))
        xrow_s[...] = jnp.where(searched, xcut, -1)

    def to_col(row_f32):
        return jnp.broadcast_to(row_f32, (LANES, tq)).T

    thr_s[...] = to_col(thr)
    x_s[...] = to_col(xrow_s[...].astype(F32)).astype(I32)
    rb = min(tq, 128)
    n_rc = tq // rb
    lane_rb = lax.broadcasted_iota(I32, (rb, LANES), 1)

    for g in range(N_KV):
        m_s[g][...] = jnp.full((rows_g, LANES), NEG_BIG, F32)
        acc_s[g][...] = jnp.zeros((rows_g, 2 * LANES), F32)
    ones_v = jnp.ones((kb, LANES), BF16)

    def attn_body(j, carry):
        start = pl.multiple_of(j * kb, kb)
        bias = []
        for rc in range(n_rc):
            thr_c = thr_s[rc * rb:(rc + 1) * rb, :]
            xcut_c = x_s[rc * rb:(rc + 1) * rb, :]
            parts = []
            for c in range(n_sub):
                kc = key_s[j, rc * rb:(rc + 1) * rb, c * LANES:(c + 1) * LANES]
                kpos = start + c * LANES + lane_rb
                sel = (kc > thr_c) | ((kc == thr_c) & (kpos <= xcut_c))
                parts.append(jnp.where(sel, 0.0, NEG_BIG))
            bias.append(jnp.concatenate(parts, axis=1))
        for g in range(N_KV):
            qg = q_ref[g * HEADS_PER_KV:(g + 1) * HEADS_PER_KV].reshape(rows_g, LANES)
            kg = k_ref[g, pl.ds(start, kb), :]
            vg = jnp.concatenate([v_ref[g, pl.ds(start, kb), :], ones_v], axis=1)
            lg_all = lax.dot_general(qg, kg, (((1,), (1,)), ((), ())), preferred_element_type=F32)
            m_prev_all = m_s[g][...]
            p_parts, a_parts, m_parts = [], [], []
            for r in range(HEADS_PER_KV):
                for rc in range(n_rc):
                    r0 = r * tq + rc * rb
                    lg = lg_all[r0:r0 + rb] + bias[rc]
                    m_prev = m_prev_all[r0:r0 + rb]
                    m_new = jnp.maximum(m_prev, jnp.max(lg, axis=1, keepdims=True))
                    p = jnp.exp2(lg - jnp.concatenate([m_new] * n_sub, axis=1))
                    p_parts.append(p.astype(BF16))
                    a_parts.append(jnp.exp2(m_prev - m_new))
                    m_parts.append(m_new)
            p_all = jnp.concatenate(p_parts, axis=0)
            alpha_all = jnp.concatenate(a_parts, axis=0)
            pv = jnp.dot(p_all, vg, preferred_element_type=F32)
            acc_s[g][...] = jnp.concatenate([alpha_all, alpha_all], axis=1) * acc_s[g][...] + pv
            m_s[g][...] = jnp.concatenate(m_parts, axis=0)
        return carry

    lax.fori_loop(0, nkb, attn_body, 0)

    for g in range(N_KV):
        acc = acc_s[g][...]
        og = acc[:, :LANES] / acc[:, LANES:]
        for r in range(HEADS_PER_KV):
            col = (g * HEADS_PER_KV + r) * LANES
            y = og[r * tq:(r + 1) * tq] * gate_ref[:, col:col + LANES].astype(F32)
            o_ref[:, col:col + LANES] = y.astype(o_ref.dtype)


def _attention(qh, kh, vh, qih, kiwi, sg, batch, seq, *, tq):
    n = batch * seq
    topk = min(TOPK_MAX, seq // 4)
    tq = min(tq, seq)
    nqt = seq // tq
    attn_w = N_HEADS * HEAD_DIM
    rows_g = HEADS_PER_KV * tq
    return pl.pallas_call(
        functools.partial(_attn_kernel, tq=tq, topk=topk, seq=seq),
        out_shape=jax.ShapeDtypeStruct((n, attn_w), BF16),
        grid=(batch, nqt),
        in_specs=[
            pl.BlockSpec((N_HEADS, tq, LANES), lambda b, t: (0, b * nqt + t, 0)),
            pl.BlockSpec((N_KV, seq, LANES), lambda b, t: (0, b, 0)),
            pl.BlockSpec((N_KV, seq, LANES), lambda b, t: (0, b, 0)),
            pl.BlockSpec((IDX_HEADS, tq, LANES), lambda b, t: (0, b * nqt + t, 0)),
            pl.BlockSpec((None, seq, LANES), lambda b, t: (0, b, 0)),
            pl.BlockSpec((None, tq, LANES), lambda b, t: (0, b * nqt + t, 0)),
            pl.BlockSpec((tq, attn_w), lambda b, t: (b * nqt + t, 0)),
        ],
        out_specs=pl.BlockSpec((tq, attn_w), lambda b, t: (b * nqt + t, 0)),
        scratch_shapes=[
            pltpu.VMEM((IDX_HEADS, tq, LANES), F32),
            pltpu.VMEM((nqt, tq, tq), F32),
            pltpu.VMEM((nqt, tq, tq), F32),
            pltpu.VMEM((tq, LANES), I32),
            pltpu.VMEM((tq, LANES), F32),
            pltpu.VMEM((1, tq), I32),
        ] + [pltpu.VMEM((rows_g, LANES), F32)] * N_KV + [pltpu.VMEM((rows_g, 2 * LANES), F32)] * N_KV,
        compiler_params=_cparams("parallel", "arbitrary"),
        name="sparse_attention",
    )(qh, kh, vh, qih, kiwi, kiwi, sg)


def _pool_kernel(p_ref, gate_ref, w_ref, scale_ref, o_ref, buf_a, buf_b, *, seq):
    pad = 16
    g = pl.program_id(1)
    p = p_ref[...]
    zeros = jnp.zeros((pad, p.shape[1]), F32)
    buf_a[0:pad, :] = zeros
    buf_b[0:pad, :] = zeros
    buf_a[pad:pad + seq, :] = p
    s2 = p + buf_a[pad - 1:pad - 1 + seq, :]
    buf_b[pad:pad + seq, :] = s2
    s4 = s2 + buf_b[pad - 2:pad - 2 + seq, :]
    buf_a[pad:pad + seq, :] = s4
    s8 = s4 + buf_a[pad - 4:pad - 4 + seq, :]
    buf_b[pad:pad + seq, :] = s8
    s16 = s8 + buf_b[pad - 8:pad - 8 + seq, :]
    t1 = (lax.broadcasted_iota(I32, p.shape, 0) + 1).astype(F32)
    win = jnp.where(g == 0, 2.0, jnp.where(g == 1, 4.0, jnp.where(g == 2, 8.0, 16.0))).astype(F32)
    total = jnp.where(g == 0, s2, jnp.where(g == 1, s4, jnp.where(g == 2, s8, s16)))
    mean = total / jnp.minimum(t1, win)
    diff = (mean - p).astype(BF16)
    y = jnp.dot(diff, w_ref[...], preferred_element_type=F32)
    o_ref[...] = (y * scale_ref[...] * gate_ref[...].astype(F32)).astype(o_ref.dtype)


def _pool_mixer(px, sg, pool_w, pool_scale, batch, seq):
    n = batch * seq
    width = pool_scale.shape[-1]
    ng = len(POOL_WINDOWS)
    cg = width // ng
    assert POOL_WINDOWS == (2, 4, 8, 16)
    return pl.pallas_call(
        functools.partial(_pool_kernel, seq=seq),
        out_shape=jax.ShapeDtypeStruct((n, width), BF16),
        grid=(batch, ng),
        in_specs=[
            pl.BlockSpec((seq, cg), lambda b, g: (b, g)),
            pl.BlockSpec((seq, cg), lambda b, g: (b, ng + g)),
            pl.BlockSpec((None, cg, cg), lambda b, g: (g, 0, 0)),
            pl.BlockSpec((1, cg), lambda b, g: (0, g)),
        ],
        out_specs=pl.BlockSpec((seq, cg), lambda b, g: (b, g)),
        scratch_shapes=[pltpu.VMEM((seq + 16, cg), F32), pltpu.VMEM((seq + 16, cg), F32)],
        compiler_params=_cparams("parallel", "arbitrary"),
        name="pool_mixer",
    )(px, sg, pool_w, pool_scale.reshape(1, width))


def _lru_kernel(x_ref, gr_ref, gate_ref, cw_ref, cb_ref, wa_ref, ba_ref, wx_ref, bx_ref, lam_ref,
                o_ref, a_s, b_s, *, seq, ct):
    x = x_ref[...]
    row = lax.broadcasted_iota(I32, (seq, ct), 0)
    xc = jnp.broadcast_to(cb_ref[...], (seq, ct))
    for tap in range(CONV_WIDTH):
        d = CONV_WIDTH - 1 - tap
        x_d = x if d == 0 else jnp.where(row >= d, pltpu.roll(x, d, 0), 0.0)
        xc = xc + x_d * cw_ref[tap:tap + 1, :]
    xcb = xc.astype(BF16)
    nb = ct // LRU_BLOCK
    r_parts, i_parts = [], []
    for blk in range(nb):
        xb = xcb[:, blk * LRU_BLOCK:(blk + 1) * LRU_BLOCK]
        r_parts.append(jnp.dot(xb, wa_ref[blk], preferred_element_type=F32))
        i_parts.append(jnp.dot(xb, wx_ref[blk], preferred_element_type=F32))
    r = _sigmoid(jnp.concatenate(r_parts, axis=1) + ba_ref[...])
    gi = _sigmoid(jnp.concatenate(i_parts, axis=1) + bx_ref[...])
    lam = lam_ref[...]
    softplus_neg_lam = jnp.log(1.0 + jnp.exp(-lam))
    log_a = -LRU_C * r * softplus_neg_lam
    a = jnp.exp(log_a)
    one_m_a2 = 1.0 - a * a
    root = jnp.where(one_m_a2 > 0.0, one_m_a2 * lax.rsqrt(one_m_a2), 0.0)
    b = root * (gi * xc)

    tiles = (seq // SUBLANES, SUBLANES, ct)
    a = a.reshape(tiles)
    b = b.reshape(tiles)
    sub = lax.broadcasted_iota(I32, tiles, 1)
    for d in (1, 2, 4):
        keep = sub >= d
        a_sh = jnp.where(keep, pltpu.roll(a, d, 1), 1.0)
        b_sh = jnp.where(keep, pltpu.roll(b, d, 1), 0.0)
        b = a * b_sh + b
        a = a * a_sh
    a_s[...] = a.reshape(seq, ct)
    b_s[...] = b.reshape(seq, ct)

    def body(t, carry):
        r0 = pl.multiple_of(t * SUBLANES, SUBLANES)
        h = a_s[pl.ds(r0, SUBLANES), :] * carry + b_s[pl.ds(r0, SUBLANES), :]
        b_s[pl.ds(r0, SUBLANES), :] = h
        return jnp.broadcast_to(h[SUBLANES - 1:SUBLANES, :], (SUBLANES, ct))

    lax.fori_loop(0, seq // SUBLANES, body, jnp.zeros((SUBLANES, ct), F32), unroll=8)
    h = b_s[...]
    o_ref[...] = (h * gr_ref[...].astype(F32) * gate_ref[...].astype(F32)).astype(o_ref.dtype)


def _lru_mixer(px, gg, sg, conv_w, conv_b, wa, ba, wx, bx, lam, batch, seq, *, ct=256):
    n = batch * seq
    width = conv_b.shape[-1]
    nct = width // ct
    nb = ct // LRU_BLOCK
    row = lambda a: a.reshape(1, width)
    rspec = pl.BlockSpec((1, ct), lambda b, j: (0, j))
    return pl.pallas_call(
        functools.partial(_lru_kernel, seq=seq, ct=ct),
        out_shape=jax.ShapeDtypeStruct((n, width), BF16),
        grid=(batch, nct),
        in_specs=[
            pl.BlockSpec((seq, ct), lambda b, j: (b, nct + j)),
            pl.BlockSpec((seq, ct), lambda b, j: (b, j)),
            pl.BlockSpec((seq, ct), lambda b, j: (b, 2 * nct + j)),
            pl.BlockSpec((CONV_WIDTH, ct), lambda b, j: (0, j)),
            rspec,
            pl.BlockSpec((nb, LRU_BLOCK, LRU_BLOCK), lambda b, j: (j, 0, 0)),
            rspec,
            pl.BlockSpec((nb, LRU_BLOCK, LRU_BLOCK), lambda b, j: (j, 0, 0)),
            rspec,
            rspec,
        ],
        out_specs=pl.BlockSpec((seq, ct), lambda b, j: (b, j)),
        scratch_shapes=[pltpu.VMEM((seq, ct), F32),
                        pltpu.VMEM((seq, ct), F32)],
        compiler_params=_cparams("parallel", "arbitrary"),
        name="rglru_mixer",
    )(px, gg, sg, conv_w, row(conv_b), wa, row(ba), wx, row(bx), row(lam))


def _top2_route(x, w_ref, ids_ref, wts_ref):
    logits = jnp.dot(x, w_ref[...], preferred_element_type=F32, precision=lax.Precision.HIGHEST)
    lane_i = lax.broadcasted_iota(I32, logits.shape, 1)
    lane = lane_i.astype(F32)
    logits = jnp.where(lane_i < N_EXPERTS, logits, -jnp.inf)
    m1 = jnp.max(logits, axis=1, keepdims=True)
    i1 = jnp.min(jnp.where(logits == m1, lane, float(LANES)), axis=1, keepdims=True)
    rest = jnp.where(lane == i1, -jnp.inf, logits)
    m2 = jnp.max(rest, axis=1, keepdims=True)
    i2 = jnp.min(jnp.where(rest == m2, lane, float(LANES)), axis=1, keepdims=True)
    e2 = jnp.exp(m2 - m1)
    w1 = 1.0 / (1.0 + e2)
    w2 = e2 / (1.0 + e2)
    ids_ref[...] = jnp.where(lane_i == 0, i1, jnp.where(lane_i == 1, i2, 0.0)).astype(I32)
    wts_ref[...] = jnp.where(lane_i == 0, w1, jnp.where(lane_i == 1, w2, 0.0))


def _out_proj_ln_kernel(ya_ref, yb_ref, yc_ref, w_ref, x_ref, g_ref, b_ref, *rest, alpha, route):
    acc = jnp.dot(ya_ref[...], w_ref[0], preferred_element_type=F32)
    acc = acc + jnp.dot(yb_ref[...], w_ref[1], preferred_element_type=F32)
    acc = acc + jnp.dot(yc_ref[...], w_ref[2], preferred_element_type=F32)
    out = _layer_norm(alpha * x_ref[...] + acc, g_ref[...], b_ref[...])
    if route:
        wr_ref, o_ref, ob_ref, ids_ref, wts_ref = rest
        _top2_route(out, wr_ref, ids_ref, wts_ref)
    else:
        o_ref, ob_ref = rest
    o_ref[...] = out
    ob_ref[...] = out.astype(BF16)


def _out_proj_ln(ya, yb, yc, w3, x, g, b, alpha, w_router=None, *, tm=256):
    n, width = ya.shape
    d = w3.shape[-1]
    tm = min(tm, n)
    aspec = pl.BlockSpec((tm, width), lambda i: (i, 0))
    tok = pl.BlockSpec((tm, d), lambda i: (i, 0))
    row = pl.BlockSpec((1, d), lambda i: (0, 0))
    in_specs = [aspec, aspec, aspec,
                pl.BlockSpec((3, width, d), lambda i: (0, 0, 0), pipeline_mode=pl.Buffered(1)),
                tok, row, row]
    args = [ya, yb, yc, w3, x, g.reshape(1, d), b.reshape(1, d)]
    out_shape = [jax.ShapeDtypeStruct((n, d), F32), jax.ShapeDtypeStruct((n, d), BF16)]
    out_specs = [tok, tok]
    if w_router is not None:
        lane_tok = pl.BlockSpec((tm, LANES), lambda i: (i, 0))
        in_specs.append(pl.BlockSpec((d, LANES), lambda i: (0, 0)))
        args.append(jnp.zeros((d, LANES), F32).at[:, :N_EXPERTS].set(w_router.astype(F32)))
        out_shape += [jax.ShapeDtypeStruct((n, LANES), I32), jax.ShapeDtypeStruct((n, LANES), F32)]
        out_specs += [lane_tok, lane_tok]
    return pl.pallas_call(
        functools.partial(_out_proj_ln_kernel, alpha=alpha, route=w_router is not None),
        out_shape=tuple(out_shape),
        grid=(n // tm,),
        in_specs=in_specs,
        out_specs=tuple(out_specs),
        compiler_params=_cparams("parallel"),
        name="out_proj_ln",
    )(*args)


def _ffn_up_kernel(x_ref, wg_hbm, wu_hbm, o_ref, stage_g, stage_u, wg_s, wu_s, sem, *, layer, tn, ff):
    j, i = pl.program_id(0), pl.program_id(1)
    nj = pl.num_programs(0)
    n_tiles = -(-ff // tn)
    tail = ff - (n_tiles - 1) * tn

    def on_tile(jj, slot, fn):
        def run(width):
            col = pl.ds(pl.multiple_of(jj * tn, LANES), width)
            for a, (w, st) in enumerate(((wg_hbm, stage_g), (wu_hbm, stage_u))):
                fn(pltpu.make_async_copy(w.at[layer, :, col], st.at[slot, :, pl.ds(0, width)], sem.at[slot, a]))

        @pl.when(jj < nj - 1)
        def _():
            run(tn)

        @pl.when(jj == nj - 1)
        def _():
            run(tail)

    @pl.when(i == 0)
    def _():
        slot = j % 2

        @pl.when(j == 0)
        def _():
            on_tile(0, 0, lambda c: c.start())

        @pl.when(j + 1 < nj)
        def _():
            on_tile(j + 1, 1 - slot, lambda c: c.start())

        on_tile(j, slot, lambda c: c.wait())
        wg_s[...] = stage_g[slot].astype(BF16)
        wu_s[...] = stage_u[slot].astype(BF16)
        if tail < tn:
            @pl.when(j == nj - 1)
            def _():
                wg_s[:, tail:] = jnp.zeros((wg_s.shape[0], tn - tail), BF16)
                wu_s[:, tail:] = jnp.zeros((wu_s.shape[0], tn - tail), BF16)

    xb = x_ref[...]
    hg = jnp.dot(xb, wg_s[...], preferred_element_type=F32)
    hu = jnp.dot(xb, wu_s[...], preferred_element_type=F32)
    o_ref[...] = (hg * _sigmoid(hg) * hu).astype(o_ref.dtype)


def _ffn_down_ln_kernel(h_ref, w_ref, x_ref, g_ref, b_ref, o_ref, ob_ref, *, alpha):
    y = jnp.dot(h_ref[...], w_ref[...], preferred_element_type=F32)
    out = _layer_norm(alpha * x_ref[...] + y, g_ref[...], b_ref[...])
    o_ref[...] = out
    ob_ref[...] = out.astype(BF16)


def _ffn_dense(xb, x, wg, wu, layer, wd, g, b, alpha, *, tm_up=1024, tn_up=512, tm_down=256):
    n, d = x.shape
    ff_real = wd.shape[0]
    n_tiles = -(-ff_real // tn_up)
    assert ff_real % LANES == 0 and (ff_real % tn_up == 0 or n_tiles >= 3)
    tm_up, tm_down = min(tm_up, n), min(tm_down, n)
    h = pl.pallas_call(
        functools.partial(_ffn_up_kernel, layer=layer, tn=tn_up, ff=ff_real),
        out_shape=jax.ShapeDtypeStruct((n, n_tiles * tn_up), BF16),
        grid=(n_tiles, n // tm_up),
        in_specs=[pl.BlockSpec((tm_up, d), lambda j, i: (i, 0)),
                  pl.BlockSpec(memory_space=pl.ANY),
                  pl.BlockSpec(memory_space=pl.ANY)],
        out_specs=pl.BlockSpec((tm_up, tn_up), lambda j, i: (i, j)),
        scratch_shapes=[pltpu.VMEM((2, d, tn_up), F32), pltpu.VMEM((2, d, tn_up), F32),
                        pltpu.VMEM((d, tn_up), BF16), pltpu.VMEM((d, tn_up), BF16),
                        pltpu.SemaphoreType.DMA((2, 2))],
        compiler_params=_cparams("arbitrary", "arbitrary"),
        name="ffn_up",
    )(xb, wg, wu)
    tok = pl.BlockSpec((tm_down, d), lambda i: (i, 0))
    row = pl.BlockSpec((1, d), lambda i: (0, 0))
    return pl.pallas_call(
        functools.partial(_ffn_down_ln_kernel, alpha=alpha),
        out_shape=(jax.ShapeDtypeStruct((n, d), F32), jax.ShapeDtypeStruct((n, d), BF16)),
        grid=(n // tm_down,),
        in_specs=[pl.BlockSpec((tm_down, ff_real), lambda i: (i, 0)),
                  pl.BlockSpec((ff_real, d), lambda i: (0, 0), pipeline_mode=pl.Buffered(1)),
                  tok, row, row],
        out_specs=(tok, tok),
        compiler_params=_cparams("parallel"),
        name="ffn_down_ln",
    )(h, wd, x, g.reshape(1, d), b.reshape(1, d))


def _gather_rows_kernel(tok_ref, x_hbm, o_ref, buf, sem, *, tm):
    i = pl.program_id(0)

    def issue(tile, slot):
        base = tile * tm

        def body(r, carry):
            t = tok_ref[base + r]
            pltpu.make_async_copy(x_hbm.at[pl.ds(t, 1), :], buf.at[slot, pl.ds(r, 1), :], sem.at[slot]).start()
            return carry

        lax.fori_loop(0, tm, body, 0, unroll=8)

    @pl.when(i == 0)
    def _():
        issue(0, 0)

    @pl.when(i + 1 < pl.num_programs(0))
    def _():
        issue(i + 1, (i + 1) % 2)

    slot = i % 2
    pltpu.make_async_copy(x_hbm.at[pl.ds(0, tm), :], buf.at[slot], sem.at[slot]).wait()
    o_ref[...] = buf[slot].astype(o_ref.dtype)


def _gather_rows(x, row_tok, *, tm=256):
    n, d = x.shape
    rows = row_tok.shape[0]
    return pl.pallas_call(
        functools.partial(_gather_rows_kernel, tm=tm),
        out_shape=jax.ShapeDtypeStruct((rows, d), BF16),
        grid_spec=pltpu.PrefetchScalarGridSpec(
            num_scalar_prefetch=1,
            grid=(rows // tm,),
            in_specs=[pl.BlockSpec(memory_space=pl.ANY)],
            out_specs=pl.BlockSpec((tm, d), lambda i, tok: (i, 0)),
            scratch_shapes=[pltpu.VMEM((2, tm, d), F32), pltpu.SemaphoreType.DMA((2,))],
        ),
        compiler_params=_cparams("arbitrary"),
        name="moe_gather_rows",
    )(row_tok, x)


def _expert_changed(te_ref, i):
    return (i == 0) | (te_ref[i] != te_ref[jnp.maximum(i - 1, 0)])


def _stream_expert_weights(j, i, n_j, col_tile, te_ref, run_ref, nxt_ref, meta_ref, w_hbm, stage, work, sem):
    @pl.when(_expert_changed(te_ref, i))
    def _():
        k = j * meta_ref[0] + run_ref[i]
        slot = k % 2

        def copies(e, jj, s):
            col = pl.ds(pl.multiple_of(jj * col_tile, col_tile), col_tile)
            return [pltpu.make_async_copy(w.at[e, :, col], st.at[s], sem.at[s, a])
                    for a, (w, st) in enumerate(zip(w_hbm, stage))]

        @pl.when(k == 0)
        def _():
            for c in copies(te_ref[i], j, slot):
                c.start()

        has_next_run = nxt_ref[i] >= 0
        e_next = jnp.where(has_next_run, nxt_ref[i], meta_ref[1])
        j_next = jnp.where(has_next_run, j, j + 1)

        @pl.when(j_next < n_j)
        def _():
            for c in copies(e_next, j_next, 1 - slot):
                c.start()

        for c in copies(te_ref[i], j, slot):
            c.wait()
        for st, wk in zip(stage, work):
            wk[...] = st[slot].astype(BF16)


def _moe_up_kernel(te_ref, nv_ref, run_ref, nxt_ref, meta_ref, x_ref, wg_hbm, wu_hbm, o_ref,
                   stage_g, stage_u, wg_s, wu_s, sem, *, tf):
    j, i = pl.program_id(0), pl.program_id(1)
    _stream_expert_weights(j, i, pl.num_programs(0), tf, te_ref, run_ref, nxt_ref, meta_ref,
                           (wg_hbm, wu_hbm), (stage_g, stage_u), (wg_s, wu_s), sem)

    @pl.when(i < nv_ref[0])
    def _():
        xb = x_ref[...]
        hg = jnp.dot(xb, wg_s[...], preferred_element_type=F32)
        hu = jnp.dot(xb, wu_s[...], preferred_element_type=F32)
        o_ref[...] = (hg * _sigmoid(hg) * hu).astype(o_ref.dtype)

    @pl.when(i >= nv_ref[0])
    def _():
        o_ref[...] = jnp.zeros(o_ref.shape, o_ref.dtype)


def _moe_up(xs, wg, wu, sched, *, tm, tf=1024):
    rows, d = xs.shape
    ff = wg.shape[-1]
    return pl.pallas_call(
        functools.partial(_moe_up_kernel, tf=tf),
        out_shape=jax.ShapeDtypeStruct((rows, ff), BF16),
        grid_spec=pltpu.PrefetchScalarGridSpec(
            num_scalar_prefetch=len(sched),
            grid=(ff // tf, rows // tm),
            in_specs=[pl.BlockSpec((tm, d), lambda j, i, *_: (i, 0)),
                      pl.BlockSpec(memory_space=pl.ANY),
                      pl.BlockSpec(memory_space=pl.ANY)],
            out_specs=pl.BlockSpec((tm, tf), lambda j, i, *_: (i, j)),
            scratch_shapes=[pltpu.VMEM((2, d, tf), F32), pltpu.VMEM((2, d, tf), F32),
                            pltpu.VMEM((d, tf), BF16), pltpu.VMEM((d, tf), BF16),
                            pltpu.SemaphoreType.DMA((2, 2))],
        ),
        compiler_params=_cparams("arbitrary", "arbitrary"),
        name="moe_up",
    )(*sched, xs, wg, wu)


def _moe_down_kernel(te_ref, nv_ref, run_ref, nxt_ref, meta_ref, h_ref, wd_hbm, o_ref, stage_d, wd_s, sem, *, tn):
    j, i = pl.program_id(0), pl.program_id(1)
    _stream_expert_weights(j, i, pl.num_programs(0), tn, te_ref, run_ref, nxt_ref, meta_ref,
                           (wd_hbm,), (stage_d,), (wd_s,), sem)

    @pl.when(i < nv_ref[0])
    def _():
        o_ref[...] = jnp.dot(h_ref[...], wd_s[...], preferred_element_type=F32)

    @pl.when(i >= nv_ref[0])
    def _():
        o_ref[...] = jnp.zeros(o_ref.shape, o_ref.dtype)


def _moe_down(h, wd, sched, *, tm, tn=512):
    rows, ff = h.shape
    d = wd.shape[-1]
    return pl.pallas_call(
        functools.partial(_moe_down_kernel, tn=tn),
        out_shape=jax.ShapeDtypeStruct((rows, d), F32),
        grid_spec=pltpu.PrefetchScalarGridSpec(
            num_scalar_prefetch=len(sched),
            grid=(d // tn, rows // tm),
            in_specs=[pl.BlockSpec((tm, ff), lambda j, i, *_: (i, 0)),
                      pl.BlockSpec(memory_space=pl.ANY)],
            out_specs=pl.BlockSpec((tm, tn), lambda j, i, *_: (i, j)),
            scratch_shapes=[pltpu.VMEM((2, ff, tn), F32), pltpu.VMEM((ff, tn), BF16),
                            pltpu.SemaphoreType.DMA((2, 1))],
        ),
        compiler_params=_cparams("arbitrary", "arbitrary"),
        name="moe_down",
    )(*sched, h, wd)


def _combine_ln_kernel(pos_ref, y_hbm, x_ref, wts_ref, g_ref, b_ref, o_ref, buf, sem, *, tm, alpha):
    i = pl.program_id(0)

    def issue(tile, slot):
        base = tile * tm

        def body(r, carry):
            for k in range(TOP_K):
                row = pos_ref[(base + r) * TOP_K + k]
                pltpu.make_async_copy(y_hbm.at[pl.ds(row, 1), :], buf.at[slot, k, pl.ds(r, 1), :],
                                      sem.at[slot]).start()
            return carry

        lax.fori_loop(0, tm, body, 0, unroll=8)

    @pl.when(i == 0)
    def _():
        issue(0, 0)

    @pl.when(i + 1 < pl.num_programs(0))
    def _():
        issue(i + 1, (i + 1) % 2)

    slot = i % 2
    for k in range(TOP_K):
        pltpu.make_async_copy(y_hbm.at[pl.ds(0, tm), :], buf.at[slot, k], sem.at[slot]).wait()
    wts = wts_ref[...]
    y = buf[slot, 0] * wts[:, 0:1] + buf[slot, 1] * wts[:, 1:2]
    o_ref[...] = _layer_norm(alpha * x_ref[...] + y, g_ref[...], b_ref[...])


def _combine_ln(yrows, pos, x, wts, g, b, alpha, *, tm=128):
    n, d = x.shape
    tm = min(tm, n)
    return pl.pallas_call(
        functools.partial(_combine_ln_kernel, tm=tm, alpha=alpha),
        out_shape=jax.ShapeDtypeStruct((n, d), F32),
        grid_spec=pltpu.PrefetchScalarGridSpec(
            num_scalar_prefetch=1,
            grid=(n // tm,),
            in_specs=[pl.BlockSpec(memory_space=pl.ANY),
                      pl.BlockSpec((tm, d), lambda i, pos: (i, 0)),
                      pl.BlockSpec((tm, LANES), lambda i, pos: (i, 0)),
                      pl.BlockSpec((1, d), lambda i, pos: (0, 0)),
                      pl.BlockSpec((1, d), lambda i, pos: (0, 0))],
            out_specs=pl.BlockSpec((tm, d), lambda i, pos: (i, 0)),
            scratch_shapes=[pltpu.VMEM((2, TOP_K, tm, d), F32), pltpu.SemaphoreType.DMA((2,))],
        ),
        compiler_params=_cparams("arbitrary"),
        name="moe_combine_ln",
    )(pos, yrows, x, wts, g.reshape(1, d), b.reshape(1, d))


def _moe_block(x, ids, wts, wg, wu, wd, g, b, alpha, *, tm=256):
    n, d = x.shape
    e_flat = ids[:, :TOP_K].reshape(-1)
    n_assign = n * TOP_K
    onehot = (e_flat[:, None] == jnp.arange(N_EXPERTS, dtype=I32)[None, :]).astype(I32)
    rank = jnp.sum((jnp.cumsum(onehot, axis=0) - onehot) * onehot, axis=1)
    counts = jnp.sum(onehot, axis=0)
    padded = (counts + tm - 1) // tm * tm
    end_padded = jnp.cumsum(padded)
    start_padded = end_padded - padded
    dest = (start_padded[e_flat] + rank).astype(I32)
    rows = n_assign + N_EXPERTS * tm
    n_tiles = rows // tm
    flat_tok = jnp.arange(n_assign, dtype=I32) // TOP_K
    row_tok = jnp.zeros((rows,), I32).at[dest].set(flat_tok)
    tile_start = jnp.arange(n_tiles, dtype=I32) * tm
    tile_e = jnp.minimum(jnp.sum((tile_start[:, None] >= end_padded[None, :]).astype(I32), axis=1),
                         N_EXPERTS - 1).astype(I32)
    n_valid = (end_padded[-1:] // tm).astype(I32)
    tile_ix = jnp.arange(n_tiles, dtype=I32)
    tile_e = jnp.where(tile_ix < n_valid[0], tile_e, tile_e[jnp.maximum(n_valid[0] - 1, 0)])
    is_start = jnp.concatenate([jnp.ones((1,), I32), (tile_e[1:] != tile_e[:-1]).astype(I32)])
    run_id = (jnp.cumsum(is_start) - 1).astype(I32)
    larger = jnp.where(tile_e[None, :] > tile_e[:, None], tile_e[None, :], N_EXPERTS)
    nxt_e = jnp.min(larger, axis=1)
    nxt_e = jnp.where(nxt_e < N_EXPERTS, nxt_e, -1).astype(I32)
    meta = jnp.stack([run_id[-1] + 1, tile_e[0]]).astype(I32)
    sched = (tile_e, n_valid, run_id, nxt_e, meta)
    xs = _gather_rows(x, row_tok, tm=tm)
    h = _moe_up(xs, wg, wu, sched, tm=tm)
    yrows = _moe_down(h, wd, sched, tm=tm)
    return _combine_ln(yrows, dest, x, wts, g, b, alpha)


def _w_in_columns(d_model):
    attn_w = N_HEADS * HEAD_DIM
    kv_w = N_KV * HEAD_DIM
    o = np.cumsum([0, attn_w, kv_w, kv_w, IDX_HEADS * IDX_DIM, IDX_DIM, IDX_HEADS, d_model, d_model, d_model])
    o = [int(v) for v in o]
    return {"q": (o[0], attn_w), "k": (o[1], kv_w), "v": (o[2], kv_w), "qi": (o[3], o[4] - o[3]),
            "kiwi": (o[4], LANES),
            "px": (o[6], 2 * d_model), "gr": (o[8], d_model), "gates": (o[9], 3 * d_model)}


def _mixer(x_mm, tabs, wt_in, layer, pool_w, pool_scale, conv_w, conv_b, wa, ba, wx, bx, lam,
           batch, seq, d_model, tq):
    cos_a, sin_a, cos_i, sin_ia, sin_ib, cos_q, sin_q, cos_iq = tabs
    cols = _w_in_columns(d_model)
    rope_a = ((cos_a, sin_a), (HEAD_DIM // 2,))
    rope_i = (LANES - IDX_DIM // 2, IDX_DIM // 2)
    qh = _proj_heads(x_mm, wt_in, layer, cols["q"], (cos_q, sin_q), rope_a[1], BF16)
    kh = _proj_heads(x_mm, wt_in, layer, cols["k"], *rope_a, BF16)
    vh = _proj_heads(x_mm, wt_in, layer, cols["v"], (), (), BF16)
    qih = _proj_heads(x_mm, wt_in, layer, cols["qi"], (cos_iq, sin_ia, sin_ib), rope_i, BF16, head_width=IDX_DIM)
    kiwi = _proj_heads(x_mm, wt_in, layer, cols["kiwi"], (cos_i, sin_ia, sin_ib), rope_i, F32)
    px = _proj_plain(x_mm, wt_in, layer, cols["px"], None, F32)
    gg = _proj_plain(x_mm, wt_in, layer, cols["gr"], "gelu", BF16)
    sg = _proj_plain(x_mm, wt_in, layer, cols["gates"], "sigmoid", BF16)
    ya = _attention(qh, kh, vh, qih, kiwi, sg, batch, seq, tq=tq)
    yb = _pool_mixer(px, sg, pool_w.astype(BF16), pool_scale, batch, seq)
    yc = _lru_mixer(px, gg, sg, conv_w, conv_b, wa.astype(BF16), ba, wx.astype(BF16), bx, lam, batch, seq)
    return ya, yb, yc


def kernel(x, positions, mix_w_in, mix_w_out, pool_w, pool_scale, conv_w, conv_b, lru_wa, lru_ba, lru_wx, lru_bx, lru_lam, ln_mix_g, ln_mix_b, ln_ffn_g, ln_ffn_b, dense_w_gate, dense_w_up, dense_w_down, moe_router, moe_w_gate, moe_w_up, moe_w_down):
    batch, seq, d_model = x.shape
    depth = mix_w_in.shape[0]
    alpha = np.float32((2 * depth) ** 0.25)
    n = batch * seq
    tabs = _rope_tables(positions)
    xf = x.reshape(n, d_model)
    x_mm = xf
    wt_in = jnp.swapaxes(mix_w_in, 1, 2)
    for layer in range(depth):
        ya, yb, yc = _mixer(x_mm, tabs, wt_in, layer, pool_w[layer], pool_scale[layer],
                            conv_w[layer], conv_b[layer], lru_wa[layer], lru_ba[layer], lru_wx[layer],
                            lru_bx[layer], lru_lam[layer], batch, seq, d_model, tq=256)
        w_out3 = mix_w_out[layer].astype(BF16).reshape(3, -1, d_model)
        j = layer // 2
        if layer % 2 == 0:
            xf, xb = _out_proj_ln(ya, yb, yc, w_out3, xf, ln_mix_g[layer], ln_mix_b[layer], alpha)
            wd = dense_w_down[j].astype(BF16)
            xf, x_mm = _ffn_dense(xb, xf, dense_w_gate, dense_w_up, j, wd,
                                  ln_ffn_g[layer], ln_ffn_b[layer], alpha)
        else:
            xf, _, ids, wts = _out_proj_ln(ya, yb, yc, w_out3, xf, ln_mix_g[layer], ln_mix_b[layer], alpha,
                                           w_router=moe_router[j])
            xf = _moe_block(xf, ids, wts, moe_w_gate[j], moe_w_up[j], moe_w_down[j],
                            ln_ffn_g[layer], ln_ffn_b[layer], alpha)
            x_mm = xf
    return xf.reshape(batch, seq, d_model)
```

```python
import functools

import jax
import jax.numpy as jnp
import numpy as np
from jax import lax
from jax.experimental import pallas as pl
from jax.experimental.pallas import tpu as pltpu

F32 = jnp.float32
BF16 = jnp.bfloat16
I32 = jnp.int32

LANES = 128
SUBLANES = 8
VMEM_LIMIT = 56 * 1024 * 1024

CHUNK = 64
N_HEADS = 16
HEAD_DIM = 128
N_KV = 4
HEADS_PER_KV = N_HEADS // N_KV
IDX_HEADS = 16
IDX_DIM = 64
TOPK_MAX = 256
ROPE_THETA = 10000.0
POOL_WINDOWS = (2, 4, 8, 16)
LRU_BLOCK = 128
CONV_WIDTH = 4
LRU_C = 8.0
N_EXPERTS = 8
TOP_K = 2
LN_EPS = 1e-5
LN_ROWS = 128
INT_MIN = -2 ** 31
NEG_BIG = -1e30


def _cparams(*sem):
    return pltpu.CompilerParams(dimension_semantics=sem, vmem_limit_bytes=VMEM_LIMIT)


def _sigmoid(x):
    return 0.5 * (1.0 + jnp.tanh(0.5 * x))


def _gelu_tanh(x):
    c = np.float32(np.sqrt(2.0 / np.pi))
    return 0.5 * x * (1.0 + jnp.tanh(c * (x + np.float32(0.044715) * (x * x * x))))


def _layer_norm(y, g, b):
    mu = jnp.mean(y, axis=-1, keepdims=True)
    d = y - mu
    var = jnp.mean(d * d, axis=-1, keepdims=True)
    return d * lax.rsqrt(var + LN_EPS) * g + b


def _rope_tab_kernel(pos_ref, inv_a_ref, inv_i_ref, sgn_a_ref, m_cos_ref, add_cos_ref,
                     m_sa_ref, m_sb_ref, cos_a_ref, sin_a_ref, cos_i_ref, sin_ia_ref, sin_ib_ref,
                     cos_q_ref, sin_q_ref, cos_iq_ref):
    pos = pos_ref[...]
    ang_a = pos * inv_a_ref[...]
    cos_a = jnp.cos(ang_a)
    sin_a = jnp.sin(ang_a) * sgn_a_ref[...]
    cos_a_ref[...] = cos_a
    sin_a_ref[...] = sin_a
    q_scale = np.float32(HEAD_DIM ** -0.5 * np.log2(np.e))
    cos_q_ref[...] = cos_a * q_scale
    sin_q_ref[...] = sin_a * q_scale
    ang_i = pos * inv_i_ref[...]
    s_i = jnp.sin(ang_i)
    cos_iq = jnp.cos(ang_i) * m_cos_ref[...]
    cos_iq_ref[...] = cos_iq
    cos_i_ref[...] = cos_iq + add_cos_ref[...]
    sin_ia_ref[...] = s_i * m_sa_ref[...]
    sin_ib_ref[...] = s_i * m_sb_ref[...]


def _rope_tables(positions):
    n = positions.size
    pos = jnp.broadcast_to(positions.reshape(n, 1).astype(F32), (n, LANES))
    lane = np.arange(LANES)
    inv_a = (ROPE_THETA ** (-jnp.arange(0, HEAD_DIM, 2, dtype=F32) / HEAD_DIM))
    inv_i = (ROPE_THETA ** (-jnp.arange(0, IDX_DIM, 2, dtype=F32) / IDX_DIM))
    inv_a_row = jnp.concatenate([inv_a, inv_a])[None, :]
    inv_i_row = jnp.concatenate([inv_i, inv_i, jnp.zeros((LANES - IDX_DIM,), F32)])[None, :]
    sgn_a = jnp.asarray(np.where(lane < HEAD_DIM // 2, -1.0, 1.0), F32)[None, :]
    m_cos = jnp.asarray((lane < IDX_DIM).astype(np.float32))[None, :]
    wi_scale = (IDX_HEADS ** -0.5) * (IDX_DIM ** -0.5)
    add_cos = jnp.asarray(np.where((lane >= IDX_DIM) & (lane < IDX_DIM + IDX_HEADS), wi_scale, 0.0), F32)[None, :]
    m_sa = jnp.asarray(np.where(lane < IDX_DIM // 2, -1.0, 0.0), F32)[None, :]
    m_sb = jnp.asarray(np.where((lane >= IDX_DIM // 2) & (lane < IDX_DIM), 1.0, 0.0), F32)[None, :]
    tm = min(n, 1024)
    row = pl.BlockSpec((1, LANES), lambda i: (0, 0))
    tok = pl.BlockSpec((tm, LANES), lambda i: (i, 0))
    out = jax.ShapeDtypeStruct((n, LANES), F32)
    return pl.pallas_call(
        _rope_tab_kernel,
        out_shape=(out,) * 8,
        grid=(n // tm,),
        in_specs=[tok] + [row] * 7,
        out_specs=(tok,) * 8,
        compiler_params=_cparams("parallel"),
        name="rope_tables",
    )(pos, inv_a_row, inv_i_row, sgn_a, m_cos, add_cos, m_sa, m_sb)


def _stream_weight_tile(wt_hbm, stage, w_s, sem, *, layer, off, tn):
    j = pl.program_id(0)

    def copy(jj, slot):
        rows = pl.ds(pl.multiple_of(off + jj * tn, SUBLANES), tn)
        return pltpu.make_async_copy(wt_hbm.at[layer, rows, :], stage.at[slot], sem.at[slot])

    @pl.when(pl.program_id(1) == 0)
    def _():
        slot = j % 2

        @pl.when(j == 0)
        def _():
            copy(0, 0).start()

        @pl.when(j + 1 < pl.num_programs(0))
        def _():
            copy(j + 1, 1 - slot).start()

        copy(j, slot).wait()
        w_s[...] = stage[slot].astype(BF16)


def _dot_nt(x, w):
    return lax.dot_general(x, w, (((1,), (1,)), ((), ())), preferred_element_type=F32)


def _proj_heads_kernel(x_ref, wt_hbm, *rest, layer, off, tn, shifts, heads, head_width):
    tabs, o_ref, stage, w_s, sem = rest[:-4], rest[-4], rest[-3], rest[-2], rest[-1]
    _stream_weight_tile(wt_hbm, stage, w_s, sem, layer=layer, off=off, tn=tn)
    acc = _dot_nt(x_ref[...].astype(BF16), w_s[...])
    per_slab = LANES // head_width
    for h in range(heads):
        xh = acc[:, (h // per_slab) * LANES:(h // per_slab + 1) * LANES]
        if h % per_slab:
            xh = pltpu.roll(xh, LANES - (h % per_slab) * head_width, 1)
        if tabs:
            y = xh * tabs[0][...]
            for s, t in zip(shifts, tabs[1:]):
                y = y + pltpu.roll(xh, s, 1) * t[...]
        else:
            y = xh
        o_ref[h] = y.astype(o_ref.dtype)


def _weight_stream_scratch(tn, d):
    return [pltpu.VMEM((2, tn, d), F32), pltpu.VMEM((tn, d), BF16), pltpu.SemaphoreType.DMA((2,))]


def _proj_heads(x, wt, layer, cols, tabs, shifts, out_dtype, *, head_width=LANES, tm=1024, tile_cols=1024):
    n, d = x.shape
    off, width = cols
    tn = min(tile_cols, width)
    hp = tn // head_width
    assert width % tn == 0 and off % SUBLANES == 0 and (head_width == LANES or tabs)
    tm = min(tm, n)
    tab_spec = pl.BlockSpec((tm, LANES), lambda j, i: (i, 0))
    return pl.pallas_call(
        functools.partial(_proj_heads_kernel, layer=layer, off=off, tn=tn, shifts=shifts, heads=hp,
                          head_width=head_width),
        out_shape=jax.ShapeDtypeStruct((width // head_width, n, LANES), out_dtype),
        grid=(width // tn, n // tm),
        in_specs=[pl.BlockSpec((tm, d), lambda j, i: (i, 0)),
                  pl.BlockSpec(memory_space=pl.ANY)] + [tab_spec] * len(tabs),
        out_specs=pl.BlockSpec((hp, tm, LANES), lambda j, i: (j, i, 0)),
        scratch_shapes=_weight_stream_scratch(tn, d),
        compiler_params=_cparams("arbitrary", "arbitrary"),
        name="proj_heads",
    )(x, wt, *tabs)


def _proj_plain_kernel(x_ref, wt_hbm, o_ref, stage, w_s, sem, *, layer, off, tn, act):
    _stream_weight_tile(wt_hbm, stage, w_s, sem, layer=layer, off=off, tn=tn)
    acc = _dot_nt(x_ref[...].astype(BF16), w_s[...])
    if act == "gelu":
        acc = _gelu_tanh(acc)
    elif act == "sigmoid":
        acc = _sigmoid(acc)
    o_ref[...] = acc.astype(o_ref.dtype)


def _proj_plain(x, wt, layer, cols, act, out_dtype, *, tm=1024, tn=1024):
    n, d = x.shape
    off, width = cols
    assert width % tn == 0 and off % SUBLANES == 0
    tm = min(tm, n)
    return pl.pallas_call(
        functools.partial(_proj_plain_kernel, layer=layer, off=off, tn=tn, act=act),
        out_shape=jax.ShapeDtypeStruct((n, width), out_dtype),
        grid=(width // tn, n // tm),
        in_specs=[pl.BlockSpec((tm, d), lambda j, i: (i, 0)),
                  pl.BlockSpec(memory_space=pl.ANY)],
        out_specs=pl.BlockSpec((tm, tn), lambda j, i: (i, j)),
        scratch_shapes=_weight_stream_scratch(tn, d),
        compiler_params=_cparams("arbitrary", "arbitrary"),
        name="proj_plain",
    )(x, wt)


def _key_to_float(key):
    return pltpu.bitcast(key ^ ((key >> 31) & 0x7FFFFFFF), F32)


def _attn_kernel(q_ref, k_ref, v_ref, qi_ref, ki_ref, wi_ref, gate_ref, o_ref,
                 wb_s, key_s, keyt_s, x_s, thr_s, xrow_s, *state, tq, topk, seq):
    m_s, acc_s = state[:N_KV], state[N_KV:]
    kb = tq
    n_sub = kb // LANES
    qt = pl.program_id(1)
    nkb = qt + 1
    n_hi = IDX_HEADS
    rows_g = HEADS_PER_KV * tq

    wi = wi_ref[...]
    for h in range(n_hi):
        wb_s[h] = jnp.broadcast_to(wi[:, IDX_DIM + h:IDX_DIM + h + 1], (tq, LANES))
    qi = qi_ref[...].reshape(n_hi * tq, LANES)
    q_row = qt * tq + lax.broadcasted_iota(I32, (tq, LANES), 0)
    limit = (q_row // CHUNK + 1) * CHUNK
    lane_pos = lax.broadcasted_iota(I32, (tq, LANES), 1)
    limit_t = ((qt * tq + lax.broadcasted_iota(I32, (1, tq), 1)) // CHUNK + 1) * CHUNK

    def score_body(j, carry):
        start = pl.multiple_of(j * kb, kb)
        ki_blk = ki_ref[pl.ds(start, kb), :].astype(BF16)
        s = lax.dot_general(qi, ki_blk, (((1,), (1,)), ((), ())), preferred_element_type=F32)
        parts = []
        for c in range(n_sub):
            sc = jnp.zeros((tq, LANES), F32)
            for h in range(n_hi):
                sh = s[h * tq:(h + 1) * tq, c * LANES:(c + 1) * LANES]
                sc = sc + wb_s[h] * jnp.maximum(sh, 0.0)
            sc = jnp.where(sc == 0.0, 0.0, sc)
            parts.append(sc)
            kpos = start + c * LANES + lane_pos
            key_s[j, :, c * LANES:(c + 1) * LANES] = jnp.where(kpos < limit, sc, -jnp.inf)
        sc_t = jnp.concatenate(parts, axis=1).T
        kpos_t = start + lax.broadcasted_iota(I32, (kb, tq), 0)
        keyt_s[j] = jnp.where(kpos_t < limit_t, sc_t, -jnp.inf)
        return carry

    lax.fori_loop(0, nkb, score_body, 0)

    kf = np.float32(topk)
    searched = limit_t > topk

    def count_keys(pred):
        def body(j, cnt):
            hit = jnp.where(pred(keyt_s[j], j), 1.0, 0.0)
            return cnt + jnp.sum(hit.reshape(kb // SUBLANES, SUBLANES, tq), axis=0)
        cnt = lax.fori_loop(0, nkb, body, jnp.zeros((SUBLANES, tq), F32))
        return jnp.sum(cnt, axis=0, keepdims=True)

    thr0 = jnp.where(count_keys(lambda s, j: s >= 0.0) >= kf, 0, INT_MIN).astype(I32)

    def thr_body(it, thr):
        cand = thr | jnp.left_shift(jnp.int32(1), 30 - it)
        cand_f = _key_to_float(cand)
        return jnp.where(count_keys(lambda s, j: s >= cand_f) >= kf, cand, thr)

    thr = _key_to_float(lax.fori_loop(0, 31, thr_body, thr0))
    thr = jnp.where(searched, thr, -jnp.inf)
    n_ge = count_keys(lambda s, j: s >= thr)
    need = kf - count_keys(lambda s, j: s > thr)
    xrow_s[...] = jnp.where(searched, seq, -1).astype(I32)
    tie_flag = jnp.max(jnp.where(searched & (n_ge > kf), 1.0, 0.0), axis=(0, 1), keepdims=True)

    @pl.when(tie_flag[0, 0] > 0.0)
    def _():
        nbits = max(int(seq - 1).bit_length(), 1)
        row_pos = lax.broadcasted_iota(I32, (kb, tq), 0)

        def x_body(it, xcut):
            cand = xcut | jnp.left_shift(jnp.int32(1), nbits - 1 - it)
            cnt = count_keys(lambda s, j: (s == thr) & (j * kb + row_pos < cand))
            return jnp.where(cnt < need, cand, xcut)

        xcut = lax.fori_loop(0, nbits, x_body, jnp.zeros((1, tq), I32))
        xrow_s[...] = jnp.where(searched, xcut, -1)

    def to_col(row_f32):
        return jnp.broadcast_to(row_f32, (LANES, tq)).T

    thr_s[...] = to_col(thr)
    x_s[...] = to_col(xrow_s[...].astype(F32)).astype(I32)
    rb = min(tq, 128)
    n_rc = tq // rb
    lane_rb = lax.broadcasted_iota(I32, (rb, LANES), 1)

    for g in range(N_KV):
        m_s[g][...] = jnp.full((rows_g, LANES), NEG_BIG, F32)
        acc_s[g][...] = jnp.zeros((rows_g, 2 * LANES), F32)
    ones_v = jnp.ones((kb, LANES), BF16)

    def attn_body(j, carry):
        start = pl.multiple_of(j * kb, kb)
        bias = []
        for rc in range(n_rc):
            thr_c = thr_s[rc * rb:(rc + 1) * rb, :]
            xcut_c = x_s[rc * rb:(rc + 1) * rb, :]
            parts = []
            for c in range(n_sub):
                kc = key_s[j, rc * rb:(rc + 1) * rb, c * LANES:(c + 1) * LANES]
                kpos = start + c * LANES + lane_rb
                sel = (kc > thr_c) | ((kc == thr_c) & (kpos <= xcut_c))
                parts.append(jnp.where(sel, 0.0, NEG_BIG))
            bias.append(jnp.concatenate(parts, axis=1))
        for g in range(N_KV):
            qg = q_ref[g * HEADS_PER_KV:(g + 1) * HEADS_PER_KV].reshape(rows_g, LANES)
            kg = k_ref[g, pl.ds(start, kb), :]
            vg = jnp.concatenate([v_ref[g, pl.ds(start, kb), :], ones_v], axis=1)
            lg_all = lax.dot_general(qg, kg, (((1,), (1,)), ((), ())), preferred_element_type=F32)
            m_prev_all = m_s[g][...]
            p_parts, a_parts, m_parts = [], [], []
            for r in range(HEADS_PER_KV):
                for rc in range(n_rc):
                    r0 = r * tq + rc * rb
                    lg = lg_all[r0:r0 + rb] + bias[rc]
                    m_prev = m_prev_all[r0:r0 + rb]
                    m_new = jnp.maximum(m_prev, jnp.max(lg, axis=1, keepdims=True))
                    p = jnp.exp2(lg - jnp.concatenate([m_new] * n_sub, axis=1))
                    p_parts.append(p.astype(BF16))
                    a_parts.append(jnp.exp2(m_prev - m_new))
                    m_parts.append(m_new)
            p_all = jnp.concatenate(p_parts, axis=0)
            alpha_all = jnp.concatenate(a_parts, axis=0)
            pv = jnp.dot(p_all, vg, preferred_element_type=F32)
            acc_s[g][...] = jnp.concatenate([alpha_all, alpha_all], axis=1) * acc_s[g][...] + pv
            m_s[g][...] = jnp.concatenate(m_parts, axis=0)
        return carry

    lax.fori_loop(0, nkb, attn_body, 0)

    for g in range(N_KV):
        acc = acc_s[g][...]
        og = acc[:, :LANES] / acc[:, LANES:]
        for r in range(HEADS_PER_KV):
            col = (g * HEADS_PER_KV + r) * LANES
            y = og[r * tq:(r + 1) * tq] * gate_ref[:, col:col + LANES].astype(F32)
            o_ref[:, col:col + LANES] = y.astype(o_ref.dtype)


def _attention(qh, kh, vh, qih, kiwi, sg, batch, seq, *, tq):
    n = batch * seq
    topk = min(TOPK_MAX, seq // 4)
    tq = min(tq, seq)
    nqt = seq // tq
    attn_w = N_HEADS * HEAD_DIM
    rows_g = HEADS_PER_KV * tq
    return pl.pallas_call(
        functools.partial(_attn_kernel, tq=tq, topk=topk, seq=seq),
        out_shape=jax.ShapeDtypeStruct((n, attn_w), BF16),
        grid=(batch, nqt),
        in_specs=[
            pl.BlockSpec((N_HEADS, tq, LANES), lambda b, t: (0, b * nqt + t, 0)),
            pl.BlockSpec((N_KV, seq, LANES), lambda b, t: (0, b, 0)),
            pl.BlockSpec((N_KV, seq, LANES), lambda b, t: (0, b, 0)),
            pl.BlockSpec((IDX_HEADS, tq, LANES), lambda b, t: (0, b * nqt + t, 0)),
            pl.BlockSpec((None, seq, LANES), lambda b, t: (0, b, 0)),
            pl.BlockSpec((None, tq, LANES), lambda b, t: (0, b * nqt + t, 0)),
            pl.BlockSpec((tq, attn_w), lambda b, t: (b * nqt + t, 0)),
        ],
        out_specs=pl.BlockSpec((tq, attn_w), lambda b, t: (b * nqt + t, 0)),
        scratch_shapes=[
            pltpu.VMEM((IDX_HEADS, tq, LANES), F32),
            pltpu.VMEM((nqt, tq, tq), F32),
            pltpu.VMEM((nqt, tq, tq), F32),
            pltpu.VMEM((tq, LANES), I32),
            pltpu.VMEM((tq, LANES), F32),
            pltpu.VMEM((1, tq), I32),
        ] + [pltpu.VMEM((rows_g, LANES), F32)] * N_KV + [pltpu.VMEM((rows_g, 2 * LANES), F32)] * N_KV,
        compiler_params=_cparams("parallel", "arbitrary"),
        name="sparse_attention",
    )(qh, kh, vh, qih, kiwi, kiwi, sg)


def _pool_kernel(p_ref, gate_ref, w_ref, scale_ref, o_ref, buf_a, buf_b, *, seq):
    pad = 16
    g = pl.program_id(1)
    p = p_ref[...]
    zeros = jnp.zeros((pad, p.shape[1]), F32)
    buf_a[0:pad, :] = zeros
    buf_b[0:pad, :] = zeros
    buf_a[pad:pad + seq, :] = p
    s2 = p + buf_a[pad - 1:pad - 1 + seq, :]
    buf_b[pad:pad + seq, :] = s2
    s4 = s2 + buf_b[pad - 2:pad - 2 + seq, :]
    buf_a[pad:pad + seq, :] = s4
    s8 = s4 + buf_a[pad - 4:pad - 4 + seq, :]
    buf_b[pad:pad + seq, :] = s8
    s16 = s8 + buf_b[pad - 8:pad - 8 + seq, :]
    t1 = (lax.broadcasted_iota(I32, p.shape, 0) + 1).astype(F32)
    win = jnp.where(g == 0, 2.0, jnp.where(g == 1, 4.0, jnp.where(g == 2, 8.0, 16.0))).astype(F32)
    total = jnp.where(g == 0, s2, jnp.where(g == 1, s4, jnp.where(g == 2, s8, s16)))
    mean = total / jnp.minimum(t1, win)
    diff = (mean - p).astype(BF16)
    y = jnp.dot(diff, w_ref[...], preferred_element_type=F32)
    o_ref[...] = (y * scale_ref[...] * gate_ref[...].astype(F32)).astype(o_ref.dtype)


def _pool_mixer(px, sg, pool_w, pool_scale, batch, seq):
    n = batch * seq
    width = pool_scale.shape[-1]
    ng = len(POOL_WINDOWS)
    cg = width // ng
    assert POOL_WINDOWS == (2, 4, 8, 16)
    return pl.pallas_call(
        functools.partial(_pool_kernel, seq=seq),
        out_shape=jax.ShapeDtypeStruct((n, width), BF16),
        grid=(batch, ng),
        in_specs=[
            pl.BlockSpec((seq, cg), lambda b, g: (b, g)),
            pl.BlockSpec((seq, cg), lambda b, g: (b, ng + g)),
            pl.BlockSpec((None, cg, cg), lambda b, g: (g, 0, 0)),
            pl.BlockSpec((1, cg), lambda b, g: (0, g)),
        ],
        out_specs=pl.BlockSpec((seq, cg), lambda b, g: (b, g)),
        scratch_shapes=[pltpu.VMEM((seq + 16, cg), F32), pltpu.VMEM((seq + 16, cg), F32)],
        compiler_params=_cparams("parallel", "arbitrary"),
        name="pool_mixer",
    )(px, sg, pool_w, pool_scale.reshape(1, width))


def _lru_kernel(x_ref, gr_ref, gate_ref, cw_ref, cb_ref, wa_ref, ba_ref, wx_ref, bx_ref, lam_ref,
                o_ref, a_s, b_s, *, seq, ct):
    x = x_ref[...]
    row = lax.broadcasted_iota(I32, (seq, ct), 0)
    xc = jnp.broadcast_to(cb_ref[...], (seq, ct))
    for tap in range(CONV_WIDTH):
        d = CONV_WIDTH - 1 - tap
        x_d = x if d == 0 else jnp.where(row >= d, pltpu.roll(x, d, 0), 0.0)
        xc = xc + x_d * cw_ref[tap:tap + 1, :]
    xcb = xc.astype(BF16)
    nb = ct // LRU_BLOCK
    r_parts, i_parts = [], []
    for blk in range(nb):
        xb = xcb[:, blk * LRU_BLOCK:(blk + 1) * LRU_BLOCK]
        r_parts.append(jnp.dot(xb, wa_ref[blk], preferred_element_type=F32))
        i_parts.append(jnp.dot(xb, wx_ref[blk], preferred_element_type=F32))
    r = _sigmoid(jnp.concatenate(r_parts, axis=1) + ba_ref[...])
    gi = _sigmoid(jnp.concatenate(i_parts, axis=1) + bx_ref[...])
    lam = lam_ref[...]
    softplus_neg_lam = jnp.log(1.0 + jnp.exp(-lam))
    log_a = -LRU_C * r * softplus_neg_lam
    a = jnp.exp(log_a)
    one_m_a2 = 1.0 - a * a
    root = jnp.where(one_m_a2 > 0.0, one_m_a2 * lax.rsqrt(one_m_a2), 0.0)
    b = root * (gi * xc)

    tiles = (seq // SUBLANES, SUBLANES, ct)
    a = a.reshape(tiles)
    b = b.reshape(tiles)
    sub = lax.broadcasted_iota(I32, tiles, 1)
    for d in (1, 2, 4):
        keep = sub >= d
        a_sh = jnp.where(keep, pltpu.roll(a, d, 1), 1.0)
        b_sh = jnp.where(keep, pltpu.roll(b, d, 1), 0.0)
        b = a * b_sh + b
        a = a * a_sh
    a_s[...] = a.reshape(seq, ct)
    b_s[...] = b.reshape(seq, ct)

    def body(t, carry):
        r0 = pl.multiple_of(t * SUBLANES, SUBLANES)
        h = a_s[pl.ds(r0, SUBLANES), :] * carry + b_s[pl.ds(r0, SUBLANES), :]
        b_s[pl.ds(r0, SUBLANES), :] = h
        return jnp.broadcast_to(h[SUBLANES - 1:SUBLANES, :], (SUBLANES, ct))

    lax.fori_loop(0, seq // SUBLANES, body, jnp.zeros((SUBLANES, ct), F32), unroll=8)
    h = b_s[...]
    o_ref[...] = (h * gr_ref[...].astype(F32) * gate_ref[...].astype(F32)).astype(o_ref.dtype)


def _lru_mixer(px, gg, sg, conv_w, conv_b, wa, ba, wx, bx, lam, batch, seq, *, ct=256):
    n = batch * seq
    width = conv_b.shape[-1]
    nct = width // ct
    nb = ct // LRU_BLOCK
    row = lambda a: a.reshape(1, width)
    rspec = pl.BlockSpec((1, ct), lambda b, j: (0, j))
    return pl.pallas_call(
        functools.partial(_lru_kernel, seq=seq, ct=ct),
        out_shape=jax.ShapeDtypeStruct((n, width), BF16),
        grid=(batch, nct),
        in_specs=[
            pl.BlockSpec((seq, ct), lambda b, j: (b, nct + j)),
            pl.BlockSpec((seq, ct), lambda b, j: (b, j)),
            pl.BlockSpec((seq, ct), lambda b, j: (b, 2 * nct + j)),
            pl.BlockSpec((CONV_WIDTH, ct), lambda b, j: (0, j)),
            rspec,
            pl.BlockSpec((nb, LRU_BLOCK, LRU_BLOCK), lambda b, j: (j, 0, 0)),
            rspec,
            pl.BlockSpec((nb, LRU_BLOCK, LRU_BLOCK), lambda b, j: (j, 0, 0)),
            rspec,
            rspec,
        ],
        out_specs=pl.BlockSpec((seq, ct), lambda b, j: (b, j)),
        scratch_shapes=[pltpu.VMEM((seq, ct), F32),
                        pltpu.VMEM((seq, ct), F32)],
        compiler_params=_cparams("parallel", "arbitrary"),
        name="rglru_mixer",
    )(px, gg, sg, conv_w, row(conv_b), wa, row(ba), wx, row(bx), row(lam))


def _top2_route(x, w_ref, ids_ref, wts_ref):
    logits = jnp.dot(x, w_ref[...], preferred_element_type=F32, precision=lax.Precision.HIGHEST)
    lane_i = lax.broadcasted_iota(I32, logits.shape, 1)
    lane = lane_i.astype(F32)
    logits = jnp.where(lane_i < N_EXPERTS, logits, -jnp.inf)
    m1 = jnp.max(logits, axis=1, keepdims=True)
    i1 = jnp.min(jnp.where(logits == m1, lane, float(LANES)), axis=1, keepdims=True)
    rest = jnp.where(lane == i1, -jnp.inf, logits)
    m2 = jnp.max(rest, axis=1, keepdims=True)
    i2 = jnp.min(jnp.where(rest == m2, lane, float(LANES)), axis=1, keepdims=True)
    e2 = jnp.exp(m2 - m1)
    w1 = 1.0 / (1.0 + e2)
    w2 = e2 / (1.0 + e2)
    ids_ref[...] = jnp.where(lane_i == 0, i1, jnp.where(lane_i == 1, i2, 0.0)).astype(I32)
    wts_ref[...] = jnp.where(lane_i == 0, w1, jnp.where(lane_i == 1, w2, 0.0))


def _router_kernel(x_ref, w_ref, ids_ref, wts_ref):
    _top2_route(x_ref[...], w_ref, ids_ref, wts_ref)


def _router(x, w_router, *, tm=512):
    n, d = x.shape
    wpad = jnp.zeros((d, LANES), F32).at[:, :N_EXPERTS].set(w_router.astype(F32))
    tm = min(tm, n)
    tok = pl.BlockSpec((tm, LANES), lambda i: (i, 0))
    return pl.pallas_call(
        _router_kernel,
        out_shape=(jax.ShapeDtypeStruct((n, LANES), I32), jax.ShapeDtypeStruct((n, LANES), F32)),
        grid=(n // tm,),
        in_specs=[pl.BlockSpec((tm, d), lambda i: (i, 0)), pl.BlockSpec((d, LANES), lambda i: (0, 0))],
        out_specs=(tok, tok),
        compiler_params=_cparams("parallel"),
        name="moe_router",
    )(x, wpad)


def _out_proj_ln_kernel(ya_ref, yb_ref, yc_ref, w_ref, x_ref, g_ref, b_ref, o_ref, ob_ref, *, alpha):
    acc = jnp.dot(ya_ref[...], w_ref[0], preferred_element_type=F32)
    acc = acc + jnp.dot(yb_ref[...], w_ref[1], preferred_element_type=F32)
    acc = acc + jnp.dot(yc_ref[...], w_ref[2], preferred_element_type=F32)
    out = _layer_norm(alpha * x_ref[...] + acc, g_ref[...], b_ref[...])
    o_ref[...] = out
    ob_ref[...] = out.astype(BF16)


def _out_proj_ln(ya, yb, yc, w3, x, g, b, alpha, *, tm=256):
    n, width = ya.shape
    d = w3.shape[-1]
    tm = min(tm, n)
    aspec = pl.BlockSpec((tm, width), lambda i: (i, 0))
    tok = pl.BlockSpec((tm, d), lambda i: (i, 0))
    row = pl.BlockSpec((1, d), lambda i: (0, 0))
    return pl.pallas_call(
        functools.partial(_out_proj_ln_kernel, alpha=alpha),
        out_shape=(jax.ShapeDtypeStruct((n, d), F32), jax.ShapeDtypeStruct((n, d), BF16)),
        grid=(n // tm,),
        in_specs=[aspec, aspec, aspec,
                  pl.BlockSpec((3, width, d), lambda i: (0, 0, 0), pipeline_mode=pl.Buffered(1)),
                  tok, row, row],
        out_specs=(tok, tok),
        compiler_params=_cparams("parallel"),
        name="out_proj_ln",
    )(ya, yb, yc, w3, x, g.reshape(1, d), b.reshape(1, d))


def _ffn_up_kernel(x_ref, wg_hbm, wu_hbm, o_ref, stage_g, stage_u, wg_s, wu_s, sem, *, layer, tn, ff):
    j, i = pl.program_id(0), pl.program_id(1)
    nj = pl.num_programs(0)
    n_tiles = -(-ff // tn)
    tail = ff - (n_tiles - 1) * tn

    def on_tile(jj, slot, fn):
        def run(width):
            col = pl.ds(pl.multiple_of(jj * tn, LANES), width)
            for a, (w, st) in enumerate(((wg_hbm, stage_g), (wu_hbm, stage_u))):
                fn(pltpu.make_async_copy(w.at[layer, :, col], st.at[slot, :, pl.ds(0, width)], sem.at[slot, a]))

        @pl.when(jj < nj - 1)
        def _():
            run(tn)

        @pl.when(jj == nj - 1)
        def _():
            run(tail)

    @pl.when(i == 0)
    def _():
        slot = j % 2

        @pl.when(j == 0)
        def _():
            on_tile(0, 0, lambda c: c.start())

        @pl.when(j + 1 < nj)
        def _():
            on_tile(j + 1, 1 - slot, lambda c: c.start())

        on_tile(j, slot, lambda c: c.wait())
        wg_s[...] = stage_g[slot].astype(BF16)
        wu_s[...] = stage_u[slot].astype(BF16)
        if tail < tn:
            @pl.when(j == nj - 1)
            def _():
                wg_s[:, tail:] = jnp.zeros((wg_s.shape[0], tn - tail), BF16)
                wu_s[:, tail:] = jnp.zeros((wu_s.shape[0], tn - tail), BF16)

    xb = x_ref[...]
    hg = jnp.dot(xb, wg_s[...], preferred_element_type=F32)
    hu = jnp.dot(xb, wu_s[...], preferred_element_type=F32)
    o_ref[...] = (hg * _sigmoid(hg) * hu).astype(o_ref.dtype)


def _ffn_down_ln_kernel(h_ref, w_ref, x_ref, g_ref, b_ref, o_ref, ob_ref, *, alpha):
    y = jnp.dot(h_ref[...], w_ref[...], preferred_element_type=F32)
    out = _layer_norm(alpha * x_ref[...] + y, g_ref[...], b_ref[...])
    o_ref[...] = out
    ob_ref[...] = out.astype(BF16)


def _ffn_dense(xb, x, wg, wu, layer, wd, g, b, alpha, *, tm_up=1024, tn_up=512, tm_down=256):
    n, d = x.shape
    ff_real = wd.shape[0]
    n_tiles = -(-ff_real // tn_up)
    assert ff_real % LANES == 0 and (ff_real % tn_up == 0 or n_tiles >= 3)
    tm_up, tm_down = min(tm_up, n), min(tm_down, n)
    h = pl.pallas_call(
        functools.partial(_ffn_up_kernel, layer=layer, tn=tn_up, ff=ff_real),
        out_shape=jax.ShapeDtypeStruct((n, n_tiles * tn_up), BF16),
        grid=(n_tiles, n // tm_up),
        in_specs=[pl.BlockSpec((tm_up, d), lambda j, i: (i, 0)),
                  pl.BlockSpec(memory_space=pl.ANY),
                  pl.BlockSpec(memory_space=pl.ANY)],
        out_specs=pl.BlockSpec((tm_up, tn_up), lambda j, i: (i, j)),
        scratch_shapes=[pltpu.VMEM((2, d, tn_up), F32), pltpu.VMEM((2, d, tn_up), F32),
                        pltpu.VMEM((d, tn_up), BF16), pltpu.VMEM((d, tn_up), BF16),
                        pltpu.SemaphoreType.DMA((2, 2))],
        compiler_params=_cparams("arbitrary", "arbitrary"),
        name="ffn_up",
    )(xb, wg, wu)
    tok = pl.BlockSpec((tm_down, d), lambda i: (i, 0))
    row = pl.BlockSpec((1, d), lambda i: (0, 0))
    return pl.pallas_call(
        functools.partial(_ffn_down_ln_kernel, alpha=alpha),
        out_shape=(jax.ShapeDtypeStruct((n, d), F32), jax.ShapeDtypeStruct((n, d), BF16)),
        grid=(n // tm_down,),
        in_specs=[pl.BlockSpec((tm_down, ff_real), lambda i: (i, 0)),
                  pl.BlockSpec((ff_real, d), lambda i: (0, 0), pipeline_mode=pl.Buffered(1)),
                  tok, row, row],
        out_specs=(tok, tok),
        compiler_params=_cparams("parallel"),
        name="ffn_down_ln",
    )(h, wd, x, g.reshape(1, d), b.reshape(1, d))


def _gather_rows_kernel(tok_ref, x_hbm, o_ref, buf, sem, *, tm):
    i = pl.program_id(0)

    def issue(tile, slot):
        base = tile * tm

        def body(r, carry):
            t = tok_ref[base + r]
            pltpu.make_async_copy(x_hbm.at[pl.ds(t, 1), :], buf.at[slot, pl.ds(r, 1), :], sem.at[slot]).start()
            return carry

        lax.fori_loop(0, tm, body, 0, unroll=8)

    @pl.when(i == 0)
    def _():
        issue(0, 0)

    @pl.when(i + 1 < pl.num_programs(0))
    def _():
        issue(i + 1, (i + 1) % 2)

    slot = i % 2
    pltpu.make_async_copy(x_hbm.at[pl.ds(0, tm), :], buf.at[slot], sem.at[slot]).wait()
    o_ref[...] = buf[slot].astype(o_ref.dtype)


def _gather_rows(x, row_tok, *, tm=256):
    n, d = x.shape
    rows = row_tok.shape[0]
    return pl.pallas_call(
        functools.partial(_gather_rows_kernel, tm=tm),
        out_shape=jax.ShapeDtypeStruct((rows, d), BF16),
        grid_spec=pltpu.PrefetchScalarGridSpec(
            num_scalar_prefetch=1,
            grid=(rows // tm,),
            in_specs=[pl.BlockSpec(memory_space=pl.ANY)],
            out_specs=pl.BlockSpec((tm, d), lambda i, tok: (i, 0)),
            scratch_shapes=[pltpu.VMEM((2, tm, d), F32), pltpu.SemaphoreType.DMA((2,))],
        ),
        compiler_params=_cparams("arbitrary"),
        name="moe_gather_rows",
    )(row_tok, x)


def _expert_changed(te_ref, i):
    return (i == 0) | (te_ref[i] != te_ref[jnp.maximum(i - 1, 0)])


def _stream_expert_weights(j, i, n_j, col_tile, te_ref, run_ref, nxt_ref, meta_ref, w_hbm, stage, work, sem):
    @pl.when(_expert_changed(te_ref, i))
    def _():
        k = j * meta_ref[0] + run_ref[i]
        slot = k % 2

        def copies(e, jj, s):
            col = pl.ds(pl.multiple_of(jj * col_tile, col_tile), col_tile)
            return [pltpu.make_async_copy(w.at[e, :, col], st.at[s], sem.at[s, a])
                    for a, (w, st) in enumerate(zip(w_hbm, stage))]

        @pl.when(k == 0)
        def _():
            for c in copies(te_ref[i], j, slot):
                c.start()

        has_next_run = nxt_ref[i] >= 0
        e_next = jnp.where(has_next_run, nxt_ref[i], meta_ref[1])
        j_next = jnp.where(has_next_run, j, j + 1)

        @pl.when(j_next < n_j)
        def _():
            for c in copies(e_next, j_next, 1 - slot):
                c.start()

        for c in copies(te_ref[i], j, slot):
            c.wait()
        for st, wk in zip(stage, work):
            wk[...] = st[slot].astype(BF16)


def _moe_up_kernel(te_ref, nv_ref, run_ref, nxt_ref, meta_ref, x_ref, wg_hbm, wu_hbm, o_ref,
                   stage_g, stage_u, wg_s, wu_s, sem, *, tf):
    j, i = pl.program_id(0), pl.program_id(1)
    _stream_expert_weights(j, i, pl.num_programs(0), tf, te_ref, run_ref, nxt_ref, meta_ref,
                           (wg_hbm, wu_hbm), (stage_g, stage_u), (wg_s, wu_s), sem)

    @pl.when(i < nv_ref[0])
    def _():
        xb = x_ref[...]
        hg = jnp.dot(xb, wg_s[...], preferred_element_type=F32)
        hu = jnp.dot(xb, wu_s[...], preferred_element_type=F32)
        o_ref[...] = (hg * _sigmoid(hg) * hu).astype(o_ref.dtype)

    @pl.when(i >= nv_ref[0])
    def _():
        o_ref[...] = jnp.zeros(o_ref.shape, o_ref.dtype)


def _moe_up(xs, wg, wu, sched, *, tm, tf=1024):
    rows, d = xs.shape
    ff = wg.shape[-1]
    return pl.pallas_call(
        functools.partial(_moe_up_kernel, tf=tf),
        out_shape=jax.ShapeDtypeStruct((rows, ff), BF16),
        grid_spec=pltpu.PrefetchScalarGridSpec(
            num_scalar_prefetch=len(sched),
            grid=(ff // tf, rows // tm),
            in_specs=[pl.BlockSpec((tm, d), lambda j, i, te, nv, *_: (jnp.minimum(i, nv[0] - 1), 0)),
                      pl.BlockSpec(memory_space=pl.ANY),
                      pl.BlockSpec(memory_space=pl.ANY)],
            out_specs=pl.BlockSpec((tm, tf), lambda j, i, *_: (i, j)),
            scratch_shapes=[pltpu.VMEM((2, d, tf), F32), pltpu.VMEM((2, d, tf), F32),
                            pltpu.VMEM((d, tf), BF16), pltpu.VMEM((d, tf), BF16),
                            pltpu.SemaphoreType.DMA((2, 2))],
        ),
        compiler_params=_cparams("arbitrary", "arbitrary"),
        name="moe_up",
    )(*sched, xs, wg, wu)


def _moe_down_kernel(te_ref, nv_ref, run_ref, nxt_ref, meta_ref, h_ref, wd_hbm, o_ref, stage_d, wd_s, sem, *, tn):
    j, i = pl.program_id(0), pl.program_id(1)
    _stream_expert_weights(j, i, pl.num_programs(0), tn, te_ref, run_ref, nxt_ref, meta_ref,
                           (wd_hbm,), (stage_d,), (wd_s,), sem)

    @pl.when(i < nv_ref[0])
    def _():
        o_ref[...] = jnp.dot(h_ref[...], wd_s[...], preferred_element_type=F32)

    @pl.when(i >= nv_ref[0])
    def _():
        o_ref[...] = jnp.zeros(o_ref.shape, o_ref.dtype)


def _moe_down(h, wd, sched, *, tm, tn=512):
    rows, ff = h.shape
    d = wd.shape[-1]
    return pl.pallas_call(
        functools.partial(_moe_down_kernel, tn=tn),
        out_shape=jax.ShapeDtypeStruct((rows, d), F32),
        grid_spec=pltpu.PrefetchScalarGridSpec(
            num_scalar_prefetch=len(sched),
            grid=(d // tn, rows // tm),
            in_specs=[pl.BlockSpec((tm, ff), lambda j, i, te, nv, *_: (jnp.minimum(i, nv[0] - 1), 0)),
                      pl.BlockSpec(memory_space=pl.ANY)],
            out_specs=pl.BlockSpec((tm, tn), lambda j, i, *_: (i, j)),
            scratch_shapes=[pltpu.VMEM((2, ff, tn), F32), pltpu.VMEM((ff, tn), BF16),
                            pltpu.SemaphoreType.DMA((2, 1))],
        ),
        compiler_params=_cparams("arbitrary", "arbitrary"),
        name="moe_down",
    )(*sched, h, wd)


def _combine_ln_kernel(pos_ref, y_hbm, x_ref, wts_ref, g_ref, b_ref, o_ref, buf, sem, *, tm, alpha):
    i = pl.program_id(0)

    def issue(tile, slot):
        base = tile * tm

        def body(r, carry):
            for k in range(TOP_K):
                row = pos_ref[(base + r) * TOP_K + k]
                pltpu.make_async_copy(y_hbm.at[pl.ds(row, 1), :], buf.at[slot, k, pl.ds(r, 1), :],
                                      sem.at[slot]).start()
            return carry

        lax.fori_loop(0, tm, body, 0, unroll=8)

    @pl.when(i == 0)
    def _():
        issue(0, 0)

    @pl.when(i + 1 < pl.num_programs(0))
    def _():
        issue(i + 1, (i + 1) % 2)

    slot = i % 2
    for k in range(TOP_K):
        pltpu.make_async_copy(y_hbm.at[pl.ds(0, tm), :], buf.at[slot, k], sem.at[slot]).wait()
    wts = wts_ref[...]
    y = buf[slot, 0] * wts[:, 0:1] + buf[slot, 1] * wts[:, 1:2]
    o_ref[...] = _layer_norm(alpha * x_ref[...] + y, g_ref[...], b_ref[...])


def _combine_ln(yrows, pos, x, wts, g, b, alpha, *, tm=128):
    n, d = x.shape
    tm = min(tm, n)
    return pl.pallas_call(
        functools.partial(_combine_ln_kernel, tm=tm, alpha=alpha),
        out_shape=jax.ShapeDtypeStruct((n, d), F32),
        grid_spec=pltpu.PrefetchScalarGridSpec(
            num_scalar_prefetch=1,
            grid=(n // tm,),
            in_specs=[pl.BlockSpec(memory_space=pl.ANY),
                      pl.BlockSpec((tm, d), lambda i, pos: (i, 0)),
                      pl.BlockSpec((tm, LANES), lambda i, pos: (i, 0)),
                      pl.BlockSpec((1, d), lambda i, pos: (0, 0)),
                      pl.BlockSpec((1, d), lambda i, pos: (0, 0))],
            out_specs=pl.BlockSpec((tm, d), lambda i, pos: (i, 0)),
            scratch_shapes=[pltpu.VMEM((2, TOP_K, tm, d), F32), pltpu.SemaphoreType.DMA((2,))],
        ),
        compiler_params=_cparams("arbitrary"),
        name="moe_combine_ln",
    )(pos, yrows, x, wts, g.reshape(1, d), b.reshape(1, d))


def _moe_block(x, w_router, wg, wu, wd, g, b, alpha, *, tm=256):
    n, d = x.shape
    ids, wts = _router(x, w_router)
    e_flat = ids[:, :TOP_K].reshape(-1)
    n_assign = n * TOP_K
    onehot = (e_flat[:, None] == jnp.arange(N_EXPERTS, dtype=I32)[None, :]).astype(I32)
    rank = jnp.sum((jnp.cumsum(onehot, axis=0) - onehot) * onehot, axis=1)
    counts = jnp.sum(onehot, axis=0)
    padded = (counts + tm - 1) // tm * tm
    end_padded = jnp.cumsum(padded)
    start_padded = end_padded - padded
    dest = (start_padded[e_flat] + rank).astype(I32)
    rows = n_assign + N_EXPERTS * tm
    n_tiles = rows // tm
    flat_tok = jnp.arange(n_assign, dtype=I32) // TOP_K
    row_tok = jnp.zeros((rows,), I32).at[dest].set(flat_tok)
    tile_start = jnp.arange(n_tiles, dtype=I32) * tm
    tile_e = jnp.minimum(jnp.sum((tile_start[:, None] >= end_padded[None, :]).astype(I32), axis=1),
                         N_EXPERTS - 1).astype(I32)
    n_valid = (end_padded[-1:] // tm).astype(I32)
    tile_ix = jnp.arange(n_tiles, dtype=I32)
    tile_e = jnp.where(tile_ix < n_valid[0], tile_e, tile_e[jnp.maximum(n_valid[0] - 1, 0)])
    is_start = jnp.concatenate([jnp.ones((1,), I32), (tile_e[1:] != tile_e[:-1]).astype(I32)])
    run_id = (jnp.cumsum(is_start) - 1).astype(I32)
    larger = jnp.where(tile_e[None, :] > tile_e[:, None], tile_e[None, :], N_EXPERTS)
    nxt_e = jnp.min(larger, axis=1)
    nxt_e = jnp.where(nxt_e < N_EXPERTS, nxt_e, -1).astype(I32)
    meta = jnp.stack([run_id[-1] + 1, tile_e[0]]).astype(I32)
    sched = (tile_e, n_valid, run_id, nxt_e, meta)
    xs = _gather_rows(x, row_tok, tm=tm)
    h = _moe_up(xs, wg, wu, sched, tm=tm)
    yrows = _moe_down(h, wd, sched, tm=tm)
    return _combine_ln(yrows, dest, x, wts, g, b, alpha)


def _w_in_columns(d_model):
    attn_w = N_HEADS * HEAD_DIM
    kv_w = N_KV * HEAD_DIM
    o = np.cumsum([0, attn_w, kv_w, kv_w, IDX_HEADS * IDX_DIM, IDX_DIM, IDX_HEADS, d_model, d_model, d_model])
    o = [int(v) for v in o]
    return {"q": (o[0], attn_w), "k": (o[1], kv_w), "v": (o[2], kv_w), "qi": (o[3], o[4] - o[3]),
            "kiwi": (o[4], LANES),
            "px": (o[6], 2 * d_model), "gr": (o[8], d_model), "gates": (o[9], 3 * d_model)}


def _mixer(x_mm, tabs, wt_in, layer, pool_w, pool_scale, conv_w, conv_b, wa, ba, wx, bx, lam,
           batch, seq, d_model, tq):
    cos_a, sin_a, cos_i, sin_ia, sin_ib, cos_q, sin_q, cos_iq = tabs
    cols = _w_in_columns(d_model)
    rope_a = ((cos_a, sin_a), (HEAD_DIM // 2,))
    rope_i = (LANES - IDX_DIM // 2, IDX_DIM // 2)
    qh = _proj_heads(x_mm, wt_in, layer, cols["q"], (cos_q, sin_q), rope_a[1], BF16)
    kh = _proj_heads(x_mm, wt_in, layer, cols["k"], *rope_a, BF16)
    vh = _proj_heads(x_mm, wt_in, layer, cols["v"], (), (), BF16)
    qih = _proj_heads(x_mm, wt_in, layer, cols["qi"], (cos_iq, sin_ia, sin_ib), rope_i, BF16, head_width=IDX_DIM)
    kiwi = _proj_heads(x_mm, wt_in, layer, cols["kiwi"], (cos_i, sin_ia, sin_ib), rope_i, F32)
    px = _proj_plain(x_mm, wt_in, layer, cols["px"], None, F32)
    gg = _proj_plain(x_mm, wt_in, layer, cols["gr"], "gelu", BF16)
    sg = _proj_plain(x_mm, wt_in, layer, cols["gates"], "sigmoid", BF16)
    ya = _attention(qh, kh, vh, qih, kiwi, sg, batch, seq, tq=tq)
    yb = _pool_mixer(px, sg, pool_w.astype(BF16), pool_scale, batch, seq)
    yc = _lru_mixer(px, gg, sg, conv_w, conv_b, wa.astype(BF16), ba, wx.astype(BF16), bx, lam, batch, seq)
    return ya, yb, yc


def kernel(x, positions, mix_w_in, mix_w_out, pool_w, pool_scale, conv_w, conv_b, lru_wa, lru_ba, lru_wx, lru_bx, lru_lam, ln_mix_g, ln_mix_b, ln_ffn_g, ln_ffn_b, dense_w_gate, dense_w_up, dense_w_down, moe_router, moe_w_gate, moe_w_up, moe_w_down):
    batch, seq, d_model = x.shape
    depth = mix_w_in.shape[0]
    alpha = np.float32((2 * depth) ** 0.25)
    n = batch * seq
    tabs = _rope_tables(positions)
    xf = x.reshape(n, d_model)
    x_mm = xf
    wt_in = jnp.swapaxes(mix_w_in, 1, 2)
    for layer in range(depth):
        ya, yb, yc = _mixer(x_mm, tabs, wt_in, layer, pool_w[layer], pool_scale[layer],
                            conv_w[layer], conv_b[layer], lru_wa[layer], lru_ba[layer], lru_wx[layer],
                            lru_bx[layer], lru_lam[layer], batch, seq, d_model, tq=256)
        w_out3 = mix_w_out[layer].astype(BF16).reshape(3, -1, d_model)
        xf, xb = _out_proj_ln(ya, yb, yc, w_out3, xf, ln_mix_g[layer], ln_mix_b[layer], alpha)
        j = layer // 2
        if layer % 2 == 0:
            wd = dense_w_down[j].astype(BF16)
            xf, x_mm = _ffn_dense(xb, xf, dense_w_gate, dense_w_up, j, wd,
                                  ln_ffn_g[layer], ln_ffn_b[layer], alpha)
        else:
            xf = _moe_block(xf, moe_router[j], moe_w_gate[j], moe_w_up[j], moe_w_down[j],
                            ln_ffn_g[layer], ln_ffn_b[layer], alpha)
            x_mm = xf
    return xf.reshape(batch, seq, d_model)
```

```python
import functools

import jax
import jax.numpy as jnp
import numpy as np
from jax import lax
from jax.experimental import pallas as pl
from jax.experimental.pallas import tpu as pltpu

F32 = jnp.float32
BF16 = jnp.bfloat16
I32 = jnp.int32

LANES = 128
SUBLANES = 8
VMEM_LIMIT = 56 * 1024 * 1024

CHUNK = 64
N_HEADS = 16
HEAD_DIM = 128
N_KV = 4
HEADS_PER_KV = N_HEADS // N_KV
IDX_HEADS = 16
IDX_DIM = 64
TOPK_MAX = 256
ROPE_THETA = 10000.0
POOL_WINDOWS = (2, 4, 8, 16)
LRU_BLOCK = 128
CONV_WIDTH = 4
LRU_C = 8.0
N_EXPERTS = 8
TOP_K = 2
LN_EPS = 1e-5
INT_MIN = -2 ** 31
NEG_BIG = -1e30


def _cparams(*sem):
    return pltpu.CompilerParams(dimension_semantics=sem, vmem_limit_bytes=VMEM_LIMIT)


def _sigmoid(x):
    return 0.5 * (1.0 + jnp.tanh(0.5 * x))


def _gelu_tanh(x):
    c = np.float32(np.sqrt(2.0 / np.pi))
    return 0.5 * x * (1.0 + jnp.tanh(c * (x + np.float32(0.044715) * (x * x * x))))


def _layer_norm(y, g, b):
    mu = jnp.mean(y, axis=-1, keepdims=True)
    d = y - mu
    var = jnp.mean(d * d, axis=-1, keepdims=True)
    return d * lax.rsqrt(var + LN_EPS) * g + b


def _rope_tab_kernel(pos_ref, inv_a_ref, inv_i_ref, sgn_a_ref, m_cos_ref, add_cos_ref,
                     m_sa_ref, m_sb_ref, cos_a_ref, sin_a_ref, cos_i_ref, sin_ia_ref, sin_ib_ref,
                     cos_q_ref, sin_q_ref, cos_iq_ref):
    pos = pos_ref[...]
    ang_a = pos * inv_a_ref[...]
    cos_a = jnp.cos(ang_a)
    sin_a = jnp.sin(ang_a) * sgn_a_ref[...]
    cos_a_ref[...] = cos_a
    sin_a_ref[...] = sin_a
    q_scale = np.float32(HEAD_DIM ** -0.5 * np.log2(np.e))
    cos_q_ref[...] = cos_a * q_scale
    sin_q_ref[...] = sin_a * q_scale
    ang_i = pos * inv_i_ref[...]
    s_i = jnp.sin(ang_i)
    cos_iq = jnp.cos(ang_i) * m_cos_ref[...]
    cos_iq_ref[...] = cos_iq
    cos_i_ref[...] = cos_iq + add_cos_ref[...]
    sin_ia_ref[...] = s_i * m_sa_ref[...]
    sin_ib_ref[...] = s_i * m_sb_ref[...]


def _rope_tables(positions):
    n = positions.size
    pos = jnp.broadcast_to(positions.reshape(n, 1).astype(F32), (n, LANES))
    lane = np.arange(LANES)
    inv_a = (ROPE_THETA ** (-jnp.arange(0, HEAD_DIM, 2, dtype=F32) / HEAD_DIM))
    inv_i = (ROPE_THETA ** (-jnp.arange(0, IDX_DIM, 2, dtype=F32) / IDX_DIM))
    inv_a_row = jnp.concatenate([inv_a, inv_a])[None, :]
    inv_i_row = jnp.concatenate([inv_i, inv_i, jnp.zeros((LANES - IDX_DIM,), F32)])[None, :]
    sgn_a = jnp.asarray(np.where(lane < HEAD_DIM // 2, -1.0, 1.0), F32)[None, :]
    m_cos = jnp.asarray((lane < IDX_DIM).astype(np.float32))[None, :]
    wi_scale = (IDX_HEADS ** -0.5) * (IDX_DIM ** -0.5)
    add_cos = jnp.asarray(np.where((lane >= IDX_DIM) & (lane < IDX_DIM + IDX_HEADS), wi_scale, 0.0), F32)[None, :]
    m_sa = jnp.asarray(np.where(lane < IDX_DIM // 2, -1.0, 0.0), F32)[None, :]
    m_sb = jnp.asarray(np.where((lane >= IDX_DIM // 2) & (lane < IDX_DIM), 1.0, 0.0), F32)[None, :]
    tm = min(n, 1024)
    row = pl.BlockSpec((1, LANES), lambda i: (0, 0))
    tok = pl.BlockSpec((tm, LANES), lambda i: (i, 0))
    out = jax.ShapeDtypeStruct((n, LANES), F32)
    return pl.pallas_call(
        _rope_tab_kernel,
        out_shape=(out,) * 8,
        grid=(n // tm,),
        in_specs=[tok] + [row] * 7,
        out_specs=(tok,) * 8,
        compiler_params=_cparams("parallel"),
        name="rope_tables",
    )(pos, inv_a_row, inv_i_row, sgn_a, m_cos, add_cos, m_sa, m_sb)


def _stream_weight_tile(wt_hbm, stage, w_s, sem, *, layer, off, tn):
    j = pl.program_id(0)

    def copy(jj, slot):
        rows = pl.ds(pl.multiple_of(off + jj * tn, SUBLANES), tn)
        return pltpu.make_async_copy(wt_hbm.at[layer, rows, :], stage.at[slot], sem.at[slot])

    @pl.when(pl.program_id(1) == 0)
    def _():
        slot = j % 2

        @pl.when(j == 0)
        def _():
            copy(0, 0).start()

        @pl.when(j + 1 < pl.num_programs(0))
        def _():
            copy(j + 1, 1 - slot).start()

        copy(j, slot).wait()
        w_s[...] = stage[slot].astype(BF16)


def _dot_nt(x, w):
    return lax.dot_general(x, w, (((1,), (1,)), ((), ())), preferred_element_type=F32)


def _proj_heads_kernel(x_ref, wt_hbm, *rest, layer, off, tn, shifts, heads, head_width, rope_heads):
    tabs, o_ref, stage, w_s, sem = rest[:-4], rest[-4], rest[-3], rest[-2], rest[-1]
    _stream_weight_tile(wt_hbm, stage, w_s, sem, layer=layer, off=off, tn=tn)
    acc = _dot_nt(x_ref[...].astype(BF16), w_s[...])
    per_slab = LANES // head_width
    for h in range(heads):
        xh = acc[:, (h // per_slab) * LANES:(h // per_slab + 1) * LANES]
        if h % per_slab:
            xh = pltpu.roll(xh, LANES - (h % per_slab) * head_width, 1)
        if tabs and h < rope_heads:
            y = xh * tabs[0][...]
            for s, t in zip(shifts, tabs[1:]):
                y = y + pltpu.roll(xh, s, 1) * t[...]
        else:
            y = xh
        o_ref[h] = y.astype(o_ref.dtype)


def _weight_stream_scratch(tn, d):
    return [pltpu.VMEM((2, tn, d), F32), pltpu.VMEM((tn, d), BF16), pltpu.SemaphoreType.DMA((2,))]


def _proj_heads(x, wt, layer, cols, tabs, shifts, out_dtype, *, head_width=LANES, rope_heads=None,
                tm=1024, tile_cols=1024):
    n, d = x.shape
    off, width = cols
    tn = min(tile_cols, width)
    hp = tn // head_width
    assert width % tn == 0 and off % SUBLANES == 0 and (head_width == LANES or tabs)
    tm = min(tm, n)
    tab_spec = pl.BlockSpec((tm, LANES), lambda j, i: (i, 0))
    return pl.pallas_call(
        functools.partial(_proj_heads_kernel, layer=layer, off=off, tn=tn, shifts=shifts, heads=hp,
                          head_width=head_width, rope_heads=hp if rope_heads is None else rope_heads),
        out_shape=jax.ShapeDtypeStruct((width // head_width, n, LANES), out_dtype),
        grid=(width // tn, n // tm),
        in_specs=[pl.BlockSpec((tm, d), lambda j, i: (i, 0)),
                  pl.BlockSpec(memory_space=pl.ANY)] + [tab_spec] * len(tabs),
        out_specs=pl.BlockSpec((hp, tm, LANES), lambda j, i: (j, i, 0)),
        scratch_shapes=_weight_stream_scratch(tn, d),
        compiler_params=_cparams("arbitrary", "arbitrary"),
        name="proj_heads",
    )(x, wt, *tabs)


def _proj_plain_kernel(x_ref, wt_hbm, o_ref, stage, w_s, sem, *, layer, off, tn, act):
    _stream_weight_tile(wt_hbm, stage, w_s, sem, layer=layer, off=off, tn=tn)
    acc = _dot_nt(x_ref[...].astype(BF16), w_s[...])
    if act == "gelu":
        acc = _gelu_tanh(acc)
    elif act == "sigmoid":
        acc = _sigmoid(acc)
    o_ref[...] = acc.astype(o_ref.dtype)


def _proj_plain(x, wt, layer, cols, act, out_dtype, *, tm=1024, tn=1024):
    n, d = x.shape
    off, width = cols
    assert width % tn == 0 and off % SUBLANES == 0
    tm = min(tm, n)
    return pl.pallas_call(
        functools.partial(_proj_plain_kernel, layer=layer, off=off, tn=tn, act=act),
        out_shape=jax.ShapeDtypeStruct((n, width), out_dtype),
        grid=(width // tn, n // tm),
        in_specs=[pl.BlockSpec((tm, d), lambda j, i: (i, 0)),
                  pl.BlockSpec(memory_space=pl.ANY)],
        out_specs=pl.BlockSpec((tm, tn), lambda j, i: (i, j)),
        scratch_shapes=_weight_stream_scratch(tn, d),
        compiler_params=_cparams("arbitrary", "arbitrary"),
        name="proj_plain",
    )(x, wt)


def _key_to_float(key):
    return pltpu.bitcast(key ^ ((key >> 31) & 0x7FFFFFFF), F32)


def _attn_kernel(q_ref, k_ref, v_ref, qi_ref, ki_ref, wi_ref, gate_ref, o_ref,
                 wb_s, key_s, keyt_s, x_s, thr_s, xrow_s, *state, tq, topk, seq):
    m_s, acc_s = state[:N_KV], state[N_KV:]
    kb = tq
    n_sub = kb // LANES
    qt = pl.program_id(1)
    nkb = qt + 1
    n_hi = IDX_HEADS
    rows_g = HEADS_PER_KV * tq

    wi = wi_ref[...]
    for h in range(n_hi):
        wb_s[h] = jnp.broadcast_to(wi[:, IDX_DIM + h:IDX_DIM + h + 1], (tq, LANES))
    qi = qi_ref[...].reshape(n_hi * tq, LANES)
    q_row = qt * tq + lax.broadcasted_iota(I32, (tq, LANES), 0)
    limit = (q_row // CHUNK + 1) * CHUNK
    lane_pos = lax.broadcasted_iota(I32, (tq, LANES), 1)
    limit_t = ((qt * tq + lax.broadcasted_iota(I32, (1, tq), 1)) // CHUNK + 1) * CHUNK

    def score_body(j, carry):
        start = pl.multiple_of(j * kb, kb)
        ki_blk = ki_ref[pl.ds(start, kb), :].astype(BF16)
        s = lax.dot_general(qi, ki_blk, (((1,), (1,)), ((), ())), preferred_element_type=F32)
        parts = []
        for c in range(n_sub):
            sc = jnp.zeros((tq, LANES), F32)
            for h in range(n_hi):
                sh = s[h * tq:(h + 1) * tq, c * LANES:(c + 1) * LANES]
                sc = sc + wb_s[h] * jnp.maximum(sh, 0.0)
            sc = jnp.where(sc == 0.0, 0.0, sc)
            parts.append(sc)
            kpos = start + c * LANES + lane_pos
            key_s[j, :, c * LANES:(c + 1) * LANES] = jnp.where(kpos < limit, sc, -jnp.inf)
        sc_t = jnp.concatenate(parts, axis=1).T
        kpos_t = start + lax.broadcasted_iota(I32, (kb, tq), 0)
        keyt_s[j] = jnp.where(kpos_t < limit_t, sc_t, -jnp.inf)
        return carry

    lax.fori_loop(0, nkb, score_body, 0)

    kf = np.float32(topk)
    searched = limit_t > topk

    def count_keys(pred):
        def body(j, cnt):
            hit = jnp.where(pred(keyt_s[j], j), 1.0, 0.0)
            return cnt + jnp.sum(hit.reshape(kb // SUBLANES, SUBLANES, tq), axis=0)
        cnt = lax.fori_loop(0, nkb, body, jnp.zeros((SUBLANES, tq), F32))
        return jnp.sum(cnt, axis=0, keepdims=True)

    thr0 = jnp.where(count_keys(lambda s, j: s >= 0.0) >= kf, 0, INT_MIN).astype(I32)

    def thr_body(it, thr):
        cand = thr | jnp.left_shift(jnp.int32(1), 30 - it)
        cand_f = _key_to_float(cand)
        return jnp.where(count_keys(lambda s, j: s >= cand_f) >= kf, cand, thr)

    thr = _key_to_float(lax.fori_loop(0, 31, thr_body, thr0))
    thr = jnp.where(searched, thr, -jnp.inf)
    n_ge = count_keys(lambda s, j: s >= thr)
    need = kf - count_keys(lambda s, j: s > thr)
    xrow_s[...] = jnp.where(searched, seq, -1).astype(I32)
    tie_flag = jnp.max(jnp.where(searched & (n_ge > kf), 1.0, 0.0), axis=(0, 1), keepdims=True)

    @pl.when(tie_flag[0, 0] > 0.0)
    def _():
        nbits = max(int(seq - 1).bit_length(), 1)
        row_pos = lax.broadcasted_iota(I32, (kb, tq), 0)

        def x_body(it, xcut):
            cand = xcut | jnp.left_shift(jnp.int32(1), nbits - 1 - it)
            cnt = count_keys(lambda s, j: (s == thr) & (j * kb + row_pos < cand))
            return jnp.where(cnt < need, cand, xcut)

        xcut = lax.fori_loop(0, nbits, x_body, jnp.zeros((1, tq), I32))
        xrow_s[...] = jnp.where(searched, xcut, -1)

    def to_col(row_f32):
        return jnp.broadcast_to(row_f32, (LANES, tq)).T

    thr_s[...] = to_col(thr)
    x_s[...] = to_col(xrow_s[...].astype(F32)).astype(I32)
    rb = min(tq, 128)
    n_rc = tq // rb
    lane_rb = lax.broadcasted_iota(I32, (rb, LANES), 1)

    for g in range(N_KV):
        m_s[g][...] = jnp.full((rows_g, LANES), NEG_BIG, F32)
        acc_s[g][...] = jnp.zeros((rows_g, 2 * LANES), F32)
    ones_v = jnp.ones((kb, LANES), BF16)

    def attn_body(j, carry):
        start = pl.multiple_of(j * kb, kb)
        bias = []
        for rc in range(n_rc):
            thr_c = thr_s[rc * rb:(rc + 1) * rb, :]
            xcut_c = x_s[rc * rb:(rc + 1) * rb, :]
            parts = []
            for c in range(n_sub):
                kc = key_s[j, rc * rb:(rc + 1) * rb, c * LANES:(c + 1) * LANES]
                kpos = start + c * LANES + lane_rb
                sel = (kc > thr_c) | ((kc == thr_c) & (kpos <= xcut_c))
                parts.append(jnp.where(sel, 0.0, NEG_BIG))
            bias.append(jnp.concatenate(parts, axis=1))
        for g in range(N_KV):
            qg = q_ref[g * HEADS_PER_KV:(g + 1) * HEADS_PER_KV].reshape(rows_g, LANES)
            kg = k_ref[g, pl.ds(start, kb), :]
            vg = jnp.concatenate([v_ref[g, pl.ds(start, kb), :], ones_v], axis=1)
            lg_all = lax.dot_general(qg, kg, (((1,), (1,)), ((), ())), preferred_element_type=F32)
            m_prev_all = m_s[g][...]
            p_parts, a_parts, m_parts = [], [], []
            for r in range(HEADS_PER_KV):
                for rc in range(n_rc):
                    r0 = r * tq + rc * rb
                    lg = lg_all[r0:r0 + rb] + bias[rc]
                    m_prev = m_prev_all[r0:r0 + rb]
                    m_new = jnp.maximum(m_prev, jnp.max(lg, axis=1, keepdims=True))
                    p = jnp.exp2(lg - jnp.concatenate([m_new] * n_sub, axis=1))
                    p_parts.append(p.astype(BF16))
                    a_parts.append(jnp.exp2(m_prev - m_new))
                    m_parts.append(m_new)
            p_all = jnp.concatenate(p_parts, axis=0)
            alpha_all = jnp.concatenate(a_parts, axis=0)
            pv = jnp.dot(p_all, vg, preferred_element_type=F32)
            acc_s[g][...] = jnp.concatenate([alpha_all, alpha_all], axis=1) * acc_s[g][...] + pv
            m_s[g][...] = jnp.concatenate(m_parts, axis=0)
        return carry

    lax.fori_loop(0, nkb, attn_body, 0)

    for g in range(N_KV):
        acc = acc_s[g][...]
        og = acc[:, :LANES] / acc[:, LANES:]
        for r in range(HEADS_PER_KV):
            col = (g * HEADS_PER_KV + r) * LANES
            y = og[r * tq:(r + 1) * tq] * gate_ref[:, col:col + LANES].astype(F32)
            o_ref[:, col:col + LANES] = y.astype(o_ref.dtype)


def _attention(qh, kvh, qih, kiwi, sg, batch, seq, *, tq):
    n = batch * seq
    topk = min(TOPK_MAX, seq // 4)
    tq = min(tq, seq)
    nqt = seq // tq
    attn_w = N_HEADS * HEAD_DIM
    rows_g = HEADS_PER_KV * tq
    return pl.pallas_call(
        functools.partial(_attn_kernel, tq=tq, topk=topk, seq=seq),
        out_shape=jax.ShapeDtypeStruct((n, attn_w), BF16),
        grid=(batch, nqt),
        in_specs=[
            pl.BlockSpec((N_HEADS, tq, LANES), lambda b, t: (0, b * nqt + t, 0)),
            pl.BlockSpec((N_KV, seq, LANES), lambda b, t: (0, b, 0)),
            pl.BlockSpec((N_KV, seq, LANES), lambda b, t: (1, b, 0)),
            pl.BlockSpec((IDX_HEADS, tq, LANES), lambda b, t: (0, b * nqt + t, 0)),
            pl.BlockSpec((None, seq, LANES), lambda b, t: (0, b, 0)),
            pl.BlockSpec((None, tq, LANES), lambda b, t: (0, b * nqt + t, 0)),
            pl.BlockSpec((tq, attn_w), lambda b, t: (b * nqt + t, 0)),
        ],
        out_specs=pl.BlockSpec((tq, attn_w), lambda b, t: (b * nqt + t, 0)),
        scratch_shapes=[
            pltpu.VMEM((IDX_HEADS, tq, LANES), F32),
            pltpu.VMEM((nqt, tq, tq), F32),
            pltpu.VMEM((nqt, tq, tq), F32),
            pltpu.VMEM((tq, LANES), I32),
            pltpu.VMEM((tq, LANES), F32),
            pltpu.VMEM((1, tq), I32),
        ] + [pltpu.VMEM((rows_g, LANES), F32)] * N_KV + [pltpu.VMEM((rows_g, 2 * LANES), F32)] * N_KV,
        compiler_params=_cparams("parallel", "arbitrary"),
        name="sparse_attention",
    )(qh, kvh, kvh, qih, kiwi, kiwi, sg)


def _pool_kernel(p_ref, gate_ref, w_ref, scale_ref, o_ref, buf_a, buf_b, *, seq):
    pad = 16
    g = pl.program_id(1)
    p = p_ref[...]
    zeros = jnp.zeros((pad, p.shape[1]), F32)
    buf_a[0:pad, :] = zeros
    buf_b[0:pad, :] = zeros
    buf_a[pad:pad + seq, :] = p
    s2 = p + buf_a[pad - 1:pad - 1 + seq, :]
    buf_b[pad:pad + seq, :] = s2
    s4 = s2 + buf_b[pad - 2:pad - 2 + seq, :]
    buf_a[pad:pad + seq, :] = s4
    s8 = s4 + buf_a[pad - 4:pad - 4 + seq, :]
    buf_b[pad:pad + seq, :] = s8
    s16 = s8 + buf_b[pad - 8:pad - 8 + seq, :]
    t1 = (lax.broadcasted_iota(I32, p.shape, 0) + 1).astype(F32)
    win = jnp.where(g == 0, 2.0, jnp.where(g == 1, 4.0, jnp.where(g == 2, 8.0, 16.0))).astype(F32)
    total = jnp.where(g == 0, s2, jnp.where(g == 1, s4, jnp.where(g == 2, s8, s16)))
    mean = total / jnp.minimum(t1, win)
    diff = (mean - p).astype(BF16)
    y = jnp.dot(diff, w_ref[...], preferred_element_type=F32)
    o_ref[...] = (y * scale_ref[...] * gate_ref[...].astype(F32)).astype(o_ref.dtype)


def _pool_mixer(px, sg, pool_w, pool_scale, batch, seq):
    n = batch * seq
    width = pool_scale.shape[-1]
    ng = len(POOL_WINDOWS)
    cg = width // ng
    assert POOL_WINDOWS == (2, 4, 8, 16)
    return pl.pallas_call(
        functools.partial(_pool_kernel, seq=seq),
        out_shape=jax.ShapeDtypeStruct((n, width), BF16),
        grid=(batch, ng),
        in_specs=[
            pl.BlockSpec((seq, cg), lambda b, g: (b, g)),
            pl.BlockSpec((seq, cg), lambda b, g: (b, ng + g)),
            pl.BlockSpec((None, cg, cg), lambda b, g: (g, 0, 0)),
            pl.BlockSpec((1, cg), lambda b, g: (0, g)),
        ],
        out_specs=pl.BlockSpec((seq, cg), lambda b, g: (b, g)),
        scratch_shapes=[pltpu.VMEM((seq + 16, cg), F32), pltpu.VMEM((seq + 16, cg), F32)],
        compiler_params=_cparams("parallel", "arbitrary"),
        name="pool_mixer",
    )(px, sg, pool_w, pool_scale.reshape(1, width))


def _lru_kernel(x_ref, gr_ref, gate_ref, cw_ref, cb_ref, wa_ref, ba_ref, wx_ref, bx_ref, lam_ref,
                o_ref, a_s, b_s, *, seq, ct):
    x = x_ref[...]
    row = lax.broadcasted_iota(I32, (seq, ct), 0)
    xc = jnp.broadcast_to(cb_ref[...], (seq, ct))
    for tap in range(CONV_WIDTH):
        d = CONV_WIDTH - 1 - tap
        x_d = x if d == 0 else jnp.where(row >= d, pltpu.roll(x, d, 0), 0.0)
        xc = xc + x_d * cw_ref[tap:tap + 1, :]
    xcb = xc.astype(BF16)
    nb = ct // LRU_BLOCK
    r_parts, i_parts = [], []
    for blk in range(nb):
        xb = xcb[:, blk * LRU_BLOCK:(blk + 1) * LRU_BLOCK]
        r_parts.append(jnp.dot(xb, wa_ref[blk], preferred_element_type=F32))
        i_parts.append(jnp.dot(xb, wx_ref[blk], preferred_element_type=F32))
    r = _sigmoid(jnp.concatenate(r_parts, axis=1) + ba_ref[...])
    gi = _sigmoid(jnp.concatenate(i_parts, axis=1) + bx_ref[...])
    lam = lam_ref[...]
    softplus_neg_lam = jnp.log(1.0 + jnp.exp(-lam))
    log_a = -LRU_C * r * softplus_neg_lam
    a = jnp.exp(log_a)
    one_m_a2 = 1.0 - a * a
    root = jnp.where(one_m_a2 > 0.0, one_m_a2 * lax.rsqrt(one_m_a2), 0.0)
    b = root * (gi * xc)

    tiles = (seq // SUBLANES, SUBLANES, ct)
    a = a.reshape(tiles)
    b = b.reshape(tiles)
    sub = lax.broadcasted_iota(I32, tiles, 1)
    for d in (1, 2, 4):
        keep = sub >= d
        a_sh = jnp.where(keep, pltpu.roll(a, d, 1), 1.0)
        b_sh = jnp.where(keep, pltpu.roll(b, d, 1), 0.0)
        b = a * b_sh + b
        a = a * a_sh
    a_s[...] = a.reshape(seq, ct)
    b_s[...] = b.reshape(seq, ct)

    def body(t, carry):
        r0 = pl.multiple_of(t * SUBLANES, SUBLANES)
        h = a_s[pl.ds(r0, SUBLANES), :] * carry + b_s[pl.ds(r0, SUBLANES), :]
        b_s[pl.ds(r0, SUBLANES), :] = h
        return jnp.broadcast_to(h[SUBLANES - 1:SUBLANES, :], (SUBLANES, ct))

    lax.fori_loop(0, seq // SUBLANES, body, jnp.zeros((SUBLANES, ct), F32), unroll=8)
    h = b_s[...]
    o_ref[...] = (h * gr_ref[...].astype(F32) * gate_ref[...].astype(F32)).astype(o_ref.dtype)


def _lru_mixer(px, gg, sg, conv_w, conv_b, wa, ba, wx, bx, lam, batch, seq, *, ct=256):
    n = batch * seq
    width = conv_b.shape[-1]
    nct = width // ct
    nb = ct // LRU_BLOCK
    row = lambda a: a.reshape(1, width)
    rspec = pl.BlockSpec((1, ct), lambda b, j: (0, j))
    return pl.pallas_call(
        functools.partial(_lru_kernel, seq=seq, ct=ct),
        out_shape=jax.ShapeDtypeStruct((n, width), BF16),
        grid=(batch, nct),
        in_specs=[
            pl.BlockSpec((seq, ct), lambda b, j: (b, nct + j)),
            pl.BlockSpec((seq, ct), lambda b, j: (b, j)),
            pl.BlockSpec((seq, ct), lambda b, j: (b, 2 * nct + j)),
            pl.BlockSpec((CONV_WIDTH, ct), lambda b, j: (0, j)),
            rspec,
            pl.BlockSpec((nb, LRU_BLOCK, LRU_BLOCK), lambda b, j: (j, 0, 0)),
            rspec,
            pl.BlockSpec((nb, LRU_BLOCK, LRU_BLOCK), lambda b, j: (j, 0, 0)),
            rspec,
            rspec,
        ],
        out_specs=pl.BlockSpec((seq, ct), lambda b, j: (b, j)),
        scratch_shapes=[pltpu.VMEM((seq, ct), F32),
                        pltpu.VMEM((seq, ct), F32)],
        compiler_params=_cparams("parallel", "arbitrary"),
        name="rglru_mixer",
    )(px, gg, sg, conv_w, row(conv_b), wa, row(ba), wx, row(bx), row(lam))


def _top2_route(x, w_ref, ids_ref, wts_ref):
    logits = jnp.dot(x, w_ref[...], preferred_element_type=F32, precision=lax.Precision.HIGHEST)
    lane_i = lax.broadcasted_iota(I32, logits.shape, 1)
    lane = lane_i.astype(F32)
    logits = jnp.where(lane_i < N_EXPERTS, logits, -jnp.inf)
    m1 = jnp.max(logits, axis=1, keepdims=True)
    i1 = jnp.min(jnp.where(logits == m1, lane, float(LANES)), axis=1, keepdims=True)
    rest = jnp.where(lane == i1, -jnp.inf, logits)
    m2 = jnp.max(rest, axis=1, keepdims=True)
    i2 = jnp.min(jnp.where(rest == m2, lane, float(LANES)), axis=1, keepdims=True)
    e2 = jnp.exp(m2 - m1)
    w1 = 1.0 / (1.0 + e2)
    w2 = e2 / (1.0 + e2)
    ids_ref[...] = jnp.where(lane_i == 0, i1, jnp.where(lane_i == 1, i2, 0.0)).astype(I32)
    wts_ref[...] = jnp.where(lane_i == 0, w1, jnp.where(lane_i == 1, w2, 0.0))


def _router_kernel(x_ref, w_ref, ids_ref, wts_ref):
    _top2_route(x_ref[...], w_ref, ids_ref, wts_ref)


def _router(x, w_router, *, tm=512):
    n, d = x.shape
    wpad = jnp.zeros((d, LANES), F32).at[:, :N_EXPERTS].set(w_router.astype(F32))
    tm = min(tm, n)
    tok = pl.BlockSpec((tm, LANES), lambda i: (i, 0))
    return pl.pallas_call(
        _router_kernel,
        out_shape=(jax.ShapeDtypeStruct((n, LANES), I32), jax.ShapeDtypeStruct((n, LANES), F32)),
        grid=(n // tm,),
        in_specs=[pl.BlockSpec((tm, d), lambda i: (i, 0)), pl.BlockSpec((d, LANES), lambda i: (0, 0))],
        out_specs=(tok, tok),
        compiler_params=_cparams("parallel"),
        name="moe_router",
    )(x, wpad)


def _out_proj_ln_kernel(ya_ref, yb_ref, yc_ref, w_ref, x_ref, g_ref, b_ref, o_ref, ob_ref, *, alpha):
    acc = jnp.dot(ya_ref[...], w_ref[0], preferred_element_type=F32)
    acc = acc + jnp.dot(yb_ref[...], w_ref[1], preferred_element_type=F32)
    acc = acc + jnp.dot(yc_ref[...], w_ref[2], preferred_element_type=F32)
    out = _layer_norm(alpha * x_ref[...] + acc, g_ref[...], b_ref[...])
    o_ref[...] = out
    ob_ref[...] = out.astype(BF16)


def _out_proj_ln(ya, yb, yc, w3, x, g, b, alpha, *, tm=256):
    n, width = ya.shape
    d = w3.shape[-1]
    tm = min(tm, n)
    aspec = pl.BlockSpec((tm, width), lambda i: (i, 0))
    tok = pl.BlockSpec((tm, d), lambda i: (i, 0))
    row = pl.BlockSpec((1, d), lambda i: (0, 0))
    return pl.pallas_call(
        functools.partial(_out_proj_ln_kernel, alpha=alpha),
        out_shape=(jax.ShapeDtypeStruct((n, d), F32), jax.ShapeDtypeStruct((n, d), BF16)),
        grid=(n // tm,),
        in_specs=[aspec, aspec, aspec,
                  pl.BlockSpec((3, width, d), lambda i: (0, 0, 0), pipeline_mode=pl.Buffered(1)),
                  tok, row, row],
        out_specs=(tok, tok),
        compiler_params=_cparams("parallel"),
        name="out_proj_ln",
    )(ya, yb, yc, w3, x, g.reshape(1, d), b.reshape(1, d))


def _ffn_up_kernel(x_ref, wg_hbm, wu_hbm, o_ref, stage_g, stage_u, wg_s, wu_s, sem, *, layer, tn, ff):
    j, i = pl.program_id(0), pl.program_id(1)
    nj = pl.num_programs(0)
    n_tiles = -(-ff // tn)
    tail = ff - (n_tiles - 1) * tn

    def on_tile(jj, slot, fn):
        def run(width):
            col = pl.ds(pl.multiple_of(jj * tn, LANES), width)
            for a, (w, st) in enumerate(((wg_hbm, stage_g), (wu_hbm, stage_u))):
                fn(pltpu.make_async_copy(w.at[layer, :, col], st.at[slot, :, pl.ds(0, width)], sem.at[slot, a]))

        @pl.when(jj < nj - 1)
        def _():
            run(tn)

        @pl.when(jj == nj - 1)
        def _():
            run(tail)

    @pl.when(i == 0)
    def _():
        slot = j % 2

        @pl.when(j == 0)
        def _():
            on_tile(0, 0, lambda c: c.start())

        @pl.when(j + 1 < nj)
        def _():
            on_tile(j + 1, 1 - slot, lambda c: c.start())

        on_tile(j, slot, lambda c: c.wait())
        wg_s[...] = stage_g[slot].astype(BF16)
        wu_s[...] = stage_u[slot].astype(BF16)
        if tail < tn:
            @pl.when(j == nj - 1)
            def _():
                wg_s[:, tail:] = jnp.zeros((wg_s.shape[0], tn - tail), BF16)
                wu_s[:, tail:] = jnp.zeros((wu_s.shape[0], tn - tail), BF16)

    xb = x_ref[...]
    hg = jnp.dot(xb, wg_s[...], preferred_element_type=F32)
    hu = jnp.dot(xb, wu_s[...], preferred_element_type=F32)
    o_ref[...] = (hg * _sigmoid(hg) * hu).astype(o_ref.dtype)


def _ffn_down_ln_kernel(h_ref, w_ref, x_ref, g_ref, b_ref, o_ref, ob_ref, *, alpha):
    y = jnp.dot(h_ref[...], w_ref[...], preferred_element_type=F32)
    out = _layer_norm(alpha * x_ref[...] + y, g_ref[...], b_ref[...])
    o_ref[...] = out
    ob_ref[...] = out.astype(BF16)


def _ffn_dense(xb, x, wg, wu, layer, wd, g, b, alpha, *, tm_up=1024, tn_up=512, tm_down=256):
    n, d = x.shape
    ff_real = wd.shape[0]
    n_tiles = -(-ff_real // tn_up)
    assert ff_real % LANES == 0 and (ff_real % tn_up == 0 or n_tiles >= 3)
    tm_up, tm_down = min(tm_up, n), min(tm_down, n)
    h = pl.pallas_call(
        functools.partial(_ffn_up_kernel, layer=layer, tn=tn_up, ff=ff_real),
        out_shape=jax.ShapeDtypeStruct((n, n_tiles * tn_up), BF16),
        grid=(n_tiles, n // tm_up),
        in_specs=[pl.BlockSpec((tm_up, d), lambda j, i: (i, 0)),
                  pl.BlockSpec(memory_space=pl.ANY),
                  pl.BlockSpec(memory_space=pl.ANY)],
        out_specs=pl.BlockSpec((tm_up, tn_up), lambda j, i: (i, j)),
        scratch_shapes=[pltpu.VMEM((2, d, tn_up), F32), pltpu.VMEM((2, d, tn_up), F32),
                        pltpu.VMEM((d, tn_up), BF16), pltpu.VMEM((d, tn_up), BF16),
                        pltpu.SemaphoreType.DMA((2, 2))],
        compiler_params=_cparams("arbitrary", "arbitrary"),
        name="ffn_up",
    )(xb, wg, wu)
    tok = pl.BlockSpec((tm_down, d), lambda i: (i, 0))
    row = pl.BlockSpec((1, d), lambda i: (0, 0))
    return pl.pallas_call(
        functools.partial(_ffn_down_ln_kernel, alpha=alpha),
        out_shape=(jax.ShapeDtypeStruct((n, d), F32), jax.ShapeDtypeStruct((n, d), BF16)),
        grid=(n // tm_down,),
        in_specs=[pl.BlockSpec((tm_down, ff_real), lambda i: (i, 0)),
                  pl.BlockSpec((ff_real, d), lambda i: (0, 0), pipeline_mode=pl.Buffered(1)),
                  tok, row, row],
        out_specs=(tok, tok),
        compiler_params=_cparams("parallel"),
        name="ffn_down_ln",
    )(h, wd, x, g.reshape(1, d), b.reshape(1, d))


def _gather_rows_kernel(tok_ref, x_hbm, o_ref, buf, sem, *, tm):
    i = pl.program_id(0)

    def issue(tile, slot):
        base = tile * tm

        def body(r, carry):
            t = tok_ref[base + r]
            pltpu.make_async_copy(x_hbm.at[pl.ds(t, 1), :], buf.at[slot, pl.ds(r, 1), :], sem.at[slot]).start()
            return carry

        lax.fori_loop(0, tm, body, 0, unroll=8)

    @pl.when(i == 0)
    def _():
        issue(0, 0)

    @pl.when(i + 1 < pl.num_programs(0))
    def _():
        issue(i + 1, (i + 1) % 2)

    slot = i % 2
    pltpu.make_async_copy(x_hbm.at[pl.ds(0, tm), :], buf.at[slot], sem.at[slot]).wait()
    o_ref[...] = buf[slot].astype(o_ref.dtype)


def _gather_rows(x, row_tok, *, tm=256):
    n, d = x.shape
    rows = row_tok.shape[0]
    return pl.pallas_call(
        functools.partial(_gather_rows_kernel, tm=tm),
        out_shape=jax.ShapeDtypeStruct((rows, d), BF16),
        grid_spec=pltpu.PrefetchScalarGridSpec(
            num_scalar_prefetch=1,
            grid=(rows // tm,),
            in_specs=[pl.BlockSpec(memory_space=pl.ANY)],
            out_specs=pl.BlockSpec((tm, d), lambda i, tok: (i, 0)),
            scratch_shapes=[pltpu.VMEM((2, tm, d), F32), pltpu.SemaphoreType.DMA((2,))],
        ),
        compiler_params=_cparams("arbitrary"),
        name="moe_gather_rows",
    )(row_tok, x)


def _expert_changed(te_ref, i):
    return (i == 0) | (te_ref[i] != te_ref[jnp.maximum(i - 1, 0)])


def _stream_expert_weights(j, i, n_j, col_tile, te_ref, run_ref, nxt_ref, meta_ref, w_hbm, stage, work, sem):
    @pl.when(_expert_changed(te_ref, i))
    def _():
        k = j * meta_ref[0] + run_ref[i]
        slot = k % 2

        def copies(e, jj, s):
            col = pl.ds(pl.multiple_of(jj * col_tile, col_tile), col_tile)
            return [pltpu.make_async_copy(w.at[e, :, col], st.at[s], sem.at[s, a])
                    for a, (w, st) in enumerate(zip(w_hbm, stage))]

        @pl.when(k == 0)
        def _():
            for c in copies(te_ref[i], j, slot):
                c.start()

        has_next_run = nxt_ref[i] >= 0
        e_next = jnp.where(has_next_run, nxt_ref[i], meta_ref[1])
        j_next = jnp.where(has_next_run, j, j + 1)

        @pl.when(j_next < n_j)
        def _():
            for c in copies(e_next, j_next, 1 - slot):
                c.start()

        for c in copies(te_ref[i], j, slot):
            c.wait()
        for st, wk in zip(stage, work):
            wk[...] = st[slot].astype(BF16)


def _moe_up_kernel(te_ref, nv_ref, run_ref, nxt_ref, meta_ref, x_ref, wg_hbm, wu_hbm, o_ref,
                   stage_g, stage_u, wg_s, wu_s, sem, *, tf):
    j, i = pl.program_id(0), pl.program_id(1)
    _stream_expert_weights(j, i, pl.num_programs(0), tf, te_ref, run_ref, nxt_ref, meta_ref,
                           (wg_hbm, wu_hbm), (stage_g, stage_u), (wg_s, wu_s), sem)

    @pl.when(i < nv_ref[0])
    def _():
        xb = x_ref[...]
        hg = jnp.dot(xb, wg_s[...], preferred_element_type=F32)
        hu = jnp.dot(xb, wu_s[...], preferred_element_type=F32)
        o_ref[...] = (hg * _sigmoid(hg) * hu).astype(o_ref.dtype)

    @pl.when(i >= nv_ref[0])
    def _():
        o_ref[...] = jnp.zeros(o_ref.shape, o_ref.dtype)


def _moe_up(xs, wg, wu, sched, *, tm, tf=1024):
    rows, d = xs.shape
    ff = wg.shape[-1]
    return pl.pallas_call(
        functools.partial(_moe_up_kernel, tf=tf),
        out_shape=jax.ShapeDtypeStruct((rows, ff), BF16),
        grid_spec=pltpu.PrefetchScalarGridSpec(
            num_scalar_prefetch=len(sched),
            grid=(ff // tf, rows // tm),
            in_specs=[pl.BlockSpec((tm, d), lambda j, i, te, nv, *_: (jnp.minimum(i, nv[0] - 1), 0)),
                      pl.BlockSpec(memory_space=pl.ANY),
                      pl.BlockSpec(memory_space=pl.ANY)],
            out_specs=pl.BlockSpec((tm, tf), lambda j, i, *_: (i, j)),
            scratch_shapes=[pltpu.VMEM((2, d, tf), F32), pltpu.VMEM((2, d, tf), F32),
                            pltpu.VMEM((d, tf), BF16), pltpu.VMEM((d, tf), BF16),
                            pltpu.SemaphoreType.DMA((2, 2))],
        ),
        compiler_params=_cparams("arbitrary", "arbitrary"),
        name="moe_up",
    )(*sched, xs, wg, wu)


def _moe_down_kernel(te_ref, nv_ref, run_ref, nxt_ref, meta_ref, h_ref, wd_hbm, o_ref, stage_d, wd_s, sem, *, tn):
    j, i = pl.program_id(0), pl.program_id(1)
    _stream_expert_weights(j, i, pl.num_programs(0), tn, te_ref, run_ref, nxt_ref, meta_ref,
                           (wd_hbm,), (stage_d,), (wd_s,), sem)

    @pl.when(i < nv_ref[0])
    def _():
        o_ref[...] = jnp.dot(h_ref[...], wd_s[...], preferred_element_type=F32)

    @pl.when(i >= nv_ref[0])
    def _():
        o_ref[...] = jnp.zeros(o_ref.shape, o_ref.dtype)


def _moe_down(h, wd, sched, *, tm, tn=512):
    rows, ff = h.shape
    d = wd.shape[-1]
    return pl.pallas_call(
        functools.partial(_moe_down_kernel, tn=tn),
        out_shape=jax.ShapeDtypeStruct((rows, d), F32),
        grid_spec=pltpu.PrefetchScalarGridSpec(
            num_scalar_prefetch=len(sched),
            grid=(d // tn, rows // tm),
            in_specs=[pl.BlockSpec((tm, ff), lambda j, i, te, nv, *_: (jnp.minimum(i, nv[0] - 1), 0)),
                      pl.BlockSpec(memory_space=pl.ANY)],
            out_specs=pl.BlockSpec((tm, tn), lambda j, i, *_: (i, j)),
            scratch_shapes=[pltpu.VMEM((2, ff, tn), F32), pltpu.VMEM((ff, tn), BF16),
                            pltpu.SemaphoreType.DMA((2, 1))],
        ),
        compiler_params=_cparams("arbitrary", "arbitrary"),
        name="moe_down",
    )(*sched, h, wd)


def _combine_ln_kernel(pos_ref, y_hbm, x_ref, wts_ref, g_ref, b_ref, o_ref, buf, sem, *, tm, alpha):
    i = pl.program_id(0)

    def issue(tile, slot):
        base = tile * tm

        def body(r, carry):
            for k in range(TOP_K):
                row = pos_ref[(base + r) * TOP_K + k]
                pltpu.make_async_copy(y_hbm.at[pl.ds(row, 1), :], buf.at[slot, k, pl.ds(r, 1), :],
                                      sem.at[slot]).start()
            return carry

        lax.fori_loop(0, tm, body, 0, unroll=8)

    @pl.when(i == 0)
    def _():
        issue(0, 0)

    @pl.when(i + 1 < pl.num_programs(0))
    def _():
        issue(i + 1, (i + 1) % 2)

    slot = i % 2
    for k in range(TOP_K):
        pltpu.make_async_copy(y_hbm.at[pl.ds(0, tm), :], buf.at[slot, k], sem.at[slot]).wait()
    wts = wts_ref[...]
    y = buf[slot, 0] * wts[:, 0:1] + buf[slot, 1] * wts[:, 1:2]
    o_ref[...] = _layer_norm(alpha * x_ref[...] + y, g_ref[...], b_ref[...])


def _combine_ln(yrows, pos, x, wts, g, b, alpha, *, tm=128):
    n, d = x.shape
    tm = min(tm, n)
    return pl.pallas_call(
        functools.partial(_combine_ln_kernel, tm=tm, alpha=alpha),
        out_shape=jax.ShapeDtypeStruct((n, d), F32),
        grid_spec=pltpu.PrefetchScalarGridSpec(
            num_scalar_prefetch=1,
            grid=(n // tm,),
            in_specs=[pl.BlockSpec(memory_space=pl.ANY),
                      pl.BlockSpec((tm, d), lambda i, pos: (i, 0)),
                      pl.BlockSpec((tm, LANES), lambda i, pos: (i, 0)),
                      pl.BlockSpec((1, d), lambda i, pos: (0, 0)),
                      pl.BlockSpec((1, d), lambda i, pos: (0, 0))],
            out_specs=pl.BlockSpec((tm, d), lambda i, pos: (i, 0)),
            scratch_shapes=[pltpu.VMEM((2, TOP_K, tm, d), F32), pltpu.SemaphoreType.DMA((2,))],
        ),
        compiler_params=_cparams("arbitrary"),
        name="moe_combine_ln",
    )(pos, yrows, x, wts, g.reshape(1, d), b.reshape(1, d))


def _moe_block(x, w_router, wg, wu, wd, g, b, alpha, *, tm=256):
    n, d = x.shape
    ids, wts = _router(x, w_router)
    e_flat = ids[:, :TOP_K].reshape(-1)
    n_assign = n * TOP_K
    onehot = (e_flat[:, None] == jnp.arange(N_EXPERTS, dtype=I32)[None, :]).astype(I32)
    rank = jnp.sum((jnp.cumsum(onehot, axis=0) - onehot) * onehot, axis=1)
    counts = jnp.sum(onehot, axis=0)
    padded = (counts + tm - 1) // tm * tm
    end_padded = jnp.cumsum(padded)
    start_padded = end_padded - padded
    dest = (start_padded[e_flat] + rank).astype(I32)
    rows = n_assign + N_EXPERTS * tm
    n_tiles = rows // tm
    flat_tok = jnp.arange(n_assign, dtype=I32) // TOP_K
    row_tok = jnp.zeros((rows,), I32).at[dest].set(flat_tok)
    tile_start = jnp.arange(n_tiles, dtype=I32) * tm
    tile_e = jnp.minimum(jnp.sum((tile_start[:, None] >= end_padded[None, :]).astype(I32), axis=1),
                         N_EXPERTS - 1).astype(I32)
    n_valid = (end_padded[-1:] // tm).astype(I32)
    tile_ix = jnp.arange(n_tiles, dtype=I32)
    tile_e = jnp.where(tile_ix < n_valid[0], tile_e, tile_e[jnp.maximum(n_valid[0] - 1, 0)])
    is_start = jnp.concatenate([jnp.ones((1,), I32), (tile_e[1:] != tile_e[:-1]).astype(I32)])
    run_id = (jnp.cumsum(is_start) - 1).astype(I32)
    larger = jnp.where(tile_e[None, :] > tile_e[:, None], tile_e[None, :], N_EXPERTS)
    nxt_e = jnp.min(larger, axis=1)
    nxt_e = jnp.where(nxt_e < N_EXPERTS, nxt_e, -1).astype(I32)
    meta = jnp.stack([run_id[-1] + 1, tile_e[0]]).astype(I32)
    sched = (tile_e, n_valid, run_id, nxt_e, meta)
    xs = _gather_rows(x, row_tok, tm=tm)
    h = _moe_up(xs, wg, wu, sched, tm=tm)
    yrows = _moe_down(h, wd, sched, tm=tm)
    return _combine_ln(yrows, dest, x, wts, g, b, alpha)


def _w_in_columns(d_model):
    attn_w = N_HEADS * HEAD_DIM
    kv_w = N_KV * HEAD_DIM
    o = np.cumsum([0, attn_w, kv_w, kv_w, IDX_HEADS * IDX_DIM, IDX_DIM, IDX_HEADS, d_model, d_model, d_model])
    o = [int(v) for v in o]
    return {"q": (o[0], attn_w), "kv": (o[1], 2 * kv_w), "qi": (o[3], o[4] - o[3]),
            "kiwi": (o[4], LANES),
            "px": (o[6], 2 * d_model), "gr": (o[8], d_model), "gates": (o[9], 3 * d_model)}


def _mixer(x_mm, tabs, wt_in, layer, pool_w, pool_scale, conv_w, conv_b, wa, ba, wx, bx, lam,
           batch, seq, d_model, tq):
    cos_a, sin_a, cos_i, sin_ia, sin_ib, cos_q, sin_q, cos_iq = tabs
    cols = _w_in_columns(d_model)
    rope_a = ((cos_a, sin_a), (HEAD_DIM // 2,))
    rope_i = (LANES - IDX_DIM // 2, IDX_DIM // 2)
    qh = _proj_heads(x_mm, wt_in, layer, cols["q"], (cos_q, sin_q), rope_a[1], BF16)
    kvh = _proj_heads(x_mm, wt_in, layer, cols["kv"], *rope_a, BF16, rope_heads=N_KV)
    qih = _proj_heads(x_mm, wt_in, layer, cols["qi"], (cos_iq, sin_ia, sin_ib), rope_i, BF16, head_width=IDX_DIM)
    kiwi = _proj_heads(x_mm, wt_in, layer, cols["kiwi"], (cos_i, sin_ia, sin_ib), rope_i, F32)
    px = _proj_plain(x_mm, wt_in, layer, cols["px"], None, F32)
    gg = _proj_plain(x_mm, wt_in, layer, cols["gr"], "gelu", BF16)
    sg = _proj_plain(x_mm, wt_in, layer, cols["gates"], "sigmoid", BF16)
    ya = _attention(qh, kvh, qih, kiwi, sg, batch, seq, tq=tq)
    yb = _pool_mixer(px, sg, pool_w.astype(BF16), pool_scale, batch, seq)
    yc = _lru_mixer(px, gg, sg, conv_w, conv_b, wa.astype(BF16), ba, wx.astype(BF16), bx, lam, batch, seq)
    return ya, yb, yc


def kernel(x, positions, mix_w_in, mix_w_out, pool_w, pool_scale, conv_w, conv_b, lru_wa, lru_ba, lru_wx, lru_bx, lru_lam, ln_mix_g, ln_mix_b, ln_ffn_g, ln_ffn_b, dense_w_gate, dense_w_up, dense_w_down, moe_router, moe_w_gate, moe_w_up, moe_w_down):
    batch, seq, d_model = x.shape
    depth = mix_w_in.shape[0]
    alpha = np.float32((2 * depth) ** 0.25)
    n = batch * seq
    tabs = _rope_tables(positions)
    xf = x.reshape(n, d_model)
    x_mm = xf
    wt_in = jnp.swapaxes(mix_w_in, 1, 2)
    for layer in range(depth):
        ya, yb, yc = _mixer(x_mm, tabs, wt_in, layer, pool_w[layer], pool_scale[layer],
                            conv_w[layer], conv_b[layer], lru_wa[layer], lru_ba[layer], lru_wx[layer],
                            lru_bx[layer], lru_lam[layer], batch, seq, d_model, tq=256)
        w_out3 = mix_w_out[layer].astype(BF16).reshape(3, -1, d_model)
        xf, xb = _out_proj_ln(ya, yb, yc, w_out3, xf, ln_mix_g[layer], ln_mix_b[layer], alpha)
        j = layer // 2
        if layer % 2 == 0:
            wd = dense_w_down[j].astype(BF16)
            xf, x_mm = _ffn_dense(xb, xf, dense_w_gate, dense_w_up, j, wd,
                                  ln_ffn_g[layer], ln_ffn_b[layer], alpha)
        else:
            xf = _moe_block(xf, moe_router[j], moe_w_gate[j], moe_w_up[j], moe_w_down[j],
                            ln_ffn_g[layer], ln_ffn_b[layer], alpha)
            x_mm = xf
    return xf.reshape(batch, seq, d_model)
```
